```python
import math
import jax
import jax.numpy as jnp
from jax import lax
import numpy as np

D_MODEL = 1024
BATCH = 2
SEQ = 8192
DEPTH = 2
DEC_BATCH = 128
DEC_SEQ = 8
PAST_LEN = 16384
PAGE_SIZE = 128

N_EVEN = (DEPTH + 1) // 2
N_ODD = DEPTH // 2
EPS = 1e-6
ROPE_THETA = 10000.0
HEAD_DIM = 64

A_HEADS = 4
A_DK = 128
A_DV = 128
A_QK = A_HEADS * A_DK
A_VW = A_HEADS * A_DV
A_CONV = 4
A_CHUNK = 64
A_F_BIAS_LO = 3.0
A_F_BIAS_HI = 6.0

B_HEADS = 8
B_WINDOWS = (128, 512, 2048)
B_DILATIONS = (1, 4, 16)
B_GROUPS = 3
B_QKV = B_GROUPS * B_HEADS * HEAD_DIM
B_OUT = B_HEADS * HEAD_DIM
B_QBLOCK = 128

C_HEADS = 8
C_HEADDIM = 64
C_DINNER = C_HEADS * C_HEADDIM
C_DSTATE = 128
C_NGROUPS = 2
C_HPG = C_HEADS // C_NGROUPS
C_CONV = 4
C_CONV_DIM = C_DINNER + 2 * C_NGROUPS * C_DSTATE
C_CHUNK = 128
C_DT_MIN = 0.001
C_DT_MAX = 0.1

D_HEADS = 8
D_KV_HEADS = 2
D_REP = D_HEADS // D_KV_HEADS
D_WINDOW = 128
D_QBLOCK = 128
D_QW = D_HEADS * HEAD_DIM
D_KVW = D_KV_HEADS * HEAD_DIM

EVEN_IN = 2 * A_QK + 2 * A_VW + 2 * A_HEADS + 3 * B_QKV
EVEN_MIX = A_VW + B_OUT
ODD_IN = C_DINNER + C_CONV_DIM + C_HEADS + D_QW + 2 * D_KVW
ODD_MIX = C_DINNER + D_QW
FFN_HIDDEN = -(-8 * D_MODEL // (3 * 256)) * 256

EVEN_KEYS = ('a_C', 'a_n', 'a_m', 'a_conv', 'b_k1', 'b_v1', 'b_k2', 'b_v2', 'b_k3', 'b_v3')
ODD_KEYS = ('c_conv', 'c_ssm', 'd_k', 'd_v')

kernel_name = 'hybrid_mlstm_dilated_ssd_swa_decode_step'


def split_cols(a, widths):
    cuts = [int(c) for c in np.cumsum(widths)[:-1]]
    return jnp.split(a, cuts, axis=-1)


def rmsnorm(x, w):
    xf = x.astype(jnp.float32)
    y = xf * lax.rsqrt(jnp.mean(xf * xf, axis=-1, keepdims=True) + EPS)
    return (y * w.astype(jnp.float32)).astype(x.dtype)


def rope(x, pos):
    half = x.shape[-1] // 2
    inv = ROPE_THETA ** (-jnp.arange(half, dtype=jnp.float32) / half)
    ang = pos.astype(jnp.float32)[:, None] * inv[None, :]
    ang = ang.reshape((1, x.shape[1]) + (1,) * (x.ndim - 3) + (half,))
    cos, sin = jnp.cos(ang), jnp.sin(ang)
    xf = x.astype(jnp.float32)
    x1, x2 = xf[..., :half], xf[..., half:]
    return jnp.concatenate([x1 * cos - x2 * sin, x2 * cos + x1 * sin], axis=-1).astype(x.dtype)


def causal_conv(x, buf, w, b):
    K, T = w.shape[0], x.shape[1]
    xe = jnp.concatenate([buf.astype(x.dtype), x], axis=1)
    y = b
    for j in range(K):
        y = y + xe[:, j:j + T] * w[j]
    return jax.nn.silu(y).astype(x.dtype), xe[:, xe.shape[1] - (K - 1):]


def to_chunks(a, L):
    Bsz, T = a.shape[:2]
    return jnp.moveaxis(a.reshape((Bsz, T // L, L) + a.shape[2:]), 1, 0)


def unblock(a):
    a = jnp.moveaxis(a, 0, 1)
    return a.reshape((a.shape[0], a.shape[1] * a.shape[2]) + a.shape[3:])


def mlstm_chunkwise(q, k, v, log_i, log_f, C0, n0, m0):
    T = q.shape[1]
    L = A_CHUNK if T % A_CHUNK == 0 else T
    causal = jnp.tril(jnp.ones((L, L), dtype=bool))

    def step(carry, inp):
        C, n, m = carry
        qc, kc, vc, li, lf = inp
        b = jnp.cumsum(lf, axis=1)
        d_log = b[:, :, None, :] - b[:, None, :, :] + li[:, None, :, :]
        d_log = jnp.where(causal[None, :, :, None], d_log, -jnp.inf)
        inter = b + m[:, None, :]
        m_t = jnp.maximum(inter, jnp.max(d_log, axis=2))
        w_intra = jnp.exp(d_log - m_t[:, :, None, :])
        w_inter = jnp.exp(inter - m_t)
        s = jnp.einsum('bthd,bshd->btsh', qc, kc) * w_intra
        num = jnp.einsum('btsh,bshv->bthv', s, vc) + w_inter[..., None] * jnp.einsum('bthd,bhdv->bthv', qc, C)
        den = jnp.sum(s, axis=2) + w_inter * jnp.einsum('bthd,bhd->bth', qc, n)
        h = num / jnp.maximum(jnp.abs(den), jnp.exp(-m_t))[..., None]
        b_end = b[:, -1]
        g = b_end[:, None, :] - b + li
        m_new = jnp.maximum(b_end + m, jnp.max(g, axis=1))
        w_state = jnp.exp(g - m_new[:, None, :])
        decay = jnp.exp(b_end + m - m_new)
        C_new = decay[..., None, None] * C + jnp.einsum('bsh,bshd,bshv->bhdv', w_state, kc, vc)
        n_new = decay[..., None] * n + jnp.einsum('bsh,bshd->bhd', w_state, kc)
        return (C_new, n_new, m_new), h

    xs = tuple(to_chunks(a.astype(jnp.float32), L) for a in (q, k, v, log_i, log_f))
    init = (C0.astype(jnp.float32), n0.astype(jnp.float32), m0.astype(jnp.float32))
    (C, n, m), h = lax.scan(step, init, xs)
    return unblock(h), C, n, m


def dilated_branch(q, k_ext, v_ext, window, dilation):
    Bsz, T, H, hd = q.shape
    off = k_ext.shape[1] - T
    nk = window // dilation + 1
    qb = B_QBLOCK if T % B_QBLOCK == 0 else T
    dist = dilation * jnp.arange(nk)
    scale = hd ** -0.5

    def block(bi):
        qi = bi * qb + jnp.arange(qb)
        idx = off + qi[:, None] - dist[None, :]
        valid = idx >= 0
        idx = jnp.maximum(idx, 0)
        kg = k_ext[:, idx].astype(jnp.float32)
        vg = v_ext[:, idx].astype(jnp.float32)
        qblk = lax.dynamic_slice_in_dim(q, bi * qb, qb, axis=1).astype(jnp.float32)
        s = jnp.einsum('bqhd,bqkhd->bqhk', qblk, kg) * scale
        s = jnp.where(valid[None, :, None, :], s, -jnp.inf)
        mx = jnp.max(s, axis=-1)
        p = jnp.exp(s - mx[..., None])
        den = jnp.sum(p, axis=-1)
        o = jnp.einsum('bqhk,bqkhd->bqhd', p, vg) / den[..., None]
        return o, mx, den

    o, mx, den = lax.map(block, jnp.arange(T // qb))
    return unblock(o), unblock(mx), unblock(den)


def ssd_chunked(x, dt, A, Bm, Cm, h0):
    T = x.shape[1]
    L = C_CHUNK if T % C_CHUNK == 0 else T
    causal = jnp.tril(jnp.ones((L, L), dtype=bool))

    def step(h, inp):
        xc, dtc, Bc, Cc = inp
        cum = jnp.cumsum(dtc * A, axis=1)
        seg = cum[:, :, None] - cum[:, None, :]
        decay = jnp.exp(jnp.where(causal[None, :, :, None, None], seg, -jnp.inf))
        cb = jnp.einsum('btgn,bsgn->btsg', Cc, Bc)
        w = cb[..., None] * decay * dtc[:, None]
        y = jnp.einsum('btsgh,bsghp->btghp', w, xc)
        y = y + jnp.exp(cum)[..., None] * jnp.einsum('btgn,bghpn->btghp', Cc, h)
        w_end = jnp.exp(cum[:, -1:] - cum) * dtc
        h_new = jnp.exp(cum[:, -1])[..., None, None] * h + jnp.einsum('bsgh,bsgn,bsghp->bghpn', w_end, Bc, xc)
        return h_new, y

    xs = tuple(to_chunks(a, L) for a in (x, dt, Bm, Cm))
    h, y = lax.scan(step, h0, xs)
    return unblock(y), h


def swa_sinks(q, k_ext, v_ext, sinks):
    Bsz, T, KVH, R, hd = q.shape
    off = k_ext.shape[1] - T
    W = D_WINDOW
    qb = D_QBLOCK if T % D_QBLOCK == 0 else T
    kw = qb + W
    kp = jnp.pad(k_ext, ((0, 0), (W, 0), (0, 0), (0, 0)))
    vp = jnp.pad(v_ext, ((0, 0), (W, 0), (0, 0), (0, 0)))
    sink = sinks.reshape(KVH, R).astype(jnp.float32)[None, :, :, None, None]
    scale = hd ** -0.5

    def block(bi):
        q0 = bi * qb
        qblk = lax.dynamic_slice_in_dim(q, q0, qb, axis=1).astype(jnp.float32)
        kblk = lax.dynamic_slice_in_dim(kp, off + q0, kw, axis=1).astype(jnp.float32)
        vblk = lax.dynamic_slice_in_dim(vp, off + q0, kw, axis=1).astype(jnp.float32)
        qpos = off + q0 + jnp.arange(qb)
        kpos = off + q0 + jnp.arange(kw) - W
        rel = qpos[:, None] - kpos[None, :]
        valid = (rel >= 0) & (rel < W) & (kpos[None, :] >= 0)
        s = jnp.einsum('bqgrd,bkgd->bgrqk', qblk, kblk) * scale
        s = jnp.where(valid[None, None, None], s, -jnp.inf)
        logits = jnp.concatenate([s, jnp.broadcast_to(sink, s.shape[:-1] + (1,))], axis=-1)
        p = jax.nn.softmax(logits, axis=-1)[..., :-1]
        return jnp.einsum('bgrqk,bkgd->bqgrd', p, vblk)

    return unblock(lax.map(block, jnp.arange(T // qb)))


def even_layer(x, pos, st, p):
    Bsz, T, _ = x.shape
    h = rmsnorm(x, p['norm'])
    proj = h @ p['w_in']
    qa, ka, va, oa, ia, fa, qb, kb, vb = split_cols(
        proj, (A_QK, A_QK, A_VW, A_VW, A_HEADS, A_HEADS, B_QKV, B_QKV, B_QKV))
    qk, conv_new = causal_conv(jnp.concatenate([qa, ka], axis=-1), st['a_conv'], p['a_conv_w'], p['a_conv_b'])
    qa, ka = jnp.split(qk, 2, axis=-1)
    gates = (jnp.concatenate([ia, fa], axis=-1) + p['b_gates']).astype(jnp.float32)
    log_i = gates[..., :A_HEADS]
    log_f = jax.nn.log_sigmoid(gates[..., A_HEADS:])
    ha, C_new, n_new, m_new = mlstm_chunkwise(
        qa.reshape(Bsz, T, A_HEADS, A_DK),
        ka.reshape(Bsz, T, A_HEADS, A_DK) * (A_DK ** -0.5),
        va.reshape(Bsz, T, A_HEADS, A_DV), log_i, log_f, st['a_C'], st['a_n'], st['a_m'])
    ha = rmsnorm(ha, p['a_norm'].reshape(A_HEADS, A_DV))
    ya = (ha.reshape(Bsz, T, A_VW) * jax.nn.sigmoid(oa.astype(jnp.float32))).astype(x.dtype)
    new_st = {'a_C': C_new, 'a_n': n_new, 'a_m': m_new, 'a_conv': conv_new}
    qb = rope(qb.reshape(Bsz, T, B_GROUPS, B_HEADS, HEAD_DIM), pos)
    kb = rope(kb.reshape(Bsz, T, B_GROUPS, B_HEADS, HEAD_DIM), pos)
    vb = vb.reshape(Bsz, T, B_GROUPS, B_HEADS, HEAD_DIM)
    outs, maxes, dens = [], [], []
    for g in range(B_GROUPS):
        kn, vn = f'b_k{g + 1}', f'b_v{g + 1}'
        k_ext = jnp.concatenate([st[kn].astype(x.dtype), kb[:, :, g]], axis=1)
        v_ext = jnp.concatenate([st[vn].astype(x.dtype), vb[:, :, g]], axis=1)
        o, mx, den = dilated_branch(qb[:, :, g], k_ext, v_ext, B_WINDOWS[g], B_DILATIONS[g])
        outs.append(o)
        maxes.append(mx)
        dens.append(den)
        keep = min(B_WINDOWS[g], k_ext.shape[1])
        new_st[kn] = k_ext[:, k_ext.shape[1] - keep:]
        new_st[vn] = v_ext[:, v_ext.shape[1] - keep:]
    mx = jnp.stack(maxes)
    wts = jnp.exp(mx - jnp.max(mx, axis=0, keepdims=True)) * jnp.stack(dens)
    wts = wts / jnp.sum(wts, axis=0, keepdims=True)
    yb = jnp.sum(wts[..., None] * jnp.stack(outs), axis=0).reshape(Bsz, T, B_OUT).astype(x.dtype)
    y = jnp.concatenate([ya, yb], axis=-1) @ p['w_out']
    return x + y, new_st


def odd_layer(x, pos, st, p):
    Bsz, T, _ = x.shape
    h = rmsnorm(x, p['norm'])
    proj = h @ p['w_in']
    z, xbc, dt, qd, kd, vd = split_cols(proj, (C_DINNER, C_CONV_DIM, C_HEADS, D_QW, D_KVW, D_KVW))
    xbc, conv_new = causal_conv(xbc, st['c_conv'], p['c_conv_w'], p['c_conv_b'])
    xc, Bc, Cc = split_cols(xbc, (C_DINNER, C_NGROUPS * C_DSTATE, C_NGROUPS * C_DSTATE))
    dt = jax.nn.softplus(dt.astype(jnp.float32) + p['c_dt_bias'].astype(jnp.float32))
    A = -jnp.exp(p['c_A_log'].astype(jnp.float32)).reshape(C_NGROUPS, C_HPG)
    xh = xc.reshape(Bsz, T, C_NGROUPS, C_HPG, C_HEADDIM).astype(jnp.float32)
    h0 = st['c_ssm'].astype(jnp.float32).reshape(Bsz, C_NGROUPS, C_HPG, C_HEADDIM, C_DSTATE)
    yc, ssm_new = ssd_chunked(
        xh, dt.reshape(Bsz, T, C_NGROUPS, C_HPG), A,
        Bc.reshape(Bsz, T, C_NGROUPS, C_DSTATE).astype(jnp.float32),
        Cc.reshape(Bsz, T, C_NGROUPS, C_DSTATE).astype(jnp.float32), h0)
    yc = yc + p['c_D'].astype(jnp.float32).reshape(C_NGROUPS, C_HPG)[:, :, None] * xh
    yc = yc.reshape(Bsz, T, C_DINNER) * jax.nn.silu(z.astype(jnp.float32))
    yc = rmsnorm(yc.reshape(Bsz, T, C_NGROUPS, -1), p['c_norm'].reshape(C_NGROUPS, -1))
    yc = yc.reshape(Bsz, T, C_DINNER).astype(x.dtype)
    qd = rope(qd.reshape(Bsz, T, D_KV_HEADS, D_REP, HEAD_DIM), pos)
    kd = rope(kd.reshape(Bsz, T, D_KV_HEADS, HEAD_DIM), pos)
    vd = vd.reshape(Bsz, T, D_KV_HEADS, HEAD_DIM)
    k_ext = jnp.concatenate([st['d_k'].astype(x.dtype), kd], axis=1)
    v_ext = jnp.concatenate([st['d_v'].astype(x.dtype), vd], axis=1)
    yd = swa_sinks(qd, k_ext, v_ext, p['d_sinks']).reshape(Bsz, T, D_QW).astype(x.dtype)
    keep = min(D_WINDOW, k_ext.shape[1])
    new_st = {'c_conv': conv_new,
              'c_ssm': ssm_new.reshape(Bsz, C_HEADS, C_HEADDIM, C_DSTATE),
              'd_k': k_ext[:, k_ext.shape[1] - keep:],
              'd_v': v_ext[:, v_ext.shape[1] - keep:]}
    y = jnp.concatenate([yc, yd], axis=-1) @ p['w_out']
    return x + y, new_st


def swiglu_ffn(x, norm_w, w_gate, w_up, w_down):
    h = rmsnorm(x, norm_w)
    return x + (jax.nn.silu(h @ w_gate) * (h @ w_up)) @ w_down


def run_trunk(x, pos, even_st, odd_st, even_p, odd_p, ffn_p, final_norm):
    even_new, odd_new = [], []
    for layer in range(DEPTH):
        j = layer // 2
        if layer % 2 == 0:
            x, ns = even_layer(x, pos, {k: even_st[k][j] for k in EVEN_KEYS},
                               {k: v[j] for k, v in even_p.items()})
            even_new.append(ns)
        else:
            x, ns = odd_layer(x, pos, {k: odd_st[k][j] for k in ODD_KEYS},
                              {k: v[j] for k, v in odd_p.items()})
            odd_new.append(ns)
        x = swiglu_ffn(x, ffn_p['norm'][layer], ffn_p['w_gate'][layer], ffn_p['w_up'][layer], ffn_p['w_down'][layer])
    new_st = {k: jnp.stack([ns[k] for ns in even_new]) for k in EVEN_KEYS}
    new_st.update({k: jnp.stack([ns[k] for ns in odd_new]) for k in ODD_KEYS})
    return rmsnorm(x, final_norm), new_st


def setup_inputs(seed: int = 0) -> dict:
    key = jax.random.key(seed)
    keys = iter(jax.random.split(key, 64))
    f32 = jnp.float32

    def nrm(shape, scale):
        return scale * jax.random.normal(next(keys), shape, f32)

    def gain(shape):
        return 1.0 + nrm(shape, 0.02)

    lb = [min(w, PAST_LEN) for w in B_WINDOWS]
    ld = min(D_WINDOW, PAST_LEN)
    dt0 = jnp.exp(jax.random.uniform(next(keys), (N_ODD, C_HEADS), f32,
                                     math.log(C_DT_MIN), math.log(C_DT_MAX)))
    return {
        'x_prompt': nrm((BATCH, SEQ, D_MODEL), 1.0),
        'x_sample': nrm((DEC_BATCH, DEC_SEQ, D_MODEL), 1.0),
        'state_a_C': nrm((N_EVEN, DEC_BATCH, A_HEADS, A_DK, A_DV), 0.1),
        'state_a_n': nrm((N_EVEN, DEC_BATCH, A_HEADS, A_DK), 0.1),
        'state_a_m': nrm((N_EVEN, DEC_BATCH, A_HEADS), 1.0),
        'state_a_conv': nrm((N_EVEN, DEC_BATCH, A_CONV - 1, 2 * A_QK), 1.0),
        'cache_b_k1': nrm((N_EVEN, DEC_BATCH, lb[0], B_HEADS, HEAD_DIM), 1.0),
        'cache_b_v1': nrm((N_EVEN, DEC_BATCH, lb[0], B_HEADS, HEAD_DIM), 1.0),
        'cache_b_k2': nrm((N_EVEN, DEC_BATCH, lb[1], B_HEADS, HEAD_DIM), 1.0),
        'cache_b_v2': nrm((N_EVEN, DEC_BATCH, lb[1], B_HEADS, HEAD_DIM), 1.0),
        'cache_b_k3': nrm((N_EVEN, DEC_BATCH, lb[2], B_HEADS, HEAD_DIM), 1.0),
        'cache_b_v3': nrm((N_EVEN, DEC_BATCH, lb[2], B_HEADS, HEAD_DIM), 1.0),
        'state_c_conv': nrm((N_ODD, DEC_BATCH, C_CONV - 1, C_CONV_DIM), 1.0),
        'state_c_ssm': nrm((N_ODD, DEC_BATCH, C_HEADS, C_HEADDIM, C_DSTATE), 0.5),
        'cache_d_k': nrm((N_ODD, DEC_BATCH, ld, D_KV_HEADS, HEAD_DIM), 1.0),
        'cache_d_v': nrm((N_ODD, DEC_BATCH, ld, D_KV_HEADS, HEAD_DIM), 1.0),
        'e_norm_mix': gain((N_EVEN, D_MODEL)),
        'e_w_in': nrm((N_EVEN, D_MODEL, EVEN_IN), D_MODEL ** -0.5),
        'e_b_gates': jnp.concatenate(
            [nrm((N_EVEN, A_HEADS), 0.1),
             jnp.linspace(A_F_BIAS_LO, A_F_BIAS_HI, A_HEADS, dtype=f32)[None, :] + nrm((N_EVEN, A_HEADS), 0.1)],
            axis=-1),
        'e_a_conv_w': nrm((N_EVEN, A_CONV, 2 * A_QK), A_CONV ** -0.5),
        'e_a_conv_b': nrm((N_EVEN, 2 * A_QK), 0.02),
        'e_a_norm': gain((N_EVEN, A_VW)),
        'e_w_out': nrm((N_EVEN, EVEN_MIX, D_MODEL), EVEN_MIX ** -0.5),
        'o_norm_mix': gain((N_ODD, D_MODEL)),
        'o_w_in': nrm((N_ODD, D_MODEL, ODD_IN), D_MODEL ** -0.5),
        'o_c_conv_w': nrm((N_ODD, C_CONV, C_CONV_DIM), C_CONV ** -0.5),
        'o_c_conv_b': nrm((N_ODD, C_CONV_DIM), 0.02),
        'o_c_dt_bias': dt0 + jnp.log(-jnp.expm1(-dt0)),
        'o_c_A_log': jnp.log(jax.random.uniform(next(keys), (N_ODD, C_HEADS), f32, 1.0, 16.0)),
        'o_c_D': 1.0 + nrm((N_ODD, C_HEADS), 0.1),
        'o_c_norm': gain((N_ODD, C_DINNER)),
        'o_d_sinks': nrm((N_ODD, D_HEADS), 0.5),
        'o_w_out': nrm((N_ODD, ODD_MIX, D_MODEL), ODD_MIX ** -0.5),
        'ffn_norm': gain((DEPTH, D_MODEL)),
        'w_gate': nrm((DEPTH, D_MODEL, FFN_HIDDEN), D_MODEL ** -0.5),
        'w_up': nrm((DEPTH, D_MODEL, FFN_HIDDEN), D_MODEL ** -0.5),
        'w_down': nrm((DEPTH, FFN_HIDDEN, D_MODEL), FFN_HIDDEN ** -0.5),
        'final_norm': gain((D_MODEL,)),
    }


def reference(x_prompt, x_sample, state_a_C, state_a_n, state_a_m, state_a_conv,
              cache_b_k1, cache_b_v1, cache_b_k2, cache_b_v2, cache_b_k3, cache_b_v3,
              state_c_conv, state_c_ssm, cache_d_k, cache_d_v,
              e_norm_mix, e_w_in, e_b_gates, e_a_conv_w, e_a_conv_b, e_a_norm, e_w_out,
              o_norm_mix, o_w_in, o_c_conv_w, o_c_conv_b, o_c_dt_bias, o_c_A_log, o_c_D, o_c_norm,
              o_d_sinks, o_w_out, ffn_norm, w_gate, w_up, w_down, final_norm):
    f32 = jnp.float32
    even_p = {'norm': e_norm_mix, 'w_in': e_w_in, 'b_gates': e_b_gates, 'a_conv_w': e_a_conv_w,
              'a_conv_b': e_a_conv_b, 'a_norm': e_a_norm, 'w_out': e_w_out}
    odd_p = {'norm': o_norm_mix, 'w_in': o_w_in, 'c_conv_w': o_c_conv_w, 'c_conv_b': o_c_conv_b,
             'c_dt_bias': o_c_dt_bias, 'c_A_log': o_c_A_log, 'c_D': o_c_D, 'c_norm': o_c_norm,
             'd_sinks': o_d_sinks, 'w_out': o_w_out}
    ffn_p = {'norm': ffn_norm, 'w_gate': w_gate, 'w_up': w_up, 'w_down': w_down}

    Bp, Tp = x_prompt.shape[0], x_prompt.shape[1]
    dtp = x_prompt.dtype
    empty_b = jnp.zeros((N_EVEN, Bp, 0, B_HEADS, HEAD_DIM), dtp)
    even_st_p = {'a_C': jnp.zeros((N_EVEN, Bp, A_HEADS, A_DK, A_DV), f32),
                 'a_n': jnp.zeros((N_EVEN, Bp, A_HEADS, A_DK), f32),
                 'a_m': jnp.zeros((N_EVEN, Bp, A_HEADS), f32),
                 'a_conv': jnp.zeros((N_EVEN, Bp, A_CONV - 1, 2 * A_QK), dtp),
                 'b_k1': empty_b, 'b_v1': empty_b, 'b_k2': empty_b, 'b_v2': empty_b,
                 'b_k3': empty_b, 'b_v3': empty_b}
    odd_st_p = {'c_conv': jnp.zeros((N_ODD, Bp, C_CONV - 1, C_CONV_DIM), dtp),
                'c_ssm': jnp.zeros((N_ODD, Bp, C_HEADS, C_HEADDIM, C_DSTATE), f32),
                'd_k': jnp.zeros((N_ODD, Bp, 0, D_KV_HEADS, HEAD_DIM), dtp),
                'd_v': jnp.zeros((N_ODD, Bp, 0, D_KV_HEADS, HEAD_DIM), dtp)}
    pos_p = jnp.arange(Tp, dtype=jnp.int32)
    y_prompt, pn = run_trunk(x_prompt, pos_p, even_st_p, odd_st_p, even_p, odd_p, ffn_p, final_norm)

    even_st_s = {'a_C': state_a_C, 'a_n': state_a_n, 'a_m': state_a_m, 'a_conv': state_a_conv,
                 'b_k1': cache_b_k1, 'b_v1': cache_b_v1, 'b_k2': cache_b_k2, 'b_v2': cache_b_v2,
                 'b_k3': cache_b_k3, 'b_v3': cache_b_v3}
    odd_st_s = {'c_conv': state_c_conv, 'c_ssm': state_c_ssm, 'd_k': cache_d_k, 'd_v': cache_d_v}
    pos_s = PAST_LEN + jnp.arange(x_sample.shape[1], dtype=jnp.int32)
    y_sample, sn = run_trunk(x_sample, pos_s, even_st_s, odd_st_s, even_p, odd_p, ffn_p, final_norm)

    return (y_prompt, y_sample,
            pn['a_C'], sn['a_C'], pn['a_n'], sn['a_n'], pn['a_m'], sn['a_m'], pn['a_conv'], sn['a_conv'],
            pn['b_k1'], sn['b_k1'], pn['b_v1'], sn['b_v1'], pn['b_k2'], sn['b_k2'], pn['b_v2'], sn['b_v2'],
            pn['b_k3'], sn['b_k3'], pn['b_v3'], sn['b_v3'],
            pn['c_conv'], sn['c_conv'], pn['c_ssm'], sn['c_ssm'], pn['d_k'], sn['d_k'], pn['d_v'], sn['d_v'])
```

```python
import functools
import math

import jax
import jax.numpy as jnp
from jax import lax
from jax.experimental import pallas as pl
from jax.experimental.pallas import tpu as pltpu

F32 = jnp.float32
BF16 = jnp.bfloat16

D_MODEL = 1024
PAST_LEN = 16384
EPS = 1e-6
ROPE_THETA = 10000.0
HEAD_DIM = 64
SLAB = 512

A_HEADS = 4
A_DK = 128
A_DV = 128
A_CONV = 4
A_CHUNK = 64

B_HEADS = 8
B_WINDOWS = (128, 512, 2048)
B_DILATIONS = (1, 4, 16)
B_GROUPS = 3

C_HEADS = 8
C_HEADDIM = 64
C_DINNER = C_HEADS * C_HEADDIM
C_DSTATE = 128
C_NGROUPS = 2
C_CONV = 4
C_CHUNK = 128

D_HEADS = 8
D_KV_HEADS = 2
D_WINDOW = 128

NEG = -1e30
VMEM_LIMIT = 56 * 1024 * 1024


def _cparams(*sem):
    return pltpu.CompilerParams(dimension_semantics=sem, vmem_limit_bytes=VMEM_LIMIT)


def _rot_half(y):
    lane = lax.broadcasted_iota(jnp.int32, y.shape, 1)
    fwd = pltpu.roll(y, SLAB - HEAD_DIM // 2, 1)
    bwd = pltpu.roll(y, HEAD_DIM // 2, 1)
    return jnp.where(lane % HEAD_DIM < HEAD_DIM // 2, fwd, bwd)


def _proj_kernel(x_ref, nw_ref, w_ref, cos_ref, sin_ref, o_ref, xn_ref, *, rope_lo, rope_hi, rope_lanes_last):
    j = pl.program_id(1)

    @pl.when(j == 0)
    def _():
        x = x_ref[...]
        ms = jnp.mean(x * x, axis=-1, keepdims=True)
        xn_ref[...] = (x * lax.rsqrt(ms + EPS) * nw_ref[...]).astype(BF16)

    y = jnp.dot(xn_ref[...], w_ref[...], preferred_element_type=F32)
    is_rope = jnp.logical_and(j >= rope_lo, j < rope_hi)

    @pl.when(is_rope)
    def _():
        cos = jnp.tile(cos_ref[...], (1, SLAB // 128))
        sin = jnp.tile(sin_ref[...], (1, SLAB // 128))
        r = y * cos + _rot_half(y) * sin
        if rope_lanes_last < SLAB:
            lane = lax.broadcasted_iota(jnp.int32, y.shape, 1)
            keep = jnp.logical_or(j < rope_hi - 1, lane < rope_lanes_last)
            r = jnp.where(keep, r, y)
        o_ref[...] = r

    @pl.when(jnp.logical_not(is_rope))
    def _():
        o_ref[...] = y


def _norm_proj(x, norm_w, w_slabs, cos_t, sin_t, *, rope_lo, rope_hi, rope_lanes_last=SLAB):
    n = x.shape[0]
    s = w_slabs.shape[0]
    p = cos_t.shape[0]
    tm = min(1024, n, p)
    npt = p // tm
    kern = functools.partial(_proj_kernel, rope_lo=rope_lo, rope_hi=rope_hi, rope_lanes_last=rope_lanes_last)
    return pl.pallas_call(
        kern,
        grid=(n // tm, s),
        in_specs=[
            pl.BlockSpec((tm, D_MODEL), lambda i, j: (i, 0)),
            pl.BlockSpec((1, D_MODEL), lambda i, j: (0, 0)),
            pl.BlockSpec((None, D_MODEL, SLAB), lambda i, j: (j, 0, 0)),
            pl.BlockSpec((tm, 128), lambda i, j: (i % npt, 0)),
            pl.BlockSpec((tm, 128), lambda i, j: (i % npt, 0)),
        ],
        out_specs=pl.BlockSpec((None, tm, SLAB), lambda i, j: (j, i, 0)),
        out_shape=jax.ShapeDtypeStruct((s, n, SLAB), F32),
        scratch_shapes=[pltpu.VMEM((tm, D_MODEL), BF16)],
        compiler_params=_cparams("parallel", "arbitrary"),
        name="norm_proj",
    )(x, norm_w.reshape(1, D_MODEL), w_slabs, cos_t, sin_t)


def _rope_tables(pos):
    half = HEAD_DIM // 2
    inv = ROPE_THETA ** (-jnp.arange(half, dtype=F32) / half)
    ang = pos.astype(F32)[:, None] * inv[None, :]
    cos, sin = jnp.cos(ang), jnp.sin(ang)
    return (jnp.concatenate([cos, cos, cos, cos], axis=-1),
            jnp.concatenate([-sin, sin, -sin, sin], axis=-1))


def _pad_cols(w, width):
    return jnp.pad(w, ((0, 0), (0, width - w.shape[1])))


def _even_slabs(w_in):
    main = w_in[:, :2048].reshape(D_MODEL, 4, SLAB)
    gates = _pad_cols(w_in[:, 2048:2056], SLAB)[:, None, :]
    qkv = w_in[:, 2056:].reshape(D_MODEL, 9, SLAB)
    return jnp.moveaxis(jnp.concatenate([main, qkv, gates], axis=1), 1, 0).astype(BF16)


def _odd_slabs(w_in):
    main = w_in[:, :1536].reshape(D_MODEL, 3, SLAB)
    qd = w_in[:, 1544:2056][:, None, :]
    kvd = _pad_cols(jnp.concatenate([w_in[:, 2056:2312], w_in[:, 1536:1544]], axis=1), SLAB)[:, None, :]
    return jnp.moveaxis(jnp.concatenate([main, qd, kvd], axis=1), 1, 0).astype(BF16)


def _split3(x):
    hi = x.astype(BF16)
    r1 = x - hi.astype(F32)
    mid = r1.astype(BF16)
    lo = (r1 - mid.astype(F32)).astype(BF16)
    return hi, mid, lo


def _cumsum_rows(x, tri):
    hi, mid, lo = _split3(x)
    acc = jnp.dot(tri, lo, preferred_element_type=F32)
    acc = acc + jnp.dot(tri, mid, preferred_element_type=F32)
    return acc + jnp.dot(tri, hi, preferred_element_type=F32)


def _transpose_slab(s):
    l = s.shape[0]
    if l < 128:
        s = jnp.concatenate([s, jnp.zeros((128 - l, 128), s.dtype)], axis=0)
    return s.T


def _conv_silu(buf_ref, x, w, b):
    tb = x.shape[0]
    buf_ref[8:8 + tb, :] = x
    y = b + w[3:4] * x
    for j in range(3):
        y = y + w[j:j + 1] * buf_ref[5 + j:5 + j + tb, :]
    buf_ref[0:8, :] = buf_ref[tb:tb + 8, :]
    return y * jax.nn.sigmoid(y)


def _log_sigmoid(x):
    return jnp.minimum(x, 0.0) - jnp.log1p(jnp.exp(-jnp.abs(x)))


def _dot_t(a, b):
    return lax.dot_general(a, b, (((1,), (1,)), ((), ())), preferred_element_type=F32)


def _tdot(a, b):
    return lax.dot_general(a, b, (((0,), (0,)), ((), ())), preferred_element_type=F32)


def _mlstm_kernel(q_ref, k_ref, v_ref, og_ref, g_ref, cw_ref, cb_ref, bg_ref, an_ref,
                  conv0_ref, c0_ref, n0_ref, m0_ref,
                  ya_ref, c_ref, n_ref, m_ref, qbuf, kbuf, *, chunk):
    sb, tb, _ = q_ref.shape
    tblk = pl.program_id(1)
    lane = lax.broadcasted_iota(jnp.int32, (chunk, 128), 1)
    lane1 = lax.broadcasted_iota(jnp.int32, (1, 128), 1)
    row = lax.broadcasted_iota(jnp.int32, (chunk, chunk), 0)
    col = lax.broadcasted_iota(jnp.int32, (chunk, chunk), 1)
    causal = col <= row
    tri = causal.astype(BF16)
    scale = A_DK ** -0.5

    @pl.when(tblk == 0)
    def _():
        c_ref[...] = c0_ref[...]
        n_ref[...] = n0_ref[...]
        m_ref[...] = m0_ref[...]
        for si in range(sb):
            qbuf[si, 0:8, :] = conv0_ref[si, :, 0:SLAB]
            kbuf[si, 0:8, :] = conv0_ref[si, :, SLAB:2 * SLAB]

    for si in range(sb):
        qc = _conv_silu(qbuf.at[si], q_ref[si], cw_ref[:, 0:SLAB], cb_ref[:, 0:SLAB])
        kc = _conv_silu(kbuf.at[si], k_ref[si], cw_ref[:, SLAB:2 * SLAB], cb_ref[:, SLAB:2 * SLAB]) * scale
        m_vec = m_ref[si]
        for c in range(tb // chunk):
            r0 = c * chunk
            gates = g_ref[si, r0:r0 + chunk, 0:128] + bg_ref[...]
            bc = _cumsum_rows(_log_sigmoid(gates), tri)
            st = _transpose_slab(jnp.where(lane < A_HEADS, gates, bc))
            m_next = m_vec
            for h in range(A_HEADS):
                hs = slice(h * 128, (h + 1) * 128)
                qh = qc[r0:r0 + chunk, hs]
                kh = kc[r0:r0 + chunk, hs]
                vh = v_ref[si, r0:r0 + chunk, hs]
                qb, kb, vb = qh.astype(BF16), kh.astype(BF16), vh.astype(BF16)
                b_t = bc[:, A_HEADS + h:A_HEADS + h + 1]
                li_t = gates[:, h:h + 1]
                li_s = st[h:h + 1, 0:chunk]
                b_s = st[A_HEADS + h:A_HEADS + h + 1, 0:chunk]
                m_h = m_vec[:, h:h + 1]
                dlog = jnp.where(causal, b_t - b_s + li_s, NEG)
                inter = b_t + m_h
                m_t = jnp.maximum(inter, jnp.max(dlog, axis=1, keepdims=True))
                w_intra = jnp.exp(dlog - m_t)
                w_inter = jnp.exp(inter - m_t)
                s = _dot_t(qb, kb) * w_intra
                c_st = c_ref[si, h]
                n_st = n_ref[si, h:h + 1, :]
                num = jnp.dot(s.astype(BF16), vb, preferred_element_type=F32) \
                    + w_inter * jnp.dot(qb, c_st.astype(BF16), preferred_element_type=F32)
                den = jnp.sum(s, axis=1, keepdims=True) + w_inter * jnp.sum(qh * n_st, axis=1, keepdims=True)
                hh = num / jnp.maximum(jnp.abs(den), jnp.exp(-m_t))
                b_end = b_t[chunk - 1:chunk, :]
                g_col = b_end - b_t + li_t
                m_new = jnp.maximum(b_end + m_h, jnp.max(g_col, axis=0, keepdims=True))
                w_state = jnp.exp(g_col - m_new)
                decay = jnp.exp(b_end + m_h - m_new)
                kw = kh * w_state
                c_ref[si, h] = decay * c_st + _tdot(kw.astype(BF16), vb)
                n_ref[si, h:h + 1, :] = decay * n_st + jnp.sum(kw, axis=0, keepdims=True)
                m_next = jnp.where(lane1 == h, m_new, m_next)
                hn = hh * lax.rsqrt(jnp.mean(hh * hh, axis=-1, keepdims=True) + EPS) * an_ref[:, hs]
                ya_ref[si, r0:r0 + chunk, hs] = hn * jax.nn.sigmoid(og_ref[si, r0:r0 + chunk, hs])
            m_vec = m_next
        m_ref[si] = m_vec


def _mlstm(proj, nseq, t, conv_w, conv_b, b_gates, a_norm, conv0, c0, n0, m0):
    chunk = A_CHUNK if t % A_CHUNK == 0 else t
    if t > chunk:
        sb, tb = 1, min(t, 4 * chunk)
    else:
        sb, tb = min(nseq, 8), t
    p4 = proj.reshape(proj.shape[0], nseq, t, SLAB)
    slab = lambda s: pl.BlockSpec((None, sb, tb, SLAB), lambda i, j, s=s: (s, i, j, 0))
    full = lambda a: pl.BlockSpec(a.shape, lambda i, j: (0,) * a.ndim)
    st = lambda a: pl.BlockSpec((sb,) + a.shape[1:], lambda i, j: (i,) + (0,) * (a.ndim - 1))
    bg = _pad_cols(b_gates.reshape(1, 2 * A_HEADS), 128)
    conv0p = jnp.pad(conv0, ((0, 0), (8 - (A_CONV - 1), 0), (0, 0)))
    m0p = _pad_cols(m0, 128).reshape(nseq, 1, 128)
    cb = conv_b.reshape(1, -1)
    an = a_norm.reshape(1, -1)
    ya, c, n, m = pl.pallas_call(
        functools.partial(_mlstm_kernel, chunk=chunk),
        grid=(nseq // sb, t // tb),
        in_specs=[slab(0), slab(1), slab(2), slab(3), slab(13),
                  full(conv_w), full(cb), full(bg), full(an),
                  st(conv0p), st(c0), st(n0), st(m0p)],
        out_specs=[pl.BlockSpec((sb, tb, SLAB), lambda i, j: (i, j, 0)), st(c0), st(n0), st(m0p)],
        out_shape=[jax.ShapeDtypeStruct((nseq, t, SLAB), F32),
                   jax.ShapeDtypeStruct(c0.shape, F32), jax.ShapeDtypeStruct(n0.shape, F32),
                   jax.ShapeDtypeStruct(m0p.shape, F32)],
        scratch_shapes=[pltpu.VMEM((sb, 8 + tb, SLAB), F32), pltpu.VMEM((sb, 8 + tb, SLAB), F32)],
        compiler_params=_cparams("parallel", "arbitrary"),
        name="mlstm",
    )(p4, p4, p4, p4, p4, conv_w, cb, bg, an, conv0p, c0, n0, m0p)
    return ya.reshape(nseq * t, SLAB), c, n, m[:, 0, :A_HEADS]


QBLK = 128


def _band_head(q_m, k2, v_m, valid, sink):
    s = jnp.where(valid, _dot_t(q_m, k2), NEG)
    m = jnp.max(s, axis=1, keepdims=True)
    if sink is not None:
        m = jnp.maximum(m, sink)
    e = jnp.exp(s - m)
    l = jnp.sum(e, axis=1, keepdims=True)
    if sink is not None:
        l = l + jnp.exp(sink - m)
    acc = jnp.dot(e.astype(BF16), v_m, preferred_element_type=F32)
    return acc, m, l


def _band_masks(i, window):
    a = lax.broadcasted_iota(jnp.int32, (QBLK, 2 * QBLK), 0)
    c = lax.broadcasted_iota(jnp.int32, (QBLK, 2 * QBLK), 1)
    rel = a - c + QBLK
    first_col = jnp.where(i > 0, 0, QBLK)
    return jnp.logical_and(jnp.logical_and(rel >= 0, rel <= window), c >= first_col)


def _dilated_kernel(q_ref, kp_ref, kc_ref, vp_ref, vc_ref, o_ref, lse_ref):
    valid = _band_masks(pl.program_id(2), QBLK)
    lane = lax.broadcasted_iota(jnp.int32, (1, 128), 1)
    scale = HEAD_DIM ** -0.5
    for pr in range(B_HEADS // 2):
        ls = slice(pr * 128, (pr + 1) * 128)
        qp = q_ref[:, ls] * scale
        k2 = jnp.concatenate([kp_ref[:, ls], kc_ref[:, ls]], axis=0).astype(BF16)
        v2 = jnp.concatenate([vp_ref[:, ls], vc_ref[:, ls]], axis=0)
        o_pair = None
        lse_pair = None
        for p in range(2):
            half = (lane < HEAD_DIM) if p == 0 else (lane >= HEAD_DIM)
            acc, m, l = _band_head(jnp.where(half, qp, 0.0).astype(BF16), k2,
                                   jnp.where(half, v2, 0.0).astype(BF16), valid, None)
            o_h = acc / l
            lse_h = m + jnp.log(l)
            o_pair = o_h if o_pair is None else o_pair + o_h
            lse_pair = lse_h if lse_pair is None else jnp.where(half, lse_h, lse_pair)
        o_ref[:, ls] = o_pair
        lse_ref[:, ls] = jnp.broadcast_to(lse_pair, (QBLK, 128))


def _dilated_prompt(proj, b, t, g):
    d = B_DILATIONS[g]
    tc = t // d
    pv = proj.reshape(proj.shape[0], b, tc, d * SLAB)
    cur = lambda s: pl.BlockSpec((None, None, QBLK, SLAB), lambda bi, r, i, s=s: (s, bi, i, r))
    prev = lambda s: pl.BlockSpec((None, None, QBLK, SLAB), lambda bi, r, i, s=s: (s, bi, jnp.maximum(i - 1, 0), r))
    ospec = pl.BlockSpec((None, QBLK, SLAB), lambda bi, r, i: (bi, i, r))
    sq, sk, sv = 4 + g, 7 + g, 10 + g
    o, lse = pl.pallas_call(
        _dilated_kernel,
        grid=(b, d, tc // QBLK),
        in_specs=[cur(sq), prev(sk), cur(sk), prev(sv), cur(sv)],
        out_specs=[ospec, ospec],
        out_shape=[jax.ShapeDtypeStruct((b, tc, d * SLAB), F32)] * 2,
        compiler_params=_cparams("parallel", "parallel", "arbitrary"),
        name=f"dilated_g{g}",
    )(pv, pv, pv, pv, pv)
    return o.reshape(b * t, SLAB), lse.reshape(b * t, SLAB)


def _merge_kernel(o1, l1, o2, l2, o3, l3, y_ref):
    m = jnp.maximum(jnp.maximum(l1[...], l2[...]), l3[...])
    w1, w2, w3 = jnp.exp(l1[...] - m), jnp.exp(l2[...] - m), jnp.exp(l3[...] - m)
    y_ref[...] = (w1 * o1[...] + w2 * o2[...] + w3 * o3[...]) / (w1 + w2 + w3)


def _merge_groups(parts):
    n = parts[0].shape[0]
    tm = min(1024, n)
    spec = pl.BlockSpec((tm, SLAB), lambda i: (i, 0))
    return pl.pallas_call(
        _merge_kernel, grid=(n // tm,), in_specs=[spec] * 6, out_specs=spec,
        out_shape=jax.ShapeDtypeStruct((n, SLAB), F32),
        compiler_params=_cparams("parallel"), name="merge_groups",
    )(*parts)


def _swa_kernel(q_ref, kp_ref, kc_ref, vp_ref, vc_ref, sink_ref, o_ref):
    valid = _band_masks(pl.program_id(1), D_WINDOW - 1)
    lane = lax.broadcasted_iota(jnp.int32, (1, 128), 1)
    scale = HEAD_DIM ** -0.5
    k2 = jnp.concatenate([kp_ref[...], kc_ref[...]], axis=0).astype(BF16)
    v2 = jnp.concatenate([vp_ref[...], vc_ref[...]], axis=0)
    v2r = pltpu.roll(v2, HEAD_DIM, 1)
    rep = D_HEADS // D_KV_HEADS
    for pr in range(D_HEADS // 2):
        ls = slice(pr * 128, (pr + 1) * 128)
        qp = q_ref[:, ls] * scale
        qpr = pltpu.roll(qp, HEAD_DIM, 1)
        o_pair = None
        for p in range(2):
            h = 2 * pr + p
            g = h // rep
            half_g = (lane < HEAD_DIM) if g == 0 else (lane >= HEAD_DIM)
            half_p = (lane < HEAD_DIM) if p == 0 else (lane >= HEAD_DIM)
            q_m = jnp.where(half_g, qp if p == g else qpr, 0.0).astype(BF16)
            v_m = jnp.where(half_p, v2 if p == g else v2r, 0.0).astype(BF16)
            acc, m, l = _band_head(q_m, k2, v_m, valid, sink_ref[:, h:h + 1])
            o_h = acc / l
            o_pair = o_h if o_pair is None else o_pair + o_h
        o_ref[:, ls] = o_pair


def _swa_prompt(proj, b, t, sinks):
    pv = proj.reshape(proj.shape[0], b, t, SLAB)
    qspec = pl.BlockSpec((None, None, QBLK, SLAB), lambda bi, i: (3, bi, i, 0))
    kv = lambda col, back: pl.BlockSpec((None, None, QBLK, 128),
                                        lambda bi, i: (4, bi, jnp.maximum(i - back, 0), col))
    o = pl.pallas_call(
        _swa_kernel,
        grid=(b, t // QBLK),
        in_specs=[qspec, kv(0, 1), kv(0, 0), kv(1, 1), kv(1, 0), pl.BlockSpec((1, 128), lambda bi, i: (0, 0))],
        out_specs=pl.BlockSpec((None, QBLK, SLAB), lambda bi, i: (bi, i, 0)),
        out_shape=jax.ShapeDtypeStruct((b, t, SLAB), F32),
        compiler_params=_cparams("parallel", "arbitrary"),
        name="swa_prompt",
    )(pv, pv, pv, pv, pv, _pad_cols(sinks.reshape(1, D_HEADS), 128))
    return o.reshape(b * t, SLAB)


def _softplus(x):
    return jnp.maximum(x, 0.0) + jnp.log1p(jnp.exp(-jnp.abs(x)))


def _ssd_kernel(z_ref, x_ref, bc_ref, dt_ref, cw_ref, cb_ref, dtb_ref, alog_ref, dl_ref, cn_ref,
                conv0_ref, h0_ref, y_ref, h_ref, xbuf, bcbuf, *, chunk):
    sb, tb, _ = x_ref.shape
    tblk = pl.program_id(1)
    lane1 = lax.broadcasted_iota(jnp.int32, (1, 128), 1)
    low = lane1 < C_HEADDIM
    row = lax.broadcasted_iota(jnp.int32, (chunk, chunk), 0)
    col = lax.broadcasted_iota(jnp.int32, (chunk, chunk), 1)
    causal = col <= row
    tri = causal.astype(BF16)
    srow_low = lax.broadcasted_iota(jnp.int32, (128, 1), 0) < C_HEADDIM
    hpg = C_HEADS // C_NGROUPS
    a_neg = -jnp.exp(alog_ref[...])

    @pl.when(tblk == 0)
    def _():
        h_ref[...] = h0_ref[...]
        for si in range(sb):
            xbuf[si, 0:8, :] = conv0_ref[si, :, 0:SLAB]
            bcbuf[si, 0:8, :] = conv0_ref[si, :, SLAB:2 * SLAB]

    for si in range(sb):
        xc = _conv_silu(xbuf.at[si], x_ref[si], cw_ref[:, 0:SLAB], cb_ref[:, 0:SLAB])
        bcc = _conv_silu(bcbuf.at[si], bc_ref[si], cw_ref[:, SLAB:2 * SLAB], cb_ref[:, SLAB:2 * SLAB])
        for c in range(tb // chunk):
            rs = slice(c * chunk, (c + 1) * chunk)
            dt = _softplus(dt_ref[si, rs, :] + dtb_ref[...])
            cum = _cumsum_rows(dt * a_neg, tri)
            cum_t_all = _transpose_slab(cum)
            dt_t_all = _transpose_slab(dt)
            cum_end = cum[chunk - 1:chunk, :]
            for g in range(C_NGROUPS):
                bg = bcc[rs, g * 128:(g + 1) * 128].astype(BF16)
                cg = bcc[rs, 256 + g * 128:256 + (g + 1) * 128].astype(BF16)
                cb = _dot_t(cg, bg)
                y_pairs = []
                for pp in range(hpg // 2):
                    pr = g * (hpg // 2) + pp
                    ls = slice(pr * 128, (pr + 1) * 128)
                    x_pair = xc[rs, ls]
                    hs = h_ref[si, pr]
                    y_pair = None
                    e_col, w_end, dec = [], [], []
                    for p in range(2):
                        h = 2 * pr + p
                        cum_t = cum[:, h:h + 1]
                        seg = jnp.where(causal, cum_t - cum_t_all[h:h + 1, 0:chunk], NEG)
                        w = cb * jnp.exp(seg) * dt_t_all[h:h + 1, 0:chunk]
                        half = low if p == 0 else jnp.logical_not(low)
                        yh = jnp.dot(w.astype(BF16), jnp.where(half, x_pair, 0.0).astype(BF16),
                                     preferred_element_type=F32)
                        y_pair = yh if y_pair is None else y_pair + yh
                        e_col.append(jnp.exp(cum_t))
                        w_end.append(jnp.exp(cum_end[:, h:h + 1] - cum_t) * dt[:, h:h + 1])
                        dec.append(jnp.exp(cum_end[:, h:h + 1]))
                    inter = _dot_t(cg, hs.astype(BF16)) * jnp.where(low, e_col[0], e_col[1])
                    y_pair = y_pair + inter + dl_ref[:, ls] * x_pair
                    xw = x_pair * jnp.where(low, w_end[0], w_end[1])
                    h_ref[si, pr] = jnp.where(srow_low, dec[0], dec[1]) * hs + _tdot(xw.astype(BF16), bg)
                    zz = z_ref[si, rs, ls]
                    y_pairs.append(y_pair * (zz * jax.nn.sigmoid(zz)))
                ms = sum(jnp.sum(yp * yp, axis=-1, keepdims=True) for yp in y_pairs) / (hpg * C_HEADDIM)
                inv = lax.rsqrt(ms + EPS)
                for pp, yp in enumerate(y_pairs):
                    ls = slice((g * (hpg // 2) + pp) * 128, (g * (hpg // 2) + pp + 1) * 128)
                    y_ref[si, rs, ls] = yp * inv * cn_ref[:, ls]


def _ssd(proj, nseq, t, conv_w, conv_b, dt_bias, a_log, d_skip, c_norm, conv0, h0):
    chunk = C_CHUNK if t % C_CHUNK == 0 else t
    if t > chunk:
        sb, tb = 1, min(t, 2 * chunk)
    else:
        sb, tb = min(nseq, 8), t
    p4 = proj.reshape(proj.shape[0], nseq, t, SLAB)
    slab = lambda s: pl.BlockSpec((None, sb, tb, SLAB), lambda i, j, s=s: (s, i, j, 0))
    full = lambda a: pl.BlockSpec(a.shape, lambda i, j: (0,) * a.ndim)
    st = lambda a: pl.BlockSpec((sb,) + a.shape[1:], lambda i, j: (i,) + (0,) * (a.ndim - 1))
    conv0p = jnp.pad(conv0, ((0, 0), (8 - (C_CONV - 1), 0), (0, 0)))
    hp = h0.reshape(nseq, C_HEADS // 2, 128, C_DSTATE)
    cb = conv_b.reshape(1, -1)
    dtb = _pad_cols(dt_bias.reshape(1, C_HEADS), 128)
    alog = _pad_cols(a_log.reshape(1, C_HEADS), 128)
    dl = jnp.repeat(d_skip, C_HEADDIM).reshape(1, C_DINNER)
    cn = c_norm.reshape(1, C_DINNER)
    y, h = pl.pallas_call(
        functools.partial(_ssd_kernel, chunk=chunk),
        grid=(nseq // sb, t // tb),
        in_specs=[slab(0), slab(1), slab(2),
                  pl.BlockSpec((None, sb, tb, 128), lambda i, j: (4, i, j, 2)),
                  full(conv_w), full(cb), full(dtb), full(alog), full(dl), full(cn), st(conv0p), st(hp)],
        out_specs=[pl.BlockSpec((sb, tb, SLAB), lambda i, j: (i, j, 0)), st(hp)],
        out_shape=[jax.ShapeDtypeStruct((nseq, t, SLAB), F32), jax.ShapeDtypeStruct(hp.shape, F32)],
        scratch_shapes=[pltpu.VMEM((sb, 8 + tb, SLAB), F32), pltpu.VMEM((sb, 8 + tb, SLAB), F32)],
        compiler_params=_cparams("parallel", "arbitrary"),
        name="ssd",
    )(p4, p4, p4, p4, conv_w, cb, dtb, alog, dl, cn, conv0p, hp)
    return y.reshape(nseq * t, SLAB), h.reshape(nseq, C_HEADS, C_HEADDIM, C_DSTATE)


FFN_CHUNK = 256


def _post_kernel(x_ref, ya_ref, yb_ref, wo_ref, fn_ref, wg_ref, wu_ref, wd_ref, on_ref, o_ref, act_ref, *, final):
    ymix = jnp.concatenate([ya_ref[...], yb_ref[...]], axis=-1).astype(BF16)
    x1 = x_ref[...] + jnp.dot(ymix, wo_ref[...], preferred_element_type=F32)
    ms = jnp.mean(x1 * x1, axis=-1, keepdims=True)
    hn = (x1 * lax.rsqrt(ms + EPS) * fn_ref[...]).astype(BF16)
    hidden = wg_ref.shape[1]
    for c in range(hidden // FFN_CHUNK):
        cs = slice(c * FFN_CHUNK, (c + 1) * FFN_CHUNK)
        gate = jnp.dot(hn, wg_ref[:, cs], preferred_element_type=F32)
        up = jnp.dot(hn, wu_ref[:, cs], preferred_element_type=F32)
        act_ref[:, cs] = (gate * jax.nn.sigmoid(gate) * up).astype(BF16)
    x2 = x1 + jnp.dot(act_ref[...], wd_ref[...], preferred_element_type=F32)
    if final:
        ms2 = jnp.mean(x2 * x2, axis=-1, keepdims=True)
        x2 = x2 * lax.rsqrt(ms2 + EPS) * on_ref[...]
    o_ref[...] = x2


def _post(x, ya, yb, w_out, ffn_norm, w_gate, w_up, w_down, out_norm, final):
    n = x.shape[0]
    tm = min(512, n)
    hidden = w_gate.shape[1]
    row = lambda w: pl.BlockSpec((tm, w), lambda i: (i, 0))
    res = lambda a: pl.BlockSpec(a.shape, lambda i: (0, 0), pipeline_mode=pl.Buffered(1))
    fn = ffn_norm.reshape(1, D_MODEL)
    on = out_norm.reshape(1, D_MODEL)
    wo, wg, wu, wd = (w.astype(BF16) for w in (w_out, w_gate, w_up, w_down))
    return pl.pallas_call(
        functools.partial(_post_kernel, final=final),
        grid=(n // tm,),
        in_specs=[row(D_MODEL), row(SLAB), row(SLAB), res(wo), res(fn), res(wg), res(wu), res(wd), res(on)],
        out_specs=row(D_MODEL),
        out_shape=jax.ShapeDtypeStruct((n, D_MODEL), F32),
        scratch_shapes=[pltpu.VMEM((tm, hidden), BF16)],
        compiler_params=_cparams("parallel"),
        name="post_ffn",
    )(x, ya, yb, wo, fn, wg, wu, wd, on)


NEWPAD = 128


def _ext_mask(rows, r, ts, window, step):
    t = lax.broadcasted_iota(jnp.int32, (rows, r + NEWPAD), 0) % ts
    c = lax.broadcasted_iota(jnp.int32, (rows, r + NEWPAD), 1)
    rel = r + t - c
    ok = jnp.logical_and(jnp.logical_and(rel >= 0, rel <= window), c < r + ts)
    return jnp.logical_and(ok, (rel & (step - 1)) == 0)


def _shift_cache(dst_ref, src_ref, new, si, r, ts):
    dst_ref[si, 0:r - ts, :] = src_ref[si, ts:r, :]
    dst_ref[si, r - ts:r, :] = new


def _pad_new(x):
    return jnp.concatenate([x, jnp.zeros((NEWPAD - x.shape[0], x.shape[1]), x.dtype)], axis=0)


def _dil_sample_kernel(q_ref, kn_ref, vn_ref, kc_ref, kx_ref, vc_ref, vx_ref,
                       o_ref, lse_ref, ko_ref, vo_ref, m_sc, l_sc, acc_sc, *, window, step, r_total):
    j = pl.program_id(1)
    nj = pl.num_programs(1)
    ts = q_ref.shape[0]
    rc = kc_ref.shape[0]
    nq = ts * B_HEADS
    scale = HEAD_DIM ** -0.5
    qm = (q_ref[...] * scale).reshape(nq, HEAD_DIM).astype(BF16)

    @pl.when(j == 0)
    def _():
        m_sc[...] = jnp.full(m_sc.shape, NEG, F32)
        l_sc[...] = jnp.zeros(l_sc.shape, F32)
        acc_sc[...] = jnp.zeros(acc_sc.shape, F32)

    def accumulate(k2, v2, valid):
        s = jnp.where(valid, _dot_t(qm, k2), NEG)
        m_old = m_sc[...]
        m_new = jnp.maximum(m_old, jnp.max(s, axis=1, keepdims=True))
        alpha = jnp.exp(m_old - m_new)
        e = jnp.where(valid, jnp.exp(s - m_new), 0.0)
        l_sc[...] = alpha * l_sc[...] + jnp.sum(e, axis=1, keepdims=True)
        acc_sc[...] = alpha * acc_sc[...] + jnp.dot(e.astype(BF16), v2, preferred_element_type=F32)
        m_sc[...] = m_new

    def mask(ncols, first_row):
        row = lax.broadcasted_iota(jnp.int32, (nq, ncols), 0)
        col = lax.broadcasted_iota(jnp.int32, (nq, ncols), 1)
        rel = r_total + row // B_HEADS - (first_row + col // B_HEADS)
        ok = jnp.logical_and(row % B_HEADS == col % B_HEADS, (rel & (step - 1)) == 0)
        return jnp.logical_and(ok, jnp.logical_and(rel >= 0, rel <= window))

    kc = kc_ref[...]
    vc = vc_ref[...]
    accumulate(kc.reshape(rc * B_HEADS, HEAD_DIM).astype(BF16), vc.reshape(rc * B_HEADS, HEAD_DIM).astype(BF16),
               mask(rc * B_HEADS, j * rc))
    ko_ref[0:rc - ts] = kc[ts:rc]
    vo_ref[0:rc - ts] = vc[ts:rc]

    @pl.when(j < nj - 1)
    def _():
        ko_ref[rc - ts:rc] = kx_ref[...]
        vo_ref[rc - ts:rc] = vx_ref[...]

    @pl.when(j == nj - 1)
    def _():
        kn, vn = kn_ref[...], vn_ref[...]
        ko_ref[rc - ts:rc] = kn
        vo_ref[rc - ts:rc] = vn
        pad = jnp.zeros((128 - nq, HEAD_DIM), BF16)
        k2 = jnp.concatenate([kn.reshape(nq, HEAD_DIM).astype(BF16), pad], axis=0)
        v2 = jnp.concatenate([vn.reshape(nq, HEAD_DIM).astype(BF16), pad], axis=0)
        col = lax.broadcasted_iota(jnp.int32, (nq, 128), 1)
        accumulate(k2, v2, jnp.logical_and(mask(128, r_total), col < nq))
        l = l_sc[...]
        o_ref[...] = (acc_sc[...] / l).reshape(ts, B_HEADS, HEAD_DIM)
        lse = m_sc[...] + jnp.log(l)
        lse_ref[...] = jnp.broadcast_to(lse, (nq, HEAD_DIM)).reshape(ts, B_HEADS, HEAD_DIM)


def _heads_major(slab_rows, nseq, ts):
    return slab_rows.reshape(nseq, ts, B_HEADS, HEAD_DIM)


def _dilated_sample(proj, nseq, ts, g, k_cache, v_cache):
    r = k_cache.shape[1]
    rc = min(r, 512)
    assert ts == 8 and r % rc == 0 and rc % ts == 0
    q4, kn4, vn4 = (_heads_major(proj[s + g], nseq, ts) for s in (4, 7, 10))
    new = pl.BlockSpec((None, ts, B_HEADS, HEAD_DIM), lambda i, j: (i, 0, 0, 0))
    chunk = pl.BlockSpec((None, rc, B_HEADS, HEAD_DIM), lambda i, j: (i, j, 0, 0))
    nxt = pl.BlockSpec((None, ts, B_HEADS, HEAD_DIM),
                       lambda i, j: (i, jnp.minimum((j + 1) * (rc // ts), r // ts - 1), 0, 0))
    nq = ts * B_HEADS
    o, lse, kn, vn = pl.pallas_call(
        functools.partial(_dil_sample_kernel, window=B_WINDOWS[g], step=B_DILATIONS[g], r_total=r),
        grid=(nseq, r // rc),
        in_specs=[new, new, new, chunk, nxt, chunk, nxt],
        out_specs=[new, new, chunk, chunk],
        out_shape=[jax.ShapeDtypeStruct(q4.shape, F32)] * 2 + [jax.ShapeDtypeStruct(k_cache.shape, F32)] * 2,
        scratch_shapes=[pltpu.VMEM((nq, 1), F32), pltpu.VMEM((nq, 1), F32), pltpu.VMEM((nq, HEAD_DIM), F32)],
        compiler_params=_cparams("parallel", "arbitrary"),
        name=f"dilated_sample_g{g}",
    )(q4, kn4, vn4, k_cache, k_cache, v_cache, v_cache)
    return o.reshape(nseq * ts, SLAB), lse.reshape(nseq * ts, SLAB), kn, vn


def _swa_sample_kernel(q_ref, k_ref, v_ref, kc_ref, vc_ref, sink_ref, o_ref, kn_ref, vn_ref):
    sb, ts, _ = q_ref.shape
    r = kc_ref.shape[1]
    rows = D_HEADS * ts
    valid = _ext_mask(rows, r, ts, D_WINDOW - 1, 1)
    lane = lax.broadcasted_iota(jnp.int32, (1, 128), 1)
    low = lane < HEAD_DIM
    rep = D_HEADS // D_KV_HEADS
    scale = HEAD_DIM ** -0.5
    sink = sink_ref[:, 0:1]
    for si in range(sb):
        kn, vn = k_ref[si], v_ref[si]
        pieces = []
        for h in range(D_HEADS):
            g, p = h // rep, h % 2
            base = q_ref[si, :, (h // 2) * 128:(h // 2 + 1) * 128] * scale
            if p != g:
                base = pltpu.roll(base, HEAD_DIM, 1)
            pieces.append(jnp.where(low if g == 0 else jnp.logical_not(low), base, 0.0))
        qrows = jnp.concatenate(pieces, axis=0).astype(BF16)
        kext = jnp.concatenate([kc_ref[si], _pad_new(kn)], axis=0).astype(BF16)
        vext = jnp.concatenate([vc_ref[si], _pad_new(vn)], axis=0).astype(BF16)
        s = jnp.where(valid, _dot_t(qrows, kext), NEG)
        m = jnp.maximum(jnp.max(s, axis=1, keepdims=True), sink)
        e = jnp.exp(s - m)
        l = jnp.sum(e, axis=1, keepdims=True) + jnp.exp(sink - m)
        o = jnp.dot(e.astype(BF16), vext, preferred_element_type=F32) / l
        pairs = []
        for pr in range(D_HEADS // 2):
            acc = None
            for p in range(2):
                h = 2 * pr + p
                g = h // rep
                piece = o[h * ts:(h + 1) * ts]
                if p != g:
                    piece = pltpu.roll(piece, HEAD_DIM, 1)
                piece = jnp.where(low if p == 0 else jnp.logical_not(low), piece, 0.0)
                acc = piece if acc is None else acc + piece
            pairs.append(acc)
        o_ref[si] = jnp.concatenate(pairs, axis=1)
        _shift_cache(kn_ref, kc_ref, kn, si, r, ts)
        _shift_cache(vn_ref, vc_ref, vn, si, r, ts)


def _swa_sample(proj, nseq, ts, sinks, k_cache, v_cache):
    r = k_cache.shape[1]
    sb = min(nseq, 8)
    kvw = D_KV_HEADS * HEAD_DIM
    p4 = proj.reshape(proj.shape[0], nseq, ts, SLAB)
    kc = k_cache.reshape(nseq, r, kvw)
    vc = v_cache.reshape(nseq, r, kvw)
    cspec = pl.BlockSpec((sb, r, kvw), lambda i: (i, 0, 0))
    new = lambda col: pl.BlockSpec((None, sb, ts, kvw), lambda i: (4, i, 0, col))
    sink_rows = jnp.broadcast_to(jnp.repeat(sinks, ts)[:, None], (D_HEADS * ts, 128))
    o, kn, vn = pl.pallas_call(
        _swa_sample_kernel,
        grid=(nseq // sb,),
        in_specs=[pl.BlockSpec((None, sb, ts, SLAB), lambda i: (3, i, 0, 0)), new(0), new(1), cspec, cspec,
                  pl.BlockSpec(sink_rows.shape, lambda i: (0, 0))],
        out_specs=[pl.BlockSpec((sb, ts, SLAB), lambda i: (i, 0, 0)), cspec, cspec],
        out_shape=[jax.ShapeDtypeStruct((nseq, ts, SLAB), F32)] + [jax.ShapeDtypeStruct(kc.shape, F32)] * 2,
        compiler_params=_cparams("parallel"),
        name="swa_sample",
    )(p4, p4, p4, kc, vc, sink_rows)
    return o.reshape(nseq * ts, SLAB), kn.reshape(k_cache.shape), vn.reshape(v_cache.shape)


def kernel(x_prompt, x_sample, state_a_C, state_a_n, state_a_m, state_a_conv, cache_b_k1, cache_b_v1, cache_b_k2, cache_b_v2, cache_b_k3, cache_b_v3, state_c_conv, state_c_ssm, cache_d_k, cache_d_v, e_norm_mix, e_w_in, e_b_gates, e_a_conv_w, e_a_conv_b, e_a_norm, e_w_out, o_norm_mix, o_w_in, o_c_conv_w, o_c_conv_b, o_c_dt_bias, o_c_A_log, o_c_D, o_c_norm, o_d_sinks, o_w_out, ffn_norm, w_gate, w_up, w_down, final_norm):
    depth = ffn_norm.shape[0]
    even_w = [_even_slabs(e_w_in[j]) for j in range((depth + 1) // 2)]
    odd_w = [_odd_slabs(o_w_in[j]) for j in range(depth // 2)]

    def trunk(x, pos, fresh, even_st, odd_st):
        nseq, t, _ = x.shape
        assert t >= A_CONV - 1 and t >= C_CONV - 1
        cos_t, sin_t = _rope_tables(pos)
        x2 = x.reshape(nseq * t, D_MODEL)
        new_even, new_odd = [], []

        def tail(slab_cols, width, keep):
            return slab_cols.reshape(nseq, t, width)[:, t - keep:]

        for layer in range(depth):
            j = layer // 2
            last = layer == depth - 1
            if layer % 2 == 0:
                pe = _norm_proj(x2, e_norm_mix[j], even_w[j], cos_t, sin_t, rope_lo=4, rope_hi=10)
                st = {k: v[j] for k, v in even_st.items()}
                ya, c_new, n_new, m_new = _mlstm(pe, nseq, t, e_a_conv_w[j], e_a_conv_b[j], e_b_gates[j],
                                                 e_a_norm[j], st['a_conv'], st['a_C'], st['a_n'], st['a_m'])
                ns = {'a_C': c_new, 'a_n': n_new, 'a_m': m_new,
                      'a_conv': tail(jnp.concatenate([pe[0], pe[1]], axis=-1), 2 * SLAB, A_CONV - 1)}
                parts = []
                for g in range(B_GROUPS):
                    kn, vn = f'b_k{g + 1}', f'b_v{g + 1}'
                    if fresh:
                        o_g, lse_g = _dilated_prompt(pe, nseq, t, g)
                        keep = min(B_WINDOWS[g], t)
                        ns[kn] = tail(pe[7 + g], SLAB, keep).reshape(nseq, keep, B_HEADS, HEAD_DIM)
                        ns[vn] = tail(pe[10 + g], SLAB, keep).reshape(nseq, keep, B_HEADS, HEAD_DIM)
                    else:
                        o_g, lse_g, ns[kn], ns[vn] = _dilated_sample(pe, nseq, t, g, st[kn], st[vn])
                    parts += [o_g, lse_g]
                yb = _merge_groups(parts)
                x2 = _post(x2, ya, yb, e_w_out[j], ffn_norm[layer], w_gate[layer], w_up[layer], w_down[layer],
                           final_norm, last)
                new_even.append(ns)
            else:
                po = _norm_proj(x2, o_norm_mix[j], odd_w[j], cos_t, sin_t, rope_lo=3, rope_hi=5,
                                rope_lanes_last=D_KV_HEADS * HEAD_DIM)
                st = {k: v[j] for k, v in odd_st.items()}
                yc, ssm_new = _ssd(po, nseq, t, o_c_conv_w[j], o_c_conv_b[j], o_c_dt_bias[j], o_c_A_log[j],
                                   o_c_D[j], o_c_norm[j], st['c_conv'], st['c_ssm'])
                ns = {'c_conv': tail(jnp.concatenate([po[1], po[2]], axis=-1), 2 * SLAB, C_CONV - 1),
                      'c_ssm': ssm_new}
                kvw = D_KV_HEADS * HEAD_DIM
                if fresh:
                    yd = _swa_prompt(po, nseq, t, o_d_sinks[j])
                    keep = min(D_WINDOW, t)
                    ns['d_k'] = tail(po[4][:, 0:kvw], kvw, keep).reshape(nseq, keep, D_KV_HEADS, HEAD_DIM)
                    ns['d_v'] = tail(po[4][:, kvw:2 * kvw], kvw, keep).reshape(nseq, keep, D_KV_HEADS, HEAD_DIM)
                else:
                    yd, ns['d_k'], ns['d_v'] = _swa_sample(po, nseq, t, o_d_sinks[j], st['d_k'], st['d_v'])
                x2 = _post(x2, yc, yd, o_w_out[j], ffn_norm[layer], w_gate[layer], w_up[layer], w_down[layer],
                           final_norm, last)
                new_odd.append(ns)
        out = {k: jnp.stack([ns[k] for ns in new_even]) for k in new_even[0]}
        out.update({k: jnp.stack([ns[k] for ns in new_odd]) for k in new_odd[0]})
        return x2.reshape(nseq, t, D_MODEL), out

    bp, tp, _ = x_prompt.shape
    bs, ts, _ = x_sample.shape
    n_even, n_odd = (depth + 1) // 2, depth // 2
    zeros = lambda *shape: jnp.zeros(shape, F32)
    even_p = {'a_C': zeros(n_even, bp, A_HEADS, A_DK, A_DV), 'a_n': zeros(n_even, bp, A_HEADS, A_DK),
              'a_m': zeros(n_even, bp, A_HEADS), 'a_conv': zeros(n_even, bp, A_CONV - 1, 2 * SLAB)}
    odd_p = {'c_conv': zeros(n_odd, bp, C_CONV - 1, 2 * SLAB),
             'c_ssm': zeros(n_odd, bp, C_HEADS, C_HEADDIM, C_DSTATE)}
    y_p, pn = trunk(x_prompt, jnp.arange(tp, dtype=jnp.int32), True, even_p, odd_p)

    even_s = {'a_C': state_a_C, 'a_n': state_a_n, 'a_m': state_a_m, 'a_conv': state_a_conv,
              'b_k1': cache_b_k1, 'b_v1': cache_b_v1, 'b_k2': cache_b_k2, 'b_v2': cache_b_v2,
              'b_k3': cache_b_k3, 'b_v3': cache_b_v3}
    odd_s = {'c_conv': state_c_conv, 'c_ssm': state_c_ssm, 'd_k': cache_d_k, 'd_v': cache_d_v}
    pos_s = jnp.tile(PAST_LEN + jnp.arange(ts, dtype=jnp.int32), bs)
    y_s, sn = trunk(x_sample, pos_s, False, even_s, odd_s)

    keys = ('a_C', 'a_n', 'a_m', 'a_conv', 'b_k1', 'b_v1', 'b_k2', 'b_v2', 'b_k3', 'b_v3',
            'c_conv', 'c_ssm', 'd_k', 'd_v')
    states = []
    for k in keys:
        states += [pn[k], sn[k]]
    return (y_p, y_s, *states)
```

```python
import functools
import math

import jax
import jax.numpy as jnp
from jax import lax
from jax.experimental import pallas as pl
from jax.experimental.pallas import tpu as pltpu

F32 = jnp.float32
BF16 = jnp.bfloat16

D_MODEL = 1024
PAST_LEN = 16384
EPS = 1e-6
ROPE_THETA = 10000.0
HEAD_DIM = 64
SLAB = 512

A_HEADS = 4
A_DK = 128
A_DV = 128
A_CONV = 4
A_CHUNK = 64

B_HEADS = 8
B_WINDOWS = (128, 512, 2048)
B_DILATIONS = (1, 4, 16)
B_GROUPS = 3

C_HEADS = 8
C_HEADDIM = 64
C_DINNER = C_HEADS * C_HEADDIM
C_DSTATE = 128
C_NGROUPS = 2
C_CONV = 4
C_CHUNK = 128

D_HEADS = 8
D_KV_HEADS = 2
D_WINDOW = 128

NEG = -1e30
VMEM_LIMIT = 56 * 1024 * 1024


def _cparams(*sem):
    return pltpu.CompilerParams(dimension_semantics=sem, vmem_limit_bytes=VMEM_LIMIT)


def _rot_half(y):
    lane = lax.broadcasted_iota(jnp.int32, y.shape, 1)
    fwd = pltpu.roll(y, SLAB - HEAD_DIM // 2, 1)
    bwd = pltpu.roll(y, HEAD_DIM // 2, 1)
    return jnp.where(lane % HEAD_DIM < HEAD_DIM // 2, fwd, bwd)


PERM_TILE = 2048


def _proj_kernel(x_ref, nw_ref, w_ref, cos_ref, sin_ref, o_ref, xn_ref, *ybuf, kinds):
    j = pl.program_id(1)
    tm = o_ref.shape[0]

    @pl.when(j == 0)
    def _():
        x = x_ref[...]
        ms = jnp.mean(x * x, axis=-1, keepdims=True)
        xn_ref[...] = (x * lax.rsqrt(ms + EPS) * nw_ref[...]).astype(BF16)

    y = jnp.dot(xn_ref[...], w_ref[...], preferred_element_type=F32)

    def emit(rope_lanes, dil):
        val = y
        if rope_lanes:
            cos = jnp.tile(cos_ref[...], (1, SLAB // 128))
            sin = jnp.tile(sin_ref[...], (1, SLAB // 128))
            val = y * cos + _rot_half(y) * sin
            if rope_lanes < SLAB:
                lane = lax.broadcasted_iota(jnp.int32, y.shape, 1)
                val = jnp.where(lane < rope_lanes, val, y)
        if dil == 1:
            o_ref[...] = val
        else:
            seg = tm // dil
            for cb in range(SLAB // 128):
                ybuf[0][cb] = val[:, cb * 128:(cb + 1) * 128]
            for r in range(dil):
                for cb in range(SLAB // 128):
                    o_ref[r * seg:(r + 1) * seg, cb * 128:(cb + 1) * 128] = ybuf[0][cb, pl.ds(r, seg, stride=dil), :]

    for kind in sorted(set(kinds)):
        member = functools.reduce(jnp.logical_or, [j == s for s, k in enumerate(kinds) if k == kind])
        pl.when(member)(functools.partial(emit, *kind))


def _norm_proj(x, norm_w, w_slabs, cos_t, sin_t, kinds):
    n = x.shape[0]
    s = w_slabs.shape[0]
    p = cos_t.shape[0]
    perm = any(d > 1 for _, d in kinds)
    tm = PERM_TILE if perm else min(1024, n, p)
    assert n % tm == 0 and p % tm == 0 and len(kinds) == s
    npt = p // tm
    scratch = [pltpu.VMEM((tm, D_MODEL), BF16)] + ([pltpu.VMEM((SLAB // 128, tm, 128), F32)] if perm else [])
    return pl.pallas_call(
        functools.partial(_proj_kernel, kinds=tuple(kinds)),
        grid=(n // tm, s),
        in_specs=[
            pl.BlockSpec((tm, D_MODEL), lambda i, j: (i, 0), pipeline_mode=pl.Buffered(1)),
            pl.BlockSpec((1, D_MODEL), lambda i, j: (0, 0)),
            pl.BlockSpec((None, D_MODEL, SLAB), lambda i, j: (j, 0, 0)),
            pl.BlockSpec((tm, 128), lambda i, j: (i % npt, 0)),
            pl.BlockSpec((tm, 128), lambda i, j: (i % npt, 0)),
        ],
        out_specs=pl.BlockSpec((None, tm, SLAB), lambda i, j: (j, i, 0)),
        out_shape=jax.ShapeDtypeStruct((s, n, SLAB), F32),
        scratch_shapes=scratch,
        compiler_params=_cparams("parallel", "arbitrary"),
        name="norm_proj",
    )(x, norm_w.reshape(1, D_MODEL), w_slabs, cos_t, sin_t)


def _even_kinds(perm):
    dil = B_DILATIONS if perm else (1,) * B_GROUPS
    return ([(0, 1)] * 4 + [(SLAB, d) for d in dil] * 2 + [(0, d) for d in dil] + [(0, 1)])


ODD_KINDS = [(0, 1)] * 3 + [(SLAB, 1), (D_KV_HEADS * HEAD_DIM, 1)]


def _rope_tables(pos):
    half = HEAD_DIM // 2
    inv = ROPE_THETA ** (-jnp.arange(half, dtype=F32) / half)
    ang = pos.astype(F32)[:, None] * inv[None, :]
    cos, sin = jnp.cos(ang), jnp.sin(ang)
    return (jnp.concatenate([cos, cos, cos, cos], axis=-1),
            jnp.concatenate([-sin, sin, -sin, sin], axis=-1))


def _pad_cols(w, width):
    return jnp.pad(w, ((0, 0), (0, width - w.shape[1])))


def _even_slabs(w_in):
    main = w_in[:, :2048].reshape(D_MODEL, 4, SLAB)
    gates = _pad_cols(w_in[:, 2048:2056], SLAB)[:, None, :]
    qkv = w_in[:, 2056:].reshape(D_MODEL, 9, SLAB)
    return jnp.moveaxis(jnp.concatenate([main, qkv, gates], axis=1), 1, 0).astype(BF16)


def _odd_slabs(w_in):
    main = w_in[:, :1536].reshape(D_MODEL, 3, SLAB)
    qd = w_in[:, 1544:2056][:, None, :]
    kvd = _pad_cols(jnp.concatenate([w_in[:, 2056:2312], w_in[:, 1536:1544]], axis=1), SLAB)[:, None, :]
    return jnp.moveaxis(jnp.concatenate([main, qd, kvd], axis=1), 1, 0).astype(BF16)


def _split3(x):
    hi = x.astype(BF16)
    r1 = x - hi.astype(F32)
    mid = r1.astype(BF16)
    lo = (r1 - mid.astype(F32)).astype(BF16)
    return hi, mid, lo


def _cumsum_rows(x, tri):
    hi, mid, lo = _split3(x)
    acc = jnp.dot(tri, lo, preferred_element_type=F32)
    acc = acc + jnp.dot(tri, mid, preferred_element_type=F32)
    return acc + jnp.dot(tri, hi, preferred_element_type=F32)


def _transpose_slab(s):
    l = s.shape[0]
    if l < 128:
        s = jnp.concatenate([s, jnp.zeros((128 - l, 128), s.dtype)], axis=0)
    return s.T


def _conv_silu(buf_ref, x, w, b):
    tb = x.shape[0]
    buf_ref[8:8 + tb, :] = x
    y = b + w[3:4] * x
    for j in range(3):
        y = y + w[j:j + 1] * buf_ref[5 + j:5 + j + tb, :]
    buf_ref[0:8, :] = buf_ref[tb:tb + 8, :]
    return y * jax.nn.sigmoid(y)


def _log_sigmoid(x):
    return jnp.minimum(x, 0.0) - jnp.log1p(jnp.exp(-jnp.abs(x)))


def _dot_t(a, b):
    return lax.dot_general(a, b, (((1,), (1,)), ((), ())), preferred_element_type=F32)


def _tdot(a, b):
    return lax.dot_general(a, b, (((0,), (0,)), ((), ())), preferred_element_type=F32)


def _mlstm_kernel(q_ref, k_ref, v_ref, og_ref, g_ref, cw_ref, cb_ref, bg_ref, an_ref,
                  conv0_ref, c0_ref, n0_ref, m0_ref,
                  ya_ref, c_ref, n_ref, m_ref, qbuf, kbuf, *, chunk):
    sb, tb, _ = q_ref.shape
    tblk = pl.program_id(1)
    lane = lax.broadcasted_iota(jnp.int32, (chunk, 128), 1)
    lane1 = lax.broadcasted_iota(jnp.int32, (1, 128), 1)
    row = lax.broadcasted_iota(jnp.int32, (chunk, chunk), 0)
    col = lax.broadcasted_iota(jnp.int32, (chunk, chunk), 1)
    causal = col <= row
    tri = causal.astype(BF16)
    scale = A_DK ** -0.5

    @pl.when(tblk == 0)
    def _():
        c_ref[...] = c0_ref[...]
        n_ref[...] = n0_ref[...]
        m_ref[...] = m0_ref[...]
        for si in range(sb):
            qbuf[si, 0:8, :] = conv0_ref[si, :, 0:SLAB]
            kbuf[si, 0:8, :] = conv0_ref[si, :, SLAB:2 * SLAB]

    for si in range(sb):
        qc = _conv_silu(qbuf.at[si], q_ref[si], cw_ref[:, 0:SLAB], cb_ref[:, 0:SLAB])
        kc = _conv_silu(kbuf.at[si], k_ref[si], cw_ref[:, SLAB:2 * SLAB], cb_ref[:, SLAB:2 * SLAB]) * scale
        m_vec = m_ref[si]
        for c in range(tb // chunk):
            r0 = c * chunk
            gates = g_ref[si, r0:r0 + chunk, 0:128] + bg_ref[...]
            bc = _cumsum_rows(_log_sigmoid(gates), tri)
            st = _transpose_slab(jnp.where(lane < A_HEADS, gates, bc))
            m_next = m_vec
            for h in range(A_HEADS):
                hs = slice(h * 128, (h + 1) * 128)
                qh = qc[r0:r0 + chunk, hs]
                kh = kc[r0:r0 + chunk, hs]
                vh = v_ref[si, r0:r0 + chunk, hs]
                qb, kb, vb = qh.astype(BF16), kh.astype(BF16), vh.astype(BF16)
                b_t = bc[:, A_HEADS + h:A_HEADS + h + 1]
                li_t = gates[:, h:h + 1]
                li_s = st[h:h + 1, 0:chunk]
                b_s = st[A_HEADS + h:A_HEADS + h + 1, 0:chunk]
                m_h = m_vec[:, h:h + 1]
                dlog = jnp.where(causal, b_t - b_s + li_s, NEG)
                inter = b_t + m_h
                m_t = jnp.maximum(inter, jnp.max(dlog, axis=1, keepdims=True))
                w_intra = jnp.exp(dlog - m_t)
                w_inter = jnp.exp(inter - m_t)
                s = _dot_t(qb, kb) * w_intra
                c_st = c_ref[si, h]
                n_st = n_ref[si, h:h + 1, :]
                num = jnp.dot(s.astype(BF16), vb, preferred_element_type=F32) \
                    + w_inter * jnp.dot(qb, c_st.astype(BF16), preferred_element_type=F32)
                den = jnp.sum(s, axis=1, keepdims=True) + w_inter * jnp.sum(qh * n_st, axis=1, keepdims=True)
                hh = num / jnp.maximum(jnp.abs(den), jnp.exp(-m_t))
                b_end = b_t[chunk - 1:chunk, :]
                g_col = b_end - b_t + li_t
                m_new = jnp.maximum(b_end + m_h, jnp.max(g_col, axis=0, keepdims=True))
                w_state = jnp.exp(g_col - m_new)
                decay = jnp.exp(b_end + m_h - m_new)
                kw = kh * w_state
                c_ref[si, h] = decay * c_st + _tdot(kw.astype(BF16), vb)
                n_ref[si, h:h + 1, :] = decay * n_st + jnp.sum(kw, axis=0, keepdims=True)
                m_next = jnp.where(lane1 == h, m_new, m_next)
                hn = hh * lax.rsqrt(jnp.mean(hh * hh, axis=-1, keepdims=True) + EPS) * an_ref[:, hs]
                ya_ref[si, r0:r0 + chunk, hs] = hn * jax.nn.sigmoid(og_ref[si, r0:r0 + chunk, hs])
            m_vec = m_next
        m_ref[si] = m_vec


def _mlstm(proj, nseq, t, conv_w, conv_b, b_gates, a_norm, conv0, c0, n0, m0):
    chunk = A_CHUNK if t % A_CHUNK == 0 else t
    if t > chunk:
        sb, tb = 1, min(t, 4 * chunk)
    else:
        sb, tb = min(nseq, 8), t
    p4 = proj.reshape(proj.shape[0], nseq, t, SLAB)
    slab = lambda s: pl.BlockSpec((None, sb, tb, SLAB), lambda i, j, s=s: (s, i, j, 0))
    full = lambda a: pl.BlockSpec(a.shape, lambda i, j: (0,) * a.ndim)
    st = lambda a: pl.BlockSpec((sb,) + a.shape[1:], lambda i, j: (i,) + (0,) * (a.ndim - 1))
    bg = _pad_cols(b_gates.reshape(1, 2 * A_HEADS), 128)
    conv0p = jnp.pad(conv0, ((0, 0), (8 - (A_CONV - 1), 0), (0, 0)))
    m0p = _pad_cols(m0, 128).reshape(nseq, 1, 128)
    cb = conv_b.reshape(1, -1)
    an = a_norm.reshape(1, -1)
    ya, c, n, m = pl.pallas_call(
        functools.partial(_mlstm_kernel, chunk=chunk),
        grid=(nseq // sb, t // tb),
        in_specs=[slab(0), slab(1), slab(2), slab(3), slab(13),
                  full(conv_w), full(cb), full(bg), full(an),
                  st(conv0p), st(c0), st(n0), st(m0p)],
        out_specs=[pl.BlockSpec((sb, tb, SLAB), lambda i, j: (i, j, 0)), st(c0), st(n0), st(m0p)],
        out_shape=[jax.ShapeDtypeStruct((nseq, t, SLAB), F32),
                   jax.ShapeDtypeStruct(c0.shape, F32), jax.ShapeDtypeStruct(n0.shape, F32),
                   jax.ShapeDtypeStruct(m0p.shape, F32)],
        scratch_shapes=[pltpu.VMEM((sb, 8 + tb, SLAB), F32), pltpu.VMEM((sb, 8 + tb, SLAB), F32)],
        compiler_params=_cparams("parallel", "arbitrary"),
        name="mlstm",
    )(p4, p4, p4, p4, p4, conv_w, cb, bg, an, conv0p, c0, n0, m0p)
    return ya.reshape(nseq * t, SLAB), c, n, m[:, 0, :A_HEADS]


QBLK = 128


def _band_head(q_m, k2, v_m, valid, sink):
    s = jnp.where(valid, _dot_t(q_m, k2), NEG)
    m = jnp.max(s, axis=1, keepdims=True)
    if sink is not None:
        m = jnp.maximum(m, sink)
    e = jnp.exp(s - m)
    l = jnp.sum(e, axis=1, keepdims=True)
    if sink is not None:
        l = l + jnp.exp(sink - m)
    acc = jnp.dot(e.astype(BF16), v_m, preferred_element_type=F32)
    return acc, m, l


def _band_masks(i, window):
    a = lax.broadcasted_iota(jnp.int32, (QBLK, 2 * QBLK), 0)
    c = lax.broadcasted_iota(jnp.int32, (QBLK, 2 * QBLK), 1)
    rel = a - c + QBLK
    first_col = jnp.where(i > 0, 0, QBLK)
    return jnp.logical_and(jnp.logical_and(rel >= 0, rel <= window), c >= first_col)


def _dilated_kernel(q_ref, kp_ref, kc_ref, vp_ref, vc_ref, o_ref, lse_ref):
    valid = _band_masks(pl.program_id(2), QBLK)
    lane = lax.broadcasted_iota(jnp.int32, (1, 128), 1)
    scale = HEAD_DIM ** -0.5
    for pr in range(B_HEADS // 2):
        ls = slice(pr * 128, (pr + 1) * 128)
        qp = q_ref[:, ls] * scale
        k2 = jnp.concatenate([kp_ref[:, ls], kc_ref[:, ls]], axis=0).astype(BF16)
        v2 = jnp.concatenate([vp_ref[:, ls], vc_ref[:, ls]], axis=0)
        o_pair = None
        lse_pair = None
        for p in range(2):
            half = (lane < HEAD_DIM) if p == 0 else (lane >= HEAD_DIM)
            acc, m, l = _band_head(jnp.where(half, qp, 0.0).astype(BF16), k2,
                                   jnp.where(half, v2, 0.0).astype(BF16), valid, None)
            o_h = acc / l
            lse_h = m + jnp.log(l)
            o_pair = o_h if o_pair is None else o_pair + o_h
            lse_pair = lse_h if lse_pair is None else jnp.where(half, lse_h, lse_pair)
        o_ref[:, ls] = o_pair
        lse_ref[:, ls] = jnp.broadcast_to(lse_pair, (QBLK, 128))


def _dilated_prompt(proj, b, t, g):
    d = B_DILATIONS[g]
    assert t % PERM_TILE == 0
    nb = PERM_TILE // d // QBLK
    per_tile = PERM_TILE // QBLK
    pv = proj.reshape(proj.shape[0], b, t, SLAB)

    def blk(i, r):
        return (i // nb) * per_tile + r * nb + i % nb

    cur = lambda s: pl.BlockSpec((None, None, QBLK, SLAB), lambda bi, r, i, s=s: (s, bi, blk(i, r), 0))
    prev = lambda s: pl.BlockSpec((None, None, QBLK, SLAB),
                                  lambda bi, r, i, s=s: (s, bi, blk(jnp.maximum(i - 1, 0), r), 0))
    ospec = pl.BlockSpec((None, QBLK, SLAB), lambda bi, r, i: (bi, blk(i, r), 0))
    sq, sk, sv = 4 + g, 7 + g, 10 + g
    o, lse = pl.pallas_call(
        _dilated_kernel,
        grid=(b, d, t // d // QBLK),
        in_specs=[cur(sq), prev(sk), cur(sk), prev(sv), cur(sv)],
        out_specs=[ospec, ospec],
        out_shape=[jax.ShapeDtypeStruct((b, t, SLAB), F32)] * 2,
        compiler_params=_cparams("parallel", "parallel", "arbitrary"),
        name=f"dilated_g{g}",
    )(pv, pv, pv, pv, pv)
    return o.reshape(b * t, SLAB), lse.reshape(b * t, SLAB)


def _merge_kernel(*refs, dils):
    ins, y_ref, bufs = refs[:6], refs[6], refs[7:]
    tm = y_ref.shape[0]
    vals = []
    nbuf = 0
    for k, ref in enumerate(ins):
        dil = dils[k // 2]
        if dil == 1:
            vals.append(ref[...])
        else:
            buf = bufs[nbuf]
            nbuf += 1
            seg = tm // dil
            for r in range(dil):
                buf[pl.ds(r, seg, stride=dil), :] = ref[r * seg:(r + 1) * seg, :]
            vals.append(buf[...])
    o1, l1, o2, l2, o3, l3 = vals
    m = jnp.maximum(jnp.maximum(l1, l2), l3)
    w1, w2, w3 = jnp.exp(l1 - m), jnp.exp(l2 - m), jnp.exp(l3 - m)
    y_ref[...] = (w1 * o1 + w2 * o2 + w3 * o3) / (w1 + w2 + w3)


def _merge_groups(parts, perm):
    n = parts[0].shape[0]
    dils = B_DILATIONS if perm else (1,) * B_GROUPS
    tm = PERM_TILE if perm else min(1024, n)
    spec = pl.BlockSpec((tm, 128), lambda i, j: (i, j))
    nbuf = 2 * sum(d > 1 for d in dils)
    return pl.pallas_call(
        functools.partial(_merge_kernel, dils=dils), grid=(n // tm, SLAB // 128), in_specs=[spec] * 6,
        out_specs=spec, out_shape=jax.ShapeDtypeStruct((n, SLAB), F32),
        scratch_shapes=[pltpu.VMEM((tm, 128), F32)] * nbuf,
        compiler_params=_cparams("parallel", "parallel"), name="merge_groups",
    )(*parts)


def _swa_kernel(q_ref, kp_ref, kc_ref, vp_ref, vc_ref, sink_ref, o_ref):
    valid = _band_masks(pl.program_id(1), D_WINDOW - 1)
    lane = lax.broadcasted_iota(jnp.int32, (1, 128), 1)
    scale = HEAD_DIM ** -0.5
    k2 = jnp.concatenate([kp_ref[...], kc_ref[...]], axis=0).astype(BF16)
    v2 = jnp.concatenate([vp_ref[...], vc_ref[...]], axis=0)
    v2r = pltpu.roll(v2, HEAD_DIM, 1)
    rep = D_HEADS // D_KV_HEADS
    for pr in range(D_HEADS // 2):
        ls = slice(pr * 128, (pr + 1) * 128)
        qp = q_ref[:, ls] * scale
        qpr = pltpu.roll(qp, HEAD_DIM, 1)
        o_pair = None
        for p in range(2):
            h = 2 * pr + p
            g = h // rep
            half_g = (lane < HEAD_DIM) if g == 0 else (lane >= HEAD_DIM)
            half_p = (lane < HEAD_DIM) if p == 0 else (lane >= HEAD_DIM)
            q_m = jnp.where(half_g, qp if p == g else qpr, 0.0).astype(BF16)
            v_m = jnp.where(half_p, v2 if p == g else v2r, 0.0).astype(BF16)
            acc, m, l = _band_head(q_m, k2, v_m, valid, sink_ref[:, h:h + 1])
            o_h = acc / l
            o_pair = o_h if o_pair is None else o_pair + o_h
        o_ref[:, ls] = o_pair


def _swa_prompt(proj, b, t, sinks):
    pv = proj.reshape(proj.shape[0], b, t, SLAB)
    qspec = pl.BlockSpec((None, None, QBLK, SLAB), lambda bi, i: (3, bi, i, 0))
    kv = lambda col, back: pl.BlockSpec((None, None, QBLK, 128),
                                        lambda bi, i: (4, bi, jnp.maximum(i - back, 0), col))
    o = pl.pallas_call(
        _swa_kernel,
        grid=(b, t // QBLK),
        in_specs=[qspec, kv(0, 1), kv(0, 0), kv(1, 1), kv(1, 0), pl.BlockSpec((1, 128), lambda bi, i: (0, 0))],
        out_specs=pl.BlockSpec((None, QBLK, SLAB), lambda bi, i: (bi, i, 0)),
        out_shape=jax.ShapeDtypeStruct((b, t, SLAB), F32),
        compiler_params=_cparams("parallel", "arbitrary"),
        name="swa_prompt",
    )(pv, pv, pv, pv, pv, _pad_cols(sinks.reshape(1, D_HEADS), 128))
    return o.reshape(b * t, SLAB)


def _softplus(x):
    return jnp.maximum(x, 0.0) + jnp.log1p(jnp.exp(-jnp.abs(x)))


def _ssd_kernel(z_ref, x_ref, bc_ref, dt_ref, cw_ref, cb_ref, dtb_ref, alog_ref, dl_ref, cn_ref,
                conv0_ref, h0_ref, y_ref, h_ref, xbuf, bcbuf, *, chunk):
    sb, tb, _ = x_ref.shape
    tblk = pl.program_id(1)
    lane1 = lax.broadcasted_iota(jnp.int32, (1, 128), 1)
    low = lane1 < C_HEADDIM
    row = lax.broadcasted_iota(jnp.int32, (chunk, chunk), 0)
    col = lax.broadcasted_iota(jnp.int32, (chunk, chunk), 1)
    causal = col <= row
    tri = causal.astype(BF16)
    srow_low = lax.broadcasted_iota(jnp.int32, (128, 1), 0) < C_HEADDIM
    hpg = C_HEADS // C_NGROUPS
    a_neg = -jnp.exp(alog_ref[...])

    @pl.when(tblk == 0)
    def _():
        h_ref[...] = h0_ref[...]
        for si in range(sb):
            xbuf[si, 0:8, :] = conv0_ref[si, :, 0:SLAB]
            bcbuf[si, 0:8, :] = conv0_ref[si, :, SLAB:2 * SLAB]

    for si in range(sb):
        xc = _conv_silu(xbuf.at[si], x_ref[si], cw_ref[:, 0:SLAB], cb_ref[:, 0:SLAB])
        bcc = _conv_silu(bcbuf.at[si], bc_ref[si], cw_ref[:, SLAB:2 * SLAB], cb_ref[:, SLAB:2 * SLAB])
        for c in range(tb // chunk):
            rs = slice(c * chunk, (c + 1) * chunk)
            dt = _softplus(dt_ref[si, rs, :] + dtb_ref[...])
            cum = _cumsum_rows(dt * a_neg, tri)
            cum_t_all = _transpose_slab(cum)
            dt_t_all = _transpose_slab(dt)
            cum_end = cum[chunk - 1:chunk, :]
            for g in range(C_NGROUPS):
                bg = bcc[rs, g * 128:(g + 1) * 128].astype(BF16)
                cg = bcc[rs, 256 + g * 128:256 + (g + 1) * 128].astype(BF16)
                cb = _dot_t(cg, bg)
                y_pairs = []
                for pp in range(hpg // 2):
                    pr = g * (hpg // 2) + pp
                    ls = slice(pr * 128, (pr + 1) * 128)
                    x_pair = xc[rs, ls]
                    hs = h_ref[si, pr]
                    y_pair = None
                    e_col, w_end, dec = [], [], []
                    for p in range(2):
                        h = 2 * pr + p
                        cum_t = cum[:, h:h + 1]
                        seg = jnp.where(causal, cum_t - cum_t_all[h:h + 1, 0:chunk], NEG)
                        w = cb * jnp.exp(seg) * dt_t_all[h:h + 1, 0:chunk]
                        half = low if p == 0 else jnp.logical_not(low)
                        yh = jnp.dot(w.astype(BF16), jnp.where(half, x_pair, 0.0).astype(BF16),
                                     preferred_element_type=F32)
                        y_pair = yh if y_pair is None else y_pair + yh
                        e_col.append(jnp.exp(cum_t))
                        w_end.append(jnp.exp(cum_end[:, h:h + 1] - cum_t) * dt[:, h:h + 1])
                        dec.append(jnp.exp(cum_end[:, h:h + 1]))
                    inter = _dot_t(cg, hs.astype(BF16)) * jnp.where(low, e_col[0], e_col[1])
                    y_pair = y_pair + inter + dl_ref[:, ls] * x_pair
                    xw = x_pair * jnp.where(low, w_end[0], w_end[1])
                    h_ref[si, pr] = jnp.where(srow_low, dec[0], dec[1]) * hs + _tdot(xw.astype(BF16), bg)
                    zz = z_ref[si, rs, ls]
                    y_pairs.append(y_pair * (zz * jax.nn.sigmoid(zz)))
                ms = sum(jnp.sum(yp * yp, axis=-1, keepdims=True) for yp in y_pairs) / (hpg * C_HEADDIM)
                inv = lax.rsqrt(ms + EPS)
                for pp, yp in enumerate(y_pairs):
                    ls = slice((g * (hpg // 2) + pp) * 128, (g * (hpg // 2) + pp + 1) * 128)
                    y_ref[si, rs, ls] = yp * inv * cn_ref[:, ls]


def _ssd(proj, nseq, t, conv_w, conv_b, dt_bias, a_log, d_skip, c_norm, conv0, h0):
    chunk = C_CHUNK if t % C_CHUNK == 0 else t
    if t > chunk:
        sb, tb = 1, min(t, 2 * chunk)
    else:
        sb, tb = min(nseq, 8), t
    p4 = proj.reshape(proj.shape[0], nseq, t, SLAB)
    slab = lambda s: pl.BlockSpec((None, sb, tb, SLAB), lambda i, j, s=s: (s, i, j, 0))
    full = lambda a: pl.BlockSpec(a.shape, lambda i, j: (0,) * a.ndim)
    st = lambda a: pl.BlockSpec((sb,) + a.shape[1:], lambda i, j: (i,) + (0,) * (a.ndim - 1))
    conv0p = jnp.pad(conv0, ((0, 0), (8 - (C_CONV - 1), 0), (0, 0)))
    hp = h0.reshape(nseq, C_HEADS // 2, 128, C_DSTATE)
    cb = conv_b.reshape(1, -1)
    dtb = _pad_cols(dt_bias.reshape(1, C_HEADS), 128)
    alog = _pad_cols(a_log.reshape(1, C_HEADS), 128)
    dl = jnp.repeat(d_skip, C_HEADDIM).reshape(1, C_DINNER)
    cn = c_norm.reshape(1, C_DINNER)
    y, h = pl.pallas_call(
        functools.partial(_ssd_kernel, chunk=chunk),
        grid=(nseq // sb, t // tb),
        in_specs=[slab(0), slab(1), slab(2),
                  pl.BlockSpec((None, sb, tb, 128), lambda i, j: (4, i, j, 2)),
                  full(conv_w), full(cb), full(dtb), full(alog), full(dl), full(cn), st(conv0p), st(hp)],
        out_specs=[pl.BlockSpec((sb, tb, SLAB), lambda i, j: (i, j, 0)), st(hp)],
        out_shape=[jax.ShapeDtypeStruct((nseq, t, SLAB), F32), jax.ShapeDtypeStruct(hp.shape, F32)],
        scratch_shapes=[pltpu.VMEM((sb, 8 + tb, SLAB), F32), pltpu.VMEM((sb, 8 + tb, SLAB), F32)],
        compiler_params=_cparams("parallel", "arbitrary"),
        name="ssd",
    )(p4, p4, p4, p4, conv_w, cb, dtb, alog, dl, cn, conv0p, hp)
    return y.reshape(nseq * t, SLAB), h.reshape(nseq, C_HEADS, C_HEADDIM, C_DSTATE)


FFN_CHUNK = 256


def _post_kernel(x_ref, ya_ref, yb_ref, wo_ref, fn_ref, wg_ref, wu_ref, wd_ref, on_ref, o_ref, act_ref, *, final):
    ymix = jnp.concatenate([ya_ref[...], yb_ref[...]], axis=-1).astype(BF16)
    x1 = x_ref[...] + jnp.dot(ymix, wo_ref[...], preferred_element_type=F32)
    ms = jnp.mean(x1 * x1, axis=-1, keepdims=True)
    hn = (x1 * lax.rsqrt(ms + EPS) * fn_ref[...]).astype(BF16)
    hidden = wg_ref.shape[1]
    for c in range(hidden // FFN_CHUNK):
        cs = slice(c * FFN_CHUNK, (c + 1) * FFN_CHUNK)
        gate = jnp.dot(hn, wg_ref[:, cs], preferred_element_type=F32)
        up = jnp.dot(hn, wu_ref[:, cs], preferred_element_type=F32)
        act_ref[:, cs] = (gate * jax.nn.sigmoid(gate) * up).astype(BF16)
    x2 = x1 + jnp.dot(act_ref[...], wd_ref[...], preferred_element_type=F32)
    if final:
        ms2 = jnp.mean(x2 * x2, axis=-1, keepdims=True)
        x2 = x2 * lax.rsqrt(ms2 + EPS) * on_ref[...]
    o_ref[...] = x2


def _post(x, ya, yb, w_out, ffn_norm, w_gate, w_up, w_down, out_norm, final):
    n = x.shape[0]
    tm = min(512, n)
    hidden = w_gate.shape[1]
    row = lambda w: pl.BlockSpec((tm, w), lambda i: (i, 0))
    res = lambda a: pl.BlockSpec(a.shape, lambda i: (0, 0), pipeline_mode=pl.Buffered(1))
    fn = ffn_norm.reshape(1, D_MODEL)
    on = out_norm.reshape(1, D_MODEL)
    wo, wg, wu, wd = (w.astype(BF16) for w in (w_out, w_gate, w_up, w_down))
    return pl.pallas_call(
        functools.partial(_post_kernel, final=final),
        grid=(n // tm,),
        in_specs=[row(D_MODEL), row(SLAB), row(SLAB), res(wo), res(fn), res(wg), res(wu), res(wd), res(on)],
        out_specs=row(D_MODEL),
        out_shape=jax.ShapeDtypeStruct((n, D_MODEL), F32),
        scratch_shapes=[pltpu.VMEM((tm, hidden), BF16)],
        compiler_params=_cparams("parallel"),
        name="post_ffn",
    )(x, ya, yb, wo, fn, wg, wu, wd, on)


NEWPAD = 128


def _ext_mask(rows, r, ts, window, step):
    t = lax.broadcasted_iota(jnp.int32, (rows, r + NEWPAD), 0) % ts
    c = lax.broadcasted_iota(jnp.int32, (rows, r + NEWPAD), 1)
    rel = r + t - c
    ok = jnp.logical_and(jnp.logical_and(rel >= 0, rel <= window), c < r + ts)
    return jnp.logical_and(ok, (rel & (step - 1)) == 0)


def _shift_cache(dst_ref, src_ref, new, si, r, ts):
    dst_ref[si, 0:r - ts, :] = src_ref[si, ts:r, :]
    dst_ref[si, r - ts:r, :] = new


def _pad_new(x):
    return jnp.concatenate([x, jnp.zeros((NEWPAD - x.shape[0], x.shape[1]), x.dtype)], axis=0)


def _dil_sample_kernel(q_ref, kn_ref, vn_ref, kc_ref, vc_ref, o_ref, lse_ref, ko_ref, vo_ref,
                       *, window, step):
    ts = q_ref.shape[0]
    r = kc_ref.shape[1]
    scale = HEAD_DIM ** -0.5
    lane = lax.broadcasted_iota(jnp.int32, (1, 128), 1)
    low = lane < HEAD_DIM
    t = lax.broadcasted_iota(jnp.int32, (2 * ts, r + 128), 0) % ts
    c = lax.broadcasted_iota(jnp.int32, (2 * ts, r + 128), 1)
    rel = r + t - jnp.where(c < 128, c, c - 128 + ts)
    valid = jnp.logical_and(jnp.logical_and(rel >= 0, rel <= window), (rel & (step - 1)) == 0)
    valid = jnp.logical_and(valid, jnp.logical_or(c < ts, c >= 128))
    pad = jnp.zeros((128 - ts, SLAB), F32)
    kn_t = jnp.concatenate([pad, kn_ref[...]], axis=0).T
    vn_t = jnp.concatenate([pad, vn_ref[...]], axis=0).T

    def shifted(src_ref, new_t, rows):
        old = src_ref[rows, :]
        rolled = pltpu.roll(old, r - ts, 1)
        tail = jnp.where(lane < 128 - ts, rolled[:, r - 128:], new_t[rows, :])
        out = tail if r == 128 else jnp.concatenate([rolled[:, :r - 128], tail], axis=1)
        return old, out

    for pr in range(B_HEADS // 2):
        rows = slice(pr * 128, (pr + 1) * 128)
        k_old, k_out = shifted(kc_ref, kn_t, rows)
        v_old, v_out = shifted(vc_ref, vn_t, rows)
        ko_ref[rows, :] = k_out
        vo_ref[rows, :] = v_out
        k_ext = jnp.concatenate([k_old[:, 0:128], k_out], axis=1).astype(BF16)
        v_ext = jnp.concatenate([v_old[:, 0:128], v_out], axis=1).astype(BF16)
        qp = q_ref[:, rows] * scale
        q2 = jnp.concatenate([jnp.where(low, qp, 0.0), jnp.where(low, 0.0, qp)], axis=0).astype(BF16)
        s = jnp.where(valid, jnp.dot(q2, k_ext, preferred_element_type=F32), NEG)
        m = jnp.max(s, axis=1, keepdims=True)
        e = jnp.exp(s - m)
        l = jnp.sum(e, axis=1, keepdims=True)
        a = _dot_t(e.astype(BF16), v_ext) / l
        lse = m + jnp.log(l)
        o_ref[:, rows] = jnp.where(low, a[0:ts], a[ts:2 * ts])
        lse_ref[:, rows] = jnp.where(low, lse[0:ts], lse[ts:2 * ts])


def _rows_minor(cache):
    nseq, r = cache.shape[:2]
    return jnp.transpose(cache, (0, 2, 3, 1)).reshape(nseq, -1, r)


def _rows_major(cache_t, heads):
    nseq, _, r = cache_t.shape
    return jnp.transpose(cache_t.reshape(nseq, heads, HEAD_DIM, r), (0, 3, 1, 2))


def _dilated_sample(proj, nseq, ts, g, k_cache, v_cache):
    r = k_cache.shape[1]
    assert ts % 8 == 0 and r % 128 == 0 and r >= B_WINDOWS[g]
    kt, vt = _rows_minor(k_cache), _rows_minor(v_cache)
    new = lambda s: pl.BlockSpec((None, ts, SLAB), lambda i, s=s: (s, i, 0))
    cspec = pl.BlockSpec((None, SLAB, r), lambda i: (i, 0, 0))
    ospec = pl.BlockSpec((ts, SLAB), lambda i: (i, 0))
    o, lse, kn, vn = pl.pallas_call(
        functools.partial(_dil_sample_kernel, window=B_WINDOWS[g], step=B_DILATIONS[g]),
        grid=(nseq,),
        in_specs=[new(4 + g), new(7 + g), new(10 + g), cspec, cspec],
        out_specs=[ospec, ospec, cspec, cspec],
        out_shape=[jax.ShapeDtypeStruct((nseq * ts, SLAB), F32)] * 2 + [jax.ShapeDtypeStruct(kt.shape, F32)] * 2,
        compiler_params=_cparams("parallel"),
        name=f"dilated_sample_g{g}",
    )(proj, proj, proj, kt, vt)
    return o, lse, _rows_major(kn, B_HEADS), _rows_major(vn, B_HEADS)


def _swa_sample_kernel(q_ref, k_ref, v_ref, kc_ref, vc_ref, sink_ref, o_ref, kn_ref, vn_ref):
    sb, ts, _ = q_ref.shape
    r = kc_ref.shape[1]
    rows = D_HEADS * ts
    valid = _ext_mask(rows, r, ts, D_WINDOW - 1, 1)
    lane = lax.broadcasted_iota(jnp.int32, (1, 128), 1)
    low = lane < HEAD_DIM
    rep = D_HEADS // D_KV_HEADS
    scale = HEAD_DIM ** -0.5
    sink = sink_ref[:, 0:1]
    for si in range(sb):
        kn, vn = k_ref[si], v_ref[si]
        pieces = []
        for h in range(D_HEADS):
            g, p = h // rep, h % 2
            base = q_ref[si, :, (h // 2) * 128:(h // 2 + 1) * 128] * scale
            if p != g:
                base = pltpu.roll(base, HEAD_DIM, 1)
            pieces.append(jnp.where(low if g == 0 else jnp.logical_not(low), base, 0.0))
        qrows = jnp.concatenate(pieces, axis=0).astype(BF16)
        kext = jnp.concatenate([kc_ref[si], _pad_new(kn)], axis=0).astype(BF16)
        vext = jnp.concatenate([vc_ref[si], _pad_new(vn)], axis=0).astype(BF16)
        s = jnp.where(valid, _dot_t(qrows, kext), NEG)
        m = jnp.maximum(jnp.max(s, axis=1, keepdims=True), sink)
        e = jnp.exp(s - m)
        l = jnp.sum(e, axis=1, keepdims=True) + jnp.exp(sink - m)
        o = jnp.dot(e.astype(BF16), vext, preferred_element_type=F32) / l
        pairs = []
        for pr in range(D_HEADS // 2):
            acc = None
            for p in range(2):
                h = 2 * pr + p
                g = h // rep
                piece = o[h * ts:(h + 1) * ts]
                if p != g:
                    piece = pltpu.roll(piece, HEAD_DIM, 1)
                piece = jnp.where(low if p == 0 else jnp.logical_not(low), piece, 0.0)
                acc = piece if acc is None else acc + piece
            pairs.append(acc)
        o_ref[si] = jnp.concatenate(pairs, axis=1)
        _shift_cache(kn_ref, kc_ref, kn, si, r, ts)
        _shift_cache(vn_ref, vc_ref, vn, si, r, ts)


def _swa_sample(proj, nseq, ts, sinks, k_cache, v_cache):
    r = k_cache.shape[1]
    sb = min(nseq, 8)
    kvw = D_KV_HEADS * HEAD_DIM
    p4 = proj.reshape(proj.shape[0], nseq, ts, SLAB)
    kc = k_cache.reshape(nseq, r, kvw)
    vc = v_cache.reshape(nseq, r, kvw)
    cspec = pl.BlockSpec((sb, r, kvw), lambda i: (i, 0, 0))
    new = lambda col: pl.BlockSpec((None, sb, ts, kvw), lambda i: (4, i, 0, col))
    sink_rows = jnp.broadcast_to(jnp.repeat(sinks, ts)[:, None], (D_HEADS * ts, 128))
    o, kn, vn = pl.pallas_call(
        _swa_sample_kernel,
        grid=(nseq // sb,),
        in_specs=[pl.BlockSpec((None, sb, ts, SLAB), lambda i: (3, i, 0, 0)), new(0), new(1), cspec, cspec,
                  pl.BlockSpec(sink_rows.shape, lambda i: (0, 0))],
        out_specs=[pl.BlockSpec((sb, ts, SLAB), lambda i: (i, 0, 0)), cspec, cspec],
        out_shape=[jax.ShapeDtypeStruct((nseq, ts, SLAB), F32)] + [jax.ShapeDtypeStruct(kc.shape, F32)] * 2,
        compiler_params=_cparams("parallel"),
        name="swa_sample",
    )(p4, p4, p4, kc, vc, sink_rows)
    return o.reshape(nseq * ts, SLAB), kn.reshape(k_cache.shape), vn.reshape(v_cache.shape)


def kernel(x_prompt, x_sample, state_a_C, state_a_n, state_a_m, state_a_conv, cache_b_k1, cache_b_v1, cache_b_k2, cache_b_v2, cache_b_k3, cache_b_v3, state_c_conv, state_c_ssm, cache_d_k, cache_d_v, e_norm_mix, e_w_in, e_b_gates, e_a_conv_w, e_a_conv_b, e_a_norm, e_w_out, o_norm_mix, o_w_in, o_c_conv_w, o_c_conv_b, o_c_dt_bias, o_c_A_log, o_c_D, o_c_norm, o_d_sinks, o_w_out, ffn_norm, w_gate, w_up, w_down, final_norm):
    depth = ffn_norm.shape[0]
    even_w = [_even_slabs(e_w_in[j]) for j in range((depth + 1) // 2)]
    odd_w = [_odd_slabs(o_w_in[j]) for j in range(depth // 2)]

    def trunk(x, pos, fresh, even_st, odd_st):
        nseq, t, _ = x.shape
        assert t >= A_CONV - 1 and t >= C_CONV - 1
        cos_t, sin_t = _rope_tables(pos)
        x2 = x.reshape(nseq * t, D_MODEL)
        new_even, new_odd = [], []

        def tail(proj, s, keep, dil=1, lanes=slice(None)):
            p4 = proj.reshape(proj.shape[0], nseq, t, SLAB)
            if dil == 1:
                return p4[s, :, t - keep:, lanes]
            seg = PERM_TILE // dil
            last = p4[s, :, t - PERM_TILE:, lanes].reshape(nseq, dil, seg, -1)[:, :, seg - keep // dil:]
            return jnp.swapaxes(last, 1, 2).reshape(nseq, keep, -1)

        for layer in range(depth):
            j = layer // 2
            last = layer == depth - 1
            if layer % 2 == 0:
                pe = _norm_proj(x2, e_norm_mix[j], even_w[j], cos_t, sin_t, _even_kinds(fresh))
                st = {k: v[j] for k, v in even_st.items()}
                ya, c_new, n_new, m_new = _mlstm(pe, nseq, t, e_a_conv_w[j], e_a_conv_b[j], e_b_gates[j],
                                                 e_a_norm[j], st['a_conv'], st['a_C'], st['a_n'], st['a_m'])
                ns = {'a_C': c_new, 'a_n': n_new, 'a_m': m_new,
                      'a_conv': jnp.concatenate([tail(pe, 0, A_CONV - 1), tail(pe, 1, A_CONV - 1)], axis=-1)}
                parts = []
                for g in range(B_GROUPS):
                    kn, vn = f'b_k{g + 1}', f'b_v{g + 1}'
                    if fresh:
                        o_g, lse_g = _dilated_prompt(pe, nseq, t, g)
                        keep = min(B_WINDOWS[g], t)
                        ns[kn] = tail(pe, 7 + g, keep, B_DILATIONS[g]).reshape(nseq, keep, B_HEADS, HEAD_DIM)
                        ns[vn] = tail(pe, 10 + g, keep, B_DILATIONS[g]).reshape(nseq, keep, B_HEADS, HEAD_DIM)
                    else:
                        o_g, lse_g, ns[kn], ns[vn] = _dilated_sample(pe, nseq, t, g, st[kn], st[vn])
                    parts += [o_g, lse_g]
                yb = _merge_groups(parts, fresh)
                x2 = _post(x2, ya, yb, e_w_out[j], ffn_norm[layer], w_gate[layer], w_up[layer], w_down[layer],
                           final_norm, last)
                new_even.append(ns)
            else:
                po = _norm_proj(x2, o_norm_mix[j], odd_w[j], cos_t, sin_t, ODD_KINDS)
                st = {k: v[j] for k, v in odd_st.items()}
                yc, ssm_new = _ssd(po, nseq, t, o_c_conv_w[j], o_c_conv_b[j], o_c_dt_bias[j], o_c_A_log[j],
                                   o_c_D[j], o_c_norm[j], st['c_conv'], st['c_ssm'])
                ns = {'c_conv': jnp.concatenate([tail(po, 1, C_CONV - 1), tail(po, 2, C_CONV - 1)], axis=-1),
                      'c_ssm': ssm_new}
                kvw = D_KV_HEADS * HEAD_DIM
                if fresh:
                    yd = _swa_prompt(po, nseq, t, o_d_sinks[j])
                    keep = min(D_WINDOW, t)
                    ns['d_k'] = tail(po, 4, keep, lanes=slice(0, kvw)).reshape(nseq, keep, D_KV_HEADS, HEAD_DIM)
                    ns['d_v'] = tail(po, 4, keep, lanes=slice(kvw, 2 * kvw)).reshape(
                        nseq, keep, D_KV_HEADS, HEAD_DIM)
                else:
                    yd, ns['d_k'], ns['d_v'] = _swa_sample(po, nseq, t, o_d_sinks[j], st['d_k'], st['d_v'])
                x2 = _post(x2, yc, yd, o_w_out[j], ffn_norm[layer], w_gate[layer], w_up[layer], w_down[layer],
                           final_norm, last)
                new_odd.append(ns)
        out = {k: jnp.stack([ns[k] for ns in new_even]) for k in new_even[0]}
        out.update({k: jnp.stack([ns[k] for ns in new_odd]) for k in new_odd[0]})
        return x2.reshape(nseq, t, D_MODEL), out

    bp, tp, _ = x_prompt.shape
    bs, ts, _ = x_sample.shape
    n_even, n_odd = (depth + 1) // 2, depth // 2
    zeros = lambda *shape: jnp.zeros(shape, F32)
    even_p = {'a_C': zeros(n_even, bp, A_HEADS, A_DK, A_DV), 'a_n': zeros(n_even, bp, A_HEADS, A_DK),
              'a_m': zeros(n_even, bp, A_HEADS), 'a_conv': zeros(n_even, bp, A_CONV - 1, 2 * SLAB)}
    odd_p = {'c_conv': zeros(n_odd, bp, C_CONV - 1, 2 * SLAB),
             'c_ssm': zeros(n_odd, bp, C_HEADS, C_HEADDIM, C_DSTATE)}
    y_p, pn = trunk(x_prompt, jnp.arange(tp, dtype=jnp.int32), True, even_p, odd_p)

    even_s = {'a_C': state_a_C, 'a_n': state_a_n, 'a_m': state_a_m, 'a_conv': state_a_conv,
              'b_k1': cache_b_k1, 'b_v1': cache_b_v1, 'b_k2': cache_b_k2, 'b_v2': cache_b_v2,
              'b_k3': cache_b_k3, 'b_v3': cache_b_v3}
    odd_s = {'c_conv': state_c_conv, 'c_ssm': state_c_ssm, 'd_k': cache_d_k, 'd_v': cache_d_v}
    pos_s = jnp.tile(PAST_LEN + jnp.arange(ts, dtype=jnp.int32), bs)
    y_s, sn = trunk(x_sample, pos_s, False, even_s, odd_s)

    keys = ('a_C', 'a_n', 'a_m', 'a_conv', 'b_k1', 'b_v1', 'b_k2', 'b_v2', 'b_k3', 'b_v3',
            'c_conv', 'c_ssm', 'd_k', 'd_v')
    states = []
    for k in keys:
        states += [pn[k], sn[k]]
    return (y_p, y_s, *states)
```

```python
import functools
import math

import jax
import jax.numpy as jnp
from jax import lax
from jax.experimental import pallas as pl
from jax.experimental.pallas import tpu as pltpu

F32 = jnp.float32
BF16 = jnp.bfloat16

D_MODEL = 1024
PAST_LEN = 16384
EPS = 1e-6
ROPE_THETA = 10000.0
HEAD_DIM = 64
SLAB = 512

A_HEADS = 4
A_DK = 128
A_DV = 128
A_CONV = 4
A_CHUNK = 64

B_HEADS = 8
B_WINDOWS = (128, 512, 2048)
B_DILATIONS = (1, 4, 16)
B_GROUPS = 3

C_HEADS = 8
C_HEADDIM = 64
C_DINNER = C_HEADS * C_HEADDIM
C_DSTATE = 128
C_NGROUPS = 2
C_CONV = 4
C_CHUNK = 128

D_HEADS = 8
D_KV_HEADS = 2
D_WINDOW = 128

NEG = -1e30
VMEM_LIMIT = 56 * 1024 * 1024


def _cparams(*sem):
    return pltpu.CompilerParams(dimension_semantics=sem, vmem_limit_bytes=VMEM_LIMIT)


def _rot_half(y):
    lane = lax.broadcasted_iota(jnp.int32, y.shape, 1)
    fwd = pltpu.roll(y, SLAB - HEAD_DIM // 2, 1)
    bwd = pltpu.roll(y, HEAD_DIM // 2, 1)
    return jnp.where(lane % HEAD_DIM < HEAD_DIM // 2, fwd, bwd)


PERM_TILE = 2048
PROJ_ROWS = 512


def _proj_kernel(x_ref, nw_ref, w_ref, cos_ref, sin_ref, o_ref, xn_ref, *ybuf, kinds):
    j = pl.program_id(1)
    tm = o_ref.shape[0]

    @pl.when(j == 0)
    def _():
        x = x_ref[...]
        ms = jnp.mean(x * x, axis=-1, keepdims=True)
        xn_ref[...] = (x * lax.rsqrt(ms + EPS) * nw_ref[...]).astype(BF16)

    rc = min(tm, PROJ_ROWS)

    def emit(rope_lanes, dil):
        seg = tm // dil
        cseg = rc // dil
        for c in range(tm // rc):
            rows = slice(c * rc, (c + 1) * rc)
            y = jnp.dot(xn_ref[rows, :], w_ref[...], preferred_element_type=F32)
            val = y
            if rope_lanes:
                cos = jnp.tile(cos_ref[rows, :], (1, SLAB // 128))
                sin = jnp.tile(sin_ref[rows, :], (1, SLAB // 128))
                val = y * cos + _rot_half(y) * sin
                if rope_lanes < SLAB:
                    lane = lax.broadcasted_iota(jnp.int32, y.shape, 1)
                    val = jnp.where(lane < rope_lanes, val, y)
            if dil == 1:
                o_ref[rows, :] = val
            else:
                for cb in range(SLAB // 128):
                    ybuf[0][cb, rows, :] = val[:, cb * 128:(cb + 1) * 128]
                for r in range(dil):
                    dst = slice(r * seg + c * cseg, r * seg + (c + 1) * cseg)
                    for cb in range(SLAB // 128):
                        o_ref[dst, cb * 128:(cb + 1) * 128] = ybuf[0][cb, pl.ds(c * rc + r, cseg, stride=dil), :]

    for kind in sorted(set(kinds)):
        member = functools.reduce(jnp.logical_or, [j == s for s, k in enumerate(kinds) if k == kind])
        pl.when(member)(functools.partial(emit, *kind))


def _norm_proj(x, norm_w, w_slabs, cos_t, sin_t, kinds):
    n = x.shape[0]
    s = w_slabs.shape[0]
    p = cos_t.shape[0]
    perm = any(d > 1 for _, d in kinds)
    tm = PERM_TILE if perm else min(1024, n, p)
    assert n % tm == 0 and p % tm == 0 and len(kinds) == s
    npt = p // tm
    scratch = [pltpu.VMEM((tm, D_MODEL), BF16)] + ([pltpu.VMEM((SLAB // 128, tm, 128), F32)] if perm else [])
    return pl.pallas_call(
        functools.partial(_proj_kernel, kinds=tuple(kinds)),
        grid=(n // tm, s),
        in_specs=[
            pl.BlockSpec((tm, D_MODEL), lambda i, j: (i, 0), pipeline_mode=pl.Buffered(1)),
            pl.BlockSpec((1, D_MODEL), lambda i, j: (0, 0)),
            pl.BlockSpec((None, D_MODEL, SLAB), lambda i, j: (j, 0, 0)),
            pl.BlockSpec((tm, 128), lambda i, j: (i % npt, 0)),
            pl.BlockSpec((tm, 128), lambda i, j: (i % npt, 0)),
        ],
        out_specs=pl.BlockSpec((None, tm, SLAB), lambda i, j: (j, i, 0)),
        out_shape=jax.ShapeDtypeStruct((s, n, SLAB), F32),
        scratch_shapes=scratch,
        compiler_params=_cparams("parallel", "arbitrary"),
        name="norm_proj",
    )(x, norm_w.reshape(1, D_MODEL), w_slabs, cos_t, sin_t)


def _even_kinds(perm):
    dil = B_DILATIONS if perm else (1,) * B_GROUPS
    return ([(0, 1)] * 4 + [(SLAB, d) for d in dil] * 2 + [(0, d) for d in dil] + [(0, 1)])


ODD_KINDS = [(0, 1)] * 3 + [(SLAB, 1), (D_KV_HEADS * HEAD_DIM, 1)]


def _rope_tables(pos):
    half = HEAD_DIM // 2
    inv = ROPE_THETA ** (-jnp.arange(half, dtype=F32) / half)
    ang = pos.astype(F32)[:, None] * inv[None, :]
    cos, sin = jnp.cos(ang), jnp.sin(ang)
    return (jnp.concatenate([cos, cos, cos, cos], axis=-1),
            jnp.concatenate([-sin, sin, -sin, sin], axis=-1))


def _pad_cols(w, width):
    return jnp.pad(w, ((0, 0), (0, width - w.shape[1])))


def _even_slabs(w_in):
    main = w_in[:, :2048].reshape(D_MODEL, 4, SLAB)
    gates = _pad_cols(w_in[:, 2048:2056], SLAB)[:, None, :]
    qkv = w_in[:, 2056:].reshape(D_MODEL, 9, SLAB)
    return jnp.moveaxis(jnp.concatenate([main, qkv, gates], axis=1), 1, 0).astype(BF16)


def _group_aligned(w, axis):
    rep = D_HEADS // D_KV_HEADS
    shape = w.shape
    split = shape[:axis] + (D_KV_HEADS, rep, HEAD_DIM) + shape[axis + 1:]
    return jnp.swapaxes(w.reshape(split), axis, axis + 1).reshape(shape)


def _odd_slabs(w_in):
    main = w_in[:, :1536].reshape(D_MODEL, 3, SLAB)
    qd = _group_aligned(w_in[:, 1544:2056], 1)[:, None, :]
    kvd = _pad_cols(jnp.concatenate([w_in[:, 2056:2312], w_in[:, 1536:1544]], axis=1), SLAB)[:, None, :]
    return jnp.moveaxis(jnp.concatenate([main, qd, kvd], axis=1), 1, 0).astype(BF16)


def _split3(x):
    hi = x.astype(BF16)
    r1 = x - hi.astype(F32)
    mid = r1.astype(BF16)
    lo = (r1 - mid.astype(F32)).astype(BF16)
    return hi, mid, lo


def _cumsum_rows(x, tri):
    hi, mid, lo = _split3(x)
    acc = jnp.dot(tri, lo, preferred_element_type=F32)
    acc = acc + jnp.dot(tri, mid, preferred_element_type=F32)
    return acc + jnp.dot(tri, hi, preferred_element_type=F32)


def _transpose_slab(s):
    l = s.shape[0]
    if l < 128:
        s = jnp.concatenate([s, jnp.zeros((128 - l, 128), s.dtype)], axis=0)
    return s.T


def _conv_silu(buf_ref, x, w, b):
    tb = x.shape[0]
    buf_ref[8:8 + tb, :] = x
    y = b + w[3:4] * x
    for j in range(3):
        y = y + w[j:j + 1] * buf_ref[5 + j:5 + j + tb, :]
    buf_ref[0:8, :] = buf_ref[tb:tb + 8, :]
    return y * jax.nn.sigmoid(y)


def _log_sigmoid(x):
    return jnp.minimum(x, 0.0) - jnp.log1p(jnp.exp(-jnp.abs(x)))


def _dot_t(a, b):
    return lax.dot_general(a, b, (((1,), (1,)), ((), ())), preferred_element_type=F32)


def _tdot(a, b):
    return lax.dot_general(a, b, (((0,), (0,)), ((), ())), preferred_element_type=F32)


def _mlstm_kernel(q_ref, k_ref, v_ref, og_ref, g_ref, cw_ref, cb_ref, bg_ref, an_ref,
                  conv0_ref, c0_ref, n0_ref, m0_ref,
                  ya_ref, c_ref, n_ref, m_ref, qbuf, kbuf, *, chunk):
    sb, tb, _ = q_ref.shape
    tblk = pl.program_id(1)
    lane = lax.broadcasted_iota(jnp.int32, (chunk, 128), 1)
    lane1 = lax.broadcasted_iota(jnp.int32, (1, 128), 1)
    row = lax.broadcasted_iota(jnp.int32, (chunk, chunk), 0)
    col = lax.broadcasted_iota(jnp.int32, (chunk, chunk), 1)
    causal = col <= row
    tri = causal.astype(BF16)
    scale = A_DK ** -0.5

    @pl.when(tblk == 0)
    def _():
        c_ref[...] = c0_ref[...]
        n_ref[...] = n0_ref[...]
        m_ref[...] = m0_ref[...]
        for si in range(sb):
            qbuf[si, 0:8, :] = conv0_ref[si, :, 0:SLAB]
            kbuf[si, 0:8, :] = conv0_ref[si, :, SLAB:2 * SLAB]

    for si in range(sb):
        qc = _conv_silu(qbuf.at[si], q_ref[si], cw_ref[:, 0:SLAB], cb_ref[:, 0:SLAB])
        kc = _conv_silu(kbuf.at[si], k_ref[si], cw_ref[:, SLAB:2 * SLAB], cb_ref[:, SLAB:2 * SLAB]) * scale
        m_vec = m_ref[si]
        for c in range(tb // chunk):
            r0 = c * chunk
            gates = g_ref[si, r0:r0 + chunk, 0:128] + bg_ref[...]
            bc = _cumsum_rows(_log_sigmoid(gates), tri)
            st = _transpose_slab(jnp.where(lane < A_HEADS, gates, bc))
            m_next = m_vec
            for h in range(A_HEADS):
                hs = slice(h * 128, (h + 1) * 128)
                qh = qc[r0:r0 + chunk, hs]
                kh = kc[r0:r0 + chunk, hs]
                vh = v_ref[si, r0:r0 + chunk, hs]
                qb, kb, vb = qh.astype(BF16), kh.astype(BF16), vh.astype(BF16)
                b_t = bc[:, A_HEADS + h:A_HEADS + h + 1]
                li_t = gates[:, h:h + 1]
                li_s = st[h:h + 1, 0:chunk]
                b_s = st[A_HEADS + h:A_HEADS + h + 1, 0:chunk]
                m_h = m_vec[:, h:h + 1]
                dlog = jnp.where(causal, b_t - b_s + li_s, NEG)
                inter = b_t + m_h
                m_t = jnp.maximum(inter, jnp.max(dlog, axis=1, keepdims=True))
                w_intra = jnp.exp(dlog - m_t)
                w_inter = jnp.exp(inter - m_t)
                s = _dot_t(qb, kb) * w_intra
                c_st = c_ref[si, h]
                n_st = n_ref[si, h:h + 1, :]
                num = jnp.dot(s.astype(BF16), vb, preferred_element_type=F32) \
                    + w_inter * jnp.dot(qb, c_st.astype(BF16), preferred_element_type=F32)
                den = jnp.sum(s, axis=1, keepdims=True) + w_inter * jnp.sum(qh * n_st, axis=1, keepdims=True)
                hh = num / jnp.maximum(jnp.abs(den), jnp.exp(-m_t))
                b_end = b_t[chunk - 1:chunk, :]
                g_col = b_end - b_t + li_t
                m_new = jnp.maximum(b_end + m_h, jnp.max(g_col, axis=0, keepdims=True))
                w_state = jnp.exp(g_col - m_new)
                decay = jnp.exp(b_end + m_h - m_new)
                kw = kh * w_state
                c_ref[si, h] = decay * c_st + _tdot(kw.astype(BF16), vb)
                n_ref[si, h:h + 1, :] = decay * n_st + jnp.sum(kw, axis=0, keepdims=True)
                m_next = jnp.where(lane1 == h, m_new, m_next)
                hn = hh * lax.rsqrt(jnp.mean(hh * hh, axis=-1, keepdims=True) + EPS) * an_ref[:, hs]
                ya_ref[si, r0:r0 + chunk, hs] = hn * jax.nn.sigmoid(og_ref[si, r0:r0 + chunk, hs])
            m_vec = m_next
        m_ref[si] = m_vec


def _mlstm(proj, nseq, t, conv_w, conv_b, b_gates, a_norm, conv0, c0, n0, m0):
    chunk = A_CHUNK if t % A_CHUNK == 0 else t
    if t > chunk:
        sb, tb = 1, min(t, 4 * chunk)
    else:
        sb, tb = min(nseq, 8), t
    p4 = proj.reshape(proj.shape[0], nseq, t, SLAB)
    slab = lambda s: pl.BlockSpec((None, sb, tb, SLAB), lambda i, j, s=s: (s, i, j, 0))
    full = lambda a: pl.BlockSpec(a.shape, lambda i, j: (0,) * a.ndim)
    st = lambda a: pl.BlockSpec((sb,) + a.shape[1:], lambda i, j: (i,) + (0,) * (a.ndim - 1))
    bg = _pad_cols(b_gates.reshape(1, 2 * A_HEADS), 128)
    conv0p = jnp.pad(conv0, ((0, 0), (8 - (A_CONV - 1), 0), (0, 0)))
    m0p = _pad_cols(m0, 128).reshape(nseq, 1, 128)
    cb = conv_b.reshape(1, -1)
    an = a_norm.reshape(1, -1)
    ya, c, n, m = pl.pallas_call(
        functools.partial(_mlstm_kernel, chunk=chunk),
        grid=(nseq // sb, t // tb),
        in_specs=[slab(0), slab(1), slab(2), slab(3), slab(13),
                  full(conv_w), full(cb), full(bg), full(an),
                  st(conv0p), st(c0), st(n0), st(m0p)],
        out_specs=[pl.BlockSpec((sb, tb, SLAB), lambda i, j: (i, j, 0)), st(c0), st(n0), st(m0p)],
        out_shape=[jax.ShapeDtypeStruct((nseq, t, SLAB), F32),
                   jax.ShapeDtypeStruct(c0.shape, F32), jax.ShapeDtypeStruct(n0.shape, F32),
                   jax.ShapeDtypeStruct(m0p.shape, F32)],
        scratch_shapes=[pltpu.VMEM((sb, 8 + tb, SLAB), F32), pltpu.VMEM((sb, 8 + tb, SLAB), F32)],
        compiler_params=_cparams("parallel", "arbitrary"),
        name="mlstm",
    )(p4, p4, p4, p4, p4, conv_w, cb, bg, an, conv0p, c0, n0, m0p)
    return ya.reshape(nseq * t, SLAB), c, n, m[:, 0, :A_HEADS]


QBLK = 128


def _band_head(q_m, k2, v_m, valid, sink):
    s = jnp.where(valid, _dot_t(q_m, k2), NEG)
    m = jnp.max(s, axis=1, keepdims=True)
    if sink is not None:
        m = jnp.maximum(m, sink)
    e = jnp.exp(s - m)
    l = jnp.sum(e, axis=1, keepdims=True)
    if sink is not None:
        l = l + jnp.exp(sink - m)
    acc = jnp.dot(e.astype(BF16), v_m, preferred_element_type=F32)
    return acc, m, l


def _band_masks(i, window):
    a = lax.broadcasted_iota(jnp.int32, (QBLK, 2 * QBLK), 0)
    c = lax.broadcasted_iota(jnp.int32, (QBLK, 2 * QBLK), 1)
    rel = a - c + QBLK
    first_col = jnp.where(i > 0, 0, QBLK)
    return jnp.logical_and(jnp.logical_and(rel >= 0, rel <= window), c >= first_col)


def _dilated_kernel(q_ref, kp_ref, kc_ref, vp_ref, vc_ref, o_ref, lse_ref):
    valid = _band_masks(pl.program_id(2), QBLK)
    lane = lax.broadcasted_iota(jnp.int32, (1, 128), 1)
    scale = HEAD_DIM ** -0.5
    for pr in range(B_HEADS // 2):
        ls = slice(pr * 128, (pr + 1) * 128)
        qp = q_ref[:, ls] * scale
        k2 = jnp.concatenate([kp_ref[:, ls], kc_ref[:, ls]], axis=0).astype(BF16)
        v2 = jnp.concatenate([vp_ref[:, ls], vc_ref[:, ls]], axis=0).astype(BF16)
        o_h, lse_h = [], []
        for p in range(2):
            half = (lane < HEAD_DIM) if p == 0 else (lane >= HEAD_DIM)
            acc, m, l = _band_head(jnp.where(half, qp, 0.0).astype(BF16), k2, v2, valid, None)
            o_h.append(acc / l)
            lse_h.append(m + jnp.log(l))
        low = lane < HEAD_DIM
        o_ref[:, ls] = jnp.where(low, o_h[0], o_h[1])
        lse_ref[:, ls] = jnp.where(low, lse_h[0], lse_h[1])


def _dilated_prompt(proj, b, t, g):
    d = B_DILATIONS[g]
    assert t % PERM_TILE == 0
    nb = PERM_TILE // d // QBLK
    per_tile = PERM_TILE // QBLK
    pv = proj.reshape(proj.shape[0], b, t, SLAB)

    def blk(i, r):
        return (i // nb) * per_tile + r * nb + i % nb

    cur = lambda s: pl.BlockSpec((None, None, QBLK, SLAB), lambda bi, r, i, s=s: (s, bi, blk(i, r), 0))
    prev = lambda s: pl.BlockSpec((None, None, QBLK, SLAB),
                                  lambda bi, r, i, s=s: (s, bi, blk(jnp.maximum(i - 1, 0), r), 0))
    ospec = pl.BlockSpec((None, QBLK, SLAB), lambda bi, r, i: (bi, blk(i, r), 0))
    sq, sk, sv = 4 + g, 7 + g, 10 + g
    o, lse = pl.pallas_call(
        _dilated_kernel,
        grid=(b, d, t // d // QBLK),
        in_specs=[cur(sq), prev(sk), cur(sk), prev(sv), cur(sv)],
        out_specs=[ospec, ospec],
        out_shape=[jax.ShapeDtypeStruct((b, t, SLAB), F32)] * 2,
        compiler_params=_cparams("parallel", "parallel", "arbitrary"),
        name=f"dilated_g{g}",
    )(pv, pv, pv, pv, pv)
    return o.reshape(b * t, SLAB), lse.reshape(b * t, SLAB)


def _merge_kernel(*refs, dils):
    ins, y_ref, bufs = refs[:6], refs[6], refs[7:]
    tm = y_ref.shape[0]
    vals = []
    nbuf = 0
    for k, ref in enumerate(ins):
        dil = dils[k // 2]
        if dil == 1:
            vals.append(ref[...])
        else:
            buf = bufs[nbuf]
            nbuf += 1
            seg = tm // dil
            for r in range(dil):
                buf[pl.ds(r, seg, stride=dil), :] = ref[r * seg:(r + 1) * seg, :]
            vals.append(buf[...])
    o1, l1, o2, l2, o3, l3 = vals
    m = jnp.maximum(jnp.maximum(l1, l2), l3)
    w1, w2, w3 = jnp.exp(l1 - m), jnp.exp(l2 - m), jnp.exp(l3 - m)
    y_ref[...] = (w1 * o1 + w2 * o2 + w3 * o3) / (w1 + w2 + w3)


def _merge_groups(parts, perm):
    n = parts[0].shape[0]
    dils = B_DILATIONS if perm else (1,) * B_GROUPS
    tm = PERM_TILE if perm else min(1024, n)
    spec = pl.BlockSpec((tm, 128), lambda i, j: (i, j))
    nbuf = 2 * sum(d > 1 for d in dils)
    return pl.pallas_call(
        functools.partial(_merge_kernel, dils=dils), grid=(n // tm, SLAB // 128), in_specs=[spec] * 6,
        out_specs=spec, out_shape=jax.ShapeDtypeStruct((n, SLAB), F32),
        scratch_shapes=[pltpu.VMEM((tm, 128), F32)] * nbuf,
        compiler_params=_cparams("parallel", "parallel"), name="merge_groups",
    )(*parts)


def _swa_kernel(q_ref, kp_ref, kc_ref, vp_ref, vc_ref, sink_ref, o_ref):
    valid = _band_masks(pl.program_id(1), D_WINDOW - 1)
    lane = lax.broadcasted_iota(jnp.int32, (1, 128), 1)
    scale = HEAD_DIM ** -0.5
    k2 = jnp.concatenate([kp_ref[...], kc_ref[...]], axis=0).astype(BF16)
    v2 = jnp.concatenate([vp_ref[...], vc_ref[...]], axis=0).astype(BF16)
    rep = D_HEADS // D_KV_HEADS
    low = lane < HEAD_DIM
    for pr in range(rep):
        ls = slice(pr * 128, (pr + 1) * 128)
        qp = q_ref[:, ls] * scale
        o_h = []
        for g in range(D_KV_HEADS):
            half = low if g == 0 else jnp.logical_not(low)
            acc, m, l = _band_head(jnp.where(half, qp, 0.0).astype(BF16), k2, v2, valid, sink_ref[g * rep + pr])
            o_h.append(acc / l)
        o_ref[:, ls] = jnp.where(low, o_h[0], o_h[1])


def _swa_prompt(proj, b, t, sinks):
    pv = proj.reshape(proj.shape[0], b, t, SLAB)
    qspec = pl.BlockSpec((None, None, QBLK, SLAB), lambda bi, i: (3, bi, i, 0))
    kv = lambda col, back: pl.BlockSpec((None, None, QBLK, 128),
                                        lambda bi, i: (4, bi, jnp.maximum(i - back, 0), col))
    o = pl.pallas_call(
        _swa_kernel,
        grid=(b, t // QBLK),
        in_specs=[qspec, kv(0, 1), kv(0, 0), kv(1, 1), kv(1, 0), pl.BlockSpec(memory_space=pltpu.SMEM)],
        out_specs=pl.BlockSpec((None, QBLK, SLAB), lambda bi, i: (bi, i, 0)),
        out_shape=jax.ShapeDtypeStruct((b, t, SLAB), F32),
        compiler_params=_cparams("parallel", "arbitrary"),
        name="swa_prompt",
    )(pv, pv, pv, pv, pv, sinks)
    return o.reshape(b * t, SLAB)


def _softplus(x):
    return jnp.maximum(x, 0.0) + jnp.log1p(jnp.exp(-jnp.abs(x)))


def _ssd_kernel(z_ref, x_ref, bc_ref, dt_ref, cw_ref, cb_ref, dtb_ref, alog_ref, dl_ref, cn_ref,
                conv0_ref, h0_ref, y_ref, h_ref, xbuf, bcbuf, *, chunk):
    sb, tb, _ = x_ref.shape
    tblk = pl.program_id(1)
    lane1 = lax.broadcasted_iota(jnp.int32, (1, 128), 1)
    low = lane1 < C_HEADDIM
    row = lax.broadcasted_iota(jnp.int32, (chunk, chunk), 0)
    col = lax.broadcasted_iota(jnp.int32, (chunk, chunk), 1)
    causal = col <= row
    tri = causal.astype(BF16)
    srow_low = lax.broadcasted_iota(jnp.int32, (128, 1), 0) < C_HEADDIM
    hpg = C_HEADS // C_NGROUPS
    a_neg = -jnp.exp(alog_ref[...])

    @pl.when(tblk == 0)
    def _():
        h_ref[...] = h0_ref[...]
        for si in range(sb):
            xbuf[si, 0:8, :] = conv0_ref[si, :, 0:SLAB]
            bcbuf[si, 0:8, :] = conv0_ref[si, :, SLAB:2 * SLAB]

    for si in range(sb):
        xc = _conv_silu(xbuf.at[si], x_ref[si], cw_ref[:, 0:SLAB], cb_ref[:, 0:SLAB])
        bcc = _conv_silu(bcbuf.at[si], bc_ref[si], cw_ref[:, SLAB:2 * SLAB], cb_ref[:, SLAB:2 * SLAB])
        for c in range(tb // chunk):
            rs = slice(c * chunk, (c + 1) * chunk)
            dt = _softplus(dt_ref[si, rs, :] + dtb_ref[...])
            cum = _cumsum_rows(dt * a_neg, tri)
            cum_t_all = _transpose_slab(cum)
            dt_t_all = _transpose_slab(dt)
            cum_end = cum[chunk - 1:chunk, :]
            for g in range(C_NGROUPS):
                bg = bcc[rs, g * 128:(g + 1) * 128].astype(BF16)
                cg = bcc[rs, 256 + g * 128:256 + (g + 1) * 128].astype(BF16)
                cb = _dot_t(cg, bg)
                y_pairs = []
                for pp in range(hpg // 2):
                    pr = g * (hpg // 2) + pp
                    ls = slice(pr * 128, (pr + 1) * 128)
                    x_pair = xc[rs, ls]
                    hs = h_ref[si, pr]
                    y_pair = None
                    e_col, w_end, dec = [], [], []
                    for p in range(2):
                        h = 2 * pr + p
                        cum_t = cum[:, h:h + 1]
                        seg = jnp.where(causal, cum_t - cum_t_all[h:h + 1, 0:chunk], NEG)
                        w = cb * jnp.exp(seg) * dt_t_all[h:h + 1, 0:chunk]
                        half = low if p == 0 else jnp.logical_not(low)
                        yh = jnp.dot(w.astype(BF16), jnp.where(half, x_pair, 0.0).astype(BF16),
                                     preferred_element_type=F32)
                        y_pair = yh if y_pair is None else y_pair + yh
                        e_col.append(jnp.exp(cum_t))
                        w_end.append(jnp.exp(cum_end[:, h:h + 1] - cum_t) * dt[:, h:h + 1])
                        dec.append(jnp.exp(cum_end[:, h:h + 1]))
                    inter = _dot_t(cg, hs.astype(BF16)) * jnp.where(low, e_col[0], e_col[1])
                    y_pair = y_pair + inter + dl_ref[:, ls] * x_pair
                    xw = x_pair * jnp.where(low, w_end[0], w_end[1])
                    h_ref[si, pr] = jnp.where(srow_low, dec[0], dec[1]) * hs + _tdot(xw.astype(BF16), bg)
                    zz = z_ref[si, rs, ls]
                    y_pairs.append(y_pair * (zz * jax.nn.sigmoid(zz)))
                ms = sum(jnp.sum(yp * yp, axis=-1, keepdims=True) for yp in y_pairs) / (hpg * C_HEADDIM)
                inv = lax.rsqrt(ms + EPS)
                for pp, yp in enumerate(y_pairs):
                    ls = slice((g * (hpg // 2) + pp) * 128, (g * (hpg // 2) + pp + 1) * 128)
                    y_ref[si, rs, ls] = yp * inv * cn_ref[:, ls]


def _ssd(proj, nseq, t, conv_w, conv_b, dt_bias, a_log, d_skip, c_norm, conv0, h0):
    chunk = C_CHUNK if t % C_CHUNK == 0 else t
    if t > chunk:
        sb, tb = min(nseq, 2), min(t, 2 * chunk)
    else:
        sb, tb = min(nseq, 8), t
    p4 = proj.reshape(proj.shape[0], nseq, t, SLAB)
    slab = lambda s: pl.BlockSpec((None, sb, tb, SLAB), lambda i, j, s=s: (s, i, j, 0))
    full = lambda a: pl.BlockSpec(a.shape, lambda i, j: (0,) * a.ndim)
    st = lambda a: pl.BlockSpec((sb,) + a.shape[1:], lambda i, j: (i,) + (0,) * (a.ndim - 1))
    conv0p = jnp.pad(conv0, ((0, 0), (8 - (C_CONV - 1), 0), (0, 0)))
    hp = h0.reshape(nseq, C_HEADS // 2, 128, C_DSTATE)
    cb = conv_b.reshape(1, -1)
    dtb = _pad_cols(dt_bias.reshape(1, C_HEADS), 128)
    alog = _pad_cols(a_log.reshape(1, C_HEADS), 128)
    dl = jnp.repeat(d_skip, C_HEADDIM).reshape(1, C_DINNER)
    cn = c_norm.reshape(1, C_DINNER)
    y, h = pl.pallas_call(
        functools.partial(_ssd_kernel, chunk=chunk),
        grid=(nseq // sb, t // tb),
        in_specs=[slab(0), slab(1), slab(2),
                  pl.BlockSpec((None, sb, tb, 128), lambda i, j: (4, i, j, 2)),
                  full(conv_w), full(cb), full(dtb), full(alog), full(dl), full(cn), st(conv0p), st(hp)],
        out_specs=[pl.BlockSpec((sb, tb, SLAB), lambda i, j: (i, j, 0)), st(hp)],
        out_shape=[jax.ShapeDtypeStruct((nseq, t, SLAB), F32), jax.ShapeDtypeStruct(hp.shape, F32)],
        scratch_shapes=[pltpu.VMEM((sb, 8 + tb, SLAB), F32), pltpu.VMEM((sb, 8 + tb, SLAB), F32)],
        compiler_params=_cparams("parallel", "arbitrary"),
        name="ssd",
    )(p4, p4, p4, p4, conv_w, cb, dtb, alog, dl, cn, conv0p, hp)
    return y.reshape(nseq * t, SLAB), h.reshape(nseq, C_HEADS, C_HEADDIM, C_DSTATE)


FFN_CHUNK = 256


def _post_kernel(x_ref, ya_ref, yb_ref, wo_ref, fn_ref, wg_ref, wu_ref, wd_ref, on_ref, o_ref, act_ref, *, final):
    ymix = jnp.concatenate([ya_ref[...], yb_ref[...]], axis=-1).astype(BF16)
    x1 = x_ref[...] + jnp.dot(ymix, wo_ref[...], preferred_element_type=F32)
    ms = jnp.mean(x1 * x1, axis=-1, keepdims=True)
    hn = (x1 * lax.rsqrt(ms + EPS) * fn_ref[...]).astype(BF16)
    hidden = wg_ref.shape[1]
    for c in range(hidden // FFN_CHUNK):
        cs = slice(c * FFN_CHUNK, (c + 1) * FFN_CHUNK)
        gate = jnp.dot(hn, wg_ref[:, cs], preferred_element_type=F32)
        up = jnp.dot(hn, wu_ref[:, cs], preferred_element_type=F32)
        act_ref[:, cs] = (gate * jax.nn.sigmoid(gate) * up).astype(BF16)
    x2 = x1 + jnp.dot(act_ref[...], wd_ref[...], preferred_element_type=F32)
    if final:
        ms2 = jnp.mean(x2 * x2, axis=-1, keepdims=True)
        x2 = x2 * lax.rsqrt(ms2 + EPS) * on_ref[...]
    o_ref[...] = x2


def _post(x, ya, yb, w_out, ffn_norm, w_gate, w_up, w_down, out_norm, final):
    n = x.shape[0]
    tm = min(512, n)
    hidden = w_gate.shape[1]
    row = lambda w: pl.BlockSpec((tm, w), lambda i: (i, 0))
    res = lambda a: pl.BlockSpec(a.shape, lambda i: (0, 0), pipeline_mode=pl.Buffered(1))
    fn = ffn_norm.reshape(1, D_MODEL)
    on = out_norm.reshape(1, D_MODEL)
    wo, wg, wu, wd = (w.astype(BF16) for w in (w_out, w_gate, w_up, w_down))
    return pl.pallas_call(
        functools.partial(_post_kernel, final=final),
        grid=(n // tm,),
        in_specs=[row(D_MODEL), row(SLAB), row(SLAB), res(wo), res(fn), res(wg), res(wu), res(wd), res(on)],
        out_specs=row(D_MODEL),
        out_shape=jax.ShapeDtypeStruct((n, D_MODEL), F32),
        scratch_shapes=[pltpu.VMEM((tm, hidden), BF16)],
        compiler_params=_cparams("parallel"),
        name="post_ffn",
    )(x, ya, yb, wo, fn, wg, wu, wd, on)


NEWPAD = 128


def _ext_mask(rows, r, ts, window, step):
    t = lax.broadcasted_iota(jnp.int32, (rows, r + NEWPAD), 0) % ts
    c = lax.broadcasted_iota(jnp.int32, (rows, r + NEWPAD), 1)
    rel = r + t - c
    ok = jnp.logical_and(jnp.logical_and(rel >= 0, rel <= window), c < r + ts)
    return jnp.logical_and(ok, (rel & (step - 1)) == 0)


def _shift_cache(dst_ref, src_ref, new, si, r, ts):
    dst_ref[si, 0:r - ts, :] = src_ref[si, ts:r, :]
    dst_ref[si, r - ts:r, :] = new


def _pad_new(x):
    return jnp.concatenate([x, jnp.zeros((NEWPAD - x.shape[0], x.shape[1]), x.dtype)], axis=0)


def _dil_sample_kernel(q_ref, kn_ref, vn_ref, kc_ref, vc_ref, o_ref, lse_ref, ko_ref, vo_ref,
                       *, window, step):
    sb, ts, _ = q_ref.shape
    r = kc_ref.shape[2]
    scale = HEAD_DIM ** -0.5
    lane = lax.broadcasted_iota(jnp.int32, (1, 128), 1)
    low = lane < HEAD_DIM
    t = lax.broadcasted_iota(jnp.int32, (2 * ts, r + 128), 0) % ts
    c = lax.broadcasted_iota(jnp.int32, (2 * ts, r + 128), 1)
    rel = r + t - jnp.where(c < 128, c, c - 128 + ts)
    valid = jnp.logical_and(jnp.logical_and(rel >= 0, rel <= window), (rel & (step - 1)) == 0)
    valid = jnp.logical_and(valid, jnp.logical_or(c < ts, c >= 128))
    pad = jnp.zeros((128 - ts, SLAB), F32)

    def shifted(src_ref, new_t, rows):
        old = src_ref[rows, :]
        rolled = pltpu.roll(old, r - ts, 1)
        tail = jnp.where(lane < 128 - ts, rolled[:, r - 128:], new_t[rows, :])
        out = tail if r == 128 else jnp.concatenate([rolled[:, :r - 128], tail], axis=1)
        return old, out

    for si, pr in [(si, pr) for si in range(sb) for pr in range(B_HEADS // 2)]:
        if pr == 0:
            kn_t = jnp.concatenate([pad, kn_ref[si]], axis=0).T
            vn_t = jnp.concatenate([pad, vn_ref[si]], axis=0).T
        rows = slice(pr * 128, (pr + 1) * 128)
        k_old, k_out = shifted(kc_ref.at[si], kn_t, rows)
        v_old, v_out = shifted(vc_ref.at[si], vn_t, rows)
        ko_ref[si, rows, :] = k_out
        vo_ref[si, rows, :] = v_out
        k_ext = jnp.concatenate([k_old[:, 0:128], k_out], axis=1).astype(BF16)
        v_ext = jnp.concatenate([v_old[:, 0:128], v_out], axis=1).astype(BF16)
        qp = q_ref[si, :, rows] * scale
        q2 = jnp.concatenate([jnp.where(low, qp, 0.0), jnp.where(low, 0.0, qp)], axis=0).astype(BF16)
        s = jnp.where(valid, jnp.dot(q2, k_ext, preferred_element_type=F32), NEG)
        m = jnp.max(s, axis=1, keepdims=True)
        e = jnp.exp(s - m)
        l = jnp.sum(e, axis=1, keepdims=True)
        a = _dot_t(e.astype(BF16), v_ext) / l
        lse = m + jnp.log(l)
        o_ref[si, :, rows] = jnp.where(low, a[0:ts], a[ts:2 * ts])
        lse_ref[si, :, rows] = jnp.where(low, lse[0:ts], lse[ts:2 * ts])


def _rows_minor(cache):
    nseq, r = cache.shape[:2]
    return jnp.transpose(cache, (0, 2, 3, 1)).reshape(nseq, -1, r)


def _rows_major(cache_t, heads):
    nseq, _, r = cache_t.shape
    return jnp.transpose(cache_t.reshape(nseq, heads, HEAD_DIM, r), (0, 3, 1, 2))


def _dilated_sample(proj, nseq, ts, g, k_cache, v_cache):
    r = k_cache.shape[1]
    assert ts % 8 == 0 and r % 128 == 0 and r >= B_WINDOWS[g]
    kt, vt = _rows_minor(k_cache), _rows_minor(v_cache)
    sb = max(1, min(nseq, 8, 1024 // r))
    p4 = proj.reshape(proj.shape[0], nseq, ts, SLAB)
    new = lambda s: pl.BlockSpec((None, sb, ts, SLAB), lambda i, s=s: (s, i, 0, 0))
    cspec = pl.BlockSpec((sb, SLAB, r), lambda i: (i, 0, 0))
    ospec = pl.BlockSpec((sb, ts, SLAB), lambda i: (i, 0, 0))
    o, lse, kn, vn = pl.pallas_call(
        functools.partial(_dil_sample_kernel, window=B_WINDOWS[g], step=B_DILATIONS[g]),
        grid=(nseq // sb,),
        in_specs=[new(4 + g), new(7 + g), new(10 + g), cspec, cspec],
        out_specs=[ospec, ospec, cspec, cspec],
        out_shape=[jax.ShapeDtypeStruct((nseq, ts, SLAB), F32)] * 2 + [jax.ShapeDtypeStruct(kt.shape, F32)] * 2,
        compiler_params=_cparams("parallel"),
        name=f"dilated_sample_g{g}",
    )(p4, p4, p4, kt, vt)
    return (o.reshape(nseq * ts, SLAB), lse.reshape(nseq * ts, SLAB),
            _rows_major(kn, B_HEADS), _rows_major(vn, B_HEADS))


def _swa_sample_kernel(q_ref, k_ref, v_ref, kc_ref, vc_ref, sink_ref, o_ref, kn_ref, vn_ref):
    sb, ts, _ = q_ref.shape
    r = kc_ref.shape[1]
    rows = D_HEADS * ts
    valid = _ext_mask(rows, r, ts, D_WINDOW - 1, 1)
    lane = lax.broadcasted_iota(jnp.int32, (1, 128), 1)
    low = lane < HEAD_DIM
    rep = D_HEADS // D_KV_HEADS
    scale = HEAD_DIM ** -0.5
    sink = sink_ref[:, 0:1]
    for si in range(sb):
        kn, vn = k_ref[si], v_ref[si]
        pieces = []
        for pr in range(rep):
            base = q_ref[si, :, pr * 128:(pr + 1) * 128] * scale
            for g in range(D_KV_HEADS):
                pieces.append(jnp.where(low if g == 0 else jnp.logical_not(low), base, 0.0))
        qrows = jnp.concatenate(pieces, axis=0).astype(BF16)
        kext = jnp.concatenate([kc_ref[si], _pad_new(kn)], axis=0).astype(BF16)
        vext = jnp.concatenate([vc_ref[si], _pad_new(vn)], axis=0).astype(BF16)
        s = jnp.where(valid, _dot_t(qrows, kext), NEG)
        m = jnp.maximum(jnp.max(s, axis=1, keepdims=True), sink)
        e = jnp.exp(s - m)
        l = jnp.sum(e, axis=1, keepdims=True) + jnp.exp(sink - m)
        o = jnp.dot(e.astype(BF16), vext, preferred_element_type=F32) / l
        pairs = [jnp.where(low, o[2 * pr * ts:(2 * pr + 1) * ts], o[(2 * pr + 1) * ts:(2 * pr + 2) * ts])
                 for pr in range(rep)]
        o_ref[si] = jnp.concatenate(pairs, axis=1)
        _shift_cache(kn_ref, kc_ref, kn, si, r, ts)
        _shift_cache(vn_ref, vc_ref, vn, si, r, ts)


def _swa_sample(proj, nseq, ts, sinks, k_cache, v_cache):
    r = k_cache.shape[1]
    sb = min(nseq, 8)
    kvw = D_KV_HEADS * HEAD_DIM
    p4 = proj.reshape(proj.shape[0], nseq, ts, SLAB)
    kc = k_cache.reshape(nseq, r, kvw)
    vc = v_cache.reshape(nseq, r, kvw)
    cspec = pl.BlockSpec((sb, r, kvw), lambda i: (i, 0, 0))
    new = lambda col: pl.BlockSpec((None, sb, ts, kvw), lambda i: (4, i, 0, col))
    rep = D_HEADS // D_KV_HEADS
    by_row_group = sinks.reshape(D_KV_HEADS, rep).T.reshape(-1)
    sink_rows = jnp.broadcast_to(jnp.repeat(by_row_group, ts)[:, None], (D_HEADS * ts, 128))
    o, kn, vn = pl.pallas_call(
        _swa_sample_kernel,
        grid=(nseq // sb,),
        in_specs=[pl.BlockSpec((None, sb, ts, SLAB), lambda i: (3, i, 0, 0)), new(0), new(1), cspec, cspec,
                  pl.BlockSpec(sink_rows.shape, lambda i: (0, 0))],
        out_specs=[pl.BlockSpec((sb, ts, SLAB), lambda i: (i, 0, 0)), cspec, cspec],
        out_shape=[jax.ShapeDtypeStruct((nseq, ts, SLAB), F32)] + [jax.ShapeDtypeStruct(kc.shape, F32)] * 2,
        compiler_params=_cparams("parallel"),
        name="swa_sample",
    )(p4, p4, p4, kc, vc, sink_rows)
    return o.reshape(nseq * ts, SLAB), kn.reshape(k_cache.shape), vn.reshape(v_cache.shape)


def kernel(x_prompt, x_sample, state_a_C, state_a_n, state_a_m, state_a_conv, cache_b_k1, cache_b_v1, cache_b_k2, cache_b_v2, cache_b_k3, cache_b_v3, state_c_conv, state_c_ssm, cache_d_k, cache_d_v, e_norm_mix, e_w_in, e_b_gates, e_a_conv_w, e_a_conv_b, e_a_norm, e_w_out, o_norm_mix, o_w_in, o_c_conv_w, o_c_conv_b, o_c_dt_bias, o_c_A_log, o_c_D, o_c_norm, o_d_sinks, o_w_out, ffn_norm, w_gate, w_up, w_down, final_norm):
    depth = ffn_norm.shape[0]
    even_w = [_even_slabs(e_w_in[j]) for j in range((depth + 1) // 2)]
    odd_w = [_odd_slabs(o_w_in[j]) for j in range(depth // 2)]
    odd_wo = [jnp.concatenate([o_w_out[j][:C_DINNER], _group_aligned(o_w_out[j][C_DINNER:], 0)], axis=0)
              for j in range(depth // 2)]

    def trunk(x, pos, fresh, even_st, odd_st):
        nseq, t, _ = x.shape
        assert t >= A_CONV - 1 and t >= C_CONV - 1
        cos_t, sin_t = _rope_tables(pos)
        x2 = x.reshape(nseq * t, D_MODEL)
        new_even, new_odd = [], []

        def tail(proj, s, keep, dil=1, lanes=slice(None)):
            p4 = proj.reshape(proj.shape[0], nseq, t, SLAB)
            if dil == 1:
                return p4[s, :, t - keep:, lanes]
            seg = PERM_TILE // dil
            last = p4[s, :, t - PERM_TILE:, lanes].reshape(nseq, dil, seg, -1)[:, :, seg - keep // dil:]
            return jnp.swapaxes(last, 1, 2).reshape(nseq, keep, -1)

        for layer in range(depth):
            j = layer // 2
            last = layer == depth - 1
            if layer % 2 == 0:
                pe = _norm_proj(x2, e_norm_mix[j], even_w[j], cos_t, sin_t, _even_kinds(fresh))
                st = {k: v[j] for k, v in even_st.items()}
                ya, c_new, n_new, m_new = _mlstm(pe, nseq, t, e_a_conv_w[j], e_a_conv_b[j], e_b_gates[j],
                                                 e_a_norm[j], st['a_conv'], st['a_C'], st['a_n'], st['a_m'])
                ns = {'a_C': c_new, 'a_n': n_new, 'a_m': m_new,
                      'a_conv': jnp.concatenate([tail(pe, 0, A_CONV - 1), tail(pe, 1, A_CONV - 1)], axis=-1)}
                parts = []
                for g in range(B_GROUPS):
                    kn, vn = f'b_k{g + 1}', f'b_v{g + 1}'
                    if fresh:
                        o_g, lse_g = _dilated_prompt(pe, nseq, t, g)
                        keep = min(B_WINDOWS[g], t)
                        ns[kn] = tail(pe, 7 + g, keep, B_DILATIONS[g]).reshape(nseq, keep, B_HEADS, HEAD_DIM)
                        ns[vn] = tail(pe, 10 + g, keep, B_DILATIONS[g]).reshape(nseq, keep, B_HEADS, HEAD_DIM)
                    else:
                        o_g, lse_g, ns[kn], ns[vn] = _dilated_sample(pe, nseq, t, g, st[kn], st[vn])
                    parts += [o_g, lse_g]
                yb = _merge_groups(parts, fresh)
                x2 = _post(x2, ya, yb, e_w_out[j], ffn_norm[layer], w_gate[layer], w_up[layer], w_down[layer],
                           final_norm, last)
                new_even.append(ns)
            else:
                po = _norm_proj(x2, o_norm_mix[j], odd_w[j], cos_t, sin_t, ODD_KINDS)
                st = {k: v[j] for k, v in odd_st.items()}
                yc, ssm_new = _ssd(po, nseq, t, o_c_conv_w[j], o_c_conv_b[j], o_c_dt_bias[j], o_c_A_log[j],
                                   o_c_D[j], o_c_norm[j], st['c_conv'], st['c_ssm'])
                ns = {'c_conv': jnp.concatenate([tail(po, 1, C_CONV - 1), tail(po, 2, C_CONV - 1)], axis=-1),
                      'c_ssm': ssm_new}
                kvw = D_KV_HEADS * HEAD_DIM
                if fresh:
                    yd = _swa_prompt(po, nseq, t, o_d_sinks[j])
                    keep = min(D_WINDOW, t)
                    ns['d_k'] = tail(po, 4, keep, lanes=slice(0, kvw)).reshape(nseq, keep, D_KV_HEADS, HEAD_DIM)
                    ns['d_v'] = tail(po, 4, keep, lanes=slice(kvw, 2 * kvw)).reshape(
                        nseq, keep, D_KV_HEADS, HEAD_DIM)
                else:
                    yd, ns['d_k'], ns['d_v'] = _swa_sample(po, nseq, t, o_d_sinks[j], st['d_k'], st['d_v'])
                x2 = _post(x2, yc, yd, odd_wo[j], ffn_norm[layer], w_gate[layer], w_up[layer], w_down[layer],
                           final_norm, last)
                new_odd.append(ns)
        out = {k: jnp.stack([ns[k] for ns in new_even]) for k in new_even[0]}
        out.update({k: jnp.stack([ns[k] for ns in new_odd]) for k in new_odd[0]})
        return x2.reshape(nseq, t, D_MODEL), out

    bp, tp, _ = x_prompt.shape
    bs, ts, _ = x_sample.shape
    n_even, n_odd = (depth + 1) // 2, depth // 2
    zeros = lambda *shape: jnp.zeros(shape, F32)
    even_p = {'a_C': zeros(n_even, bp, A_HEADS, A_DK, A_DV), 'a_n': zeros(n_even, bp, A_HEADS, A_DK),
              'a_m': zeros(n_even, bp, A_HEADS), 'a_conv': zeros(n_even, bp, A_CONV - 1, 2 * SLAB)}
    odd_p = {'c_conv': zeros(n_odd, bp, C_CONV - 1, 2 * SLAB),
             'c_ssm': zeros(n_odd, bp, C_HEADS, C_HEADDIM, C_DSTATE)}
    y_p, pn = trunk(x_prompt, jnp.arange(tp, dtype=jnp.int32), True, even_p, odd_p)

    even_s = {'a_C': state_a_C, 'a_n': state_a_n, 'a_m': state_a_m, 'a_conv': state_a_conv,
              'b_k1': cache_b_k1, 'b_v1': cache_b_v1, 'b_k2': cache_b_k2, 'b_v2': cache_b_v2,
              'b_k3': cache_b_k3, 'b_v3': cache_b_v3}
    odd_s = {'c_conv': state_c_conv, 'c_ssm': state_c_ssm, 'd_k': cache_d_k, 'd_v': cache_d_v}
    pos_s = jnp.tile(PAST_LEN + jnp.arange(ts, dtype=jnp.int32), bs)
    y_s, sn = trunk(x_sample, pos_s, False, even_s, odd_s)

    keys = ('a_C', 'a_n', 'a_m', 'a_conv', 'b_k1', 'b_v1', 'b_k2', 'b_v2', 'b_k3', 'b_v3',
            'c_conv', 'c_ssm', 'd_k', 'd_v')
    states = []
    for k in keys:
        states += [pn[k], sn[k]]
    return (y_p, y_s, *states)
```

```python
import functools
import math

import jax
import jax.numpy as jnp
from jax import lax
from jax.experimental import pallas as pl
from jax.experimental.pallas import tpu as pltpu

F32 = jnp.float32
BF16 = jnp.bfloat16

D_MODEL = 1024
PAST_LEN = 16384
EPS = 1e-6
ROPE_THETA = 10000.0
HEAD_DIM = 64
SLAB = 512

A_HEADS = 4
A_DK = 128
A_DV = 128
A_CONV = 4
A_CHUNK = 64

B_HEADS = 8
B_WINDOWS = (128, 512, 2048)
B_DILATIONS = (1, 4, 16)
B_GROUPS = 3

C_HEADS = 8
C_HEADDIM = 64
C_DINNER = C_HEADS * C_HEADDIM
C_DSTATE = 128
C_NGROUPS = 2
C_CONV = 4
C_CHUNK = 128

D_HEADS = 8
D_KV_HEADS = 2
D_WINDOW = 128

NEG = -1e30
VMEM_LIMIT = 56 * 1024 * 1024


def _cparams(*sem):
    return pltpu.CompilerParams(dimension_semantics=sem, vmem_limit_bytes=VMEM_LIMIT)


def _rot_half(y):
    lane = lax.broadcasted_iota(jnp.int32, y.shape, 1)
    fwd = pltpu.roll(y, SLAB - HEAD_DIM // 2, 1)
    bwd = pltpu.roll(y, HEAD_DIM // 2, 1)
    return jnp.where(lane % HEAD_DIM < HEAD_DIM // 2, fwd, bwd)


PERM_TILE = 2048
PROJ_ROWS = 512


def _proj_kernel(x_ref, nw_ref, w_ref, cos_ref, sin_ref, o_ref, xn_ref, *ybuf, kinds):
    j = pl.program_id(1)
    tm = o_ref.shape[0]

    @pl.when(j == 0)
    def _():
        x = x_ref[...]
        ms = jnp.mean(x * x, axis=-1, keepdims=True)
        xn_ref[...] = (x * lax.rsqrt(ms + EPS) * nw_ref[...]).astype(BF16)

    rc = min(tm, PROJ_ROWS)

    def emit(rope_lanes, dil):
        seg = tm // dil
        cseg = rc // dil
        for c in range(tm // rc):
            rows = slice(c * rc, (c + 1) * rc)
            y = jnp.dot(xn_ref[rows, :], w_ref[...], preferred_element_type=F32)
            val = y
            if rope_lanes:
                cos = jnp.tile(cos_ref[rows, :], (1, SLAB // 128))
                sin = jnp.tile(sin_ref[rows, :], (1, SLAB // 128))
                val = y * cos + _rot_half(y) * sin
                if rope_lanes < SLAB:
                    lane = lax.broadcasted_iota(jnp.int32, y.shape, 1)
                    val = jnp.where(lane < rope_lanes, val, y)
            if dil == 1:
                o_ref[rows, :] = val
            else:
                for cb in range(SLAB // 128):
                    ybuf[0][cb, rows, :] = val[:, cb * 128:(cb + 1) * 128]
                for r in range(dil):
                    dst = slice(r * seg + c * cseg, r * seg + (c + 1) * cseg)
                    for cb in range(SLAB // 128):
                        o_ref[dst, cb * 128:(cb + 1) * 128] = ybuf[0][cb, pl.ds(c * rc + r, cseg, stride=dil), :]

    for kind in sorted(set(kinds)):
        member = functools.reduce(jnp.logical_or, [j == s for s, k in enumerate(kinds) if k == kind])
        pl.when(member)(functools.partial(emit, *kind))


def _norm_proj(x, norm_w, w_slabs, cos_t, sin_t, kinds):
    n = x.shape[0]
    s = w_slabs.shape[0]
    p = cos_t.shape[0]
    perm = any(d > 1 for _, d in kinds)
    big = perm or (n % PERM_TILE == 0 and p % PERM_TILE == 0)
    tm = PERM_TILE if big else min(1024, n, p)
    assert n % tm == 0 and p % tm == 0 and len(kinds) == s
    npt = p // tm
    scratch = [pltpu.VMEM((tm, D_MODEL), BF16)] + ([pltpu.VMEM((SLAB // 128, tm, 128), F32)] if perm else [])
    return pl.pallas_call(
        functools.partial(_proj_kernel, kinds=tuple(kinds)),
        grid=(n // tm, s),
        in_specs=[
            pl.BlockSpec((tm, D_MODEL), lambda i, j: (i, 0), pipeline_mode=pl.Buffered(1)),
            pl.BlockSpec((1, D_MODEL), lambda i, j: (0, 0)),
            pl.BlockSpec((None, D_MODEL, SLAB), lambda i, j: (j, 0, 0)),
            pl.BlockSpec((tm, 128), lambda i, j: (i % npt, 0)),
            pl.BlockSpec((tm, 128), lambda i, j: (i % npt, 0)),
        ],
        out_specs=pl.BlockSpec((None, tm, SLAB), lambda i, j: (j, i, 0)),
        out_shape=jax.ShapeDtypeStruct((s, n, SLAB), F32),
        scratch_shapes=scratch,
        compiler_params=_cparams("parallel", "arbitrary"),
        name="norm_proj",
    )(x, norm_w.reshape(1, D_MODEL), w_slabs, cos_t, sin_t)


def _even_kinds(perm):
    dil = B_DILATIONS if perm else (1,) * B_GROUPS
    return ([(0, 1)] * 4 + [(SLAB, d) for d in dil] * 2 + [(0, d) for d in dil] + [(0, 1)])


ODD_KINDS = [(0, 1)] * 3 + [(SLAB, 1), (D_KV_HEADS * HEAD_DIM, 1)]


def _rope_tables(pos):
    half = HEAD_DIM // 2
    inv = ROPE_THETA ** (-jnp.arange(half, dtype=F32) / half)
    ang = pos.astype(F32)[:, None] * inv[None, :]
    cos, sin = jnp.cos(ang), jnp.sin(ang)
    return (jnp.concatenate([cos, cos, cos, cos], axis=-1),
            jnp.concatenate([-sin, sin, -sin, sin], axis=-1))


def _pad_cols(w, width):
    return jnp.pad(w, ((0, 0), (0, width - w.shape[1])))


def _even_slabs(w_in):
    main = w_in[:, :2048].reshape(D_MODEL, 4, SLAB)
    gates = _pad_cols(w_in[:, 2048:2056], SLAB)[:, None, :]
    qkv = w_in[:, 2056:].reshape(D_MODEL, 9, SLAB)
    return jnp.moveaxis(jnp.concatenate([main, qkv, gates], axis=1), 1, 0).astype(BF16)


def _group_aligned(w, axis):
    rep = D_HEADS // D_KV_HEADS
    shape = w.shape
    split = shape[:axis] + (D_KV_HEADS, rep, HEAD_DIM) + shape[axis + 1:]
    return jnp.swapaxes(w.reshape(split), axis, axis + 1).reshape(shape)


def _odd_slabs(w_in):
    main = w_in[:, :1536].reshape(D_MODEL, 3, SLAB)
    qd = _group_aligned(w_in[:, 1544:2056], 1)[:, None, :]
    kvd = _pad_cols(jnp.concatenate([w_in[:, 2056:2312], w_in[:, 1536:1544]], axis=1), SLAB)[:, None, :]
    return jnp.moveaxis(jnp.concatenate([main, qd, kvd], axis=1), 1, 0).astype(BF16)


def _split3(x):
    hi = x.astype(BF16)
    r1 = x - hi.astype(F32)
    mid = r1.astype(BF16)
    lo = (r1 - mid.astype(F32)).astype(BF16)
    return hi, mid, lo


def _cumsum_rows(x, tri):
    hi, mid, lo = _split3(x)
    acc = jnp.dot(tri, lo, preferred_element_type=F32)
    acc = acc + jnp.dot(tri, mid, preferred_element_type=F32)
    return acc + jnp.dot(tri, hi, preferred_element_type=F32)


def _transpose_slab(s):
    l = s.shape[0]
    if l < 128:
        s = jnp.concatenate([s, jnp.zeros((128 - l, 128), s.dtype)], axis=0)
    return s.T


def _conv_silu(buf_ref, x, w, b):
    tb = x.shape[0]
    buf_ref[8:8 + tb, :] = x
    y = b + w[3:4] * x
    for j in range(3):
        y = y + w[j:j + 1] * buf_ref[5 + j:5 + j + tb, :]
    buf_ref[0:8, :] = buf_ref[tb:tb + 8, :]
    return y * jax.nn.sigmoid(y)


def _log_sigmoid(x):
    return jnp.minimum(x, 0.0) - jnp.log1p(jnp.exp(-jnp.abs(x)))


def _dot_t(a, b):
    return lax.dot_general(a, b, (((1,), (1,)), ((), ())), preferred_element_type=F32)


def _tdot(a, b):
    return lax.dot_general(a, b, (((0,), (0,)), ((), ())), preferred_element_type=F32)


def _mlstm_kernel(q_ref, k_ref, v_ref, og_ref, g_ref, cw_ref, cb_ref, bg_ref, an_ref,
                  conv0_ref, c0_ref, n0_ref, m0_ref,
                  ya_ref, c_ref, n_ref, m_ref, qbuf, kbuf, *, chunk):
    sb, tb, _ = q_ref.shape
    tblk = pl.program_id(1)
    lane = lax.broadcasted_iota(jnp.int32, (chunk, 128), 1)
    lane1 = lax.broadcasted_iota(jnp.int32, (1, 128), 1)
    row = lax.broadcasted_iota(jnp.int32, (chunk, chunk), 0)
    col = lax.broadcasted_iota(jnp.int32, (chunk, chunk), 1)
    causal = col <= row
    tri = causal.astype(BF16)
    scale = A_DK ** -0.5

    @pl.when(tblk == 0)
    def _():
        c_ref[...] = c0_ref[...]
        n_ref[...] = n0_ref[...]
        m_ref[...] = m0_ref[...]
        for si in range(sb):
            qbuf[si, 0:8, :] = conv0_ref[si, :, 0:SLAB]
            kbuf[si, 0:8, :] = conv0_ref[si, :, SLAB:2 * SLAB]

    for si in range(sb):
        qc = _conv_silu(qbuf.at[si], q_ref[si], cw_ref[:, 0:SLAB], cb_ref[:, 0:SLAB])
        kc = _conv_silu(kbuf.at[si], k_ref[si], cw_ref[:, SLAB:2 * SLAB], cb_ref[:, SLAB:2 * SLAB]) * scale
        m_vec = m_ref[si]
        for c in range(tb // chunk):
            r0 = c * chunk
            gates = g_ref[si, r0:r0 + chunk, 0:128] + bg_ref[...]
            bc = _cumsum_rows(_log_sigmoid(gates), tri)
            st = _transpose_slab(jnp.where(lane < A_HEADS, gates, bc))
            m_next = m_vec
            for h in range(A_HEADS):
                hs = slice(h * 128, (h + 1) * 128)
                qh = qc[r0:r0 + chunk, hs]
                kh = kc[r0:r0 + chunk, hs]
                vh = v_ref[si, r0:r0 + chunk, hs]
                qb, kb, vb = qh.astype(BF16), kh.astype(BF16), vh.astype(BF16)
                b_t = bc[:, A_HEADS + h:A_HEADS + h + 1]
                li_t = gates[:, h:h + 1]
                li_s = st[h:h + 1, 0:chunk]
                b_s = st[A_HEADS + h:A_HEADS + h + 1, 0:chunk]
                m_h = m_vec[:, h:h + 1]
                dlog = jnp.where(causal, b_t - b_s + li_s, NEG)
                inter = b_t + m_h
                m_t = jnp.maximum(inter, jnp.max(dlog, axis=1, keepdims=True))
                w_intra = jnp.exp(dlog - m_t)
                w_inter = jnp.exp(inter - m_t)
                s = _dot_t(qb, kb) * w_intra
                c_st = c_ref[si, h]
                n_st = n_ref[si, h:h + 1, :]
                num = jnp.dot(s.astype(BF16), vb, preferred_element_type=F32) \
                    + w_inter * jnp.dot(qb, c_st.astype(BF16), preferred_element_type=F32)
                den = jnp.sum(s, axis=1, keepdims=True) + w_inter * jnp.sum(qh * n_st, axis=1, keepdims=True)
                hh = num / jnp.maximum(jnp.abs(den), jnp.exp(-m_t))
                b_end = b_t[chunk - 1:chunk, :]
                g_col = b_end - b_t + li_t
                m_new = jnp.maximum(b_end + m_h, jnp.max(g_col, axis=0, keepdims=True))
                w_state = jnp.exp(g_col - m_new)
                decay = jnp.exp(b_end + m_h - m_new)
                kw = kh * w_state
                c_ref[si, h] = decay * c_st + _tdot(kw.astype(BF16), vb)
                n_ref[si, h:h + 1, :] = decay * n_st + jnp.sum(kw, axis=0, keepdims=True)
                m_next = jnp.where(lane1 == h, m_new, m_next)
                hn = hh * lax.rsqrt(jnp.mean(hh * hh, axis=-1, keepdims=True) + EPS) * an_ref[:, hs]
                ya_ref[si, r0:r0 + chunk, hs] = hn * jax.nn.sigmoid(og_ref[si, r0:r0 + chunk, hs])
            m_vec = m_next
        m_ref[si] = m_vec


def _mlstm(proj, nseq, t, conv_w, conv_b, b_gates, a_norm, conv0, c0, n0, m0):
    chunk = A_CHUNK if t % A_CHUNK == 0 else t
    if t > chunk:
        sb, tb = 1, min(t, 4 * chunk)
    else:
        sb, tb = min(nseq, 8), t
    p4 = proj.reshape(proj.shape[0], nseq, t, SLAB)
    slab = lambda s: pl.BlockSpec((None, sb, tb, SLAB), lambda i, j, s=s: (s, i, j, 0))
    full = lambda a: pl.BlockSpec(a.shape, lambda i, j: (0,) * a.ndim)
    st = lambda a: pl.BlockSpec((sb,) + a.shape[1:], lambda i, j: (i,) + (0,) * (a.ndim - 1))
    bg = _pad_cols(b_gates.reshape(1, 2 * A_HEADS), 128)
    conv0p = jnp.pad(conv0, ((0, 0), (8 - (A_CONV - 1), 0), (0, 0)))
    m0p = _pad_cols(m0, 128).reshape(nseq, 1, 128)
    cb = conv_b.reshape(1, -1)
    an = a_norm.reshape(1, -1)
    ya, c, n, m = pl.pallas_call(
        functools.partial(_mlstm_kernel, chunk=chunk),
        grid=(nseq // sb, t // tb),
        in_specs=[slab(0), slab(1), slab(2), slab(3), slab(13),
                  full(conv_w), full(cb), full(bg), full(an),
                  st(conv0p), st(c0), st(n0), st(m0p)],
        out_specs=[pl.BlockSpec((sb, tb, SLAB), lambda i, j: (i, j, 0)), st(c0), st(n0), st(m0p)],
        out_shape=[jax.ShapeDtypeStruct((nseq, t, SLAB), F32),
                   jax.ShapeDtypeStruct(c0.shape, F32), jax.ShapeDtypeStruct(n0.shape, F32),
                   jax.ShapeDtypeStruct(m0p.shape, F32)],
        scratch_shapes=[pltpu.VMEM((sb, 8 + tb, SLAB), F32), pltpu.VMEM((sb, 8 + tb, SLAB), F32)],
        compiler_params=_cparams("parallel", "arbitrary"),
        name="mlstm",
    )(p4, p4, p4, p4, p4, conv_w, cb, bg, an, conv0p, c0, n0, m0p)
    return ya.reshape(nseq * t, SLAB), c, n, m[:, 0, :A_HEADS]


QBLK = 128


def _band_head(q_m, k2, v_m, valid, sink):
    s = jnp.where(valid, _dot_t(q_m, k2), NEG)
    m = jnp.max(s, axis=1, keepdims=True)
    if sink is not None:
        m = jnp.maximum(m, sink)
    e = jnp.exp(s - m)
    l = jnp.sum(e, axis=1, keepdims=True)
    if sink is not None:
        l = l + jnp.exp(sink - m)
    acc = jnp.dot(e.astype(BF16), v_m, preferred_element_type=F32)
    return acc, m, l


def _band_masks(has_prev, window):
    a = lax.broadcasted_iota(jnp.int32, (QBLK, 2 * QBLK), 0)
    c = lax.broadcasted_iota(jnp.int32, (QBLK, 2 * QBLK), 1)
    rel = a - c + QBLK
    band = jnp.logical_and(rel >= 0, rel <= window)
    if has_prev is True:
        return band
    return jnp.logical_and(band, c >= jnp.where(has_prev, 0, QBLK))


def _key_rows(prev_ref, cur_ref, a, ls):
    before = prev_ref[:, ls] if a == 0 else cur_ref[(a - 1) * QBLK:a * QBLK, ls]
    return jnp.concatenate([before, cur_ref[a * QBLK:(a + 1) * QBLK, ls]], axis=0).astype(BF16)


def _dilated_kernel(q_ref, kp_ref, kc_ref, vp_ref, vc_ref, o_ref, lse_ref):
    lane = lax.broadcasted_iota(jnp.int32, (1, 128), 1)
    low = lane < HEAD_DIM
    scale = HEAD_DIM ** -0.5
    for a in range(q_ref.shape[0] // QBLK):
        qs = slice(a * QBLK, (a + 1) * QBLK)
        valid = _band_masks(pl.program_id(2) > 0 if a == 0 else True, QBLK)
        for pr in range(B_HEADS // 2):
            ls = slice(pr * 128, (pr + 1) * 128)
            qp = q_ref[qs, ls] * scale
            k2 = _key_rows(kp_ref, kc_ref, a, ls)
            v2 = _key_rows(vp_ref, vc_ref, a, ls)
            o_h, lse_h = [], []
            for p in range(2):
                half = low if p == 0 else jnp.logical_not(low)
                acc, m, l = _band_head(jnp.where(half, qp, 0.0).astype(BF16), k2, v2, valid, None)
                o_h.append(acc / l)
                lse_h.append(m + jnp.log(l))
            o_ref[qs, ls] = jnp.where(low, o_h[0], o_h[1])
            lse_ref[qs, ls] = jnp.where(low, lse_h[0], lse_h[1])


def _dilated_prompt(proj, b, t, g):
    d = B_DILATIONS[g]
    assert t % PERM_TILE == 0
    nb = PERM_TILE // d // QBLK
    per_tile = PERM_TILE // QBLK
    pv = proj.reshape(proj.shape[0], b, t, SLAB)

    nq = 2 if nb % 2 == 0 else 1

    def blk(i, r):
        return (i // nb) * per_tile + r * nb + i % nb

    cur = lambda s: pl.BlockSpec((None, None, nq * QBLK, SLAB),
                                 lambda bi, r, i, s=s: (s, bi, blk(nq * i, r) // nq, 0))
    prev = lambda s: pl.BlockSpec((None, None, QBLK, SLAB),
                                  lambda bi, r, i, s=s: (s, bi, blk(jnp.maximum(nq * i - 1, 0), r), 0))
    ospec = pl.BlockSpec((None, nq * QBLK, SLAB), lambda bi, r, i: (bi, blk(nq * i, r) // nq, 0))
    sq, sk, sv = 4 + g, 7 + g, 10 + g
    o, lse = pl.pallas_call(
        _dilated_kernel,
        grid=(b, d, t // d // QBLK // nq),
        in_specs=[cur(sq), prev(sk), cur(sk), prev(sv), cur(sv)],
        out_specs=[ospec, ospec],
        out_shape=[jax.ShapeDtypeStruct((b, t, SLAB), F32)] * 2,
        compiler_params=_cparams("parallel", "parallel", "arbitrary"),
        name=f"dilated_g{g}",
    )(pv, pv, pv, pv, pv)
    return o.reshape(b * t, SLAB), lse.reshape(b * t, SLAB)


def _merge_kernel(*refs, dils):
    ins, y_ref, bufs = refs[:6], refs[6], refs[7:]
    tm = y_ref.shape[0]
    vals = []
    nbuf = 0
    for k, ref in enumerate(ins):
        dil = dils[k // 2]
        if dil == 1:
            vals.append(ref[...])
        else:
            buf = bufs[nbuf]
            nbuf += 1
            seg = tm // dil
            for r in range(dil):
                buf[pl.ds(r, seg, stride=dil), :] = ref[r * seg:(r + 1) * seg, :]
            vals.append(buf[...])
    o1, l1, o2, l2, o3, l3 = vals
    m = jnp.maximum(jnp.maximum(l1, l2), l3)
    w1, w2, w3 = jnp.exp(l1 - m), jnp.exp(l2 - m), jnp.exp(l3 - m)
    y_ref[...] = (w1 * o1 + w2 * o2 + w3 * o3) / (w1 + w2 + w3)


def _merge_groups(parts, perm):
    n = parts[0].shape[0]
    dils = B_DILATIONS if perm else (1,) * B_GROUPS
    tm = PERM_TILE if perm else min(1024, n)
    spec = pl.BlockSpec((tm, 128), lambda i, j: (i, j))
    nbuf = 2 * sum(d > 1 for d in dils)
    return pl.pallas_call(
        functools.partial(_merge_kernel, dils=dils), grid=(n // tm, SLAB // 128), in_specs=[spec] * 6,
        out_specs=spec, out_shape=jax.ShapeDtypeStruct((n, SLAB), F32),
        scratch_shapes=[pltpu.VMEM((tm, 128), F32)] * nbuf,
        compiler_params=_cparams("parallel", "parallel"), name="merge_groups",
    )(*parts)


def _swa_kernel(q_ref, kp_ref, kc_ref, vp_ref, vc_ref, sink_ref, o_ref):
    lane = lax.broadcasted_iota(jnp.int32, (1, 128), 1)
    scale = HEAD_DIM ** -0.5
    rep = D_HEADS // D_KV_HEADS
    low = lane < HEAD_DIM
    allk = slice(None)
    for a in range(q_ref.shape[0] // QBLK):
        qs = slice(a * QBLK, (a + 1) * QBLK)
        valid = _band_masks(pl.program_id(1) > 0 if a == 0 else True, D_WINDOW - 1)
        k2 = _key_rows(kp_ref, kc_ref, a, allk)
        v2 = _key_rows(vp_ref, vc_ref, a, allk)
        for pr in range(rep):
            ls = slice(pr * 128, (pr + 1) * 128)
            qp = q_ref[qs, ls] * scale
            o_h = []
            for g in range(D_KV_HEADS):
                half = low if g == 0 else jnp.logical_not(low)
                acc, m, l = _band_head(jnp.where(half, qp, 0.0).astype(BF16), k2, v2, valid,
                                       sink_ref[g * rep + pr])
                o_h.append(acc / l)
            o_ref[qs, ls] = jnp.where(low, o_h[0], o_h[1])


def _swa_prompt(proj, b, t, sinks):
    pv = proj.reshape(proj.shape[0], b, t, SLAB)
    nq = 2 if t % (2 * QBLK) == 0 else 1
    qspec = pl.BlockSpec((None, None, nq * QBLK, SLAB), lambda bi, i: (3, bi, i, 0))

    def kv(col, prev):
        if prev:
            return pl.BlockSpec((None, None, QBLK, 128), lambda bi, i: (4, bi, jnp.maximum(nq * i - 1, 0), col))
        return pl.BlockSpec((None, None, nq * QBLK, 128), lambda bi, i: (4, bi, i, col))

    o = pl.pallas_call(
        _swa_kernel,
        grid=(b, t // QBLK // nq),
        in_specs=[qspec, kv(0, True), kv(0, False), kv(1, True), kv(1, False),
                  pl.BlockSpec(memory_space=pltpu.SMEM)],
        out_specs=pl.BlockSpec((None, nq * QBLK, SLAB), lambda bi, i: (bi, i, 0)),
        out_shape=jax.ShapeDtypeStruct((b, t, SLAB), F32),
        compiler_params=_cparams("parallel", "arbitrary"),
        name="swa_prompt",
    )(pv, pv, pv, pv, pv, sinks)
    return o.reshape(b * t, SLAB)


def _softplus(x):
    return jnp.maximum(x, 0.0) + jnp.log1p(jnp.exp(-jnp.abs(x)))


def _ssd_kernel(z_ref, x_ref, bc_ref, dt_ref, cw_ref, cb_ref, dtb_ref, alog_ref, dl_ref, cn_ref,
                conv0_ref, h0_ref, y_ref, h_ref, xbuf, bcbuf, *, chunk):
    sb, tb, _ = x_ref.shape
    tblk = pl.program_id(1)
    lane1 = lax.broadcasted_iota(jnp.int32, (1, 128), 1)
    low = lane1 < C_HEADDIM
    row = lax.broadcasted_iota(jnp.int32, (chunk, chunk), 0)
    col = lax.broadcasted_iota(jnp.int32, (chunk, chunk), 1)
    causal = col <= row
    tri = causal.astype(BF16)
    srow_low = lax.broadcasted_iota(jnp.int32, (128, 1), 0) < C_HEADDIM
    hpg = C_HEADS // C_NGROUPS
    a_neg = -jnp.exp(alog_ref[...])

    @pl.when(tblk == 0)
    def _():
        h_ref[...] = h0_ref[...]
        for si in range(sb):
            xbuf[si, 0:8, :] = conv0_ref[si, :, 0:SLAB]
            bcbuf[si, 0:8, :] = conv0_ref[si, :, SLAB:2 * SLAB]

    for si in range(sb):
        xc = _conv_silu(xbuf.at[si], x_ref[si], cw_ref[:, 0:SLAB], cb_ref[:, 0:SLAB])
        bcc = _conv_silu(bcbuf.at[si], bc_ref[si], cw_ref[:, SLAB:2 * SLAB], cb_ref[:, SLAB:2 * SLAB])
        for c in range(tb // chunk):
            rs = slice(c * chunk, (c + 1) * chunk)
            dt = _softplus(dt_ref[si, rs, :] + dtb_ref[...])
            cum = _cumsum_rows(dt * a_neg, tri)
            cum_t_all = _transpose_slab(cum)
            dt_t_all = _transpose_slab(dt)
            cum_end = cum[chunk - 1:chunk, :]
            for g in range(C_NGROUPS):
                bg = bcc[rs, g * 128:(g + 1) * 128].astype(BF16)
                cg = bcc[rs, 256 + g * 128:256 + (g + 1) * 128].astype(BF16)
                cb = _dot_t(cg, bg)
                y_pairs = []
                for pp in range(hpg // 2):
                    pr = g * (hpg // 2) + pp
                    ls = slice(pr * 128, (pr + 1) * 128)
                    x_pair = xc[rs, ls]
                    hs = h_ref[si, pr]
                    y_pair = None
                    e_col, w_end, dec = [], [], []
                    for p in range(2):
                        h = 2 * pr + p
                        cum_t = cum[:, h:h + 1]
                        seg = jnp.where(causal, cum_t - cum_t_all[h:h + 1, 0:chunk], NEG)
                        w = cb * jnp.exp(seg) * dt_t_all[h:h + 1, 0:chunk]
                        half = low if p == 0 else jnp.logical_not(low)
                        yh = jnp.dot(w.astype(BF16), jnp.where(half, x_pair, 0.0).astype(BF16),
                                     preferred_element_type=F32)
                        y_pair = yh if y_pair is None else y_pair + yh
                        e_col.append(jnp.exp(cum_t))
                        w_end.append(jnp.exp(cum_end[:, h:h + 1] - cum_t) * dt[:, h:h + 1])
                        dec.append(jnp.exp(cum_end[:, h:h + 1]))
                    inter = _dot_t(cg, hs.astype(BF16)) * jnp.where(low, e_col[0], e_col[1])
                    y_pair = y_pair + inter + dl_ref[:, ls] * x_pair
                    xw = x_pair * jnp.where(low, w_end[0], w_end[1])
                    h_ref[si, pr] = jnp.where(srow_low, dec[0], dec[1]) * hs + _tdot(xw.astype(BF16), bg)
                    zz = z_ref[si, rs, ls]
                    y_pairs.append(y_pair * (zz * jax.nn.sigmoid(zz)))
                ms = sum(jnp.sum(yp * yp, axis=-1, keepdims=True) for yp in y_pairs) / (hpg * C_HEADDIM)
                inv = lax.rsqrt(ms + EPS)
                for pp, yp in enumerate(y_pairs):
                    ls = slice((g * (hpg // 2) + pp) * 128, (g * (hpg // 2) + pp + 1) * 128)
                    y_ref[si, rs, ls] = yp * inv * cn_ref[:, ls]


def _ssd(proj, nseq, t, conv_w, conv_b, dt_bias, a_log, d_skip, c_norm, conv0, h0):
    chunk = C_CHUNK if t % C_CHUNK == 0 else t
    if t > chunk:
        sb, tb = min(nseq, 2), min(t, 2 * chunk)
    else:
        sb, tb = min(nseq, 8), t
    p4 = proj.reshape(proj.shape[0], nseq, t, SLAB)
    slab = lambda s: pl.BlockSpec((None, sb, tb, SLAB), lambda i, j, s=s: (s, i, j, 0))
    full = lambda a: pl.BlockSpec(a.shape, lambda i, j: (0,) * a.ndim)
    st = lambda a: pl.BlockSpec((sb,) + a.shape[1:], lambda i, j: (i,) + (0,) * (a.ndim - 1))
    conv0p = jnp.pad(conv0, ((0, 0), (8 - (C_CONV - 1), 0), (0, 0)))
    hp = h0.reshape(nseq, C_HEADS // 2, 128, C_DSTATE)
    cb = conv_b.reshape(1, -1)
    dtb = _pad_cols(dt_bias.reshape(1, C_HEADS), 128)
    alog = _pad_cols(a_log.reshape(1, C_HEADS), 128)
    dl = jnp.repeat(d_skip, C_HEADDIM).reshape(1, C_DINNER)
    cn = c_norm.reshape(1, C_DINNER)
    y, h = pl.pallas_call(
        functools.partial(_ssd_kernel, chunk=chunk),
        grid=(nseq // sb, t // tb),
        in_specs=[slab(0), slab(1), slab(2),
                  pl.BlockSpec((None, sb, tb, 128), lambda i, j: (4, i, j, 2)),
                  full(conv_w), full(cb), full(dtb), full(alog), full(dl), full(cn), st(conv0p), st(hp)],
        out_specs=[pl.BlockSpec((sb, tb, SLAB), lambda i, j: (i, j, 0)), st(hp)],
        out_shape=[jax.ShapeDtypeStruct((nseq, t, SLAB), F32), jax.ShapeDtypeStruct(hp.shape, F32)],
        scratch_shapes=[pltpu.VMEM((sb, 8 + tb, SLAB), F32), pltpu.VMEM((sb, 8 + tb, SLAB), F32)],
        compiler_params=_cparams("parallel", "arbitrary"),
        name="ssd",
    )(p4, p4, p4, p4, conv_w, cb, dtb, alog, dl, cn, conv0p, hp)
    return y.reshape(nseq * t, SLAB), h.reshape(nseq, C_HEADS, C_HEADDIM, C_DSTATE)


FFN_CHUNK = 256


def _post_kernel(x_ref, ya_ref, yb_ref, wo_ref, fn_ref, wg_ref, wu_ref, wd_ref, on_ref, o_ref, act_ref, *, final):
    ymix = jnp.concatenate([ya_ref[...], yb_ref[...]], axis=-1).astype(BF16)
    x1 = x_ref[...] + jnp.dot(ymix, wo_ref[...], preferred_element_type=F32)
    ms = jnp.mean(x1 * x1, axis=-1, keepdims=True)
    hn = (x1 * lax.rsqrt(ms + EPS) * fn_ref[...]).astype(BF16)
    hidden = wg_ref.shape[1]
    for c in range(hidden // FFN_CHUNK):
        cs = slice(c * FFN_CHUNK, (c + 1) * FFN_CHUNK)
        gate = jnp.dot(hn, wg_ref[:, cs], preferred_element_type=F32)
        up = jnp.dot(hn, wu_ref[:, cs], preferred_element_type=F32)
        act_ref[:, cs] = (gate * jax.nn.sigmoid(gate) * up).astype(BF16)
    x2 = x1 + jnp.dot(act_ref[...], wd_ref[...], preferred_element_type=F32)
    if final:
        ms2 = jnp.mean(x2 * x2, axis=-1, keepdims=True)
        x2 = x2 * lax.rsqrt(ms2 + EPS) * on_ref[...]
    o_ref[...] = x2


def _post(x, ya, yb, w_out, layer, ffn_norm, w_gate, w_up, w_down, out_norm, final):
    n = x.shape[0]
    tm = min(1024, n)
    hidden = w_gate.shape[2]
    row = lambda w: pl.BlockSpec((tm, w), lambda i: (i, 0))
    res = lambda a: pl.BlockSpec(a.shape, lambda i: (0, 0), pipeline_mode=pl.Buffered(1))
    lay = lambda a: pl.BlockSpec((None,) + a.shape[1:], lambda i: (layer, 0, 0), pipeline_mode=pl.Buffered(1))
    fn = ffn_norm.reshape(-1, 1, D_MODEL)
    on = out_norm.reshape(1, D_MODEL)
    return pl.pallas_call(
        functools.partial(_post_kernel, final=final),
        grid=(n // tm,),
        in_specs=[row(D_MODEL), row(SLAB), row(SLAB), res(w_out), lay(fn), lay(w_gate), lay(w_up), lay(w_down),
                  res(on)],
        out_specs=row(D_MODEL),
        out_shape=jax.ShapeDtypeStruct((n, D_MODEL), F32),
        scratch_shapes=[pltpu.VMEM((tm, hidden), BF16)],
        compiler_params=_cparams("parallel"),
        name="post_ffn",
    )(x, ya, yb, w_out, fn, w_gate, w_up, w_down, on)


NEWPAD = 128


def _ext_mask(rows, r, ts, window, step):
    t = lax.broadcasted_iota(jnp.int32, (rows, r + NEWPAD), 0) % ts
    c = lax.broadcasted_iota(jnp.int32, (rows, r + NEWPAD), 1)
    rel = r + t - c
    ok = jnp.logical_and(jnp.logical_and(rel >= 0, rel <= window), c < r + ts)
    return jnp.logical_and(ok, (rel & (step - 1)) == 0)


def _shift_cache(dst_ref, src_ref, new, si, r, ts):
    dst_ref[si, 0:r - ts, :] = src_ref[si, ts:r, :]
    dst_ref[si, r - ts:r, :] = new


def _pad_new(x):
    return jnp.concatenate([x, jnp.zeros((NEWPAD - x.shape[0], x.shape[1]), x.dtype)], axis=0)


def _dil_sample_kernel(q_ref, kn_ref, vn_ref, kc_ref, vc_ref, o_ref, lse_ref, ko_ref, vo_ref,
                       *, window, step):
    sb, ts, _ = q_ref.shape
    r = kc_ref.shape[2]
    scale = HEAD_DIM ** -0.5
    lane = lax.broadcasted_iota(jnp.int32, (1, 128), 1)
    low = lane < HEAD_DIM
    t = lax.broadcasted_iota(jnp.int32, (2 * ts, r + 128), 0) % ts
    c = lax.broadcasted_iota(jnp.int32, (2 * ts, r + 128), 1)
    rel = r + t - jnp.where(c < 128, c, c - 128 + ts)
    valid = jnp.logical_and(jnp.logical_and(rel >= 0, rel <= window), (rel & (step - 1)) == 0)
    valid = jnp.logical_and(valid, jnp.logical_or(c < ts, c >= 128))
    pad = jnp.zeros((128 - ts, SLAB), F32)

    def shifted(src_ref, new_t, rows):
        old = src_ref[rows, :]
        rolled = pltpu.roll(old, r - ts, 1)
        tail = jnp.where(lane < 128 - ts, rolled[:, r - 128:], new_t[rows, :])
        out = tail if r == 128 else jnp.concatenate([rolled[:, :r - 128], tail], axis=1)
        return old, out

    for si, pr in [(si, pr) for si in range(sb) for pr in range(B_HEADS // 2)]:
        if pr == 0:
            kn_t = jnp.concatenate([pad, kn_ref[si]], axis=0).T
            vn_t = jnp.concatenate([pad, vn_ref[si]], axis=0).T
        rows = slice(pr * 128, (pr + 1) * 128)
        k_old, k_out = shifted(kc_ref.at[si], kn_t, rows)
        v_old, v_out = shifted(vc_ref.at[si], vn_t, rows)
        ko_ref[si, rows, :] = k_out
        vo_ref[si, rows, :] = v_out
        k_ext = jnp.concatenate([k_old[:, 0:128], k_out], axis=1).astype(BF16)
        v_ext = jnp.concatenate([v_old[:, 0:128], v_out], axis=1).astype(BF16)
        qp = q_ref[si, :, rows] * scale
        q2 = jnp.concatenate([jnp.where(low, qp, 0.0), jnp.where(low, 0.0, qp)], axis=0).astype(BF16)
        s = jnp.where(valid, jnp.dot(q2, k_ext, preferred_element_type=F32), NEG)
        m = jnp.max(s, axis=1, keepdims=True)
        e = jnp.exp(s - m)
        l = jnp.sum(e, axis=1, keepdims=True)
        a = _dot_t(e.astype(BF16), v_ext) / l
        lse = m + jnp.log(l)
        o_ref[si, :, rows] = jnp.where(low, a[0:ts], a[ts:2 * ts])
        lse_ref[si, :, rows] = jnp.where(low, lse[0:ts], lse[ts:2 * ts])


def _rows_minor(cache):
    nseq, r = cache.shape[:2]
    return jnp.transpose(cache, (0, 2, 3, 1)).reshape(nseq, -1, r)


def _rows_major(cache_t, heads):
    nseq, _, r = cache_t.shape
    return jnp.transpose(cache_t.reshape(nseq, heads, HEAD_DIM, r), (0, 3, 1, 2))


def _dilated_sample(proj, nseq, ts, g, k_cache, v_cache):
    r = k_cache.shape[1]
    assert ts % 8 == 0 and r % 128 == 0 and r >= B_WINDOWS[g]
    kt, vt = _rows_minor(k_cache), _rows_minor(v_cache)
    sb = max(1, min(nseq, 8, 1024 // r))
    p4 = proj.reshape(proj.shape[0], nseq, ts, SLAB)
    new = lambda s: pl.BlockSpec((None, sb, ts, SLAB), lambda i, s=s: (s, i, 0, 0))
    cspec = pl.BlockSpec((sb, SLAB, r), lambda i: (i, 0, 0))
    ospec = pl.BlockSpec((sb, ts, SLAB), lambda i: (i, 0, 0))
    o, lse, kn, vn = pl.pallas_call(
        functools.partial(_dil_sample_kernel, window=B_WINDOWS[g], step=B_DILATIONS[g]),
        grid=(nseq // sb,),
        in_specs=[new(4 + g), new(7 + g), new(10 + g), cspec, cspec],
        out_specs=[ospec, ospec, cspec, cspec],
        out_shape=[jax.ShapeDtypeStruct((nseq, ts, SLAB), F32)] * 2 + [jax.ShapeDtypeStruct(kt.shape, F32)] * 2,
        compiler_params=_cparams("parallel"),
        name=f"dilated_sample_g{g}",
    )(p4, p4, p4, kt, vt)
    return (o.reshape(nseq * ts, SLAB), lse.reshape(nseq * ts, SLAB),
            _rows_major(kn, B_HEADS), _rows_major(vn, B_HEADS))


def _swa_sample_kernel(q_ref, k_ref, v_ref, kc_ref, vc_ref, sink_ref, o_ref, kn_ref, vn_ref):
    sb, ts, _ = q_ref.shape
    r = kc_ref.shape[1]
    rows = D_HEADS * ts
    valid = _ext_mask(rows, r, ts, D_WINDOW - 1, 1)
    lane = lax.broadcasted_iota(jnp.int32, (1, 128), 1)
    low = lane < HEAD_DIM
    rep = D_HEADS // D_KV_HEADS
    scale = HEAD_DIM ** -0.5
    sink = sink_ref[:, 0:1]
    for si in range(sb):
        kn, vn = k_ref[si], v_ref[si]
        pieces = []
        for pr in range(rep):
            base = q_ref[si, :, pr * 128:(pr + 1) * 128] * scale
            for g in range(D_KV_HEADS):
                pieces.append(jnp.where(low if g == 0 else jnp.logical_not(low), base, 0.0))
        qrows = jnp.concatenate(pieces, axis=0).astype(BF16)
        kext = jnp.concatenate([kc_ref[si], _pad_new(kn)], axis=0).astype(BF16)
        vext = jnp.concatenate([vc_ref[si], _pad_new(vn)], axis=0).astype(BF16)
        s = jnp.where(valid, _dot_t(qrows, kext), NEG)
        m = jnp.maximum(jnp.max(s, axis=1, keepdims=True), sink)
        e = jnp.exp(s - m)
        l = jnp.sum(e, axis=1, keepdims=True) + jnp.exp(sink - m)
        o = jnp.dot(e.astype(BF16), vext, preferred_element_type=F32) / l
        pairs = [jnp.where(low, o[2 * pr * ts:(2 * pr + 1) * ts], o[(2 * pr + 1) * ts:(2 * pr + 2) * ts])
                 for pr in range(rep)]
        o_ref[si] = jnp.concatenate(pairs, axis=1)
        _shift_cache(kn_ref, kc_ref, kn, si, r, ts)
        _shift_cache(vn_ref, vc_ref, vn, si, r, ts)


def _swa_sample(proj, nseq, ts, sinks, k_cache, v_cache):
    r = k_cache.shape[1]
    sb = min(nseq, 8)
    kvw = D_KV_HEADS * HEAD_DIM
    p4 = proj.reshape(proj.shape[0], nseq, ts, SLAB)
    kc = k_cache.reshape(nseq, r, kvw)
    vc = v_cache.reshape(nseq, r, kvw)
    cspec = pl.BlockSpec((sb, r, kvw), lambda i: (i, 0, 0))
    new = lambda col: pl.BlockSpec((None, sb, ts, kvw), lambda i: (4, i, 0, col))
    rep = D_HEADS // D_KV_HEADS
    by_row_group = sinks.reshape(D_KV_HEADS, rep).T.reshape(-1)
    sink_rows = jnp.broadcast_to(jnp.repeat(by_row_group, ts)[:, None], (D_HEADS * ts, 128))
    o, kn, vn = pl.pallas_call(
        _swa_sample_kernel,
        grid=(nseq // sb,),
        in_specs=[pl.BlockSpec((None, sb, ts, SLAB), lambda i: (3, i, 0, 0)), new(0), new(1), cspec, cspec,
                  pl.BlockSpec(sink_rows.shape, lambda i: (0, 0))],
        out_specs=[pl.BlockSpec((sb, ts, SLAB), lambda i: (i, 0, 0)), cspec, cspec],
        out_shape=[jax.ShapeDtypeStruct((nseq, ts, SLAB), F32)] + [jax.ShapeDtypeStruct(kc.shape, F32)] * 2,
        compiler_params=_cparams("parallel"),
        name="swa_sample",
    )(p4, p4, p4, kc, vc, sink_rows)
    return o.reshape(nseq * ts, SLAB), kn.reshape(k_cache.shape), vn.reshape(v_cache.shape)


def kernel(x_prompt, x_sample, state_a_C, state_a_n, state_a_m, state_a_conv, cache_b_k1, cache_b_v1, cache_b_k2, cache_b_v2, cache_b_k3, cache_b_v3, state_c_conv, state_c_ssm, cache_d_k, cache_d_v, e_norm_mix, e_w_in, e_b_gates, e_a_conv_w, e_a_conv_b, e_a_norm, e_w_out, o_norm_mix, o_w_in, o_c_conv_w, o_c_conv_b, o_c_dt_bias, o_c_A_log, o_c_D, o_c_norm, o_d_sinks, o_w_out, ffn_norm, w_gate, w_up, w_down, final_norm):
    depth = ffn_norm.shape[0]
    even_w = [_even_slabs(e_w_in[j]) for j in range((depth + 1) // 2)]
    odd_w = [_odd_slabs(o_w_in[j]) for j in range(depth // 2)]
    odd_wo = [jnp.concatenate([o_w_out[j][:C_DINNER], _group_aligned(o_w_out[j][C_DINNER:], 0)],
                              axis=0).astype(BF16) for j in range(depth // 2)]
    even_wo = [e_w_out[j].astype(BF16) for j in range((depth + 1) // 2)]
    wg16, wu16, wd16 = w_gate.astype(BF16), w_up.astype(BF16), w_down.astype(BF16)

    def trunk(x, pos, fresh, even_st, odd_st):
        nseq, t, _ = x.shape
        assert t >= A_CONV - 1 and t >= C_CONV - 1
        cos_t, sin_t = _rope_tables(pos)
        x2 = x.reshape(nseq * t, D_MODEL)
        new_even, new_odd = [], []

        def tail(proj, s, keep, dil=1, lanes=slice(None)):
            p4 = proj.reshape(proj.shape[0], nseq, t, SLAB)
            if dil == 1:
                return p4[s, :, t - keep:, lanes]
            seg = PERM_TILE // dil
            last = p4[s, :, t - PERM_TILE:, lanes].reshape(nseq, dil, seg, -1)[:, :, seg - keep // dil:]
            return jnp.swapaxes(last, 1, 2).reshape(nseq, keep, -1)

        for layer in range(depth):
            j = layer // 2
            last = layer == depth - 1
            if layer % 2 == 0:
                pe = _norm_proj(x2, e_norm_mix[j], even_w[j], cos_t, sin_t, _even_kinds(fresh))
                st = {k: v[j] for k, v in even_st.items()}
                ya, c_new, n_new, m_new = _mlstm(pe, nseq, t, e_a_conv_w[j], e_a_conv_b[j], e_b_gates[j],
                                                 e_a_norm[j], st['a_conv'], st['a_C'], st['a_n'], st['a_m'])
                ns = {'a_C': c_new, 'a_n': n_new, 'a_m': m_new,
                      'a_conv': jnp.concatenate([tail(pe, 0, A_CONV - 1), tail(pe, 1, A_CONV - 1)], axis=-1)}
                parts = []
                for g in range(B_GROUPS):
                    kn, vn = f'b_k{g + 1}', f'b_v{g + 1}'
                    if fresh:
                        o_g, lse_g = _dilated_prompt(pe, nseq, t, g)
                        keep = min(B_WINDOWS[g], t)
                        ns[kn] = tail(pe, 7 + g, keep, B_DILATIONS[g]).reshape(nseq, keep, B_HEADS, HEAD_DIM)
                        ns[vn] = tail(pe, 10 + g, keep, B_DILATIONS[g]).reshape(nseq, keep, B_HEADS, HEAD_DIM)
                    else:
                        o_g, lse_g, ns[kn], ns[vn] = _dilated_sample(pe, nseq, t, g, st[kn], st[vn])
                    parts += [o_g, lse_g]
                yb = _merge_groups(parts, fresh)
                x2 = _post(x2, ya, yb, even_wo[j], layer, ffn_norm, wg16, wu16, wd16, final_norm, last)
                new_even.append(ns)
            else:
                po = _norm_proj(x2, o_norm_mix[j], odd_w[j], cos_t, sin_t, ODD_KINDS)
                st = {k: v[j] for k, v in odd_st.items()}
                yc, ssm_new = _ssd(po, nseq, t, o_c_conv_w[j], o_c_conv_b[j], o_c_dt_bias[j], o_c_A_log[j],
                                   o_c_D[j], o_c_norm[j], st['c_conv'], st['c_ssm'])
                ns = {'c_conv': jnp.concatenate([tail(po, 1, C_CONV - 1), tail(po, 2, C_CONV - 1)], axis=-1),
                      'c_ssm': ssm_new}
                kvw = D_KV_HEADS * HEAD_DIM
                if fresh:
                    yd = _swa_prompt(po, nseq, t, o_d_sinks[j])
                    keep = min(D_WINDOW, t)
                    ns['d_k'] = tail(po, 4, keep, lanes=slice(0, kvw)).reshape(nseq, keep, D_KV_HEADS, HEAD_DIM)
                    ns['d_v'] = tail(po, 4, keep, lanes=slice(kvw, 2 * kvw)).reshape(
                        nseq, keep, D_KV_HEADS, HEAD_DIM)
                else:
                    yd, ns['d_k'], ns['d_v'] = _swa_sample(po, nseq, t, o_d_sinks[j], st['d_k'], st['d_v'])
                x2 = _post(x2, yc, yd, odd_wo[j], layer, ffn_norm, wg16, wu16, wd16, final_norm, last)
                new_odd.append(ns)
        out = {k: jnp.stack([ns[k] for ns in new_even]) for k in new_even[0]}
        out.update({k: jnp.stack([ns[k] for ns in new_odd]) for k in new_odd[0]})
        return x2.reshape(nseq, t, D_MODEL), out

    bp, tp, _ = x_prompt.shape
    bs, ts, _ = x_sample.shape
    n_even, n_odd = (depth + 1) // 2, depth // 2
    zeros = lambda *shape: jnp.zeros(shape, F32)
    even_p = {'a_C': zeros(n_even, bp, A_HEADS, A_DK, A_DV), 'a_n': zeros(n_even, bp, A_HEADS, A_DK),
              'a_m': zeros(n_even, bp, A_HEADS), 'a_conv': zeros(n_even, bp, A_CONV - 1, 2 * SLAB)}
    odd_p = {'c_conv': zeros(n_odd, bp, C_CONV - 1, 2 * SLAB),
             'c_ssm': zeros(n_odd, bp, C_HEADS, C_HEADDIM, C_DSTATE)}
    y_p, pn = trunk(x_prompt, jnp.arange(tp, dtype=jnp.int32), True, even_p, odd_p)

    even_s = {'a_C': state_a_C, 'a_n': state_a_n, 'a_m': state_a_m, 'a_conv': state_a_conv,
              'b_k1': cache_b_k1, 'b_v1': cache_b_v1, 'b_k2': cache_b_k2, 'b_v2': cache_b_v2,
              'b_k3': cache_b_k3, 'b_v3': cache_b_v3}
    odd_s = {'c_conv': state_c_conv, 'c_ssm': state_c_ssm, 'd_k': cache_d_k, 'd_v': cache_d_v}
    pos_s = jnp.tile(PAST_LEN + jnp.arange(ts, dtype=jnp.int32), bs)
    y_s, sn = trunk(x_sample, pos_s, False, even_s, odd_s)

    keys = ('a_C', 'a_n', 'a_m', 'a_conv', 'b_k1', 'b_v1', 'b_k2', 'b_v2', 'b_k3', 'b_v3',
            'c_conv', 'c_ssm', 'd_k', 'd_v')
    states = []
    for k in keys:
        states += [pn[k], sn[k]]
    return (y_p, y_s, *states)
```

```python
import functools
import math

import jax
import jax.numpy as jnp
from jax import lax
from jax.experimental import pallas as pl
from jax.experimental.pallas import tpu as pltpu

F32 = jnp.float32
BF16 = jnp.bfloat16

D_MODEL = 1024
PAST_LEN = 16384
EPS = 1e-6
ROPE_THETA = 10000.0
HEAD_DIM = 64
SLAB = 512

A_HEADS = 4
A_DK = 128
A_DV = 128
A_CONV = 4
A_CHUNK = 64

B_HEADS = 8
B_WINDOWS = (128, 512, 2048)
B_DILATIONS = (1, 4, 16)
B_GROUPS = 3

C_HEADS = 8
C_HEADDIM = 64
C_DINNER = C_HEADS * C_HEADDIM
C_DSTATE = 128
C_NGROUPS = 2
C_CONV = 4
C_CHUNK = 128

D_HEADS = 8
D_KV_HEADS = 2
D_WINDOW = 128

NEG = -1e30
VMEM_LIMIT = 56 * 1024 * 1024


VMEM_LIMIT_RIDER = 60 * 1024 * 1024


def _cparams(*sem, vmem=VMEM_LIMIT):
    return pltpu.CompilerParams(dimension_semantics=sem, vmem_limit_bytes=vmem)


def _rot_half(y):
    lane = lax.broadcasted_iota(jnp.int32, y.shape, 1)
    fwd = pltpu.roll(y, SLAB - HEAD_DIM // 2, 1)
    bwd = pltpu.roll(y, HEAD_DIM // 2, 1)
    return jnp.where(lane % HEAD_DIM < HEAD_DIM // 2, fwd, bwd)


PERM_TILE = 2048
PROJ_ROWS = 512


def _proj_kernel(x_ref, nw_ref, w_ref, cos_ref, sin_ref, o_ref, xn_ref, *ybuf, kinds):
    j = pl.program_id(1)
    tm = o_ref.shape[0]

    @pl.when(j == 0)
    def _():
        x = x_ref[...]
        ms = jnp.mean(x * x, axis=-1, keepdims=True)
        xn_ref[...] = (x * lax.rsqrt(ms + EPS) * nw_ref[...]).astype(BF16)

    rc = min(tm, PROJ_ROWS)

    def emit(rope_lanes, dil):
        seg = tm // dil
        cseg = rc // dil
        for c in range(tm // rc):
            rows = slice(c * rc, (c + 1) * rc)
            y = jnp.dot(xn_ref[rows, :], w_ref[...], preferred_element_type=F32)
            val = y
            if rope_lanes:
                cos = jnp.tile(cos_ref[rows, :], (1, SLAB // 128))
                sin = jnp.tile(sin_ref[rows, :], (1, SLAB // 128))
                val = y * cos + _rot_half(y) * sin
                if rope_lanes < SLAB:
                    lane = lax.broadcasted_iota(jnp.int32, y.shape, 1)
                    val = jnp.where(lane < rope_lanes, val, y)
            if dil == 1:
                o_ref[rows, :] = val
            else:
                for cb in range(SLAB // 128):
                    ybuf[0][cb, rows, :] = val[:, cb * 128:(cb + 1) * 128]
                for r in range(dil):
                    dst = slice(r * seg + c * cseg, r * seg + (c + 1) * cseg)
                    for cb in range(SLAB // 128):
                        o_ref[dst, cb * 128:(cb + 1) * 128] = ybuf[0][cb, pl.ds(c * rc + r, cseg, stride=dil), :]

    for kind in sorted(set(kinds)):
        member = functools.reduce(jnp.logical_or, [j == s for s, k in enumerate(kinds) if k == kind])
        pl.when(member)(functools.partial(emit, *kind))


def _norm_proj(x, norm_w, w_slabs, cos_t, sin_t, kinds):
    n = x.shape[0]
    s = w_slabs.shape[0]
    p = cos_t.shape[0]
    perm = any(d > 1 for _, d in kinds)
    big = perm or (n % PERM_TILE == 0 and p % PERM_TILE == 0)
    tm = PERM_TILE if big else min(1024, n, p)
    assert n % tm == 0 and p % tm == 0 and len(kinds) == s
    npt = p // tm
    scratch = [pltpu.VMEM((tm, D_MODEL), BF16)] + ([pltpu.VMEM((SLAB // 128, tm, 128), F32)] if perm else [])
    return pl.pallas_call(
        functools.partial(_proj_kernel, kinds=tuple(kinds)),
        grid=(n // tm, s),
        in_specs=[
            pl.BlockSpec((tm, D_MODEL), lambda i, j: (i, 0), pipeline_mode=pl.Buffered(1)),
            pl.BlockSpec((1, D_MODEL), lambda i, j: (0, 0)),
            pl.BlockSpec((None, D_MODEL, SLAB), lambda i, j: (j, 0, 0)),
            pl.BlockSpec((tm, 128), lambda i, j: (i % npt, 0)),
            pl.BlockSpec((tm, 128), lambda i, j: (i % npt, 0)),
        ],
        out_specs=pl.BlockSpec((None, tm, SLAB), lambda i, j: (j, i, 0)),
        out_shape=jax.ShapeDtypeStruct((s, n, SLAB), F32),
        scratch_shapes=scratch,
        compiler_params=_cparams("parallel", "arbitrary"),
        name="norm_proj",
    )(x, norm_w.reshape(1, D_MODEL), w_slabs, cos_t, sin_t)


def _even_kinds(perm):
    dil = B_DILATIONS if perm else (1,) * B_GROUPS
    return ([(0, 1)] * 4 + [(SLAB, d) for d in dil] * 2 + [(0, d) for d in dil] + [(0, 1)])


ODD_KINDS = [(0, 1)] * 3 + [(SLAB, 1), (D_KV_HEADS * HEAD_DIM, 1)]


def _rope_tables(pos):
    half = HEAD_DIM // 2
    inv = ROPE_THETA ** (-jnp.arange(half, dtype=F32) / half)
    ang = pos.astype(F32)[:, None] * inv[None, :]
    cos, sin = jnp.cos(ang), jnp.sin(ang)
    return (jnp.concatenate([cos, cos, cos, cos], axis=-1),
            jnp.concatenate([-sin, sin, -sin, sin], axis=-1))


def _pad_cols(w, width):
    return jnp.pad(w, ((0, 0), (0, width - w.shape[1])))


def _even_slabs(w_in):
    main = w_in[:, :2048].reshape(D_MODEL, 4, SLAB)
    gates = _pad_cols(w_in[:, 2048:2056], SLAB)[:, None, :]
    qkv = w_in[:, 2056:].reshape(D_MODEL, 9, SLAB)
    return jnp.moveaxis(jnp.concatenate([main, qkv, gates], axis=1), 1, 0).astype(BF16)


def _group_aligned(w, axis):
    rep = D_HEADS // D_KV_HEADS
    shape = w.shape
    split = shape[:axis] + (D_KV_HEADS, rep, HEAD_DIM) + shape[axis + 1:]
    return jnp.swapaxes(w.reshape(split), axis, axis + 1).reshape(shape)


def _odd_slabs(w_in):
    main = w_in[:, :1536].reshape(D_MODEL, 3, SLAB)
    qd = _group_aligned(w_in[:, 1544:2056], 1)[:, None, :]
    kvd = _pad_cols(jnp.concatenate([w_in[:, 2056:2312], w_in[:, 1536:1544]], axis=1), SLAB)[:, None, :]
    return jnp.moveaxis(jnp.concatenate([main, qd, kvd], axis=1), 1, 0).astype(BF16)


def _split3(x):
    hi = x.astype(BF16)
    r1 = x - hi.astype(F32)
    mid = r1.astype(BF16)
    lo = (r1 - mid.astype(F32)).astype(BF16)
    return hi, mid, lo


def _cumsum_rows(x, tri):
    hi, mid, lo = _split3(x)
    acc = jnp.dot(tri, lo, preferred_element_type=F32)
    acc = acc + jnp.dot(tri, mid, preferred_element_type=F32)
    return acc + jnp.dot(tri, hi, preferred_element_type=F32)


def _transpose_slab(s):
    l = s.shape[0]
    if l < 128:
        s = jnp.concatenate([s, jnp.zeros((128 - l, 128), s.dtype)], axis=0)
    return s.T


def _conv_silu(buf_ref, x, w, b):
    tb = x.shape[0]
    buf_ref[8:8 + tb, :] = x
    y = b + w[3:4] * x
    for j in range(3):
        y = y + w[j:j + 1] * buf_ref[5 + j:5 + j + tb, :]
    buf_ref[0:8, :] = buf_ref[tb:tb + 8, :]
    return y * jax.nn.sigmoid(y)


def _log_sigmoid(x):
    return jnp.minimum(x, 0.0) - jnp.log1p(jnp.exp(-jnp.abs(x)))


def _dot_t(a, b):
    return lax.dot_general(a, b, (((1,), (1,)), ((), ())), preferred_element_type=F32)


def _tdot(a, b):
    return lax.dot_general(a, b, (((0,), (0,)), ((), ())), preferred_element_type=F32)


def _mlstm_kernel(q_ref, k_ref, v_ref, og_ref, g_ref, cw_ref, cb_ref, bg_ref, an_ref,
                  conv0_ref, c0_ref, n0_ref, m0_ref,
                  ya_ref, c_ref, n_ref, m_ref, qbuf, kbuf, *, chunk):
    sb, tb, _ = q_ref.shape
    tblk = pl.program_id(1)
    lane = lax.broadcasted_iota(jnp.int32, (chunk, 128), 1)
    lane1 = lax.broadcasted_iota(jnp.int32, (1, 128), 1)
    row = lax.broadcasted_iota(jnp.int32, (chunk, chunk), 0)
    col = lax.broadcasted_iota(jnp.int32, (chunk, chunk), 1)
    causal = col <= row
    tri = causal.astype(BF16)
    scale = A_DK ** -0.5

    @pl.when(tblk == 0)
    def _():
        c_ref[...] = c0_ref[...]
        n_ref[...] = n0_ref[...]
        m_ref[...] = m0_ref[...]
        for si in range(sb):
            qbuf[si, 0:8, :] = conv0_ref[si, :, 0:SLAB]
            kbuf[si, 0:8, :] = conv0_ref[si, :, SLAB:2 * SLAB]

    for si in range(sb):
        qc = _conv_silu(qbuf.at[si], q_ref[si], cw_ref[:, 0:SLAB], cb_ref[:, 0:SLAB])
        kc = _conv_silu(kbuf.at[si], k_ref[si], cw_ref[:, SLAB:2 * SLAB], cb_ref[:, SLAB:2 * SLAB]) * scale
        m_vec = m_ref[si]
        for c in range(tb // chunk):
            r0 = c * chunk
            gates = g_ref[si, r0:r0 + chunk, 0:128] + bg_ref[...]
            bc = _cumsum_rows(_log_sigmoid(gates), tri)
            st = _transpose_slab(jnp.where(lane < A_HEADS, gates, bc))
            m_next = m_vec
            for h in range(A_HEADS):
                hs = slice(h * 128, (h + 1) * 128)
                qh = qc[r0:r0 + chunk, hs]
                kh = kc[r0:r0 + chunk, hs]
                vh = v_ref[si, r0:r0 + chunk, hs]
                qb, kb, vb = qh.astype(BF16), kh.astype(BF16), vh.astype(BF16)
                b_t = bc[:, A_HEADS + h:A_HEADS + h + 1]
                li_t = gates[:, h:h + 1]
                li_s = st[h:h + 1, 0:chunk]
                b_s = st[A_HEADS + h:A_HEADS + h + 1, 0:chunk]
                m_h = m_vec[:, h:h + 1]
                dlog = jnp.where(causal, b_t - b_s + li_s, NEG)
                inter = b_t + m_h
                m_t = jnp.maximum(inter, jnp.max(dlog, axis=1, keepdims=True))
                w_intra = jnp.exp(dlog - m_t)
                w_inter = jnp.exp(inter - m_t)
                s = _dot_t(qb, kb) * w_intra
                c_st = c_ref[si, h]
                n_st = n_ref[si, h:h + 1, :]
                num = jnp.dot(s.astype(BF16), vb, preferred_element_type=F32) \
                    + w_inter * jnp.dot(qb, c_st.astype(BF16), preferred_element_type=F32)
                den = jnp.sum(s, axis=1, keepdims=True) + w_inter * jnp.sum(qh * n_st, axis=1, keepdims=True)
                hh = num / jnp.maximum(jnp.abs(den), jnp.exp(-m_t))
                b_end = b_t[chunk - 1:chunk, :]
                g_col = b_end - b_t + li_t
                m_new = jnp.maximum(b_end + m_h, jnp.max(g_col, axis=0, keepdims=True))
                w_state = jnp.exp(g_col - m_new)
                decay = jnp.exp(b_end + m_h - m_new)
                kw = kh * w_state
                c_ref[si, h] = decay * c_st + _tdot(kw.astype(BF16), vb)
                n_ref[si, h:h + 1, :] = decay * n_st + jnp.sum(kw, axis=0, keepdims=True)
                m_next = jnp.where(lane1 == h, m_new, m_next)
                hn = hh * lax.rsqrt(jnp.mean(hh * hh, axis=-1, keepdims=True) + EPS) * an_ref[:, hs]
                ya_ref[si, r0:r0 + chunk, hs] = hn * jax.nn.sigmoid(og_ref[si, r0:r0 + chunk, hs])
            m_vec = m_next
        m_ref[si] = m_vec


def _mlstm(proj, nseq, t, conv_w, conv_b, b_gates, a_norm, conv0, c0, n0, m0):
    chunk = A_CHUNK if t % A_CHUNK == 0 else t
    if t > chunk:
        sb, tb = 1, min(t, 4 * chunk)
    else:
        sb, tb = min(nseq, 8), t
    p4 = proj.reshape(proj.shape[0], nseq, t, SLAB)
    slab = lambda s: pl.BlockSpec((None, sb, tb, SLAB), lambda i, j, s=s: (s, i, j, 0))
    full = lambda a: pl.BlockSpec(a.shape, lambda i, j: (0,) * a.ndim)
    st = lambda a: pl.BlockSpec((sb,) + a.shape[1:], lambda i, j: (i,) + (0,) * (a.ndim - 1))
    bg = _pad_cols(b_gates.reshape(1, 2 * A_HEADS), 128)
    conv0p = jnp.pad(conv0, ((0, 0), (8 - (A_CONV - 1), 0), (0, 0)))
    m0p = _pad_cols(m0, 128).reshape(nseq, 1, 128)
    cb = conv_b.reshape(1, -1)
    an = a_norm.reshape(1, -1)
    ya, c, n, m = pl.pallas_call(
        functools.partial(_mlstm_kernel, chunk=chunk),
        grid=(nseq // sb, t // tb),
        in_specs=[slab(0), slab(1), slab(2), slab(3), slab(13),
                  full(conv_w), full(cb), full(bg), full(an),
                  st(conv0p), st(c0), st(n0), st(m0p)],
        out_specs=[pl.BlockSpec((sb, tb, SLAB), lambda i, j: (i, j, 0)), st(c0), st(n0), st(m0p)],
        out_shape=[jax.ShapeDtypeStruct((nseq, t, SLAB), F32),
                   jax.ShapeDtypeStruct(c0.shape, F32), jax.ShapeDtypeStruct(n0.shape, F32),
                   jax.ShapeDtypeStruct(m0p.shape, F32)],
        scratch_shapes=[pltpu.VMEM((sb, 8 + tb, SLAB), F32), pltpu.VMEM((sb, 8 + tb, SLAB), F32)],
        compiler_params=_cparams("parallel", "arbitrary"),
        name="mlstm",
    )(p4, p4, p4, p4, p4, conv_w, cb, bg, an, conv0p, c0, n0, m0p)
    return ya.reshape(nseq * t, SLAB), c, n, m[:, 0, :A_HEADS]


QBLK = 128


def _band_head(q_m, k2, v_m, valid, sink):
    s = jnp.where(valid, _dot_t(q_m, k2), NEG)
    m = jnp.max(s, axis=1, keepdims=True)
    if sink is not None:
        m = jnp.maximum(m, sink)
    e = jnp.exp(s - m)
    l = jnp.sum(e, axis=1, keepdims=True)
    if sink is not None:
        l = l + jnp.exp(sink - m)
    acc = jnp.dot(e.astype(BF16), v_m, preferred_element_type=F32)
    return acc, m, l


def _band_masks(has_prev, window):
    a = lax.broadcasted_iota(jnp.int32, (QBLK, 2 * QBLK), 0)
    c = lax.broadcasted_iota(jnp.int32, (QBLK, 2 * QBLK), 1)
    rel = a - c + QBLK
    band = jnp.logical_and(rel >= 0, rel <= window)
    if has_prev is True:
        return band
    return jnp.logical_and(band, c >= jnp.where(has_prev, 0, QBLK))


def _key_rows(prev_ref, cur_ref, a, ls):
    before = prev_ref[:, ls] if a == 0 else cur_ref[(a - 1) * QBLK:a * QBLK, ls]
    return jnp.concatenate([before, cur_ref[a * QBLK:(a + 1) * QBLK, ls]], axis=0).astype(BF16)


def _dilated_kernel(q_ref, kp_ref, kc_ref, vp_ref, vc_ref, o_ref, lse_ref):
    lane = lax.broadcasted_iota(jnp.int32, (1, 128), 1)
    low = lane < HEAD_DIM
    scale = HEAD_DIM ** -0.5
    for a in range(q_ref.shape[0] // QBLK):
        qs = slice(a * QBLK, (a + 1) * QBLK)
        valid = _band_masks(pl.program_id(2) > 0 if a == 0 else True, QBLK)
        for pr in range(B_HEADS // 2):
            ls = slice(pr * 128, (pr + 1) * 128)
            qp = q_ref[qs, ls] * scale
            k2 = _key_rows(kp_ref, kc_ref, a, ls)
            v2 = _key_rows(vp_ref, vc_ref, a, ls)
            o_h, lse_h = [], []
            for p in range(2):
                half = low if p == 0 else jnp.logical_not(low)
                acc, m, l = _band_head(jnp.where(half, qp, 0.0).astype(BF16), k2, v2, valid, None)
                o_h.append(acc / l)
                lse_h.append(m + jnp.log(l))
            o_ref[qs, ls] = jnp.where(low, o_h[0], o_h[1])
            lse_ref[qs, ls] = jnp.where(low, lse_h[0], lse_h[1])


def _dilated_prompt(proj, b, t, g):
    d = B_DILATIONS[g]
    assert t % PERM_TILE == 0
    nb = PERM_TILE // d // QBLK
    per_tile = PERM_TILE // QBLK
    pv = proj.reshape(proj.shape[0], b, t, SLAB)

    nq = 2 if nb % 2 == 0 else 1

    def blk(i, r):
        return (i // nb) * per_tile + r * nb + i % nb

    cur = lambda s: pl.BlockSpec((None, None, nq * QBLK, SLAB),
                                 lambda bi, r, i, s=s: (s, bi, blk(nq * i, r) // nq, 0))
    prev = lambda s: pl.BlockSpec((None, None, QBLK, SLAB),
                                  lambda bi, r, i, s=s: (s, bi, blk(jnp.maximum(nq * i - 1, 0), r), 0))
    ospec = pl.BlockSpec((None, nq * QBLK, SLAB), lambda bi, r, i: (bi, blk(nq * i, r) // nq, 0))
    sq, sk, sv = 4 + g, 7 + g, 10 + g
    o, lse = pl.pallas_call(
        _dilated_kernel,
        grid=(b, d, t // d // QBLK // nq),
        in_specs=[cur(sq), prev(sk), cur(sk), prev(sv), cur(sv)],
        out_specs=[ospec, ospec],
        out_shape=[jax.ShapeDtypeStruct((b, t, SLAB), F32)] * 2,
        compiler_params=_cparams("parallel", "parallel", "arbitrary"),
        name=f"dilated_g{g}",
    )(pv, pv, pv, pv, pv)
    return o.reshape(b * t, SLAB), lse.reshape(b * t, SLAB)


def _merge_kernel(*refs, dils):
    ins, y_ref, bufs = refs[:6], refs[6], refs[7:]
    tm = y_ref.shape[0]
    vals = []
    nbuf = 0
    for k, ref in enumerate(ins):
        dil = dils[k // 2]
        if dil == 1:
            vals.append(ref[...])
        else:
            buf = bufs[nbuf]
            nbuf += 1
            seg = tm // dil
            for r in range(dil):
                buf[pl.ds(r, seg, stride=dil), :] = ref[r * seg:(r + 1) * seg, :]
            vals.append(buf[...])
    o1, l1, o2, l2, o3, l3 = vals
    m = jnp.maximum(jnp.maximum(l1, l2), l3)
    w1, w2, w3 = jnp.exp(l1 - m), jnp.exp(l2 - m), jnp.exp(l3 - m)
    y_ref[...] = (w1 * o1 + w2 * o2 + w3 * o3) / (w1 + w2 + w3)


def _merge_groups(parts, perm):
    n = parts[0].shape[0]
    dils = B_DILATIONS if perm else (1,) * B_GROUPS
    tm = PERM_TILE if perm else min(1024, n)
    spec = pl.BlockSpec((tm, 128), lambda i, j: (i, j))
    nbuf = 2 * sum(d > 1 for d in dils)
    return pl.pallas_call(
        functools.partial(_merge_kernel, dils=dils), grid=(n // tm, SLAB // 128), in_specs=[spec] * 6,
        out_specs=spec, out_shape=jax.ShapeDtypeStruct((n, SLAB), F32),
        scratch_shapes=[pltpu.VMEM((tm, 128), F32)] * nbuf,
        compiler_params=_cparams("parallel", "parallel"), name="merge_groups",
    )(*parts)


def _swa_kernel(q_ref, kp_ref, kc_ref, vp_ref, vc_ref, sink_ref, o_ref):
    lane = lax.broadcasted_iota(jnp.int32, (1, 128), 1)
    scale = HEAD_DIM ** -0.5
    rep = D_HEADS // D_KV_HEADS
    low = lane < HEAD_DIM
    allk = slice(None)
    for a in range(q_ref.shape[0] // QBLK):
        qs = slice(a * QBLK, (a + 1) * QBLK)
        valid = _band_masks(pl.program_id(1) > 0 if a == 0 else True, D_WINDOW - 1)
        k2 = _key_rows(kp_ref, kc_ref, a, allk)
        v2 = _key_rows(vp_ref, vc_ref, a, allk)
        for pr in range(rep):
            ls = slice(pr * 128, (pr + 1) * 128)
            qp = q_ref[qs, ls] * scale
            o_h = []
            for g in range(D_KV_HEADS):
                half = low if g == 0 else jnp.logical_not(low)
                acc, m, l = _band_head(jnp.where(half, qp, 0.0).astype(BF16), k2, v2, valid,
                                       sink_ref[g * rep + pr])
                o_h.append(acc / l)
            o_ref[qs, ls] = jnp.where(low, o_h[0], o_h[1])


def _swa_prompt(proj, b, t, sinks):
    pv = proj.reshape(proj.shape[0], b, t, SLAB)
    nq = 2 if t % (2 * QBLK) == 0 else 1
    qspec = pl.BlockSpec((None, None, nq * QBLK, SLAB), lambda bi, i: (3, bi, i, 0))

    def kv(col, prev):
        if prev:
            return pl.BlockSpec((None, None, QBLK, 128), lambda bi, i: (4, bi, jnp.maximum(nq * i - 1, 0), col))
        return pl.BlockSpec((None, None, nq * QBLK, 128), lambda bi, i: (4, bi, i, col))

    o = pl.pallas_call(
        _swa_kernel,
        grid=(b, t // QBLK // nq),
        in_specs=[qspec, kv(0, True), kv(0, False), kv(1, True), kv(1, False),
                  pl.BlockSpec(memory_space=pltpu.SMEM)],
        out_specs=pl.BlockSpec((None, nq * QBLK, SLAB), lambda bi, i: (bi, i, 0)),
        out_shape=jax.ShapeDtypeStruct((b, t, SLAB), F32),
        compiler_params=_cparams("parallel", "arbitrary"),
        name="swa_prompt",
    )(pv, pv, pv, pv, pv, sinks)
    return o.reshape(b * t, SLAB)


def _softplus(x):
    return jnp.maximum(x, 0.0) + jnp.log1p(jnp.exp(-jnp.abs(x)))


def _ssd_kernel(z_ref, x_ref, bc_ref, dt_ref, cw_ref, cb_ref, dtb_ref, alog_ref, dl_ref, cn_ref,
                conv0_ref, h0_ref, y_ref, h_ref, xbuf, bcbuf, *, chunk):
    sb, tb, _ = x_ref.shape
    tblk = pl.program_id(1)
    lane1 = lax.broadcasted_iota(jnp.int32, (1, 128), 1)
    low = lane1 < C_HEADDIM
    row = lax.broadcasted_iota(jnp.int32, (chunk, chunk), 0)
    col = lax.broadcasted_iota(jnp.int32, (chunk, chunk), 1)
    causal = col <= row
    tri = causal.astype(BF16)
    srow_low = lax.broadcasted_iota(jnp.int32, (128, 1), 0) < C_HEADDIM
    hpg = C_HEADS // C_NGROUPS
    a_neg = -jnp.exp(alog_ref[...])

    @pl.when(tblk == 0)
    def _():
        h_ref[...] = h0_ref[...]
        for si in range(sb):
            xbuf[si, 0:8, :] = conv0_ref[si, :, 0:SLAB]
            bcbuf[si, 0:8, :] = conv0_ref[si, :, SLAB:2 * SLAB]

    for si in range(sb):
        xc = _conv_silu(xbuf.at[si], x_ref[si], cw_ref[:, 0:SLAB], cb_ref[:, 0:SLAB])
        bcc = _conv_silu(bcbuf.at[si], bc_ref[si], cw_ref[:, SLAB:2 * SLAB], cb_ref[:, SLAB:2 * SLAB])
        for c in range(tb // chunk):
            rs = slice(c * chunk, (c + 1) * chunk)
            dt = _softplus(dt_ref[si, rs, :] + dtb_ref[...])
            cum = _cumsum_rows(dt * a_neg, tri)
            cum_t_all = _transpose_slab(cum)
            dt_t_all = _transpose_slab(dt)
            cum_end = cum[chunk - 1:chunk, :]
            for g in range(C_NGROUPS):
                bg = bcc[rs, g * 128:(g + 1) * 128].astype(BF16)
                cg = bcc[rs, 256 + g * 128:256 + (g + 1) * 128].astype(BF16)
                cb = _dot_t(cg, bg)
                y_pairs = []
                for pp in range(hpg // 2):
                    pr = g * (hpg // 2) + pp
                    ls = slice(pr * 128, (pr + 1) * 128)
                    x_pair = xc[rs, ls]
                    hs = h_ref[si, pr]
                    y_pair = None
                    e_col, w_end, dec = [], [], []
                    for p in range(2):
                        h = 2 * pr + p
                        cum_t = cum[:, h:h + 1]
                        seg = jnp.where(causal, cum_t - cum_t_all[h:h + 1, 0:chunk], NEG)
                        w = cb * jnp.exp(seg) * dt_t_all[h:h + 1, 0:chunk]
                        half = low if p == 0 else jnp.logical_not(low)
                        yh = jnp.dot(w.astype(BF16), jnp.where(half, x_pair, 0.0).astype(BF16),
                                     preferred_element_type=F32)
                        y_pair = yh if y_pair is None else y_pair + yh
                        e_col.append(jnp.exp(cum_t))
                        w_end.append(jnp.exp(cum_end[:, h:h + 1] - cum_t) * dt[:, h:h + 1])
                        dec.append(jnp.exp(cum_end[:, h:h + 1]))
                    inter = _dot_t(cg, hs.astype(BF16)) * jnp.where(low, e_col[0], e_col[1])
                    y_pair = y_pair + inter + dl_ref[:, ls] * x_pair
                    xw = x_pair * jnp.where(low, w_end[0], w_end[1])
                    h_ref[si, pr] = jnp.where(srow_low, dec[0], dec[1]) * hs + _tdot(xw.astype(BF16), bg)
                    zz = z_ref[si, rs, ls]
                    y_pairs.append(y_pair * (zz * jax.nn.sigmoid(zz)))
                ms = sum(jnp.sum(yp * yp, axis=-1, keepdims=True) for yp in y_pairs) / (hpg * C_HEADDIM)
                inv = lax.rsqrt(ms + EPS)
                for pp, yp in enumerate(y_pairs):
                    ls = slice((g * (hpg // 2) + pp) * 128, (g * (hpg // 2) + pp + 1) * 128)
                    y_ref[si, rs, ls] = yp * inv * cn_ref[:, ls]


def _ssd(proj, nseq, t, conv_w, conv_b, dt_bias, a_log, d_skip, c_norm, conv0, h0):
    chunk = C_CHUNK if t % C_CHUNK == 0 else t
    if t > chunk:
        sb, tb = min(nseq, 2), min(t, 2 * chunk)
    else:
        sb, tb = min(nseq, 8), t
    p4 = proj.reshape(proj.shape[0], nseq, t, SLAB)
    slab = lambda s: pl.BlockSpec((None, sb, tb, SLAB), lambda i, j, s=s: (s, i, j, 0))
    full = lambda a: pl.BlockSpec(a.shape, lambda i, j: (0,) * a.ndim)
    st = lambda a: pl.BlockSpec((sb,) + a.shape[1:], lambda i, j: (i,) + (0,) * (a.ndim - 1))
    conv0p = jnp.pad(conv0, ((0, 0), (8 - (C_CONV - 1), 0), (0, 0)))
    hp = h0.reshape(nseq, C_HEADS // 2, 128, C_DSTATE)
    cb = conv_b.reshape(1, -1)
    dtb = _pad_cols(dt_bias.reshape(1, C_HEADS), 128)
    alog = _pad_cols(a_log.reshape(1, C_HEADS), 128)
    dl = jnp.repeat(d_skip, C_HEADDIM).reshape(1, C_DINNER)
    cn = c_norm.reshape(1, C_DINNER)
    y, h = pl.pallas_call(
        functools.partial(_ssd_kernel, chunk=chunk),
        grid=(nseq // sb, t // tb),
        in_specs=[slab(0), slab(1), slab(2),
                  pl.BlockSpec((None, sb, tb, 128), lambda i, j: (4, i, j, 2)),
                  full(conv_w), full(cb), full(dtb), full(alog), full(dl), full(cn), st(conv0p), st(hp)],
        out_specs=[pl.BlockSpec((sb, tb, SLAB), lambda i, j: (i, j, 0)), st(hp)],
        out_shape=[jax.ShapeDtypeStruct((nseq, t, SLAB), F32), jax.ShapeDtypeStruct(hp.shape, F32)],
        scratch_shapes=[pltpu.VMEM((sb, 8 + tb, SLAB), F32), pltpu.VMEM((sb, 8 + tb, SLAB), F32)],
        compiler_params=_cparams("parallel", "arbitrary"),
        name="ssd",
    )(p4, p4, p4, p4, conv_w, cb, dtb, alog, dl, cn, conv0p, hp)
    return y.reshape(nseq * t, SLAB), h.reshape(nseq, C_HEADS, C_HEADDIM, C_DSTATE)


FFN_CHUNK = 256


def _post_kernel(x_ref, ya_ref, yb_ref, wo_ref, fn_ref, wg_ref, wu_ref, wd_ref, on_ref, o_ref, act_ref, *, final):
    ymix = jnp.concatenate([ya_ref[...], yb_ref[...]], axis=-1).astype(BF16)
    x1 = x_ref[...] + jnp.dot(ymix, wo_ref[...], preferred_element_type=F32)
    ms = jnp.mean(x1 * x1, axis=-1, keepdims=True)
    hn = (x1 * lax.rsqrt(ms + EPS) * fn_ref[...]).astype(BF16)
    hidden = wg_ref.shape[1]
    for c in range(hidden // FFN_CHUNK):
        cs = slice(c * FFN_CHUNK, (c + 1) * FFN_CHUNK)
        gate = jnp.dot(hn, wg_ref[:, cs], preferred_element_type=F32)
        up = jnp.dot(hn, wu_ref[:, cs], preferred_element_type=F32)
        act_ref[:, cs] = (gate * jax.nn.sigmoid(gate) * up).astype(BF16)
    x2 = x1 + jnp.dot(act_ref[...], wd_ref[...], preferred_element_type=F32)
    if final:
        ms2 = jnp.mean(x2 * x2, axis=-1, keepdims=True)
        x2 = x2 * lax.rsqrt(ms2 + EPS) * on_ref[...]
    o_ref[...] = x2


def _post(x, ya, yb, w_out, layer, ffn_norm, w_gate, w_up, w_down, out_norm, final):
    n = x.shape[0]
    tm = min(1024, n)
    hidden = w_gate.shape[2]
    row = lambda w: pl.BlockSpec((tm, w), lambda i: (i, 0))
    res = lambda a: pl.BlockSpec(a.shape, lambda i: (0, 0), pipeline_mode=pl.Buffered(1))
    lay = lambda a: pl.BlockSpec((None,) + a.shape[1:], lambda i: (layer, 0, 0), pipeline_mode=pl.Buffered(1))
    fn = ffn_norm.reshape(-1, 1, D_MODEL)
    on = out_norm.reshape(1, D_MODEL)
    return pl.pallas_call(
        functools.partial(_post_kernel, final=final),
        grid=(n // tm,),
        in_specs=[row(D_MODEL), row(SLAB), row(SLAB), res(w_out), lay(fn), lay(w_gate), lay(w_up), lay(w_down),
                  res(on)],
        out_specs=row(D_MODEL),
        out_shape=jax.ShapeDtypeStruct((n, D_MODEL), F32),
        scratch_shapes=[pltpu.VMEM((tm, hidden), BF16)],
        compiler_params=_cparams("parallel"),
        name="post_ffn",
    )(x, ya, yb, w_out, fn, w_gate, w_up, w_down, on)


RIDER_PHASES = 4


def _rider_mask(ts, r, window, step):
    t = lax.broadcasted_iota(jnp.int32, (2 * ts, r + 128), 0) % ts
    c = lax.broadcasted_iota(jnp.int32, (2 * ts, r + 128), 1)
    rel = r + t - jnp.where(c < 128, c, c - 128 + ts)
    valid = jnp.logical_and(jnp.logical_and(rel >= 0, rel <= window), (rel & (step - 1)) == 0)
    return jnp.logical_and(valid, jnp.logical_or(c < ts, c >= 128))


def _shift_rows_minor(src_ref, new_t, rows, ts):
    r = src_ref.shape[1]
    lane = lax.broadcasted_iota(jnp.int32, (1, 128), 1)
    old = src_ref[rows, :]
    rolled = pltpu.roll(old, r - ts, 1)
    tail = jnp.where(lane < 128 - ts, rolled[:, r - 128:], new_t[rows, :])
    out = tail if r == 128 else jnp.concatenate([rolled[:, :r - 128], tail], axis=1)
    return old, out


def _new_rows_t(new_ref):
    ts = new_ref.shape[0]
    pad = jnp.zeros((128 - ts, SLAB), F32)
    return jnp.concatenate([pad, new_ref[...]], axis=0).T


def _rider_k(q_ref, kn_ref, kc_ref, ko_ref, p_ref, lse_ref, window, step):
    ts = q_ref.shape[0]
    r = kc_ref.shape[1]
    low = lax.broadcasted_iota(jnp.int32, (1, 128), 1) < HEAD_DIM
    valid = _rider_mask(ts, r, window, step)
    kn_t = _new_rows_t(kn_ref)
    scale = HEAD_DIM ** -0.5
    for pr in range(B_HEADS // 2):
        rows = slice(pr * 128, (pr + 1) * 128)
        k_old, k_out = _shift_rows_minor(kc_ref, kn_t, rows, ts)
        ko_ref[rows, :] = k_out
        k_ext = jnp.concatenate([k_old[:, 0:128], k_out], axis=1).astype(BF16)
        qp = q_ref[:, rows] * scale
        q2 = jnp.concatenate([jnp.where(low, qp, 0.0), jnp.where(low, 0.0, qp)], axis=0).astype(BF16)
        s = jnp.where(valid, jnp.dot(q2, k_ext, preferred_element_type=F32), NEG)
        m = jnp.max(s, axis=1, keepdims=True)
        e = jnp.exp(s - m)
        l = jnp.sum(e, axis=1, keepdims=True)
        p_ref[pr * 2 * ts:(pr + 1) * 2 * ts, :] = (e / l).astype(BF16)
        lse = m + jnp.log(l)
        lse_ref[:, rows] = jnp.where(low, lse[0:ts], lse[ts:2 * ts])


def _rider_v(p_ref, vn_ref, vc_ref, vo_ref, o_ref):
    ts = vn_ref.shape[0]
    low = lax.broadcasted_iota(jnp.int32, (1, 128), 1) < HEAD_DIM
    vn_t = _new_rows_t(vn_ref)
    for pr in range(B_HEADS // 2):
        rows = slice(pr * 128, (pr + 1) * 128)
        v_old, v_out = _shift_rows_minor(vc_ref, vn_t, rows, ts)
        vo_ref[rows, :] = v_out
        v_ext = jnp.concatenate([v_old[:, 0:128], v_out], axis=1).astype(BF16)
        a = _dot_t(p_ref[pr * 2 * ts:(pr + 1) * 2 * ts, :], v_ext)
        o_ref[:, rows] = jnp.where(low, a[0:ts], a[ts:2 * ts])


def _post_rider_kernel(x_ref, ya_ref, yb_ref, wo_ref, fn_ref, wg_ref, wu_ref, wd_ref, on_ref, *rest,
                       final, side, window, step):
    if side == 'k':
        q_ref, kn_ref, kc_ref, o_ref, ko_ref, p_ref, lse_ref, act_ref, x1_ref, hn_ref = rest
        ride = functools.partial(_rider_k, q_ref, kn_ref, kc_ref, ko_ref, p_ref, lse_ref, window, step)
    else:
        p_ref, vn_ref, vc_ref, o_ref, vo_ref, ro_ref, act_ref, x1_ref, hn_ref = rest
        ride = functools.partial(_rider_v, p_ref, vn_ref, vc_ref, vo_ref, ro_ref)
    u = pl.program_id(1)
    nchunk = wg_ref.shape[1] // FFN_CHUNK
    per = -(-nchunk // (RIDER_PHASES - 1))
    first = nchunk - per * (RIDER_PHASES - 2)

    def hidden_chunks(lo, hi):
        hn = hn_ref[...]
        for c in range(lo, hi):
            cs = slice(c * FFN_CHUNK, (c + 1) * FFN_CHUNK)
            gate = jnp.dot(hn, wg_ref[:, cs], preferred_element_type=F32)
            up = jnp.dot(hn, wu_ref[:, cs], preferred_element_type=F32)
            act_ref[:, cs] = (gate * jax.nn.sigmoid(gate) * up).astype(BF16)

    def phase(k):
        def body():
            if k == 0:
                ymix = jnp.concatenate([ya_ref[...], yb_ref[...]], axis=-1).astype(BF16)
                x1 = x_ref[...] + jnp.dot(ymix, wo_ref[...], preferred_element_type=F32)
                ms = jnp.mean(x1 * x1, axis=-1, keepdims=True)
                x1_ref[...] = x1
                hn_ref[...] = (x1 * lax.rsqrt(ms + EPS) * fn_ref[...]).astype(BF16)
                hidden_chunks(0, first)
            elif k < RIDER_PHASES - 1:
                hidden_chunks(first + (k - 1) * per, first + k * per)
            else:
                x2 = x1_ref[...] + jnp.dot(act_ref[...], wd_ref[...], preferred_element_type=F32)
                if final:
                    ms2 = jnp.mean(x2 * x2, axis=-1, keepdims=True)
                    x2 = x2 * lax.rsqrt(ms2 + EPS) * on_ref[...]
                o_ref[...] = x2
            ride()
        return body

    for k in range(RIDER_PHASES):
        pl.when(u == k)(phase(k))


def _post_rider(x, ya, yb, w_out, layer, ffn_norm, w_gate, w_up, w_down, out_norm, final, side, g, proj_s,
                cache_t, probs):
    n = x.shape[0]
    nseq, _, r = cache_t.shape
    ts = proj_s.shape[1] // nseq
    tm = n * RIDER_PHASES // nseq
    hidden = w_gate.shape[2]
    row = lambda w: pl.BlockSpec((tm, w), lambda i, u: (i, 0))
    res = lambda a: pl.BlockSpec(a.shape, lambda i, u: (0, 0), pipeline_mode=pl.Buffered(1))
    lay = lambda a: pl.BlockSpec((None,) + a.shape[1:], lambda i, u: (layer, 0, 0), pipeline_mode=pl.Buffered(1))
    seq = lambda i, u: i * RIDER_PHASES + u
    new = lambda s: pl.BlockSpec((None, ts, SLAB), lambda i, u: (s, seq(i, u), 0))
    cspec = pl.BlockSpec((None, SLAB, r), lambda i, u: (seq(i, u), 0, 0))
    pspec = pl.BlockSpec((None, B_HEADS * ts, r + 128), lambda i, u: (seq(i, u), 0, 0))
    rows8 = pl.BlockSpec((ts, SLAB), lambda i, u: (seq(i, u), 0))
    fn = ffn_norm.reshape(-1, 1, D_MODEL)
    on = out_norm.reshape(1, D_MODEL)
    common = [row(D_MODEL), row(SLAB), row(SLAB), res(w_out), lay(fn), lay(w_gate), lay(w_up), lay(w_down), res(on)]
    xout = jax.ShapeDtypeStruct((n, D_MODEL), F32)
    cout = jax.ShapeDtypeStruct(cache_t.shape, F32)
    small = jax.ShapeDtypeStruct((nseq * ts, SLAB), F32)
    if side == 'k':
        ins = [new(4 + g), new(7 + g), cspec]
        args = (proj_s, proj_s, cache_t)
        outs = [row(D_MODEL), cspec, pspec, rows8]
        shapes = [xout, cout, jax.ShapeDtypeStruct((nseq, B_HEADS * ts, r + 128), BF16), small]
    else:
        ins = [pspec, new(10 + g), cspec]
        args = (probs, proj_s, cache_t)
        outs = [row(D_MODEL), cspec, rows8]
        shapes = [xout, cout, small]
    return pl.pallas_call(
        functools.partial(_post_rider_kernel, final=final, side=side, window=B_WINDOWS[g], step=B_DILATIONS[g]),
        grid=(n // tm, RIDER_PHASES),
        in_specs=common + ins,
        out_specs=outs,
        out_shape=shapes,
        scratch_shapes=[pltpu.VMEM((tm, hidden), BF16), pltpu.VMEM((tm, D_MODEL), F32),
                        pltpu.VMEM((tm, D_MODEL), BF16)],
        compiler_params=_cparams("parallel", "arbitrary", vmem=VMEM_LIMIT_RIDER),
        name=f"post_ffn_ride_{side}",
    )(x, ya, yb, w_out, fn, w_gate, w_up, w_down, on, *args)


NEWPAD = 128


def _ext_mask(rows, r, ts, window, step):
    t = lax.broadcasted_iota(jnp.int32, (rows, r + NEWPAD), 0) % ts
    c = lax.broadcasted_iota(jnp.int32, (rows, r + NEWPAD), 1)
    rel = r + t - c
    ok = jnp.logical_and(jnp.logical_and(rel >= 0, rel <= window), c < r + ts)
    return jnp.logical_and(ok, (rel & (step - 1)) == 0)


def _shift_cache(dst_ref, src_ref, new, si, r, ts):
    dst_ref[si, 0:r - ts, :] = src_ref[si, ts:r, :]
    dst_ref[si, r - ts:r, :] = new


def _pad_new(x):
    return jnp.concatenate([x, jnp.zeros((NEWPAD - x.shape[0], x.shape[1]), x.dtype)], axis=0)


def _dil_sample_kernel(q_ref, kn_ref, vn_ref, kc_ref, vc_ref, o_ref, lse_ref, ko_ref, vo_ref,
                       *, window, step):
    sb, ts, _ = q_ref.shape
    r = kc_ref.shape[2]
    scale = HEAD_DIM ** -0.5
    lane = lax.broadcasted_iota(jnp.int32, (1, 128), 1)
    low = lane < HEAD_DIM
    t = lax.broadcasted_iota(jnp.int32, (2 * ts, r + 128), 0) % ts
    c = lax.broadcasted_iota(jnp.int32, (2 * ts, r + 128), 1)
    rel = r + t - jnp.where(c < 128, c, c - 128 + ts)
    valid = jnp.logical_and(jnp.logical_and(rel >= 0, rel <= window), (rel & (step - 1)) == 0)
    valid = jnp.logical_and(valid, jnp.logical_or(c < ts, c >= 128))
    pad = jnp.zeros((128 - ts, SLAB), F32)

    def shifted(src_ref, new_t, rows):
        old = src_ref[rows, :]
        rolled = pltpu.roll(old, r - ts, 1)
        tail = jnp.where(lane < 128 - ts, rolled[:, r - 128:], new_t[rows, :])
        out = tail if r == 128 else jnp.concatenate([rolled[:, :r - 128], tail], axis=1)
        return old, out

    for si, pr in [(si, pr) for si in range(sb) for pr in range(B_HEADS // 2)]:
        if pr == 0:
            kn_t = jnp.concatenate([pad, kn_ref[si]], axis=0).T
            vn_t = jnp.concatenate([pad, vn_ref[si]], axis=0).T
        rows = slice(pr * 128, (pr + 1) * 128)
        k_old, k_out = shifted(kc_ref.at[si], kn_t, rows)
        v_old, v_out = shifted(vc_ref.at[si], vn_t, rows)
        ko_ref[si, rows, :] = k_out
        vo_ref[si, rows, :] = v_out
        k_ext = jnp.concatenate([k_old[:, 0:128], k_out], axis=1).astype(BF16)
        v_ext = jnp.concatenate([v_old[:, 0:128], v_out], axis=1).astype(BF16)
        qp = q_ref[si, :, rows] * scale
        q2 = jnp.concatenate([jnp.where(low, qp, 0.0), jnp.where(low, 0.0, qp)], axis=0).astype(BF16)
        s = jnp.where(valid, jnp.dot(q2, k_ext, preferred_element_type=F32), NEG)
        m = jnp.max(s, axis=1, keepdims=True)
        e = jnp.exp(s - m)
        l = jnp.sum(e, axis=1, keepdims=True)
        a = _dot_t(e.astype(BF16), v_ext) / l
        lse = m + jnp.log(l)
        o_ref[si, :, rows] = jnp.where(low, a[0:ts], a[ts:2 * ts])
        lse_ref[si, :, rows] = jnp.where(low, lse[0:ts], lse[ts:2 * ts])


def _rows_minor(cache):
    nseq, r = cache.shape[:2]
    return jnp.transpose(cache, (0, 2, 3, 1)).reshape(nseq, -1, r)


def _rows_major(cache_t, heads):
    nseq, _, r = cache_t.shape
    return jnp.transpose(cache_t.reshape(nseq, heads, HEAD_DIM, r), (0, 3, 1, 2))


def _dilated_sample(proj, nseq, ts, g, k_cache, v_cache):
    r = k_cache.shape[1]
    assert ts % 8 == 0 and r % 128 == 0 and r >= B_WINDOWS[g]
    kt, vt = _rows_minor(k_cache), _rows_minor(v_cache)
    sb = max(1, min(nseq, 8, 1024 // r))
    p4 = proj.reshape(proj.shape[0], nseq, ts, SLAB)
    new = lambda s: pl.BlockSpec((None, sb, ts, SLAB), lambda i, s=s: (s, i, 0, 0))
    cspec = pl.BlockSpec((sb, SLAB, r), lambda i: (i, 0, 0))
    ospec = pl.BlockSpec((sb, ts, SLAB), lambda i: (i, 0, 0))
    o, lse, kn, vn = pl.pallas_call(
        functools.partial(_dil_sample_kernel, window=B_WINDOWS[g], step=B_DILATIONS[g]),
        grid=(nseq // sb,),
        in_specs=[new(4 + g), new(7 + g), new(10 + g), cspec, cspec],
        out_specs=[ospec, ospec, cspec, cspec],
        out_shape=[jax.ShapeDtypeStruct((nseq, ts, SLAB), F32)] * 2 + [jax.ShapeDtypeStruct(kt.shape, F32)] * 2,
        compiler_params=_cparams("parallel"),
        name=f"dilated_sample_g{g}",
    )(p4, p4, p4, kt, vt)
    return (o.reshape(nseq * ts, SLAB), lse.reshape(nseq * ts, SLAB),
            _rows_major(kn, B_HEADS), _rows_major(vn, B_HEADS))


def _swa_sample_kernel(q_ref, k_ref, v_ref, kc_ref, vc_ref, sink_ref, o_ref, kn_ref, vn_ref):
    sb, ts, _ = q_ref.shape
    r = kc_ref.shape[1]
    rows = D_HEADS * ts
    valid = _ext_mask(rows, r, ts, D_WINDOW - 1, 1)
    lane = lax.broadcasted_iota(jnp.int32, (1, 128), 1)
    low = lane < HEAD_DIM
    rep = D_HEADS // D_KV_HEADS
    scale = HEAD_DIM ** -0.5
    sink = sink_ref[:, 0:1]
    for si in range(sb):
        kn, vn = k_ref[si], v_ref[si]
        pieces = []
        for pr in range(rep):
            base = q_ref[si, :, pr * 128:(pr + 1) * 128] * scale
            for g in range(D_KV_HEADS):
                pieces.append(jnp.where(low if g == 0 else jnp.logical_not(low), base, 0.0))
        qrows = jnp.concatenate(pieces, axis=0).astype(BF16)
        kext = jnp.concatenate([kc_ref[si], _pad_new(kn)], axis=0).astype(BF16)
        vext = jnp.concatenate([vc_ref[si], _pad_new(vn)], axis=0).astype(BF16)
        s = jnp.where(valid, _dot_t(qrows, kext), NEG)
        m = jnp.maximum(jnp.max(s, axis=1, keepdims=True), sink)
        e = jnp.exp(s - m)
        l = jnp.sum(e, axis=1, keepdims=True) + jnp.exp(sink - m)
        o = jnp.dot(e.astype(BF16), vext, preferred_element_type=F32) / l
        pairs = [jnp.where(low, o[2 * pr * ts:(2 * pr + 1) * ts], o[(2 * pr + 1) * ts:(2 * pr + 2) * ts])
                 for pr in range(rep)]
        o_ref[si] = jnp.concatenate(pairs, axis=1)
        _shift_cache(kn_ref, kc_ref, kn, si, r, ts)
        _shift_cache(vn_ref, vc_ref, vn, si, r, ts)


def _swa_sample(proj, nseq, ts, sinks, k_cache, v_cache):
    r = k_cache.shape[1]
    sb = min(nseq, 8)
    kvw = D_KV_HEADS * HEAD_DIM
    p4 = proj.reshape(proj.shape[0], nseq, ts, SLAB)
    kc = k_cache.reshape(nseq, r, kvw)
    vc = v_cache.reshape(nseq, r, kvw)
    cspec = pl.BlockSpec((sb, r, kvw), lambda i: (i, 0, 0))
    new = lambda col: pl.BlockSpec((None, sb, ts, kvw), lambda i: (4, i, 0, col))
    rep = D_HEADS // D_KV_HEADS
    by_row_group = sinks.reshape(D_KV_HEADS, rep).T.reshape(-1)
    sink_rows = jnp.broadcast_to(jnp.repeat(by_row_group, ts)[:, None], (D_HEADS * ts, 128))
    o, kn, vn = pl.pallas_call(
        _swa_sample_kernel,
        grid=(nseq // sb,),
        in_specs=[pl.BlockSpec((None, sb, ts, SLAB), lambda i: (3, i, 0, 0)), new(0), new(1), cspec, cspec,
                  pl.BlockSpec(sink_rows.shape, lambda i: (0, 0))],
        out_specs=[pl.BlockSpec((sb, ts, SLAB), lambda i: (i, 0, 0)), cspec, cspec],
        out_shape=[jax.ShapeDtypeStruct((nseq, ts, SLAB), F32)] + [jax.ShapeDtypeStruct(kc.shape, F32)] * 2,
        compiler_params=_cparams("parallel"),
        name="swa_sample",
    )(p4, p4, p4, kc, vc, sink_rows)
    return o.reshape(nseq * ts, SLAB), kn.reshape(k_cache.shape), vn.reshape(v_cache.shape)


def kernel(x_prompt, x_sample, state_a_C, state_a_n, state_a_m, state_a_conv, cache_b_k1, cache_b_v1, cache_b_k2, cache_b_v2, cache_b_k3, cache_b_v3, state_c_conv, state_c_ssm, cache_d_k, cache_d_v, e_norm_mix, e_w_in, e_b_gates, e_a_conv_w, e_a_conv_b, e_a_norm, e_w_out, o_norm_mix, o_w_in, o_c_conv_w, o_c_conv_b, o_c_dt_bias, o_c_A_log, o_c_D, o_c_norm, o_d_sinks, o_w_out, ffn_norm, w_gate, w_up, w_down, final_norm):
    depth = ffn_norm.shape[0]
    even_w = [_even_slabs(e_w_in[j]) for j in range((depth + 1) // 2)]
    odd_w = [_odd_slabs(o_w_in[j]) for j in range(depth // 2)]
    odd_wo = [jnp.concatenate([o_w_out[j][:C_DINNER], _group_aligned(o_w_out[j][C_DINNER:], 0)],
                              axis=0).astype(BF16) for j in range(depth // 2)]
    even_wo = [e_w_out[j].astype(BF16) for j in range((depth + 1) // 2)]
    wg16, wu16, wd16 = w_gate.astype(BF16), w_up.astype(BF16), w_down.astype(BF16)

    def trunk(x, pos, fresh, even_st, odd_st, ride=None, pre=None):
        nseq, t, _ = x.shape
        rode = {}
        assert t >= A_CONV - 1 and t >= C_CONV - 1
        cos_t, sin_t = _rope_tables(pos)
        x2 = x.reshape(nseq * t, D_MODEL)
        new_even, new_odd = [], []

        def tail(proj, s, keep, dil=1, lanes=slice(None)):
            p4 = proj.reshape(proj.shape[0], nseq, t, SLAB)
            if dil == 1:
                return p4[s, :, t - keep:, lanes]
            seg = PERM_TILE // dil
            last = p4[s, :, t - PERM_TILE:, lanes].reshape(nseq, dil, seg, -1)[:, :, seg - keep // dil:]
            return jnp.swapaxes(last, 1, 2).reshape(nseq, keep, -1)

        for layer in range(depth):
            j = layer // 2
            last = layer == depth - 1
            if layer % 2 == 0:
                if pre is not None and layer == 0:
                    pe = pre['proj']
                else:
                    pe = _norm_proj(x2, e_norm_mix[j], even_w[j], cos_t, sin_t, _even_kinds(fresh))
                st = {k: v[j] for k, v in even_st.items()}
                ya, c_new, n_new, m_new = _mlstm(pe, nseq, t, e_a_conv_w[j], e_a_conv_b[j], e_b_gates[j],
                                                 e_a_norm[j], st['a_conv'], st['a_C'], st['a_n'], st['a_m'])
                ns = {'a_C': c_new, 'a_n': n_new, 'a_m': m_new,
                      'a_conv': jnp.concatenate([tail(pe, 0, A_CONV - 1), tail(pe, 1, A_CONV - 1)], axis=-1)}
                parts = []
                for g in range(B_GROUPS):
                    kn, vn = f'b_k{g + 1}', f'b_v{g + 1}'
                    if fresh:
                        o_g, lse_g = _dilated_prompt(pe, nseq, t, g)
                        keep = min(B_WINDOWS[g], t)
                        ns[kn] = tail(pe, 7 + g, keep, B_DILATIONS[g]).reshape(nseq, keep, B_HEADS, HEAD_DIM)
                        ns[vn] = tail(pe, 10 + g, keep, B_DILATIONS[g]).reshape(nseq, keep, B_HEADS, HEAD_DIM)
                    elif pre is not None and layer == 0 and g == pre['g']:
                        o_g, lse_g, ns[kn], ns[vn] = pre['o'], pre['lse'], pre['k'], pre['v']
                    else:
                        o_g, lse_g, ns[kn], ns[vn] = _dilated_sample(pe, nseq, t, g, st[kn], st[vn])
                    parts += [o_g, lse_g]
                yb = _merge_groups(parts, fresh)
                if ride is not None and layer == 0:
                    x2, rode['k'], rode['p'], rode['lse'] = _post_rider(
                        x2, ya, yb, even_wo[j], layer, ffn_norm, wg16, wu16, wd16, final_norm, last, 'k',
                        ride['g'], ride['proj'], ride['kt'], None)
                else:
                    x2 = _post(x2, ya, yb, even_wo[j], layer, ffn_norm, wg16, wu16, wd16, final_norm, last)
                new_even.append(ns)
            else:
                po = _norm_proj(x2, o_norm_mix[j], odd_w[j], cos_t, sin_t, ODD_KINDS)
                st = {k: v[j] for k, v in odd_st.items()}
                yc, ssm_new = _ssd(po, nseq, t, o_c_conv_w[j], o_c_conv_b[j], o_c_dt_bias[j], o_c_A_log[j],
                                   o_c_D[j], o_c_norm[j], st['c_conv'], st['c_ssm'])
                ns = {'c_conv': jnp.concatenate([tail(po, 1, C_CONV - 1), tail(po, 2, C_CONV - 1)], axis=-1),
                      'c_ssm': ssm_new}
                kvw = D_KV_HEADS * HEAD_DIM
                if fresh:
                    yd = _swa_prompt(po, nseq, t, o_d_sinks[j])
                    keep = min(D_WINDOW, t)
                    ns['d_k'] = tail(po, 4, keep, lanes=slice(0, kvw)).reshape(nseq, keep, D_KV_HEADS, HEAD_DIM)
                    ns['d_v'] = tail(po, 4, keep, lanes=slice(kvw, 2 * kvw)).reshape(
                        nseq, keep, D_KV_HEADS, HEAD_DIM)
                else:
                    yd, ns['d_k'], ns['d_v'] = _swa_sample(po, nseq, t, o_d_sinks[j], st['d_k'], st['d_v'])
                if ride is not None and layer == 1:
                    x2, rode['v'], rode['o'] = _post_rider(
                        x2, yc, yd, odd_wo[j], layer, ffn_norm, wg16, wu16, wd16, final_norm, last, 'v',
                        ride['g'], ride['proj'], ride['vt'], rode['p'])
                else:
                    x2 = _post(x2, yc, yd, odd_wo[j], layer, ffn_norm, wg16, wu16, wd16, final_norm, last)
                new_odd.append(ns)
        out = {k: jnp.stack([ns[k] for ns in new_even]) for k in new_even[0]}
        out.update({k: jnp.stack([ns[k] for ns in new_odd]) for k in new_odd[0]})
        return x2.reshape(nseq, t, D_MODEL), out, rode

    bp, tp, _ = x_prompt.shape
    bs, ts, _ = x_sample.shape
    n_even, n_odd = (depth + 1) // 2, depth // 2
    zeros = lambda *shape: jnp.zeros(shape, F32)
    even_p = {'a_C': zeros(n_even, bp, A_HEADS, A_DK, A_DV), 'a_n': zeros(n_even, bp, A_HEADS, A_DK),
              'a_m': zeros(n_even, bp, A_HEADS), 'a_conv': zeros(n_even, bp, A_CONV - 1, 2 * SLAB)}
    odd_p = {'c_conv': zeros(n_odd, bp, C_CONV - 1, 2 * SLAB),
             'c_ssm': zeros(n_odd, bp, C_HEADS, C_HEADDIM, C_DSTATE)}
    even_s = {'a_C': state_a_C, 'a_n': state_a_n, 'a_m': state_a_m, 'a_conv': state_a_conv,
              'b_k1': cache_b_k1, 'b_v1': cache_b_v1, 'b_k2': cache_b_k2, 'b_v2': cache_b_v2,
              'b_k3': cache_b_k3, 'b_v3': cache_b_v3}
    odd_s = {'c_conv': state_c_conv, 'c_ssm': state_c_ssm, 'd_k': cache_d_k, 'd_v': cache_d_v}
    pos_p = jnp.arange(tp, dtype=jnp.int32)
    pos_s = jnp.tile(PAST_LEN + jnp.arange(ts, dtype=jnp.int32), bs)

    g_ride = B_GROUPS - 1
    tile = bp * tp * RIDER_PHASES // bs
    can_ride = (depth >= 2 and bp * tp * RIDER_PHASES % bs == 0 and tile % 8 == 0 and 8 <= tile <= 1024
                and ts == 8 and cache_b_k3.shape[2] % 128 == 0 and cache_b_k3.shape[2] >= B_WINDOWS[g_ride])
    if can_ride:
        cos_s, sin_s = _rope_tables(pos_s)
        pe_s = _norm_proj(x_sample.reshape(bs * ts, D_MODEL), e_norm_mix[0], even_w[0], cos_s, sin_s,
                          _even_kinds(False))
        ride = {'g': g_ride, 'proj': pe_s, 'kt': _rows_minor(cache_b_k3[0]), 'vt': _rows_minor(cache_b_v3[0])}
        y_p, pn, rode = trunk(x_prompt, pos_p, True, even_p, odd_p, ride=ride)
        pre = {'g': g_ride, 'proj': pe_s, 'o': rode['o'], 'lse': rode['lse'],
               'k': _rows_major(rode['k'], B_HEADS), 'v': _rows_major(rode['v'], B_HEADS)}
        y_s, sn, _ = trunk(x_sample, pos_s, False, even_s, odd_s, pre=pre)
    else:
        y_p, pn, _ = trunk(x_prompt, pos_p, True, even_p, odd_p)
        y_s, sn, _ = trunk(x_sample, pos_s, False, even_s, odd_s)

    keys = ('a_C', 'a_n', 'a_m', 'a_conv', 'b_k1', 'b_v1', 'b_k2', 'b_v2', 'b_k3', 'b_v3',
            'c_conv', 'c_ssm', 'd_k', 'd_v')
    states = []
    for k in keys:
        states += [pn[k], sn[k]]
    return (y_p, y_s, *states)
```

```python
import functools
import math

import jax
import jax.numpy as jnp
from jax import lax
from jax.experimental import pallas as pl
from jax.experimental.pallas import tpu as pltpu

F32 = jnp.float32
BF16 = jnp.bfloat16

D_MODEL = 1024
PAST_LEN = 16384
EPS = 1e-6
ROPE_THETA = 10000.0
HEAD_DIM = 64
SLAB = 512

A_HEADS = 4
A_DK = 128
A_DV = 128
A_CONV = 4
A_CHUNK = 64

B_HEADS = 8
B_WINDOWS = (128, 512, 2048)
B_DILATIONS = (1, 4, 16)
B_GROUPS = 3

C_HEADS = 8
C_HEADDIM = 64
C_DINNER = C_HEADS * C_HEADDIM
C_DSTATE = 128
C_NGROUPS = 2
C_CONV = 4
C_CHUNK = 128

D_HEADS = 8
D_KV_HEADS = 2
D_WINDOW = 128

NEG = -1e30
VMEM_LIMIT = 56 * 1024 * 1024


VMEM_LIMIT_RIDER = 60 * 1024 * 1024


def _cparams(*sem, vmem=VMEM_LIMIT):
    return pltpu.CompilerParams(dimension_semantics=sem, vmem_limit_bytes=vmem)


def _rot_half(y):
    lane = lax.broadcasted_iota(jnp.int32, y.shape, 1)
    fwd = pltpu.roll(y, SLAB - HEAD_DIM // 2, 1)
    bwd = pltpu.roll(y, HEAD_DIM // 2, 1)
    return jnp.where(lane % HEAD_DIM < HEAD_DIM // 2, fwd, bwd)


PERM_TILE = 2048
PROJ_ROWS = 512
PERM_STRIDE = 4


def _proj_kernel(x_ref, nw_ref, w_ref, cos_ref, sin_ref, o_ref, xn_ref, *ybuf, kinds):
    j = pl.program_id(1)
    tm = o_ref.shape[0]

    @pl.when(j == 0)
    def _():
        x = x_ref[...]
        ms = jnp.mean(x * x, axis=-1, keepdims=True)
        xn_ref[...] = (x * lax.rsqrt(ms + EPS) * nw_ref[...]).astype(BF16)

    rc = min(tm, PROJ_ROWS)

    def emit(rope_lanes, dil):
        seg = tm // dil
        cseg = rc // dil
        for c in range(tm // rc):
            rows = slice(c * rc, (c + 1) * rc)
            y = jnp.dot(xn_ref[rows, :], w_ref[...], preferred_element_type=F32)
            val = y
            if rope_lanes:
                cos = jnp.tile(cos_ref[rows, :], (1, SLAB // 128))
                sin = jnp.tile(sin_ref[rows, :], (1, SLAB // 128))
                val = y * cos + _rot_half(y) * sin
                if rope_lanes < SLAB:
                    lane = lax.broadcasted_iota(jnp.int32, y.shape, 1)
                    val = jnp.where(lane < rope_lanes, val, y)
            if dil == 1:
                o_ref[rows, :] = val
            else:
                for cb in range(SLAB // 128):
                    ybuf[0][cb, rows, :] = val[:, cb * 128:(cb + 1) * 128]
                if dil <= PERM_STRIDE:
                    for r in range(dil):
                        dst = slice(r * seg + c * cseg, r * seg + (c + 1) * cseg)
                        for cb in range(SLAB // 128):
                            o_ref[dst, cb * 128:(cb + 1) * 128] = ybuf[0][cb, pl.ds(c * rc + r, cseg, stride=dil), :]
                else:
                    d2 = dil // PERM_STRIDE
                    half = rc // PERM_STRIDE
                    for cb in range(SLAB // 128):
                        for r0 in range(PERM_STRIDE):
                            ybuf[1][cb, r0 * half:(r0 + 1) * half, :] = \
                                ybuf[0][cb, pl.ds(c * rc + r0, half, stride=PERM_STRIDE), :]
                    for r0 in range(PERM_STRIDE):
                        for r1 in range(d2):
                            r = r1 * PERM_STRIDE + r0
                            dst = slice(r * seg + c * cseg, r * seg + (c + 1) * cseg)
                            for cb in range(SLAB // 128):
                                o_ref[dst, cb * 128:(cb + 1) * 128] = \
                                    ybuf[1][cb, pl.ds(r0 * half + r1, cseg, stride=d2), :]

    for kind in sorted(set(kinds)):
        member = functools.reduce(jnp.logical_or, [j == s for s, k in enumerate(kinds) if k == kind])
        pl.when(member)(functools.partial(emit, *kind))


def _norm_proj(x, norm_w, w_slabs, cos_t, sin_t, kinds):
    n = x.shape[0]
    s = w_slabs.shape[0]
    p = cos_t.shape[0]
    perm = any(d > 1 for _, d in kinds)
    big = perm or (n % PERM_TILE == 0 and p % PERM_TILE == 0)
    tm = PERM_TILE if big else min(1024, n, p)
    assert n % tm == 0 and p % tm == 0 and len(kinds) == s
    npt = p // tm
    scratch = [pltpu.VMEM((tm, D_MODEL), BF16)]
    if perm:
        scratch.append(pltpu.VMEM((SLAB // 128, tm, 128), F32))
    if any(d > PERM_STRIDE for _, d in kinds):
        scratch.append(pltpu.VMEM((SLAB // 128, min(tm, PROJ_ROWS), 128), F32))
    return pl.pallas_call(
        functools.partial(_proj_kernel, kinds=tuple(kinds)),
        grid=(n // tm, s),
        in_specs=[
            pl.BlockSpec((tm, D_MODEL), lambda i, j: (i, 0), pipeline_mode=pl.Buffered(1)),
            pl.BlockSpec((1, D_MODEL), lambda i, j: (0, 0)),
            pl.BlockSpec((None, D_MODEL, SLAB), lambda i, j: (j, 0, 0)),
            pl.BlockSpec((tm, 128), lambda i, j: (i % npt, 0)),
            pl.BlockSpec((tm, 128), lambda i, j: (i % npt, 0)),
        ],
        out_specs=pl.BlockSpec((None, tm, SLAB), lambda i, j: (j, i, 0)),
        out_shape=jax.ShapeDtypeStruct((s, n, SLAB), F32),
        scratch_shapes=scratch,
        compiler_params=_cparams("parallel", "arbitrary"),
        name="norm_proj",
    )(x, norm_w.reshape(1, D_MODEL), w_slabs, cos_t, sin_t)


def _even_kinds(perm):
    dil = B_DILATIONS if perm else (1,) * B_GROUPS
    return ([(0, 1)] * 4 + [(SLAB, d) for d in dil] * 2 + [(0, d) for d in dil] + [(0, 1)])


ODD_KINDS = [(0, 1)] * 3 + [(SLAB, 1), (D_KV_HEADS * HEAD_DIM, 1)]


def _rope_tables(pos):
    half = HEAD_DIM // 2
    inv = ROPE_THETA ** (-jnp.arange(half, dtype=F32) / half)
    ang = pos.astype(F32)[:, None] * inv[None, :]
    cos, sin = jnp.cos(ang), jnp.sin(ang)
    return (jnp.concatenate([cos, cos, cos, cos], axis=-1),
            jnp.concatenate([-sin, sin, -sin, sin], axis=-1))


def _pad_cols(w, width):
    return jnp.pad(w, ((0, 0), (0, width - w.shape[1])))


def _even_slabs(w_in):
    main = w_in[:, :2048].reshape(D_MODEL, 4, SLAB)
    gates = _pad_cols(w_in[:, 2048:2056], SLAB)[:, None, :]
    qkv = w_in[:, 2056:].reshape(D_MODEL, 9, SLAB)
    return jnp.moveaxis(jnp.concatenate([main, qkv, gates], axis=1), 1, 0).astype(BF16)


def _group_aligned(w, axis):
    rep = D_HEADS // D_KV_HEADS
    shape = w.shape
    split = shape[:axis] + (D_KV_HEADS, rep, HEAD_DIM) + shape[axis + 1:]
    return jnp.swapaxes(w.reshape(split), axis, axis + 1).reshape(shape)


def _odd_slabs(w_in):
    main = w_in[:, :1536].reshape(D_MODEL, 3, SLAB)
    qd = _group_aligned(w_in[:, 1544:2056], 1)[:, None, :]
    kvd = _pad_cols(jnp.concatenate([w_in[:, 2056:2312], w_in[:, 1536:1544]], axis=1), SLAB)[:, None, :]
    return jnp.moveaxis(jnp.concatenate([main, qd, kvd], axis=1), 1, 0).astype(BF16)


def _split3(x):
    hi = x.astype(BF16)
    r1 = x - hi.astype(F32)
    mid = r1.astype(BF16)
    lo = (r1 - mid.astype(F32)).astype(BF16)
    return hi, mid, lo


def _cumsum_rows(x, tri):
    hi, mid, lo = _split3(x)
    acc = jnp.dot(tri, lo, preferred_element_type=F32)
    acc = acc + jnp.dot(tri, mid, preferred_element_type=F32)
    return acc + jnp.dot(tri, hi, preferred_element_type=F32)


def _transpose_slab(s):
    l = s.shape[0]
    if l < 128:
        s = jnp.concatenate([s, jnp.zeros((128 - l, 128), s.dtype)], axis=0)
    return s.T


def _conv_silu(buf_ref, x, w, b):
    tb = x.shape[0]
    buf_ref[8:8 + tb, :] = x
    y = b + w[3:4] * x
    for j in range(3):
        y = y + w[j:j + 1] * buf_ref[5 + j:5 + j + tb, :]
    buf_ref[0:8, :] = buf_ref[tb:tb + 8, :]
    return y * jax.nn.sigmoid(y)


def _log_sigmoid(x):
    return jnp.minimum(x, 0.0) - jnp.log1p(jnp.exp(-jnp.abs(x)))


def _dot_t(a, b):
    return lax.dot_general(a, b, (((1,), (1,)), ((), ())), preferred_element_type=F32)


def _tdot(a, b):
    return lax.dot_general(a, b, (((0,), (0,)), ((), ())), preferred_element_type=F32)


def _mlstm_kernel(q_ref, k_ref, v_ref, og_ref, g_ref, cw_ref, cb_ref, bg_ref, an_ref,
                  conv0_ref, c0_ref, n0_ref, m0_ref,
                  ya_ref, c_ref, n_ref, m_ref, qbuf, kbuf, *, chunk):
    sb, tb, _ = q_ref.shape
    tblk = pl.program_id(1)
    lane = lax.broadcasted_iota(jnp.int32, (chunk, 128), 1)
    lane1 = lax.broadcasted_iota(jnp.int32, (1, 128), 1)
    row = lax.broadcasted_iota(jnp.int32, (chunk, chunk), 0)
    col = lax.broadcasted_iota(jnp.int32, (chunk, chunk), 1)
    causal = col <= row
    tri = causal.astype(BF16)
    scale = A_DK ** -0.5

    @pl.when(tblk == 0)
    def _():
        c_ref[...] = c0_ref[...]
        n_ref[...] = n0_ref[...]
        m_ref[...] = m0_ref[...]
        for si in range(sb):
            qbuf[si, 0:8, :] = conv0_ref[si, :, 0:SLAB]
            kbuf[si, 0:8, :] = conv0_ref[si, :, SLAB:2 * SLAB]

    for si in range(sb):
        qc = _conv_silu(qbuf.at[si], q_ref[si], cw_ref[:, 0:SLAB], cb_ref[:, 0:SLAB])
        kc = _conv_silu(kbuf.at[si], k_ref[si], cw_ref[:, SLAB:2 * SLAB], cb_ref[:, SLAB:2 * SLAB]) * scale
        local = {}
        for c in range(tb // chunk):
            r0 = c * chunk
            gates = g_ref[si, r0:r0 + chunk, 0:128] + bg_ref[...]
            bc = _cumsum_rows(_log_sigmoid(gates), tri)
            st = _transpose_slab(jnp.where(lane < A_HEADS, gates, bc))
            for h in range(A_HEADS):
                hs = slice(h * 128, (h + 1) * 128)
                qh = qc[r0:r0 + chunk, hs]
                kh = kc[r0:r0 + chunk, hs]
                qb, kb = qh.astype(BF16), kh.astype(BF16)
                vb = v_ref[si, r0:r0 + chunk, hs].astype(BF16)
                b_t = bc[:, A_HEADS + h:A_HEADS + h + 1]
                li_t = gates[:, h:h + 1]
                li_s = st[h:h + 1, 0:chunk]
                b_s = st[A_HEADS + h:A_HEADS + h + 1, 0:chunk]
                dlog = jnp.where(causal, b_t - b_s + li_s, NEG)
                m_loc = jnp.max(dlog, axis=1, keepdims=True)
                s_loc = _dot_t(qb, kb) * jnp.exp(dlog - m_loc)
                b_end = b_t[chunk - 1:chunk, :]
                g_col = b_end - b_t + li_t
                g_max = jnp.max(g_col, axis=0, keepdims=True)
                kw = kh * jnp.exp(g_col - g_max)
                local[c, h] = dict(
                    qh=qh, qb=qb, b_t=b_t, b_end=b_end, m_loc=m_loc, g_max=g_max,
                    sv=jnp.dot(s_loc.astype(BF16), vb, preferred_element_type=F32),
                    rs=jnp.sum(s_loc, axis=1, keepdims=True),
                    u=_tdot(kw.astype(BF16), vb), nk=jnp.sum(kw, axis=0, keepdims=True))
        m_vec = m_ref[si]
        for h in range(A_HEADS):
            hs = slice(h * 128, (h + 1) * 128)
            c_st = c_ref[si, h]
            n_st = n_ref[si, h:h + 1, :]
            m_h = m_vec[:, h:h + 1]
            for c in range(tb // chunk):
                r0 = c * chunk
                lc = local[c, h]
                inter = lc['b_t'] + m_h
                m_t = jnp.maximum(inter, lc['m_loc'])
                w_loc = jnp.exp(lc['m_loc'] - m_t)
                w_inter = jnp.exp(inter - m_t)
                num = w_loc * lc['sv'] + w_inter * jnp.dot(lc['qb'], c_st.astype(BF16), preferred_element_type=F32)
                den = w_loc * lc['rs'] + w_inter * jnp.sum(lc['qh'] * n_st, axis=1, keepdims=True)
                hh = num / jnp.maximum(jnp.abs(den), jnp.exp(-m_t))
                m_new = jnp.maximum(lc['b_end'] + m_h, lc['g_max'])
                decay = jnp.exp(lc['b_end'] + m_h - m_new)
                gain = jnp.exp(lc['g_max'] - m_new)
                c_st = decay * c_st + gain * lc['u']
                n_st = decay * n_st + gain * lc['nk']
                m_h = m_new
                hn = hh * lax.rsqrt(jnp.mean(hh * hh, axis=-1, keepdims=True) + EPS) * an_ref[:, hs]
                ya_ref[si, r0:r0 + chunk, hs] = hn * jax.nn.sigmoid(og_ref[si, r0:r0 + chunk, hs])
            c_ref[si, h] = c_st
            n_ref[si, h:h + 1, :] = n_st
            m_vec = jnp.where(lane1 == h, m_h, m_vec)
        m_ref[si] = m_vec


def _mlstm(proj, nseq, t, conv_w, conv_b, b_gates, a_norm, conv0, c0, n0, m0):
    chunk = A_CHUNK if t % A_CHUNK == 0 else t
    if t > chunk:
        sb, tb = 1, min(t, 4 * chunk)
    else:
        sb, tb = min(nseq, 8), t
    p4 = proj.reshape(proj.shape[0], nseq, t, SLAB)
    slab = lambda s: pl.BlockSpec((None, sb, tb, SLAB), lambda i, j, s=s: (s, i, j, 0))
    full = lambda a: pl.BlockSpec(a.shape, lambda i, j: (0,) * a.ndim)
    st = lambda a: pl.BlockSpec((sb,) + a.shape[1:], lambda i, j: (i,) + (0,) * (a.ndim - 1))
    bg = _pad_cols(b_gates.reshape(1, 2 * A_HEADS), 128)
    conv0p = jnp.pad(conv0, ((0, 0), (8 - (A_CONV - 1), 0), (0, 0)))
    m0p = _pad_cols(m0, 128).reshape(nseq, 1, 128)
    cb = conv_b.reshape(1, -1)
    an = a_norm.reshape(1, -1)
    ya, c, n, m = pl.pallas_call(
        functools.partial(_mlstm_kernel, chunk=chunk),
        grid=(nseq // sb, t // tb),
        in_specs=[slab(0), slab(1), slab(2), slab(3), slab(13),
                  full(conv_w), full(cb), full(bg), full(an),
                  st(conv0p), st(c0), st(n0), st(m0p)],
        out_specs=[pl.BlockSpec((sb, tb, SLAB), lambda i, j: (i, j, 0)), st(c0), st(n0), st(m0p)],
        out_shape=[jax.ShapeDtypeStruct((nseq, t, SLAB), F32),
                   jax.ShapeDtypeStruct(c0.shape, F32), jax.ShapeDtypeStruct(n0.shape, F32),
                   jax.ShapeDtypeStruct(m0p.shape, F32)],
        scratch_shapes=[pltpu.VMEM((sb, 8 + tb, SLAB), F32), pltpu.VMEM((sb, 8 + tb, SLAB), F32)],
        compiler_params=_cparams("parallel", "arbitrary"),
        name="mlstm",
    )(p4, p4, p4, p4, p4, conv_w, cb, bg, an, conv0p, c0, n0, m0p)
    return ya.reshape(nseq * t, SLAB), c, n, m[:, 0, :A_HEADS]


QBLK = 128


def _band_head(q_m, k2, v_m, valid, sink):
    s = jnp.where(valid, _dot_t(q_m, k2), NEG)
    m = jnp.max(s, axis=1, keepdims=True)
    if sink is not None:
        m = jnp.maximum(m, sink)
    e = jnp.exp(s - m)
    l = jnp.sum(e, axis=1, keepdims=True)
    if sink is not None:
        l = l + jnp.exp(sink - m)
    acc = jnp.dot(e.astype(BF16), v_m, preferred_element_type=F32)
    return acc, m, l


def _band_masks(has_prev, window):
    a = lax.broadcasted_iota(jnp.int32, (QBLK, 2 * QBLK), 0)
    c = lax.broadcasted_iota(jnp.int32, (QBLK, 2 * QBLK), 1)
    rel = a - c + QBLK
    band = jnp.logical_and(rel >= 0, rel <= window)
    if has_prev is True:
        return band
    return jnp.logical_and(band, c >= jnp.where(has_prev, 0, QBLK))


def _key_rows(prev_ref, cur_ref, a, ls):
    before = prev_ref[:, ls] if a == 0 else cur_ref[(a - 1) * QBLK:a * QBLK, ls]
    return jnp.concatenate([before, cur_ref[a * QBLK:(a + 1) * QBLK, ls]], axis=0).astype(BF16)


def _dilated_kernel(q_ref, kp_ref, kc_ref, vp_ref, vc_ref, o_ref, lse_ref):
    lane = lax.broadcasted_iota(jnp.int32, (1, 128), 1)
    low = lane < HEAD_DIM
    scale = HEAD_DIM ** -0.5
    for a in range(q_ref.shape[0] // QBLK):
        qs = slice(a * QBLK, (a + 1) * QBLK)
        valid = _band_masks(pl.program_id(2) > 0 if a == 0 else True, QBLK)
        for pr in range(B_HEADS // 2):
            ls = slice(pr * 128, (pr + 1) * 128)
            qp = q_ref[qs, ls] * scale
            k2 = _key_rows(kp_ref, kc_ref, a, ls)
            v2 = _key_rows(vp_ref, vc_ref, a, ls)
            o_h, lse_h = [], []
            for p in range(2):
                half = low if p == 0 else jnp.logical_not(low)
                acc, m, l = _band_head(jnp.where(half, qp, 0.0).astype(BF16), k2, v2, valid, None)
                o_h.append(acc / l)
                lse_h.append(m + jnp.log(l))
            o_ref[qs, ls] = jnp.where(low, o_h[0], o_h[1])
            lse_ref[qs, ls] = jnp.where(low, lse_h[0], lse_h[1])


def _dilated_prompt(proj, b, t, g):
    d = B_DILATIONS[g]
    assert t % PERM_TILE == 0
    nb = PERM_TILE // d // QBLK
    per_tile = PERM_TILE // QBLK
    pv = proj.reshape(proj.shape[0], b, t, SLAB)

    nq = 2 if nb % 2 == 0 else 1

    def blk(i, r):
        return (i // nb) * per_tile + r * nb + i % nb

    cur = lambda s: pl.BlockSpec((None, None, nq * QBLK, SLAB),
                                 lambda bi, r, i, s=s: (s, bi, blk(nq * i, r) // nq, 0))
    prev = lambda s: pl.BlockSpec((None, None, QBLK, SLAB),
                                  lambda bi, r, i, s=s: (s, bi, blk(jnp.maximum(nq * i - 1, 0), r), 0))
    ospec = pl.BlockSpec((None, nq * QBLK, SLAB), lambda bi, r, i: (bi, blk(nq * i, r) // nq, 0))
    sq, sk, sv = 4 + g, 7 + g, 10 + g
    o, lse = pl.pallas_call(
        _dilated_kernel,
        grid=(b, d, t // d // QBLK // nq),
        in_specs=[cur(sq), prev(sk), cur(sk), prev(sv), cur(sv)],
        out_specs=[ospec, ospec],
        out_shape=[jax.ShapeDtypeStruct((b, t, SLAB), F32)] * 2,
        compiler_params=_cparams("parallel", "parallel", "arbitrary"),
        name=f"dilated_g{g}",
    )(pv, pv, pv, pv, pv)
    return o.reshape(b * t, SLAB), lse.reshape(b * t, SLAB)


def _merge_kernel(*refs, dils):
    ins, y_ref, bufs = refs[:6], refs[6], refs[7:]
    tm = y_ref.shape[0]
    vals = []
    nbuf = 0
    for k, ref in enumerate(ins):
        dil = dils[k // 2]
        if dil == 1:
            vals.append(ref[...])
        else:
            buf = bufs[nbuf]
            nbuf += 1
            seg = tm // dil
            for r in range(dil):
                buf[pl.ds(r, seg, stride=dil), :] = ref[r * seg:(r + 1) * seg, :]
            vals.append(buf[...])
    o1, l1, o2, l2, o3, l3 = vals
    m = jnp.maximum(jnp.maximum(l1, l2), l3)
    w1, w2, w3 = jnp.exp(l1 - m), jnp.exp(l2 - m), jnp.exp(l3 - m)
    y_ref[...] = (w1 * o1 + w2 * o2 + w3 * o3) / (w1 + w2 + w3)


def _merge_groups(parts, perm):
    n = parts[0].shape[0]
    dils = B_DILATIONS if perm else (1,) * B_GROUPS
    tm = PERM_TILE if perm else min(1024, n)
    spec = pl.BlockSpec((tm, 128), lambda i, j: (i, j))
    nbuf = 2 * sum(d > 1 for d in dils)
    return pl.pallas_call(
        functools.partial(_merge_kernel, dils=dils), grid=(n // tm, SLAB // 128), in_specs=[spec] * 6,
        out_specs=spec, out_shape=jax.ShapeDtypeStruct((n, SLAB), F32),
        scratch_shapes=[pltpu.VMEM((tm, 128), F32)] * nbuf,
        compiler_params=_cparams("parallel", "parallel"), name="merge_groups",
    )(*parts)


def _swa_kernel(q_ref, kp_ref, kc_ref, vp_ref, vc_ref, sink_ref, o_ref):
    lane = lax.broadcasted_iota(jnp.int32, (1, 128), 1)
    scale = HEAD_DIM ** -0.5
    rep = D_HEADS // D_KV_HEADS
    low = lane < HEAD_DIM
    allk = slice(None)
    for a in range(q_ref.shape[0] // QBLK):
        qs = slice(a * QBLK, (a + 1) * QBLK)
        valid = _band_masks(pl.program_id(1) > 0 if a == 0 else True, D_WINDOW - 1)
        k2 = _key_rows(kp_ref, kc_ref, a, allk)
        v2 = _key_rows(vp_ref, vc_ref, a, allk)
        for pr in range(rep):
            ls = slice(pr * 128, (pr + 1) * 128)
            qp = q_ref[qs, ls] * scale
            o_h = []
            for g in range(D_KV_HEADS):
                half = low if g == 0 else jnp.logical_not(low)
                acc, m, l = _band_head(jnp.where(half, qp, 0.0).astype(BF16), k2, v2, valid,
                                       sink_ref[g * rep + pr])
                o_h.append(acc / l)
            o_ref[qs, ls] = jnp.where(low, o_h[0], o_h[1])


def _swa_prompt(proj, b, t, sinks):
    pv = proj.reshape(proj.shape[0], b, t, SLAB)
    nq = 2 if t % (2 * QBLK) == 0 else 1
    qspec = pl.BlockSpec((None, None, nq * QBLK, SLAB), lambda bi, i: (3, bi, i, 0))

    def kv(col, prev):
        if prev:
            return pl.BlockSpec((None, None, QBLK, 128), lambda bi, i: (4, bi, jnp.maximum(nq * i - 1, 0), col))
        return pl.BlockSpec((None, None, nq * QBLK, 128), lambda bi, i: (4, bi, i, col))

    o = pl.pallas_call(
        _swa_kernel,
        grid=(b, t // QBLK // nq),
        in_specs=[qspec, kv(0, True), kv(0, False), kv(1, True), kv(1, False),
                  pl.BlockSpec(memory_space=pltpu.SMEM)],
        out_specs=pl.BlockSpec((None, nq * QBLK, SLAB), lambda bi, i: (bi, i, 0)),
        out_shape=jax.ShapeDtypeStruct((b, t, SLAB), F32),
        compiler_params=_cparams("parallel", "arbitrary"),
        name="swa_prompt",
    )(pv, pv, pv, pv, pv, sinks)
    return o.reshape(b * t, SLAB)


def _softplus(x):
    return jnp.maximum(x, 0.0) + jnp.log1p(jnp.exp(-jnp.abs(x)))


def _ssd_kernel(z_ref, x_ref, bc_ref, dt_ref, cw_ref, cb_ref, dtb_ref, alog_ref, dl_ref, cn_ref,
                conv0_ref, h0_ref, y_ref, h_ref, xbuf, bcbuf, *, chunk):
    sb, tb, _ = x_ref.shape
    tblk = pl.program_id(1)
    lane1 = lax.broadcasted_iota(jnp.int32, (1, 128), 1)
    low = lane1 < C_HEADDIM
    row = lax.broadcasted_iota(jnp.int32, (chunk, chunk), 0)
    col = lax.broadcasted_iota(jnp.int32, (chunk, chunk), 1)
    causal = col <= row
    tri = causal.astype(BF16)
    srow_low = lax.broadcasted_iota(jnp.int32, (128, 1), 0) < C_HEADDIM
    hpg = C_HEADS // C_NGROUPS
    a_neg = -jnp.exp(alog_ref[...])

    @pl.when(tblk == 0)
    def _():
        h_ref[...] = h0_ref[...]
        for si in range(sb):
            xbuf[si, 0:8, :] = conv0_ref[si, :, 0:SLAB]
            bcbuf[si, 0:8, :] = conv0_ref[si, :, SLAB:2 * SLAB]

    for si in range(sb):
        xc = _conv_silu(xbuf.at[si], x_ref[si], cw_ref[:, 0:SLAB], cb_ref[:, 0:SLAB])
        bcc = _conv_silu(bcbuf.at[si], bc_ref[si], cw_ref[:, SLAB:2 * SLAB], cb_ref[:, SLAB:2 * SLAB])
        for c in range(tb // chunk):
            rs = slice(c * chunk, (c + 1) * chunk)
            dt = _softplus(dt_ref[si, rs, :] + dtb_ref[...])
            cum = _cumsum_rows(dt * a_neg, tri)
            cum_t_all = _transpose_slab(cum)
            dt_t_all = _transpose_slab(dt)
            cum_end = cum[chunk - 1:chunk, :]
            for g in range(C_NGROUPS):
                bg = bcc[rs, g * 128:(g + 1) * 128].astype(BF16)
                cg = bcc[rs, 256 + g * 128:256 + (g + 1) * 128].astype(BF16)
                cb = _dot_t(cg, bg)
                y_pairs = []
                for pp in range(hpg // 2):
                    pr = g * (hpg // 2) + pp
                    ls = slice(pr * 128, (pr + 1) * 128)
                    x_pair = xc[rs, ls]
                    hs = h_ref[si, pr]
                    y_pair = None
                    e_col, w_end, dec = [], [], []
                    for p in range(2):
                        h = 2 * pr + p
                        cum_t = cum[:, h:h + 1]
                        seg = jnp.where(causal, cum_t - cum_t_all[h:h + 1, 0:chunk], NEG)
                        w = cb * jnp.exp(seg) * dt_t_all[h:h + 1, 0:chunk]
                        half = low if p == 0 else jnp.logical_not(low)
                        yh = jnp.dot(w.astype(BF16), jnp.where(half, x_pair, 0.0).astype(BF16),
                                     preferred_element_type=F32)
                        y_pair = yh if y_pair is None else y_pair + yh
                        e_col.append(jnp.exp(cum_t))
                        w_end.append(jnp.exp(cum_end[:, h:h + 1] - cum_t) * dt[:, h:h + 1])
                        dec.append(jnp.exp(cum_end[:, h:h + 1]))
                    inter = _dot_t(cg, hs.astype(BF16)) * jnp.where(low, e_col[0], e_col[1])
                    y_pair = y_pair + inter + dl_ref[:, ls] * x_pair
                    xw = x_pair * jnp.where(low, w_end[0], w_end[1])
                    h_ref[si, pr] = jnp.where(srow_low, dec[0], dec[1]) * hs + _tdot(xw.astype(BF16), bg)
                    zz = z_ref[si, rs, ls]
                    y_pairs.append(y_pair * (zz * jax.nn.sigmoid(zz)))
                ms = sum(jnp.sum(yp * yp, axis=-1, keepdims=True) for yp in y_pairs) / (hpg * C_HEADDIM)
                inv = lax.rsqrt(ms + EPS)
                for pp, yp in enumerate(y_pairs):
                    ls = slice((g * (hpg // 2) + pp) * 128, (g * (hpg // 2) + pp + 1) * 128)
                    y_ref[si, rs, ls] = yp * inv * cn_ref[:, ls]


def _ssd(proj, nseq, t, conv_w, conv_b, dt_bias, a_log, d_skip, c_norm, conv0, h0):
    chunk = C_CHUNK if t % C_CHUNK == 0 else t
    if t > chunk:
        sb, tb = min(nseq, 2), min(t, 2 * chunk)
    else:
        sb, tb = min(nseq, 8), t
    p4 = proj.reshape(proj.shape[0], nseq, t, SLAB)
    slab = lambda s: pl.BlockSpec((None, sb, tb, SLAB), lambda i, j, s=s: (s, i, j, 0))
    full = lambda a: pl.BlockSpec(a.shape, lambda i, j: (0,) * a.ndim)
    st = lambda a: pl.BlockSpec((sb,) + a.shape[1:], lambda i, j: (i,) + (0,) * (a.ndim - 1))
    conv0p = jnp.pad(conv0, ((0, 0), (8 - (C_CONV - 1), 0), (0, 0)))
    hp = h0.reshape(nseq, C_HEADS // 2, 128, C_DSTATE)
    cb = conv_b.reshape(1, -1)
    dtb = _pad_cols(dt_bias.reshape(1, C_HEADS), 128)
    alog = _pad_cols(a_log.reshape(1, C_HEADS), 128)
    dl = jnp.repeat(d_skip, C_HEADDIM).reshape(1, C_DINNER)
    cn = c_norm.reshape(1, C_DINNER)
    y, h = pl.pallas_call(
        functools.partial(_ssd_kernel, chunk=chunk),
        grid=(nseq // sb, t // tb),
        in_specs=[slab(0), slab(1), slab(2),
                  pl.BlockSpec((None, sb, tb, 128), lambda i, j: (4, i, j, 2)),
                  full(conv_w), full(cb), full(dtb), full(alog), full(dl), full(cn), st(conv0p), st(hp)],
        out_specs=[pl.BlockSpec((sb, tb, SLAB), lambda i, j: (i, j, 0)), st(hp)],
        out_shape=[jax.ShapeDtypeStruct((nseq, t, SLAB), F32), jax.ShapeDtypeStruct(hp.shape, F32)],
        scratch_shapes=[pltpu.VMEM((sb, 8 + tb, SLAB), F32), pltpu.VMEM((sb, 8 + tb, SLAB), F32)],
        compiler_params=_cparams("parallel", "arbitrary"),
        name="ssd",
    )(p4, p4, p4, p4, conv_w, cb, dtb, alog, dl, cn, conv0p, hp)
    return y.reshape(nseq * t, SLAB), h.reshape(nseq, C_HEADS, C_HEADDIM, C_DSTATE)


FFN_CHUNK = 256


def _post_kernel(x_ref, ya_ref, yb_ref, wo_ref, fn_ref, wg_ref, wu_ref, wd_ref, on_ref, o_ref, act_ref, *, final):
    ymix = jnp.concatenate([ya_ref[...], yb_ref[...]], axis=-1).astype(BF16)
    x1 = x_ref[...] + jnp.dot(ymix, wo_ref[...], preferred_element_type=F32)
    ms = jnp.mean(x1 * x1, axis=-1, keepdims=True)
    hn = (x1 * lax.rsqrt(ms + EPS) * fn_ref[...]).astype(BF16)
    hidden = wg_ref.shape[1]
    for c in range(hidden // FFN_CHUNK):
        cs = slice(c * FFN_CHUNK, (c + 1) * FFN_CHUNK)
        gate = jnp.dot(hn, wg_ref[:, cs], preferred_element_type=F32)
        up = jnp.dot(hn, wu_ref[:, cs], preferred_element_type=F32)
        act_ref[:, cs] = (gate * jax.nn.sigmoid(gate) * up).astype(BF16)
    x2 = x1 + jnp.dot(act_ref[...], wd_ref[...], preferred_element_type=F32)
    if final:
        ms2 = jnp.mean(x2 * x2, axis=-1, keepdims=True)
        x2 = x2 * lax.rsqrt(ms2 + EPS) * on_ref[...]
    o_ref[...] = x2


def _post(x, ya, yb, w_out, layer, ffn_norm, w_gate, w_up, w_down, out_norm, final):
    n = x.shape[0]
    tm = min(1024, n)
    hidden = w_gate.shape[2]
    row = lambda w: pl.BlockSpec((tm, w), lambda i: (i, 0))
    res = lambda a: pl.BlockSpec(a.shape, lambda i: (0, 0), pipeline_mode=pl.Buffered(1))
    lay = lambda a: pl.BlockSpec((None,) + a.shape[1:], lambda i: (layer, 0, 0), pipeline_mode=pl.Buffered(1))
    fn = ffn_norm.reshape(-1, 1, D_MODEL)
    on = out_norm.reshape(1, D_MODEL)
    return pl.pallas_call(
        functools.partial(_post_kernel, final=final),
        grid=(n // tm,),
        in_specs=[row(D_MODEL), row(SLAB), row(SLAB), res(w_out), lay(fn), lay(w_gate), lay(w_up), lay(w_down),
                  res(on)],
        out_specs=row(D_MODEL),
        out_shape=jax.ShapeDtypeStruct((n, D_MODEL), F32),
        scratch_shapes=[pltpu.VMEM((tm, hidden), BF16)],
        compiler_params=_cparams("parallel"),
        name="post_ffn",
    )(x, ya, yb, w_out, fn, w_gate, w_up, w_down, on)


RIDER_PHASES = 4


def _rider_mask(ts, r, window, step):
    t = lax.broadcasted_iota(jnp.int32, (2 * ts, r + 128), 0) % ts
    c = lax.broadcasted_iota(jnp.int32, (2 * ts, r + 128), 1)
    rel = r + t - jnp.where(c < 128, c, c - 128 + ts)
    valid = jnp.logical_and(jnp.logical_and(rel >= 0, rel <= window), (rel & (step - 1)) == 0)
    return jnp.logical_and(valid, jnp.logical_or(c < ts, c >= 128))


def _shift_rows_minor(src_ref, new_t, rows, ts):
    r = src_ref.shape[1]
    lane = lax.broadcasted_iota(jnp.int32, (1, 128), 1)
    old = src_ref[rows, :]
    rolled = pltpu.roll(old, r - ts, 1)
    tail = jnp.where(lane < 128 - ts, rolled[:, r - 128:], new_t[rows, :])
    out = tail if r == 128 else jnp.concatenate([rolled[:, :r - 128], tail], axis=1)
    return old, out


def _new_rows_t(new_ref):
    ts = new_ref.shape[0]
    pad = jnp.zeros((128 - ts, SLAB), F32)
    return jnp.concatenate([pad, new_ref[...]], axis=0).T


def _rider_k(q_ref, kn_ref, kc_ref, ko_ref, p_ref, lse_ref, window, step):
    ts = q_ref.shape[0]
    r = kc_ref.shape[1]
    low = lax.broadcasted_iota(jnp.int32, (1, 128), 1) < HEAD_DIM
    valid = _rider_mask(ts, r, window, step)
    kn_t = _new_rows_t(kn_ref)
    scale = HEAD_DIM ** -0.5
    for pr in range(B_HEADS // 2):
        rows = slice(pr * 128, (pr + 1) * 128)
        k_old, k_out = _shift_rows_minor(kc_ref, kn_t, rows, ts)
        ko_ref[rows, :] = k_out
        k_ext = jnp.concatenate([k_old[:, 0:128], k_out], axis=1).astype(BF16)
        qp = q_ref[:, rows] * scale
        q2 = jnp.concatenate([jnp.where(low, qp, 0.0), jnp.where(low, 0.0, qp)], axis=0).astype(BF16)
        s = jnp.where(valid, jnp.dot(q2, k_ext, preferred_element_type=F32), NEG)
        m = jnp.max(s, axis=1, keepdims=True)
        e = jnp.exp(s - m)
        l = jnp.sum(e, axis=1, keepdims=True)
        p_ref[pr * 2 * ts:(pr + 1) * 2 * ts, :] = (e / l).astype(BF16)
        lse = m + jnp.log(l)
        lse_ref[:, rows] = jnp.where(low, lse[0:ts], lse[ts:2 * ts])


def _rider_v(p_ref, vn_ref, vc_ref, vo_ref, o_ref):
    ts = vn_ref.shape[0]
    low = lax.broadcasted_iota(jnp.int32, (1, 128), 1) < HEAD_DIM
    vn_t = _new_rows_t(vn_ref)
    for pr in range(B_HEADS // 2):
        rows = slice(pr * 128, (pr + 1) * 128)
        v_old, v_out = _shift_rows_minor(vc_ref, vn_t, rows, ts)
        vo_ref[rows, :] = v_out
        v_ext = jnp.concatenate([v_old[:, 0:128], v_out], axis=1).astype(BF16)
        a = _dot_t(p_ref[pr * 2 * ts:(pr + 1) * 2 * ts, :], v_ext)
        o_ref[:, rows] = jnp.where(low, a[0:ts], a[ts:2 * ts])


def _post_rider_kernel(x_ref, ya_ref, yb_ref, wo_ref, fn_ref, wg_ref, wu_ref, wd_ref, on_ref, *rest,
                       final, side, groups):
    ng = len(groups)
    nout = 3 if side == 'k' else 2
    ins, o_ref, outs = rest[:3 * ng], rest[3 * ng], rest[3 * ng + 1:3 * ng + 1 + nout * ng]
    act_ref, hn_ref = rest[3 * ng + 1 + nout * ng:]
    x1_ref = o_ref

    def ride():
        for k, g in enumerate(groups):
            i3, o3 = ins[3 * k:3 * k + 3], outs[nout * k:nout * k + nout]
            if side == 'k':
                _rider_k(*i3, *o3, B_WINDOWS[g], B_DILATIONS[g])
            else:
                _rider_v(*i3, *o3)

    u = pl.program_id(1)
    nchunk = wg_ref.shape[1] // FFN_CHUNK
    per = -(-nchunk // (RIDER_PHASES - 1))
    first = nchunk - per * (RIDER_PHASES - 2)

    def hidden_chunks(lo, hi):
        hn = hn_ref[...]
        for c in range(lo, hi):
            cs = slice(c * FFN_CHUNK, (c + 1) * FFN_CHUNK)
            gate = jnp.dot(hn, wg_ref[:, cs], preferred_element_type=F32)
            up = jnp.dot(hn, wu_ref[:, cs], preferred_element_type=F32)
            act_ref[:, cs] = (gate * jax.nn.sigmoid(gate) * up).astype(BF16)

    def phase(k):
        def body():
            if k == 0:
                ymix = jnp.concatenate([ya_ref[...], yb_ref[...]], axis=-1).astype(BF16)
                x1 = x_ref[...] + jnp.dot(ymix, wo_ref[...], preferred_element_type=F32)
                ms = jnp.mean(x1 * x1, axis=-1, keepdims=True)
                x1_ref[...] = x1
                hn_ref[...] = (x1 * lax.rsqrt(ms + EPS) * fn_ref[...]).astype(BF16)
                hidden_chunks(0, first)
            elif k < RIDER_PHASES - 1:
                hidden_chunks(first + (k - 1) * per, first + k * per)
            else:
                x2 = x1_ref[...] + jnp.dot(act_ref[...], wd_ref[...], preferred_element_type=F32)
                if final:
                    ms2 = jnp.mean(x2 * x2, axis=-1, keepdims=True)
                    x2 = x2 * lax.rsqrt(ms2 + EPS) * on_ref[...]
                o_ref[...] = x2
            ride()
        return body

    for k in range(RIDER_PHASES):
        pl.when(u == k)(phase(k))


def _post_rider(x, ya, yb, w_out, layer, ffn_norm, w_gate, w_up, w_down, out_norm, final, side, groups, proj_s,
                caches_t, probs):
    n = x.shape[0]
    nseq = caches_t[0].shape[0]
    ts = proj_s.shape[1] // nseq
    tm = n * RIDER_PHASES // nseq
    hidden = w_gate.shape[2]
    row = lambda w, **kw: pl.BlockSpec((tm, w), lambda i, u: (i, 0), **kw)
    res = lambda a: pl.BlockSpec(a.shape, lambda i, u: (0, 0), pipeline_mode=pl.Buffered(1))
    lay = lambda a: pl.BlockSpec((None,) + a.shape[1:], lambda i, u: (layer, 0, 0), pipeline_mode=pl.Buffered(1))
    seq = lambda i, u: i * RIDER_PHASES + u
    new = lambda s: pl.BlockSpec((None, ts, SLAB), lambda i, u: (s, seq(i, u), 0))
    rows8 = pl.BlockSpec((ts, SLAB), lambda i, u: (seq(i, u), 0))
    fn = ffn_norm.reshape(-1, 1, D_MODEL)
    on = out_norm.reshape(1, D_MODEL)
    single = dict(pipeline_mode=pl.Buffered(1))
    common = [row(D_MODEL), row(SLAB, **single), row(SLAB, **single), res(w_out), lay(fn), lay(w_gate),
              lay(w_up), lay(w_down), res(on)]
    small = jax.ShapeDtypeStruct((nseq * ts, SLAB), F32)
    ins, args, outs, shapes = [], [], [row(D_MODEL)], [jax.ShapeDtypeStruct((n, D_MODEL), F32)]
    for k, g in enumerate(groups):
        r = caches_t[k].shape[2]
        cspec = pl.BlockSpec((None, SLAB, r), lambda i, u: (seq(i, u), 0, 0))
        pspec = pl.BlockSpec((None, B_HEADS * ts, r + 128), lambda i, u: (seq(i, u), 0, 0))
        cout = jax.ShapeDtypeStruct(caches_t[k].shape, F32)
        if side == 'k':
            ins += [new(4 + g), new(7 + g), cspec]
            args += [proj_s, proj_s, caches_t[k]]
            outs += [cspec, pspec, rows8]
            shapes += [cout, jax.ShapeDtypeStruct((nseq, B_HEADS * ts, r + 128), BF16), small]
        else:
            ins += [pspec, new(10 + g), cspec]
            args += [probs[k], proj_s, caches_t[k]]
            outs += [cspec, rows8]
            shapes += [cout, small]
    res_all = pl.pallas_call(
        functools.partial(_post_rider_kernel, final=final, side=side, groups=tuple(groups)),
        grid=(n // tm, RIDER_PHASES),
        in_specs=common + ins,
        out_specs=outs,
        out_shape=shapes,
        scratch_shapes=[pltpu.VMEM((tm, hidden), BF16), pltpu.VMEM((tm, D_MODEL), BF16)],
        compiler_params=_cparams("parallel", "arbitrary", vmem=VMEM_LIMIT_RIDER),
        name=f"post_ffn_ride_{side}",
    )(x, ya, yb, w_out, fn, w_gate, w_up, w_down, on, *args)
    nout = 3 if side == 'k' else 2
    return res_all[0], [res_all[1 + nout * k:1 + nout * (k + 1)] for k in range(len(groups))]


NEWPAD = 128


def _ext_mask(rows, r, ts, window, step):
    t = lax.broadcasted_iota(jnp.int32, (rows, r + NEWPAD), 0) % ts
    c = lax.broadcasted_iota(jnp.int32, (rows, r + NEWPAD), 1)
    rel = r + t - c
    ok = jnp.logical_and(jnp.logical_and(rel >= 0, rel <= window), c < r + ts)
    return jnp.logical_and(ok, (rel & (step - 1)) == 0)


def _shift_cache(dst_ref, src_ref, new, si, r, ts):
    dst_ref[si, 0:r - ts, :] = src_ref[si, ts:r, :]
    dst_ref[si, r - ts:r, :] = new


def _pad_new(x):
    return jnp.concatenate([x, jnp.zeros((NEWPAD - x.shape[0], x.shape[1]), x.dtype)], axis=0)


def _dil_sample_kernel(q_ref, kn_ref, vn_ref, kc_ref, vc_ref, o_ref, lse_ref, ko_ref, vo_ref,
                       *, window, step):
    sb, ts, _ = q_ref.shape
    r = kc_ref.shape[2]
    scale = HEAD_DIM ** -0.5
    lane = lax.broadcasted_iota(jnp.int32, (1, 128), 1)
    low = lane < HEAD_DIM
    t = lax.broadcasted_iota(jnp.int32, (2 * ts, r + 128), 0) % ts
    c = lax.broadcasted_iota(jnp.int32, (2 * ts, r + 128), 1)
    rel = r + t - jnp.where(c < 128, c, c - 128 + ts)
    valid = jnp.logical_and(jnp.logical_and(rel >= 0, rel <= window), (rel & (step - 1)) == 0)
    valid = jnp.logical_and(valid, jnp.logical_or(c < ts, c >= 128))
    pad = jnp.zeros((128 - ts, SLAB), F32)

    def shifted(src_ref, new_t, rows):
        old = src_ref[rows, :]
        rolled = pltpu.roll(old, r - ts, 1)
        tail = jnp.where(lane < 128 - ts, rolled[:, r - 128:], new_t[rows, :])
        out = tail if r == 128 else jnp.concatenate([rolled[:, :r - 128], tail], axis=1)
        return old, out

    for si, pr in [(si, pr) for si in range(sb) for pr in range(B_HEADS // 2)]:
        if pr == 0:
            kn_t = jnp.concatenate([pad, kn_ref[si]], axis=0).T
            vn_t = jnp.concatenate([pad, vn_ref[si]], axis=0).T
        rows = slice(pr * 128, (pr + 1) * 128)
        k_old, k_out = shifted(kc_ref.at[si], kn_t, rows)
        v_old, v_out = shifted(vc_ref.at[si], vn_t, rows)
        ko_ref[si, rows, :] = k_out
        vo_ref[si, rows, :] = v_out
        k_ext = jnp.concatenate([k_old[:, 0:128], k_out], axis=1).astype(BF16)
        v_ext = jnp.concatenate([v_old[:, 0:128], v_out], axis=1).astype(BF16)
        qp = q_ref[si, :, rows] * scale
        q2 = jnp.concatenate([jnp.where(low, qp, 0.0), jnp.where(low, 0.0, qp)], axis=0).astype(BF16)
        s = jnp.where(valid, jnp.dot(q2, k_ext, preferred_element_type=F32), NEG)
        m = jnp.max(s, axis=1, keepdims=True)
        e = jnp.exp(s - m)
        l = jnp.sum(e, axis=1, keepdims=True)
        a = _dot_t(e.astype(BF16), v_ext) / l
        lse = m + jnp.log(l)
        o_ref[si, :, rows] = jnp.where(low, a[0:ts], a[ts:2 * ts])
        lse_ref[si, :, rows] = jnp.where(low, lse[0:ts], lse[ts:2 * ts])


def _rows_minor(cache):
    nseq, r = cache.shape[:2]
    return jnp.transpose(cache, (0, 2, 3, 1)).reshape(nseq, -1, r)


def _rows_major(cache_t, heads):
    nseq, _, r = cache_t.shape
    return jnp.transpose(cache_t.reshape(nseq, heads, HEAD_DIM, r), (0, 3, 1, 2))


def _dilated_sample(proj, nseq, ts, g, k_cache, v_cache):
    r = k_cache.shape[1]
    assert ts % 8 == 0 and r % 128 == 0 and r >= B_WINDOWS[g]
    kt, vt = _rows_minor(k_cache), _rows_minor(v_cache)
    sb = max(1, min(nseq, 8, 1024 // r))
    p4 = proj.reshape(proj.shape[0], nseq, ts, SLAB)
    new = lambda s: pl.BlockSpec((None, sb, ts, SLAB), lambda i, s=s: (s, i, 0, 0))
    cspec = pl.BlockSpec((sb, SLAB, r), lambda i: (i, 0, 0))
    ospec = pl.BlockSpec((sb, ts, SLAB), lambda i: (i, 0, 0))
    o, lse, kn, vn = pl.pallas_call(
        functools.partial(_dil_sample_kernel, window=B_WINDOWS[g], step=B_DILATIONS[g]),
        grid=(nseq // sb,),
        in_specs=[new(4 + g), new(7 + g), new(10 + g), cspec, cspec],
        out_specs=[ospec, ospec, cspec, cspec],
        out_shape=[jax.ShapeDtypeStruct((nseq, ts, SLAB), F32)] * 2 + [jax.ShapeDtypeStruct(kt.shape, F32)] * 2,
        compiler_params=_cparams("parallel"),
        name=f"dilated_sample_g{g}",
    )(p4, p4, p4, kt, vt)
    return (o.reshape(nseq * ts, SLAB), lse.reshape(nseq * ts, SLAB),
            _rows_major(kn, B_HEADS), _rows_major(vn, B_HEADS))


def _swa_sample_kernel(q_ref, k_ref, v_ref, kc_ref, vc_ref, sink_ref, o_ref, kn_ref, vn_ref):
    sb, ts, _ = q_ref.shape
    r = kc_ref.shape[1]
    rows = D_HEADS * ts
    valid = _ext_mask(rows, r, ts, D_WINDOW - 1, 1)
    lane = lax.broadcasted_iota(jnp.int32, (1, 128), 1)
    low = lane < HEAD_DIM
    rep = D_HEADS // D_KV_HEADS
    scale = HEAD_DIM ** -0.5
    sink = sink_ref[:, 0:1]
    for si in range(sb):
        kn, vn = k_ref[si], v_ref[si]
        pieces = []
        for pr in range(rep):
            base = q_ref[si, :, pr * 128:(pr + 1) * 128] * scale
            for g in range(D_KV_HEADS):
                pieces.append(jnp.where(low if g == 0 else jnp.logical_not(low), base, 0.0))
        qrows = jnp.concatenate(pieces, axis=0).astype(BF16)
        kext = jnp.concatenate([kc_ref[si], _pad_new(kn)], axis=0).astype(BF16)
        vext = jnp.concatenate([vc_ref[si], _pad_new(vn)], axis=0).astype(BF16)
        s = jnp.where(valid, _dot_t(qrows, kext), NEG)
        m = jnp.maximum(jnp.max(s, axis=1, keepdims=True), sink)
        e = jnp.exp(s - m)
        l = jnp.sum(e, axis=1, keepdims=True) + jnp.exp(sink - m)
        o = jnp.dot(e.astype(BF16), vext, preferred_element_type=F32) / l
        pairs = [jnp.where(low, o[2 * pr * ts:(2 * pr + 1) * ts], o[(2 * pr + 1) * ts:(2 * pr + 2) * ts])
                 for pr in range(rep)]
        o_ref[si] = jnp.concatenate(pairs, axis=1)
        _shift_cache(kn_ref, kc_ref, kn, si, r, ts)
        _shift_cache(vn_ref, vc_ref, vn, si, r, ts)


def _swa_sample(proj, nseq, ts, sinks, k_cache, v_cache):
    r = k_cache.shape[1]
    sb = min(nseq, 8)
    kvw = D_KV_HEADS * HEAD_DIM
    p4 = proj.reshape(proj.shape[0], nseq, ts, SLAB)
    kc = k_cache.reshape(nseq, r, kvw)
    vc = v_cache.reshape(nseq, r, kvw)
    cspec = pl.BlockSpec((sb, r, kvw), lambda i: (i, 0, 0))
    new = lambda col: pl.BlockSpec((None, sb, ts, kvw), lambda i: (4, i, 0, col))
    rep = D_HEADS // D_KV_HEADS
    by_row_group = sinks.reshape(D_KV_HEADS, rep).T.reshape(-1)
    sink_rows = jnp.broadcast_to(jnp.repeat(by_row_group, ts)[:, None], (D_HEADS * ts, 128))
    o, kn, vn = pl.pallas_call(
        _swa_sample_kernel,
        grid=(nseq // sb,),
        in_specs=[pl.BlockSpec((None, sb, ts, SLAB), lambda i: (3, i, 0, 0)), new(0), new(1), cspec, cspec,
                  pl.BlockSpec(sink_rows.shape, lambda i: (0, 0))],
        out_specs=[pl.BlockSpec((sb, ts, SLAB), lambda i: (i, 0, 0)), cspec, cspec],
        out_shape=[jax.ShapeDtypeStruct((nseq, ts, SLAB), F32)] + [jax.ShapeDtypeStruct(kc.shape, F32)] * 2,
        compiler_params=_cparams("parallel"),
        name="swa_sample",
    )(p4, p4, p4, kc, vc, sink_rows)
    return o.reshape(nseq * ts, SLAB), kn.reshape(k_cache.shape), vn.reshape(v_cache.shape)


def kernel(x_prompt, x_sample, state_a_C, state_a_n, state_a_m, state_a_conv, cache_b_k1, cache_b_v1, cache_b_k2, cache_b_v2, cache_b_k3, cache_b_v3, state_c_conv, state_c_ssm, cache_d_k, cache_d_v, e_norm_mix, e_w_in, e_b_gates, e_a_conv_w, e_a_conv_b, e_a_norm, e_w_out, o_norm_mix, o_w_in, o_c_conv_w, o_c_conv_b, o_c_dt_bias, o_c_A_log, o_c_D, o_c_norm, o_d_sinks, o_w_out, ffn_norm, w_gate, w_up, w_down, final_norm):
    depth = ffn_norm.shape[0]
    even_w = [_even_slabs(e_w_in[j]) for j in range((depth + 1) // 2)]
    odd_w = [_odd_slabs(o_w_in[j]) for j in range(depth // 2)]
    odd_wo = [jnp.concatenate([o_w_out[j][:C_DINNER], _group_aligned(o_w_out[j][C_DINNER:], 0)],
                              axis=0).astype(BF16) for j in range(depth // 2)]
    even_wo = [e_w_out[j].astype(BF16) for j in range((depth + 1) // 2)]
    wg16, wu16, wd16 = w_gate.astype(BF16), w_up.astype(BF16), w_down.astype(BF16)

    def trunk(x, pos, fresh, even_st, odd_st, ride=None, pre=None):
        nseq, t, _ = x.shape
        rode = {}
        assert t >= A_CONV - 1 and t >= C_CONV - 1
        cos_t, sin_t = _rope_tables(pos)
        x2 = x.reshape(nseq * t, D_MODEL)
        new_even, new_odd = [], []

        def tail(proj, s, keep, dil=1, lanes=slice(None)):
            p4 = proj.reshape(proj.shape[0], nseq, t, SLAB)
            if dil == 1:
                return p4[s, :, t - keep:, lanes]
            seg = PERM_TILE // dil
            last = p4[s, :, t - PERM_TILE:, lanes].reshape(nseq, dil, seg, -1)[:, :, seg - keep // dil:]
            return jnp.swapaxes(last, 1, 2).reshape(nseq, keep, -1)

        for layer in range(depth):
            j = layer // 2
            last = layer == depth - 1
            if layer % 2 == 0:
                if pre is not None and layer == 0:
                    pe = pre['proj']
                else:
                    pe = _norm_proj(x2, e_norm_mix[j], even_w[j], cos_t, sin_t, _even_kinds(fresh))
                st = {k: v[j] for k, v in even_st.items()}
                ya, c_new, n_new, m_new = _mlstm(pe, nseq, t, e_a_conv_w[j], e_a_conv_b[j], e_b_gates[j],
                                                 e_a_norm[j], st['a_conv'], st['a_C'], st['a_n'], st['a_m'])
                ns = {'a_C': c_new, 'a_n': n_new, 'a_m': m_new,
                      'a_conv': jnp.concatenate([tail(pe, 0, A_CONV - 1), tail(pe, 1, A_CONV - 1)], axis=-1)}
                parts = []
                for g in range(B_GROUPS):
                    kn, vn = f'b_k{g + 1}', f'b_v{g + 1}'
                    if fresh:
                        o_g, lse_g = _dilated_prompt(pe, nseq, t, g)
                        keep = min(B_WINDOWS[g], t)
                        ns[kn] = tail(pe, 7 + g, keep, B_DILATIONS[g]).reshape(nseq, keep, B_HEADS, HEAD_DIM)
                        ns[vn] = tail(pe, 10 + g, keep, B_DILATIONS[g]).reshape(nseq, keep, B_HEADS, HEAD_DIM)
                    elif pre is not None and layer == 0 and g in pre['groups']:
                        o_g, lse_g, ns[kn], ns[vn] = pre[g]
                    else:
                        o_g, lse_g, ns[kn], ns[vn] = _dilated_sample(pe, nseq, t, g, st[kn], st[vn])
                    parts += [o_g, lse_g]
                yb = _merge_groups(parts, fresh)
                if ride is not None and layer == 0:
                    x2, rode['k'] = _post_rider(
                        x2, ya, yb, even_wo[j], layer, ffn_norm, wg16, wu16, wd16, final_norm, last, 'k',
                        ride['groups'], ride['proj'], ride['kt'], None)
                else:
                    x2 = _post(x2, ya, yb, even_wo[j], layer, ffn_norm, wg16, wu16, wd16, final_norm, last)
                new_even.append(ns)
            else:
                po = _norm_proj(x2, o_norm_mix[j], odd_w[j], cos_t, sin_t, ODD_KINDS)
                st = {k: v[j] for k, v in odd_st.items()}
                yc, ssm_new = _ssd(po, nseq, t, o_c_conv_w[j], o_c_conv_b[j], o_c_dt_bias[j], o_c_A_log[j],
                                   o_c_D[j], o_c_norm[j], st['c_conv'], st['c_ssm'])
                ns = {'c_conv': jnp.concatenate([tail(po, 1, C_CONV - 1), tail(po, 2, C_CONV - 1)], axis=-1),
                      'c_ssm': ssm_new}
                kvw = D_KV_HEADS * HEAD_DIM
                if fresh:
                    yd = _swa_prompt(po, nseq, t, o_d_sinks[j])
                    keep = min(D_WINDOW, t)
                    ns['d_k'] = tail(po, 4, keep, lanes=slice(0, kvw)).reshape(nseq, keep, D_KV_HEADS, HEAD_DIM)
                    ns['d_v'] = tail(po, 4, keep, lanes=slice(kvw, 2 * kvw)).reshape(
                        nseq, keep, D_KV_HEADS, HEAD_DIM)
                else:
                    yd, ns['d_k'], ns['d_v'] = _swa_sample(po, nseq, t, o_d_sinks[j], st['d_k'], st['d_v'])
                if ride is not None and layer == 1:
                    x2, rode['v'] = _post_rider(
                        x2, yc, yd, odd_wo[j], layer, ffn_norm, wg16, wu16, wd16, final_norm, last, 'v',
                        ride['groups'], ride['proj'], ride['vt'], [kpl[1] for kpl in rode['k']])
                else:
                    x2 = _post(x2, yc, yd, odd_wo[j], layer, ffn_norm, wg16, wu16, wd16, final_norm, last)
                new_odd.append(ns)
        out = {k: jnp.stack([ns[k] for ns in new_even]) for k in new_even[0]}
        out.update({k: jnp.stack([ns[k] for ns in new_odd]) for k in new_odd[0]})
        return x2.reshape(nseq, t, D_MODEL), out, rode

    bp, tp, _ = x_prompt.shape
    bs, ts, _ = x_sample.shape
    n_even, n_odd = (depth + 1) // 2, depth // 2
    zeros = lambda *shape: jnp.zeros(shape, F32)
    even_p = {'a_C': zeros(n_even, bp, A_HEADS, A_DK, A_DV), 'a_n': zeros(n_even, bp, A_HEADS, A_DK),
              'a_m': zeros(n_even, bp, A_HEADS), 'a_conv': zeros(n_even, bp, A_CONV - 1, 2 * SLAB)}
    odd_p = {'c_conv': zeros(n_odd, bp, C_CONV - 1, 2 * SLAB),
             'c_ssm': zeros(n_odd, bp, C_HEADS, C_HEADDIM, C_DSTATE)}
    even_s = {'a_C': state_a_C, 'a_n': state_a_n, 'a_m': state_a_m, 'a_conv': state_a_conv,
              'b_k1': cache_b_k1, 'b_v1': cache_b_v1, 'b_k2': cache_b_k2, 'b_v2': cache_b_v2,
              'b_k3': cache_b_k3, 'b_v3': cache_b_v3}
    odd_s = {'c_conv': state_c_conv, 'c_ssm': state_c_ssm, 'd_k': cache_d_k, 'd_v': cache_d_v}
    pos_p = jnp.arange(tp, dtype=jnp.int32)
    pos_s = jnp.tile(PAST_LEN + jnp.arange(ts, dtype=jnp.int32), bs)

    groups = tuple(range(B_GROUPS))
    k_caches = [even_s[f'b_k{g + 1}'][0] for g in groups]
    v_caches = [even_s[f'b_v{g + 1}'][0] for g in groups]
    tile = bp * tp * RIDER_PHASES // bs
    can_ride = (depth >= 2 and bp * tp * RIDER_PHASES % bs == 0 and tile % 8 == 0 and 8 <= tile <= 1024
                and ts == 8 and all(c.shape[1] % 128 == 0 and c.shape[1] >= B_WINDOWS[g]
                                    for g, c in zip(groups, k_caches)))
    if can_ride:
        cos_s, sin_s = _rope_tables(pos_s)
        pe_s = _norm_proj(x_sample.reshape(bs * ts, D_MODEL), e_norm_mix[0], even_w[0], cos_s, sin_s,
                          _even_kinds(False))
        ride = {'groups': groups, 'proj': pe_s, 'kt': [_rows_minor(c) for c in k_caches],
                'vt': [_rows_minor(c) for c in v_caches]}
        y_p, pn, rode = trunk(x_prompt, pos_p, True, even_p, odd_p, ride=ride)
        pre = {'groups': groups, 'proj': pe_s}
        for k, g in enumerate(groups):
            (k_out, _, lse_g), (v_out, o_g) = rode['k'][k], rode['v'][k]
            pre[g] = (o_g, lse_g, _rows_major(k_out, B_HEADS), _rows_major(v_out, B_HEADS))
        y_s, sn, _ = trunk(x_sample, pos_s, False, even_s, odd_s, pre=pre)
    else:
        y_p, pn, _ = trunk(x_prompt, pos_p, True, even_p, odd_p)
        y_s, sn, _ = trunk(x_sample, pos_s, False, even_s, odd_s)

    keys = ('a_C', 'a_n', 'a_m', 'a_conv', 'b_k1', 'b_v1', 'b_k2', 'b_v2', 'b_k3', 'b_v3',
            'c_conv', 'c_ssm', 'd_k', 'd_v')
    states = []
    for k in keys:
        states += [pn[k], sn[k]]
    return (y_p, y_s, *states)
```

```python
import functools
import math

import jax
import jax.numpy as jnp
from jax import lax
from jax.experimental import pallas as pl
from jax.experimental.pallas import tpu as pltpu

F32 = jnp.float32
BF16 = jnp.bfloat16

D_MODEL = 1024
PAST_LEN = 16384
EPS = 1e-6
ROPE_THETA = 10000.0
HEAD_DIM = 64
SLAB = 512

A_HEADS = 4
A_DK = 128
A_DV = 128
A_CONV = 4
A_CHUNK = 64

B_HEADS = 8
B_WINDOWS = (128, 512, 2048)
B_DILATIONS = (1, 4, 16)
B_GROUPS = 3

C_HEADS = 8
C_HEADDIM = 64
C_DINNER = C_HEADS * C_HEADDIM
C_DSTATE = 128
C_NGROUPS = 2
C_CONV = 4
C_CHUNK = 128

D_HEADS = 8
D_KV_HEADS = 2
D_WINDOW = 128

NEG = -1e30
VMEM_LIMIT = 56 * 1024 * 1024


VMEM_LIMIT_RIDER = 60 * 1024 * 1024


def _cparams(*sem, vmem=VMEM_LIMIT):
    return pltpu.CompilerParams(dimension_semantics=sem, vmem_limit_bytes=vmem)


def _rot_half(y):
    lane = lax.broadcasted_iota(jnp.int32, y.shape, 1)
    fwd = pltpu.roll(y, SLAB - HEAD_DIM // 2, 1)
    bwd = pltpu.roll(y, HEAD_DIM // 2, 1)
    return jnp.where(lane % HEAD_DIM < HEAD_DIM // 2, fwd, bwd)


PERM_TILE = 2048
PROJ_ROWS = 512
PERM_STRIDE = 4


def _proj_kernel(x_ref, nw_ref, w_ref, cos_ref, sin_ref, o_ref, xn_ref, *ybuf, kinds):
    j = pl.program_id(1)
    tm = o_ref.shape[0]

    @pl.when(j == 0)
    def _():
        x = x_ref[...]
        ms = jnp.mean(x * x, axis=-1, keepdims=True)
        xn_ref[...] = (x * lax.rsqrt(ms + EPS) * nw_ref[...]).astype(BF16)

    rc = min(tm, PROJ_ROWS)

    def emit(rope_lanes, dil):
        seg = tm // dil
        cseg = rc // dil
        for c in range(tm // rc):
            rows = slice(c * rc, (c + 1) * rc)
            y = jnp.dot(xn_ref[rows, :], w_ref[...], preferred_element_type=F32)
            val = y
            if rope_lanes:
                cos = jnp.tile(cos_ref[rows, :], (1, SLAB // 128))
                sin = jnp.tile(sin_ref[rows, :], (1, SLAB // 128))
                val = y * cos + _rot_half(y) * sin
                if rope_lanes < SLAB:
                    lane = lax.broadcasted_iota(jnp.int32, y.shape, 1)
                    val = jnp.where(lane < rope_lanes, val, y)
            if dil == 1:
                o_ref[rows, :] = val
            else:
                for cb in range(SLAB // 128):
                    ybuf[0][cb, rows, :] = val[:, cb * 128:(cb + 1) * 128]
                if dil <= PERM_STRIDE:
                    for r in range(dil):
                        dst = slice(r * seg + c * cseg, r * seg + (c + 1) * cseg)
                        for cb in range(SLAB // 128):
                            o_ref[dst, cb * 128:(cb + 1) * 128] = ybuf[0][cb, pl.ds(c * rc + r, cseg, stride=dil), :]
                else:
                    d2 = dil // PERM_STRIDE
                    half = rc // PERM_STRIDE
                    for cb in range(SLAB // 128):
                        for r0 in range(PERM_STRIDE):
                            ybuf[1][cb, r0 * half:(r0 + 1) * half, :] = \
                                ybuf[0][cb, pl.ds(c * rc + r0, half, stride=PERM_STRIDE), :]
                    for r0 in range(PERM_STRIDE):
                        for r1 in range(d2):
                            r = r1 * PERM_STRIDE + r0
                            dst = slice(r * seg + c * cseg, r * seg + (c + 1) * cseg)
                            for cb in range(SLAB // 128):
                                o_ref[dst, cb * 128:(cb + 1) * 128] = \
                                    ybuf[1][cb, pl.ds(r0 * half + r1, cseg, stride=d2), :]

    for kind in sorted(set(kinds)):
        member = functools.reduce(jnp.logical_or, [j == s for s, k in enumerate(kinds) if k == kind])
        pl.when(member)(functools.partial(emit, *kind))


def _norm_proj(x, norm_w, w_slabs, cos_t, sin_t, kinds):
    n = x.shape[0]
    s = w_slabs.shape[0]
    p = cos_t.shape[0]
    perm = any(d > 1 for _, d in kinds)
    big = perm or (n % PERM_TILE == 0 and p % PERM_TILE == 0)
    tm = PERM_TILE if big else min(1024, n, p)
    assert n % tm == 0 and p % tm == 0 and len(kinds) == s
    npt = p // tm
    scratch = [pltpu.VMEM((tm, D_MODEL), BF16)]
    if perm:
        scratch.append(pltpu.VMEM((SLAB // 128, tm, 128), F32))
    if any(d > PERM_STRIDE for _, d in kinds):
        scratch.append(pltpu.VMEM((SLAB // 128, min(tm, PROJ_ROWS), 128), F32))
    return pl.pallas_call(
        functools.partial(_proj_kernel, kinds=tuple(kinds)),
        grid=(n // tm, s),
        in_specs=[
            pl.BlockSpec((tm, D_MODEL), lambda i, j: (i, 0), pipeline_mode=pl.Buffered(1)),
            pl.BlockSpec((1, D_MODEL), lambda i, j: (0, 0)),
            pl.BlockSpec((None, D_MODEL, SLAB), lambda i, j: (j, 0, 0)),
            pl.BlockSpec((tm, 128), lambda i, j: (i % npt, 0)),
            pl.BlockSpec((tm, 128), lambda i, j: (i % npt, 0)),
        ],
        out_specs=pl.BlockSpec((None, tm, SLAB), lambda i, j: (j, i, 0)),
        out_shape=jax.ShapeDtypeStruct((s, n, SLAB), F32),
        scratch_shapes=scratch,
        compiler_params=_cparams("parallel", "arbitrary"),
        name="norm_proj",
    )(x, norm_w.reshape(1, D_MODEL), w_slabs, cos_t, sin_t)


def _even_kinds(perm):
    dil = B_DILATIONS if perm else (1,) * B_GROUPS
    return ([(0, 1)] * 4 + [(SLAB, d) for d in dil] * 2 + [(0, d) for d in dil] + [(0, 1)])


ODD_KINDS = [(0, 1)] * 3 + [(SLAB, 1), (D_KV_HEADS * HEAD_DIM, 1)]


def _rope_tables(pos):
    half = HEAD_DIM // 2
    inv = ROPE_THETA ** (-jnp.arange(half, dtype=F32) / half)
    ang = pos.astype(F32)[:, None] * inv[None, :]
    cos, sin = jnp.cos(ang), jnp.sin(ang)
    return (jnp.concatenate([cos, cos, cos, cos], axis=-1),
            jnp.concatenate([-sin, sin, -sin, sin], axis=-1))


def _pad_cols(w, width):
    return jnp.pad(w, ((0, 0), (0, width - w.shape[1])))


def _even_slabs(w_in):
    main = w_in[:, :2048].reshape(D_MODEL, 4, SLAB)
    gates = _pad_cols(w_in[:, 2048:2056], SLAB)[:, None, :]
    qkv = w_in[:, 2056:].reshape(D_MODEL, 9, SLAB)
    return jnp.moveaxis(jnp.concatenate([main, qkv, gates], axis=1), 1, 0).astype(BF16)


def _group_aligned(w, axis):
    rep = D_HEADS // D_KV_HEADS
    shape = w.shape
    split = shape[:axis] + (D_KV_HEADS, rep, HEAD_DIM) + shape[axis + 1:]
    return jnp.swapaxes(w.reshape(split), axis, axis + 1).reshape(shape)


def _odd_slabs(w_in):
    main = w_in[:, :1536].reshape(D_MODEL, 3, SLAB)
    qd = _group_aligned(w_in[:, 1544:2056], 1)[:, None, :]
    kvd = _pad_cols(jnp.concatenate([w_in[:, 2056:2312], w_in[:, 1536:1544]], axis=1), SLAB)[:, None, :]
    return jnp.moveaxis(jnp.concatenate([main, qd, kvd], axis=1), 1, 0).astype(BF16)


def _split3(x):
    hi = x.astype(BF16)
    r1 = x - hi.astype(F32)
    mid = r1.astype(BF16)
    lo = (r1 - mid.astype(F32)).astype(BF16)
    return hi, mid, lo


def _cumsum_rows(x, tri):
    hi, mid, lo = _split3(x)
    acc = jnp.dot(tri, lo, preferred_element_type=F32)
    acc = acc + jnp.dot(tri, mid, preferred_element_type=F32)
    return acc + jnp.dot(tri, hi, preferred_element_type=F32)


def _transpose_slab(s):
    l = s.shape[0]
    if l < 128:
        s = jnp.concatenate([s, jnp.zeros((128 - l, 128), s.dtype)], axis=0)
    return s.T


def _conv_silu(buf_ref, x, w, b):
    tb = x.shape[0]
    buf_ref[8:8 + tb, :] = x
    y = b + w[3:4] * x
    for j in range(3):
        y = y + w[j:j + 1] * buf_ref[5 + j:5 + j + tb, :]
    buf_ref[0:8, :] = buf_ref[tb:tb + 8, :]
    return y * jax.nn.sigmoid(y)


def _log_sigmoid(x):
    return jnp.minimum(x, 0.0) - jnp.log1p(jnp.exp(-jnp.abs(x)))


def _dot_t(a, b):
    return lax.dot_general(a, b, (((1,), (1,)), ((), ())), preferred_element_type=F32)


def _tdot(a, b):
    return lax.dot_general(a, b, (((0,), (0,)), ((), ())), preferred_element_type=F32)


def _mlstm_kernel(q_ref, k_ref, v_ref, og_ref, g_ref, cw_ref, cb_ref, bg_ref, an_ref,
                  conv0_ref, c0_ref, n0_ref, m0_ref,
                  ya_ref, c_ref, n_ref, m_ref, qbuf, kbuf, *, chunk):
    sb, tb, _ = q_ref.shape
    tblk = pl.program_id(1)
    lane = lax.broadcasted_iota(jnp.int32, (chunk, 128), 1)
    lane1 = lax.broadcasted_iota(jnp.int32, (1, 128), 1)
    row = lax.broadcasted_iota(jnp.int32, (chunk, chunk), 0)
    col = lax.broadcasted_iota(jnp.int32, (chunk, chunk), 1)
    causal = col <= row
    tri = causal.astype(BF16)
    scale = A_DK ** -0.5

    @pl.when(tblk == 0)
    def _():
        c_ref[...] = c0_ref[...]
        n_ref[...] = n0_ref[...]
        m_ref[...] = m0_ref[...]
        for si in range(sb):
            qbuf[si, 0:8, :] = conv0_ref[si, :, 0:SLAB]
            kbuf[si, 0:8, :] = conv0_ref[si, :, SLAB:2 * SLAB]

    nchunk = tb // chunk
    seqs, chunks, heads = range(sb), range(nchunk), range(A_HEADS)
    rows = lambda c: slice(c * chunk, (c + 1) * chunk)
    cols = lambda h: slice(h * 128, (h + 1) * 128)
    inst = [(si, c, h) for si in seqs for c in chunks for h in heads]
    qc = [_conv_silu(qbuf.at[si], q_ref[si], cw_ref[:, 0:SLAB], cb_ref[:, 0:SLAB]) for si in seqs]
    kc = [_conv_silu(kbuf.at[si], k_ref[si], cw_ref[:, SLAB:2 * SLAB], cb_ref[:, SLAB:2 * SLAB]) * scale
          for si in seqs]
    gates = {(si, c): g_ref[si, rows(c), 0:128] + bg_ref[...] for si in seqs for c in chunks}
    bc = {k: _cumsum_rows(_log_sigmoid(g), tri) for k, g in gates.items()}
    st = {k: _transpose_slab(jnp.where(lane < A_HEADS, gates[k], bc[k])) for k in gates}
    qh = {(si, c, h): qc[si][rows(c), cols(h)] for si, c, h in inst}
    kh = {(si, c, h): kc[si][rows(c), cols(h)] for si, c, h in inst}
    qb = {k: v.astype(BF16) for k, v in qh.items()}
    vb = {(si, c, h): v_ref[si, rows(c), cols(h)].astype(BF16) for si, c, h in inst}
    b_t = {(si, c, h): bc[si, c][:, A_HEADS + h:A_HEADS + h + 1] for si, c, h in inst}
    dlog = {(si, c, h): jnp.where(causal, b_t[si, c, h] - st[si, c][A_HEADS + h:A_HEADS + h + 1, 0:chunk]
                                  + st[si, c][h:h + 1, 0:chunk], NEG) for si, c, h in inst}
    m_loc = {k: jnp.max(v, axis=1, keepdims=True) for k, v in dlog.items()}
    qk = {k: _dot_t(qb[k], kh[k].astype(BF16)) for k in inst}
    s_loc = {k: qk[k] * jnp.exp(dlog[k] - m_loc[k]) for k in inst}
    sv = {k: jnp.dot(s_loc[k].astype(BF16), vb[k], preferred_element_type=F32) for k in inst}
    rs = {k: jnp.sum(s_loc[k], axis=1, keepdims=True) for k in inst}
    b_end = {k: b_t[k][chunk - 1:chunk, :] for k in inst}
    g_col = {(si, c, h): b_end[si, c, h] - b_t[si, c, h] + gates[si, c][:, h:h + 1] for si, c, h in inst}
    g_max = {k: jnp.max(v, axis=0, keepdims=True) for k, v in g_col.items()}
    kw = {k: kh[k] * jnp.exp(g_col[k] - g_max[k]) for k in inst}
    u = {k: _tdot(kw[k].astype(BF16), vb[k]) for k in inst}
    nk = {k: jnp.sum(kw[k], axis=0, keepdims=True) for k in inst}
    c_st = {(si, h): c_ref[si, h] for si in seqs for h in heads}
    n_st = {(si, h): n_ref[si, h:h + 1, :] for si in seqs for h in heads}
    m_h = {(si, h): m_ref[si][:, h:h + 1] for si in seqs for h in heads}
    for c in chunks:
        cur = [(si, c, h) for si in seqs for h in heads]
        inter = {k: b_t[k] + m_h[k[0], k[2]] for k in cur}
        m_t = {k: jnp.maximum(inter[k], m_loc[k]) for k in cur}
        w_loc = {k: jnp.exp(m_loc[k] - m_t[k]) for k in cur}
        w_inter = {k: jnp.exp(inter[k] - m_t[k]) for k in cur}
        qc_st = {k: jnp.dot(qb[k], c_st[k[0], k[2]].astype(BF16), preferred_element_type=F32) for k in cur}
        qn = {k: jnp.sum(qh[k] * n_st[k[0], k[2]], axis=1, keepdims=True) for k in cur}
        num = {k: w_loc[k] * sv[k] + w_inter[k] * qc_st[k] for k in cur}
        den = {k: w_loc[k] * rs[k] + w_inter[k] * qn[k] for k in cur}
        hh = {k: num[k] / jnp.maximum(jnp.abs(den[k]), jnp.exp(-m_t[k])) for k in cur}
        m_new = {k: jnp.maximum(b_end[k] + m_h[k[0], k[2]], g_max[k]) for k in cur}
        decay = {k: jnp.exp(b_end[k] + m_h[k[0], k[2]] - m_new[k]) for k in cur}
        gain = {k: jnp.exp(g_max[k] - m_new[k]) for k in cur}
        for k in cur:
            sh = (k[0], k[2])
            c_st[sh] = decay[k] * c_st[sh] + gain[k] * u[k]
            n_st[sh] = decay[k] * n_st[sh] + gain[k] * nk[k]
            m_h[sh] = m_new[k]
        ms = {k: jnp.mean(hh[k] * hh[k], axis=-1, keepdims=True) for k in cur}
        for si, _, h in cur:
            hn = hh[si, c, h] * lax.rsqrt(ms[si, c, h] + EPS) * an_ref[:, cols(h)]
            ya_ref[si, rows(c), cols(h)] = hn * jax.nn.sigmoid(og_ref[si, rows(c), cols(h)])
    for si in seqs:
        m_vec = m_ref[si]
        for h in heads:
            c_ref[si, h] = c_st[si, h]
            n_ref[si, h:h + 1, :] = n_st[si, h]
            m_vec = jnp.where(lane1 == h, m_h[si, h], m_vec)
        m_ref[si] = m_vec


def _mlstm(proj, nseq, t, conv_w, conv_b, b_gates, a_norm, conv0, c0, n0, m0):
    chunk = A_CHUNK if t % A_CHUNK == 0 else t
    if t > chunk:
        sb, tb = 1, min(t, 4 * chunk)
    else:
        sb, tb = min(nseq, 8), t
    p4 = proj.reshape(proj.shape[0], nseq, t, SLAB)
    slab = lambda s: pl.BlockSpec((None, sb, tb, SLAB), lambda i, j, s=s: (s, i, j, 0))
    full = lambda a: pl.BlockSpec(a.shape, lambda i, j: (0,) * a.ndim)
    st = lambda a: pl.BlockSpec((sb,) + a.shape[1:], lambda i, j: (i,) + (0,) * (a.ndim - 1))
    bg = _pad_cols(b_gates.reshape(1, 2 * A_HEADS), 128)
    conv0p = jnp.pad(conv0, ((0, 0), (8 - (A_CONV - 1), 0), (0, 0)))
    m0p = _pad_cols(m0, 128).reshape(nseq, 1, 128)
    cb = conv_b.reshape(1, -1)
    an = a_norm.reshape(1, -1)
    ya, c, n, m = pl.pallas_call(
        functools.partial(_mlstm_kernel, chunk=chunk),
        grid=(nseq // sb, t // tb),
        in_specs=[slab(0), slab(1), slab(2), slab(3), slab(13),
                  full(conv_w), full(cb), full(bg), full(an),
                  st(conv0p), st(c0), st(n0), st(m0p)],
        out_specs=[pl.BlockSpec((sb, tb, SLAB), lambda i, j: (i, j, 0)), st(c0), st(n0), st(m0p)],
        out_shape=[jax.ShapeDtypeStruct((nseq, t, SLAB), F32),
                   jax.ShapeDtypeStruct(c0.shape, F32), jax.ShapeDtypeStruct(n0.shape, F32),
                   jax.ShapeDtypeStruct(m0p.shape, F32)],
        scratch_shapes=[pltpu.VMEM((sb, 8 + tb, SLAB), F32), pltpu.VMEM((sb, 8 + tb, SLAB), F32)],
        compiler_params=_cparams("parallel", "arbitrary"),
        name="mlstm",
    )(p4, p4, p4, p4, p4, conv_w, cb, bg, an, conv0p, c0, n0, m0p)
    return ya.reshape(nseq * t, SLAB), c, n, m[:, 0, :A_HEADS]


QBLK = 128


def _band_heads(q_m, k2, v2, valid, sinks):
    n = range(len(q_m))
    s = [jnp.where(valid[i], _dot_t(q_m[i], k2[i]), NEG) for i in n]
    m = [jnp.max(s[i], axis=1, keepdims=True) for i in n]
    if sinks is not None:
        m = [jnp.maximum(m[i], sinks[i]) for i in n]
    e = [jnp.exp(s[i] - m[i]) for i in n]
    l = [jnp.sum(e[i], axis=1, keepdims=True) for i in n]
    if sinks is not None:
        l = [l[i] + jnp.exp(sinks[i] - m[i]) for i in n]
    acc = [jnp.dot(e[i].astype(BF16), v2[i], preferred_element_type=F32) for i in n]
    return acc, m, l


def _band_masks(has_prev, window):
    a = lax.broadcasted_iota(jnp.int32, (QBLK, 2 * QBLK), 0)
    c = lax.broadcasted_iota(jnp.int32, (QBLK, 2 * QBLK), 1)
    rel = a - c + QBLK
    band = jnp.logical_and(rel >= 0, rel <= window)
    if has_prev is True:
        return band
    return jnp.logical_and(band, c >= jnp.where(has_prev, 0, QBLK))


def _key_rows(prev_ref, cur_ref, a, ls):
    if prev_ref.shape[0] == cur_ref.shape[0]:
        before = prev_ref[a * QBLK:(a + 1) * QBLK, ls]
    else:
        before = prev_ref[:, ls] if a == 0 else cur_ref[(a - 1) * QBLK:a * QBLK, ls]
    return jnp.concatenate([before, cur_ref[a * QBLK:(a + 1) * QBLK, ls]], axis=0).astype(BF16)


def _dilated_kernel(q_ref, kp_ref, kc_ref, vp_ref, vc_ref, o_ref, lse_ref):
    lane = lax.broadcasted_iota(jnp.int32, (1, 128), 1)
    low = lane < HEAD_DIM
    scale = HEAD_DIM ** -0.5
    chained = kp_ref.shape[0] != kc_ref.shape[0]
    subs = range(q_ref.shape[0] // QBLK)
    pairs = range(B_HEADS // 2)
    rows = lambda a: slice(a * QBLK, (a + 1) * QBLK)
    cols = lambda pr: slice(pr * 128, (pr + 1) * 128)
    valid = [_band_masks(True if (chained and a > 0) else pl.program_id(2) > 0, QBLK) for a in subs]
    qp = {(a, pr): q_ref[rows(a), cols(pr)] * scale for a in subs for pr in pairs}
    k2 = {(a, pr): _key_rows(kp_ref, kc_ref, a, cols(pr)) for a in subs for pr in pairs}
    v2 = {(a, pr): _key_rows(vp_ref, vc_ref, a, cols(pr)) for a in subs for pr in pairs}
    inst = [(a, pr, p) for a in subs for pr in pairs for p in range(2)]
    acc, m, l = _band_heads(
        [jnp.where(low if p == 0 else jnp.logical_not(low), qp[a, pr], 0.0).astype(BF16) for a, pr, p in inst],
        [k2[a, pr] for a, pr, _ in inst], [v2[a, pr] for a, pr, _ in inst], [valid[a] for a, _, _ in inst], None)
    o_h = [acc[i] / l[i] for i in range(len(inst))]
    lse_h = [m[i] + jnp.log(l[i]) for i in range(len(inst))]
    for i in range(0, len(inst), 2):
        a, pr, _ = inst[i]
        o_ref[rows(a), cols(pr)] = jnp.where(low, o_h[i], o_h[i + 1])
        lse_ref[rows(a), cols(pr)] = jnp.where(low, lse_h[i], lse_h[i + 1])


def _dilated_prompt(proj, b, t, g):
    d = B_DILATIONS[g]
    assert t % PERM_TILE == 0
    nb = PERM_TILE // d // QBLK
    per_tile = PERM_TILE // QBLK
    pv = proj.reshape(proj.shape[0], b, t, SLAB)

    def blk(i, r):
        return (i // nb) * per_tile + r * nb + i % nb

    if nb % 2 == 0:
        nq, classes, steps = 2, d, t // d // QBLK // 2
        cur_idx = lambda i, r: blk(2 * i, r) // 2
        prev_rows, prev_idx = QBLK, lambda i, r: blk(jnp.maximum(2 * i - 1, 0), r)
    elif nb == 1 and d % 2 == 0:
        nq, classes, steps = 2, d // 2, t // d // QBLK
        cur_idx = lambda i, r: blk(i, 2 * r) // 2
        prev_rows, prev_idx = 2 * QBLK, lambda i, r: blk(jnp.maximum(i - 1, 0), 2 * r) // 2
    else:
        nq, classes, steps = 1, d, t // d // QBLK
        cur_idx = blk
        prev_rows, prev_idx = QBLK, lambda i, r: blk(jnp.maximum(i - 1, 0), r)
    cur = lambda s: pl.BlockSpec((None, None, nq * QBLK, SLAB), lambda bi, r, i, s=s: (s, bi, cur_idx(i, r), 0))
    prev = lambda s: pl.BlockSpec((None, None, prev_rows, SLAB), lambda bi, r, i, s=s: (s, bi, prev_idx(i, r), 0))
    ospec = pl.BlockSpec((None, nq * QBLK, SLAB), lambda bi, r, i: (bi, cur_idx(i, r), 0))
    sq, sk, sv = 4 + g, 7 + g, 10 + g
    o, lse = pl.pallas_call(
        _dilated_kernel,
        grid=(b, classes, steps),
        in_specs=[cur(sq), prev(sk), cur(sk), prev(sv), cur(sv)],
        out_specs=[ospec, ospec],
        out_shape=[jax.ShapeDtypeStruct((b, t, SLAB), F32)] * 2,
        compiler_params=_cparams("parallel", "parallel", "arbitrary"),
        name=f"dilated_g{g}",
    )(pv, pv, pv, pv, pv)
    return o.reshape(b * t, SLAB), lse.reshape(b * t, SLAB)


def _merge_kernel(*refs, dils):
    ins, y_ref, bufs = refs[:6], refs[6], refs[7:]
    tm = y_ref.shape[0]
    vals = []
    nbuf = 0
    for k, ref in enumerate(ins):
        dil = dils[k // 2]
        if dil == 1:
            vals.append(ref[...])
        else:
            buf = bufs[nbuf]
            nbuf += 1
            seg = tm // dil
            for r in range(dil):
                buf[pl.ds(r, seg, stride=dil), :] = ref[r * seg:(r + 1) * seg, :]
            vals.append(buf[...])
    o1, l1, o2, l2, o3, l3 = vals
    m = jnp.maximum(jnp.maximum(l1, l2), l3)
    w1, w2, w3 = jnp.exp(l1 - m), jnp.exp(l2 - m), jnp.exp(l3 - m)
    y_ref[...] = (w1 * o1 + w2 * o2 + w3 * o3) / (w1 + w2 + w3)


def _merge_groups(parts, perm):
    n = parts[0].shape[0]
    dils = B_DILATIONS if perm else (1,) * B_GROUPS
    tm = PERM_TILE if perm else min(1024, n)
    spec = pl.BlockSpec((tm, 128), lambda i, j: (i, j))
    nbuf = 2 * sum(d > 1 for d in dils)
    return pl.pallas_call(
        functools.partial(_merge_kernel, dils=dils), grid=(n // tm, SLAB // 128), in_specs=[spec] * 6,
        out_specs=spec, out_shape=jax.ShapeDtypeStruct((n, SLAB), F32),
        scratch_shapes=[pltpu.VMEM((tm, 128), F32)] * nbuf,
        compiler_params=_cparams("parallel", "parallel"), name="merge_groups",
    )(*parts)


def _swa_kernel(q_ref, kp_ref, kc_ref, vp_ref, vc_ref, sink_ref, o_ref):
    lane = lax.broadcasted_iota(jnp.int32, (1, 128), 1)
    scale = HEAD_DIM ** -0.5
    rep = D_HEADS // D_KV_HEADS
    low = lane < HEAD_DIM
    allk = slice(None)
    subs = range(q_ref.shape[0] // QBLK)
    rows = lambda a: slice(a * QBLK, (a + 1) * QBLK)
    cols = lambda pr: slice(pr * 128, (pr + 1) * 128)
    valid = [_band_masks(pl.program_id(1) > 0 if a == 0 else True, D_WINDOW - 1) for a in subs]
    k2 = [_key_rows(kp_ref, kc_ref, a, allk) for a in subs]
    v2 = [_key_rows(vp_ref, vc_ref, a, allk) for a in subs]
    qp = {(a, pr): q_ref[rows(a), cols(pr)] * scale for a in subs for pr in range(rep)}
    inst = [(a, pr, g) for a in subs for pr in range(rep) for g in range(D_KV_HEADS)]
    acc, m, l = _band_heads(
        [jnp.where(low if g == 0 else jnp.logical_not(low), qp[a, pr], 0.0).astype(BF16) for a, pr, g in inst],
        [k2[a] for a, _, _ in inst], [v2[a] for a, _, _ in inst], [valid[a] for a, _, _ in inst],
        [sink_ref[g * rep + pr] for _, pr, g in inst])
    o_h = [acc[i] / l[i] for i in range(len(inst))]
    for i in range(0, len(inst), 2):
        a, pr, _ = inst[i]
        o_ref[rows(a), cols(pr)] = jnp.where(low, o_h[i], o_h[i + 1])


def _swa_prompt(proj, b, t, sinks):
    pv = proj.reshape(proj.shape[0], b, t, SLAB)
    nq = 2 if t % (2 * QBLK) == 0 else 1
    qspec = pl.BlockSpec((None, None, nq * QBLK, SLAB), lambda bi, i: (3, bi, i, 0))

    def kv(col, prev):
        if prev:
            return pl.BlockSpec((None, None, QBLK, 128), lambda bi, i: (4, bi, jnp.maximum(nq * i - 1, 0), col))
        return pl.BlockSpec((None, None, nq * QBLK, 128), lambda bi, i: (4, bi, i, col))

    o = pl.pallas_call(
        _swa_kernel,
        grid=(b, t // QBLK // nq),
        in_specs=[qspec, kv(0, True), kv(0, False), kv(1, True), kv(1, False),
                  pl.BlockSpec(memory_space=pltpu.SMEM)],
        out_specs=pl.BlockSpec((None, nq * QBLK, SLAB), lambda bi, i: (bi, i, 0)),
        out_shape=jax.ShapeDtypeStruct((b, t, SLAB), F32),
        compiler_params=_cparams("parallel", "arbitrary"),
        name="swa_prompt",
    )(pv, pv, pv, pv, pv, sinks)
    return o.reshape(b * t, SLAB)


def _softplus(x):
    return jnp.maximum(x, 0.0) + jnp.log1p(jnp.exp(-jnp.abs(x)))


def _ssd_kernel(z_ref, x_ref, bc_ref, dt_ref, cw_ref, cb_ref, dtb_ref, alog_ref, dl_ref, cn_ref,
                conv0_ref, h0_ref, y_ref, h_ref, xbuf, bcbuf, *, chunk):
    sb, tb, _ = x_ref.shape
    tblk = pl.program_id(1)
    lane1 = lax.broadcasted_iota(jnp.int32, (1, 128), 1)
    low = lane1 < C_HEADDIM
    row = lax.broadcasted_iota(jnp.int32, (chunk, chunk), 0)
    col = lax.broadcasted_iota(jnp.int32, (chunk, chunk), 1)
    causal = col <= row
    tri = causal.astype(BF16)
    srow_low = lax.broadcasted_iota(jnp.int32, (128, 1), 0) < C_HEADDIM
    hpg = C_HEADS // C_NGROUPS
    a_neg = -jnp.exp(alog_ref[...])

    @pl.when(tblk == 0)
    def _():
        h_ref[...] = h0_ref[...]
        for si in range(sb):
            xbuf[si, 0:8, :] = conv0_ref[si, :, 0:SLAB]
            bcbuf[si, 0:8, :] = conv0_ref[si, :, SLAB:2 * SLAB]

    seqs, chunks = range(sb), range(tb // chunk)
    ppg = hpg // 2
    npair = C_HEADS // 2
    rows = lambda c: slice(c * chunk, (c + 1) * chunk)
    cols = lambda pr: slice(pr * 128, (pr + 1) * 128)
    sc = [(si, c) for si in seqs for c in chunks]
    sch = [(si, c, h) for si, c in sc for h in range(C_HEADS)]
    scp = [(si, c, pr) for si, c in sc for pr in range(npair)]
    xc = [_conv_silu(xbuf.at[si], x_ref[si], cw_ref[:, 0:SLAB], cb_ref[:, 0:SLAB]) for si in seqs]
    bcc = [_conv_silu(bcbuf.at[si], bc_ref[si], cw_ref[:, SLAB:2 * SLAB], cb_ref[:, SLAB:2 * SLAB]) for si in seqs]
    dt = {(si, c): _softplus(dt_ref[si, rows(c), :] + dtb_ref[...]) for si, c in sc}
    cum = {k: _cumsum_rows(v * a_neg, tri) for k, v in dt.items()}
    cum_rows = {k: _transpose_slab(v) for k, v in cum.items()}
    dt_rows = {k: _transpose_slab(v) for k, v in dt.items()}
    bgm = {(si, c, g): bcc[si][rows(c), g * 128:(g + 1) * 128].astype(BF16) for si, c in sc
           for g in range(C_NGROUPS)}
    cgm = {(si, c, g): bcc[si][rows(c), 256 + g * 128:256 + (g + 1) * 128].astype(BF16) for si, c in sc
           for g in range(C_NGROUPS)}
    cbm = {k: _dot_t(cgm[k], bgm[k]) for k in bgm}
    x_pair = {(si, c, pr): xc[si][rows(c), cols(pr)] for si, c, pr in scp}
    cum_t = {(si, c, h): cum[si, c][:, h:h + 1] for si, c, h in sch}
    cum_e = {(si, c, h): cum[si, c][chunk - 1:chunk, h:h + 1] for si, c, h in sch}
    seg = {(si, c, h): jnp.where(causal, cum_t[si, c, h] - cum_rows[si, c][h:h + 1, 0:chunk], NEG)
           for si, c, h in sch}
    w = {(si, c, h): cbm[si, c, h // hpg] * jnp.exp(seg[si, c, h]) * dt_rows[si, c][h:h + 1, 0:chunk]
         for si, c, h in sch}
    xhalf = {(si, c, h): jnp.where(low if h % 2 == 0 else jnp.logical_not(low), x_pair[si, c, h // 2],
                                   0.0).astype(BF16) for si, c, h in sch}
    yh = {k: jnp.dot(w[k].astype(BF16), xhalf[k], preferred_element_type=F32) for k in sch}
    e_col = {k: jnp.exp(cum_t[k]) for k in sch}
    w_end = {(si, c, h): jnp.exp(cum_e[si, c, h] - cum_t[si, c, h]) * dt[si, c][:, h:h + 1] for si, c, h in sch}
    dec = {k: jnp.exp(cum_e[k]) for k in sch}
    y_intra = {(si, c, pr): yh[si, c, 2 * pr] + yh[si, c, 2 * pr + 1] + dl_ref[:, cols(pr)] * x_pair[si, c, pr]
               for si, c, pr in scp}
    xw = {(si, c, pr): x_pair[si, c, pr] * jnp.where(low, w_end[si, c, 2 * pr], w_end[si, c, 2 * pr + 1])
          for si, c, pr in scp}
    upd = {(si, c, pr): _tdot(xw[si, c, pr].astype(BF16), bgm[si, c, pr // ppg]) for si, c, pr in scp}
    e_sel = {(si, c, pr): jnp.where(low, e_col[si, c, 2 * pr], e_col[si, c, 2 * pr + 1]) for si, c, pr in scp}
    d_sel = {(si, c, pr): jnp.where(srow_low, dec[si, c, 2 * pr], dec[si, c, 2 * pr + 1]) for si, c, pr in scp}
    gate = {}
    for si, c, pr in scp:
        zz = z_ref[si, rows(c), cols(pr)]
        gate[si, c, pr] = zz * jax.nn.sigmoid(zz)
    hs = {(si, pr): h_ref[si, pr] for si in seqs for pr in range(npair)}
    for c in chunks:
        cur = [(si, c, pr) for si in seqs for pr in range(npair)]
        inter = {k: _dot_t(cgm[k[0], c, k[2] // ppg], hs[k[0], k[2]].astype(BF16)) * e_sel[k] for k in cur}
        for k in cur:
            hs[k[0], k[2]] = d_sel[k] * hs[k[0], k[2]] + upd[k]
        yz = {k: (y_intra[k] + inter[k]) * gate[k] for k in cur}
        sq = {k: jnp.sum(yz[k] * yz[k], axis=-1, keepdims=True) for k in cur}
        for si in seqs:
            for g in range(C_NGROUPS):
                prs = range(g * ppg, (g + 1) * ppg)
                inv = lax.rsqrt(sum(sq[si, c, pr] for pr in prs) / (hpg * C_HEADDIM) + EPS)
                for pr in prs:
                    y_ref[si, rows(c), cols(pr)] = yz[si, c, pr] * inv * cn_ref[:, cols(pr)]
    for si in seqs:
        for pr in range(npair):
            h_ref[si, pr] = hs[si, pr]


def _ssd(proj, nseq, t, conv_w, conv_b, dt_bias, a_log, d_skip, c_norm, conv0, h0):
    chunk = C_CHUNK if t % C_CHUNK == 0 else t
    if t > chunk:
        sb, tb = min(nseq, 2), min(t, 2 * chunk)
    else:
        sb, tb = min(nseq, 8), t
    p4 = proj.reshape(proj.shape[0], nseq, t, SLAB)
    slab = lambda s: pl.BlockSpec((None, sb, tb, SLAB), lambda i, j, s=s: (s, i, j, 0))
    full = lambda a: pl.BlockSpec(a.shape, lambda i, j: (0,) * a.ndim)
    st = lambda a: pl.BlockSpec((sb,) + a.shape[1:], lambda i, j: (i,) + (0,) * (a.ndim - 1))
    conv0p = jnp.pad(conv0, ((0, 0), (8 - (C_CONV - 1), 0), (0, 0)))
    hp = h0.reshape(nseq, C_HEADS // 2, 128, C_DSTATE)
    cb = conv_b.reshape(1, -1)
    dtb = _pad_cols(dt_bias.reshape(1, C_HEADS), 128)
    alog = _pad_cols(a_log.reshape(1, C_HEADS), 128)
    dl = jnp.repeat(d_skip, C_HEADDIM).reshape(1, C_DINNER)
    cn = c_norm.reshape(1, C_DINNER)
    y, h = pl.pallas_call(
        functools.partial(_ssd_kernel, chunk=chunk),
        grid=(nseq // sb, t // tb),
        in_specs=[slab(0), slab(1), slab(2),
                  pl.BlockSpec((None, sb, tb, 128), lambda i, j: (4, i, j, 2)),
                  full(conv_w), full(cb), full(dtb), full(alog), full(dl), full(cn), st(conv0p), st(hp)],
        out_specs=[pl.BlockSpec((sb, tb, SLAB), lambda i, j: (i, j, 0)), st(hp)],
        out_shape=[jax.ShapeDtypeStruct((nseq, t, SLAB), F32), jax.ShapeDtypeStruct(hp.shape, F32)],
        scratch_shapes=[pltpu.VMEM((sb, 8 + tb, SLAB), F32), pltpu.VMEM((sb, 8 + tb, SLAB), F32)],
        compiler_params=_cparams("parallel", "arbitrary"),
        name="ssd",
    )(p4, p4, p4, p4, conv_w, cb, dtb, alog, dl, cn, conv0p, hp)
    return y.reshape(nseq * t, SLAB), h.reshape(nseq, C_HEADS, C_HEADDIM, C_DSTATE)


FFN_CHUNK = 256


def _post_kernel(x_ref, ya_ref, yb_ref, wo_ref, fn_ref, wg_ref, wu_ref, wd_ref, on_ref, o_ref, act_ref, *, final):
    ymix = jnp.concatenate([ya_ref[...], yb_ref[...]], axis=-1).astype(BF16)
    x1 = x_ref[...] + jnp.dot(ymix, wo_ref[...], preferred_element_type=F32)
    ms = jnp.mean(x1 * x1, axis=-1, keepdims=True)
    hn = (x1 * lax.rsqrt(ms + EPS) * fn_ref[...]).astype(BF16)
    hidden = wg_ref.shape[1]
    for c in range(hidden // FFN_CHUNK):
        cs = slice(c * FFN_CHUNK, (c + 1) * FFN_CHUNK)
        gate = jnp.dot(hn, wg_ref[:, cs], preferred_element_type=F32)
        up = jnp.dot(hn, wu_ref[:, cs], preferred_element_type=F32)
        act_ref[:, cs] = (gate * jax.nn.sigmoid(gate) * up).astype(BF16)
    x2 = x1 + jnp.dot(act_ref[...], wd_ref[...], preferred_element_type=F32)
    if final:
        ms2 = jnp.mean(x2 * x2, axis=-1, keepdims=True)
        x2 = x2 * lax.rsqrt(ms2 + EPS) * on_ref[...]
    o_ref[...] = x2


def _post(x, ya, yb, w_out, layer, ffn_norm, w_gate, w_up, w_down, out_norm, final):
    n = x.shape[0]
    tm = min(1024, n)
    hidden = w_gate.shape[2]
    row = lambda w: pl.BlockSpec((tm, w), lambda i: (i, 0))
    res = lambda a: pl.BlockSpec(a.shape, lambda i: (0, 0), pipeline_mode=pl.Buffered(1))
    lay = lambda a: pl.BlockSpec((None,) + a.shape[1:], lambda i: (layer, 0, 0), pipeline_mode=pl.Buffered(1))
    fn = ffn_norm.reshape(-1, 1, D_MODEL)
    on = out_norm.reshape(1, D_MODEL)
    return pl.pallas_call(
        functools.partial(_post_kernel, final=final),
        grid=(n // tm,),
        in_specs=[row(D_MODEL), row(SLAB), row(SLAB), res(w_out), lay(fn), lay(w_gate), lay(w_up), lay(w_down),
                  res(on)],
        out_specs=row(D_MODEL),
        out_shape=jax.ShapeDtypeStruct((n, D_MODEL), F32),
        scratch_shapes=[pltpu.VMEM((tm, hidden), BF16)],
        compiler_params=_cparams("parallel"),
        name="post_ffn",
    )(x, ya, yb, w_out, fn, w_gate, w_up, w_down, on)


RIDER_PHASES = 4


def _rider_mask(ts, r, window, step):
    t = lax.broadcasted_iota(jnp.int32, (2 * ts, r + 128), 0) % ts
    c = lax.broadcasted_iota(jnp.int32, (2 * ts, r + 128), 1)
    rel = r + t - jnp.where(c < 128, c, c - 128 + ts)
    valid = jnp.logical_and(jnp.logical_and(rel >= 0, rel <= window), (rel & (step - 1)) == 0)
    return jnp.logical_and(valid, jnp.logical_or(c < ts, c >= 128))


def _shift_rows_minor(src_ref, new_t, rows, ts):
    r = src_ref.shape[1]
    lane = lax.broadcasted_iota(jnp.int32, (1, 128), 1)
    old = src_ref[rows, :]
    rolled = pltpu.roll(old, r - ts, 1)
    tail = jnp.where(lane < 128 - ts, rolled[:, r - 128:], new_t[rows, :])
    out = tail if r == 128 else jnp.concatenate([rolled[:, :r - 128], tail], axis=1)
    return old, out


def _new_rows_t(new_ref):
    ts = new_ref.shape[0]
    pad = jnp.zeros((128 - ts, SLAB), F32)
    return jnp.concatenate([pad, new_ref[...]], axis=0).T


def _rider_k(q_ref, kn_ref, kc_ref, ko_ref, p_ref, lse_ref, window, step):
    ts = q_ref.shape[0]
    r = kc_ref.shape[1]
    low = lax.broadcasted_iota(jnp.int32, (1, 128), 1) < HEAD_DIM
    valid = _rider_mask(ts, r, window, step)
    kn_t = _new_rows_t(kn_ref)
    scale = HEAD_DIM ** -0.5
    pairs = range(B_HEADS // 2)
    rows = lambda pr: slice(pr * 128, (pr + 1) * 128)
    s = []
    for pr in pairs:
        k_old, k_out = _shift_rows_minor(kc_ref, kn_t, rows(pr), ts)
        ko_ref[rows(pr), :] = k_out
        k_ext = jnp.concatenate([k_old[:, 0:128], k_out], axis=1).astype(BF16)
        qp = q_ref[:, rows(pr)] * scale
        q2 = jnp.concatenate([jnp.where(low, qp, 0.0), jnp.where(low, 0.0, qp)], axis=0).astype(BF16)
        s.append(jnp.where(valid, jnp.dot(q2, k_ext, preferred_element_type=F32), NEG))
    m = [jnp.max(s[pr], axis=1, keepdims=True) for pr in pairs]
    e = [jnp.exp(s[pr] - m[pr]) for pr in pairs]
    l = [jnp.sum(e[pr], axis=1, keepdims=True) for pr in pairs]
    for pr in pairs:
        p_ref[pr * 2 * ts:(pr + 1) * 2 * ts, :] = (e[pr] / l[pr]).astype(BF16)
        lse = m[pr] + jnp.log(l[pr])
        lse_ref[:, rows(pr)] = jnp.where(low, lse[0:ts], lse[ts:2 * ts])


def _rider_v(p_ref, vn_ref, vc_ref, vo_ref, o_ref):
    ts = vn_ref.shape[0]
    low = lax.broadcasted_iota(jnp.int32, (1, 128), 1) < HEAD_DIM
    vn_t = _new_rows_t(vn_ref)
    for pr in range(B_HEADS // 2):
        rows = slice(pr * 128, (pr + 1) * 128)
        v_old, v_out = _shift_rows_minor(vc_ref, vn_t, rows, ts)
        vo_ref[rows, :] = v_out
        v_ext = jnp.concatenate([v_old[:, 0:128], v_out], axis=1).astype(BF16)
        a = _dot_t(p_ref[pr * 2 * ts:(pr + 1) * 2 * ts, :], v_ext)
        o_ref[:, rows] = jnp.where(low, a[0:ts], a[ts:2 * ts])


def _post_rider_kernel(x_ref, ya_ref, yb_ref, wo_ref, fn_ref, wg_ref, wu_ref, wd_ref, on_ref, *rest,
                       final, side, groups):
    ng = len(groups)
    nout = 3 if side == 'k' else 2
    ins, o_ref, outs = rest[:3 * ng], rest[3 * ng], rest[3 * ng + 1:3 * ng + 1 + nout * ng]
    act_ref, hn_ref = rest[3 * ng + 1 + nout * ng:]
    x1_ref = o_ref

    def ride():
        for k, g in enumerate(groups):
            i3, o3 = ins[3 * k:3 * k + 3], outs[nout * k:nout * k + nout]
            if side == 'k':
                _rider_k(*i3, *o3, B_WINDOWS[g], B_DILATIONS[g])
            else:
                _rider_v(*i3, *o3)

    u = pl.program_id(1)
    nchunk = wg_ref.shape[1] // FFN_CHUNK
    per = -(-nchunk // (RIDER_PHASES - 1))
    first = nchunk - per * (RIDER_PHASES - 2)

    def hidden_chunks(lo, hi):
        hn = hn_ref[...]
        for c in range(lo, hi):
            cs = slice(c * FFN_CHUNK, (c + 1) * FFN_CHUNK)
            gate = jnp.dot(hn, wg_ref[:, cs], preferred_element_type=F32)
            up = jnp.dot(hn, wu_ref[:, cs], preferred_element_type=F32)
            act_ref[:, cs] = (gate * jax.nn.sigmoid(gate) * up).astype(BF16)

    def phase(k):
        def body():
            if k == 0:
                ymix = jnp.concatenate([ya_ref[...], yb_ref[...]], axis=-1).astype(BF16)
                x1 = x_ref[...] + jnp.dot(ymix, wo_ref[...], preferred_element_type=F32)
                ms = jnp.mean(x1 * x1, axis=-1, keepdims=True)
                x1_ref[...] = x1
                hn_ref[...] = (x1 * lax.rsqrt(ms + EPS) * fn_ref[...]).astype(BF16)
                hidden_chunks(0, first)
            elif k < RIDER_PHASES - 1:
                hidden_chunks(first + (k - 1) * per, first + k * per)
            else:
                x2 = x1_ref[...] + jnp.dot(act_ref[...], wd_ref[...], preferred_element_type=F32)
                if final:
                    ms2 = jnp.mean(x2 * x2, axis=-1, keepdims=True)
                    x2 = x2 * lax.rsqrt(ms2 + EPS) * on_ref[...]
                o_ref[...] = x2
            ride()
        return body

    for k in range(RIDER_PHASES):
        pl.when(u == k)(phase(k))


def _post_rider(x, ya, yb, w_out, layer, ffn_norm, w_gate, w_up, w_down, out_norm, final, side, groups, proj_s,
                caches_t, probs):
    n = x.shape[0]
    nseq = caches_t[0].shape[0]
    ts = proj_s.shape[1] // nseq
    tm = n * RIDER_PHASES // nseq
    hidden = w_gate.shape[2]
    row = lambda w, **kw: pl.BlockSpec((tm, w), lambda i, u: (i, 0), **kw)
    res = lambda a: pl.BlockSpec(a.shape, lambda i, u: (0, 0), pipeline_mode=pl.Buffered(1))
    lay = lambda a: pl.BlockSpec((None,) + a.shape[1:], lambda i, u: (layer, 0, 0), pipeline_mode=pl.Buffered(1))
    seq = lambda i, u: i * RIDER_PHASES + u
    new = lambda s: pl.BlockSpec((None, ts, SLAB), lambda i, u: (s, seq(i, u), 0))
    rows8 = pl.BlockSpec((ts, SLAB), lambda i, u: (seq(i, u), 0))
    fn = ffn_norm.reshape(-1, 1, D_MODEL)
    on = out_norm.reshape(1, D_MODEL)
    single = dict(pipeline_mode=pl.Buffered(1))
    common = [row(D_MODEL), row(SLAB, **single), row(SLAB, **single), res(w_out), lay(fn), lay(w_gate),
              lay(w_up), lay(w_down), res(on)]
    small = jax.ShapeDtypeStruct((nseq * ts, SLAB), F32)
    ins, args, outs, shapes = [], [], [row(D_MODEL)], [jax.ShapeDtypeStruct((n, D_MODEL), F32)]
    for k, g in enumerate(groups):
        r = caches_t[k].shape[2]
        cspec = pl.BlockSpec((None, SLAB, r), lambda i, u: (seq(i, u), 0, 0))
        pspec = pl.BlockSpec((None, B_HEADS * ts, r + 128), lambda i, u: (seq(i, u), 0, 0))
        cout = jax.ShapeDtypeStruct(caches_t[k].shape, F32)
        if side == 'k':
            ins += [new(4 + g), new(7 + g), cspec]
            args += [proj_s, proj_s, caches_t[k]]
            outs += [cspec, pspec, rows8]
            shapes += [cout, jax.ShapeDtypeStruct((nseq, B_HEADS * ts, r + 128), BF16), small]
        else:
            ins += [pspec, new(10 + g), cspec]
            args += [probs[k], proj_s, caches_t[k]]
            outs += [cspec, rows8]
            shapes += [cout, small]
    res_all = pl.pallas_call(
        functools.partial(_post_rider_kernel, final=final, side=side, groups=tuple(groups)),
        grid=(n // tm, RIDER_PHASES),
        in_specs=common + ins,
        out_specs=outs,
        out_shape=shapes,
        scratch_shapes=[pltpu.VMEM((tm, hidden), BF16), pltpu.VMEM((tm, D_MODEL), BF16)],
        compiler_params=_cparams("parallel", "arbitrary", vmem=VMEM_LIMIT_RIDER),
        name=f"post_ffn_ride_{side}",
    )(x, ya, yb, w_out, fn, w_gate, w_up, w_down, on, *args)
    nout = 3 if side == 'k' else 2
    return res_all[0], [res_all[1 + nout * k:1 + nout * (k + 1)] for k in range(len(groups))]


NEWPAD = 128


def _ext_mask(rows, r, ts, window, step):
    t = lax.broadcasted_iota(jnp.int32, (rows, r + NEWPAD), 0) % ts
    c = lax.broadcasted_iota(jnp.int32, (rows, r + NEWPAD), 1)
    rel = r + t - c
    ok = jnp.logical_and(jnp.logical_and(rel >= 0, rel <= window), c < r + ts)
    return jnp.logical_and(ok, (rel & (step - 1)) == 0)


def _shift_cache(dst_ref, src_ref, new, si, r, ts):
    dst_ref[si, 0:r - ts, :] = src_ref[si, ts:r, :]
    dst_ref[si, r - ts:r, :] = new


def _pad_new(x):
    return jnp.concatenate([x, jnp.zeros((NEWPAD - x.shape[0], x.shape[1]), x.dtype)], axis=0)


def _dil_sample_kernel(q_ref, kn_ref, vn_ref, kc_ref, vc_ref, o_ref, lse_ref, ko_ref, vo_ref,
                       *, window, step):
    sb, ts, _ = q_ref.shape
    r = kc_ref.shape[2]
    scale = HEAD_DIM ** -0.5
    lane = lax.broadcasted_iota(jnp.int32, (1, 128), 1)
    low = lane < HEAD_DIM
    t = lax.broadcasted_iota(jnp.int32, (2 * ts, r + 128), 0) % ts
    c = lax.broadcasted_iota(jnp.int32, (2 * ts, r + 128), 1)
    rel = r + t - jnp.where(c < 128, c, c - 128 + ts)
    valid = jnp.logical_and(jnp.logical_and(rel >= 0, rel <= window), (rel & (step - 1)) == 0)
    valid = jnp.logical_and(valid, jnp.logical_or(c < ts, c >= 128))
    pad = jnp.zeros((128 - ts, SLAB), F32)

    def shifted(src_ref, new_t, rows):
        old = src_ref[rows, :]
        rolled = pltpu.roll(old, r - ts, 1)
        tail = jnp.where(lane < 128 - ts, rolled[:, r - 128:], new_t[rows, :])
        out = tail if r == 128 else jnp.concatenate([rolled[:, :r - 128], tail], axis=1)
        return old, out

    for si, pr in [(si, pr) for si in range(sb) for pr in range(B_HEADS // 2)]:
        if pr == 0:
            kn_t = jnp.concatenate([pad, kn_ref[si]], axis=0).T
            vn_t = jnp.concatenate([pad, vn_ref[si]], axis=0).T
        rows = slice(pr * 128, (pr + 1) * 128)
        k_old, k_out = shifted(kc_ref.at[si], kn_t, rows)
        v_old, v_out = shifted(vc_ref.at[si], vn_t, rows)
        ko_ref[si, rows, :] = k_out
        vo_ref[si, rows, :] = v_out
        k_ext = jnp.concatenate([k_old[:, 0:128], k_out], axis=1).astype(BF16)
        v_ext = jnp.concatenate([v_old[:, 0:128], v_out], axis=1).astype(BF16)
        qp = q_ref[si, :, rows] * scale
        q2 = jnp.concatenate([jnp.where(low, qp, 0.0), jnp.where(low, 0.0, qp)], axis=0).astype(BF16)
        s = jnp.where(valid, jnp.dot(q2, k_ext, preferred_element_type=F32), NEG)
        m = jnp.max(s, axis=1, keepdims=True)
        e = jnp.exp(s - m)
        l = jnp.sum(e, axis=1, keepdims=True)
        a = _dot_t(e.astype(BF16), v_ext) / l
        lse = m + jnp.log(l)
        o_ref[si, :, rows] = jnp.where(low, a[0:ts], a[ts:2 * ts])
        lse_ref[si, :, rows] = jnp.where(low, lse[0:ts], lse[ts:2 * ts])


def _rows_minor(cache):
    nseq, r = cache.shape[:2]
    return jnp.transpose(cache, (0, 2, 3, 1)).reshape(nseq, -1, r)


def _rows_major(cache_t, heads):
    nseq, _, r = cache_t.shape
    return jnp.transpose(cache_t.reshape(nseq, heads, HEAD_DIM, r), (0, 3, 1, 2))


def _dilated_sample(proj, nseq, ts, g, k_cache, v_cache):
    r = k_cache.shape[1]
    assert ts % 8 == 0 and r % 128 == 0 and r >= B_WINDOWS[g]
    kt, vt = _rows_minor(k_cache), _rows_minor(v_cache)
    sb = max(1, min(nseq, 8, 1024 // r))
    p4 = proj.reshape(proj.shape[0], nseq, ts, SLAB)
    new = lambda s: pl.BlockSpec((None, sb, ts, SLAB), lambda i, s=s: (s, i, 0, 0))
    cspec = pl.BlockSpec((sb, SLAB, r), lambda i: (i, 0, 0))
    ospec = pl.BlockSpec((sb, ts, SLAB), lambda i: (i, 0, 0))
    o, lse, kn, vn = pl.pallas_call(
        functools.partial(_dil_sample_kernel, window=B_WINDOWS[g], step=B_DILATIONS[g]),
        grid=(nseq // sb,),
        in_specs=[new(4 + g), new(7 + g), new(10 + g), cspec, cspec],
        out_specs=[ospec, ospec, cspec, cspec],
        out_shape=[jax.ShapeDtypeStruct((nseq, ts, SLAB), F32)] * 2 + [jax.ShapeDtypeStruct(kt.shape, F32)] * 2,
        compiler_params=_cparams("parallel"),
        name=f"dilated_sample_g{g}",
    )(p4, p4, p4, kt, vt)
    return (o.reshape(nseq * ts, SLAB), lse.reshape(nseq * ts, SLAB),
            _rows_major(kn, B_HEADS), _rows_major(vn, B_HEADS))


def _swa_sample_kernel(q_ref, k_ref, v_ref, kc_ref, vc_ref, sink_ref, o_ref, kn_ref, vn_ref):
    sb, ts, _ = q_ref.shape
    r = kc_ref.shape[1]
    rows = D_HEADS * ts
    valid = _ext_mask(rows, r, ts, D_WINDOW - 1, 1)
    lane = lax.broadcasted_iota(jnp.int32, (1, 128), 1)
    low = lane < HEAD_DIM
    rep = D_HEADS // D_KV_HEADS
    scale = HEAD_DIM ** -0.5
    sink = sink_ref[:, 0:1]
    seqs = range(sb)

    def q_rows(si):
        pieces = []
        for pr in range(rep):
            base = q_ref[si, :, pr * 128:(pr + 1) * 128] * scale
            for g in range(D_KV_HEADS):
                pieces.append(jnp.where(low if g == 0 else jnp.logical_not(low), base, 0.0))
        return jnp.concatenate(pieces, axis=0).astype(BF16)

    kn = [k_ref[si] for si in seqs]
    vn = [v_ref[si] for si in seqs]
    qrows = [q_rows(si) for si in seqs]
    kext = [jnp.concatenate([kc_ref[si], _pad_new(kn[si])], axis=0).astype(BF16) for si in seqs]
    vext = [jnp.concatenate([vc_ref[si], _pad_new(vn[si])], axis=0).astype(BF16) for si in seqs]
    s = [jnp.where(valid, _dot_t(qrows[si], kext[si]), NEG) for si in seqs]
    m = [jnp.maximum(jnp.max(s[si], axis=1, keepdims=True), sink) for si in seqs]
    e = [jnp.exp(s[si] - m[si]) for si in seqs]
    l = [jnp.sum(e[si], axis=1, keepdims=True) + jnp.exp(sink - m[si]) for si in seqs]
    o = [jnp.dot(e[si].astype(BF16), vext[si], preferred_element_type=F32) / l[si] for si in seqs]
    for si in seqs:
        pairs = [jnp.where(low, o[si][2 * pr * ts:(2 * pr + 1) * ts], o[si][(2 * pr + 1) * ts:(2 * pr + 2) * ts])
                 for pr in range(rep)]
        o_ref[si] = jnp.concatenate(pairs, axis=1)
        _shift_cache(kn_ref, kc_ref, kn[si], si, r, ts)
        _shift_cache(vn_ref, vc_ref, vn[si], si, r, ts)


def _swa_sample(proj, nseq, ts, sinks, k_cache, v_cache):
    r = k_cache.shape[1]
    sb = min(nseq, 8)
    kvw = D_KV_HEADS * HEAD_DIM
    p4 = proj.reshape(proj.shape[0], nseq, ts, SLAB)
    kc = k_cache.reshape(nseq, r, kvw)
    vc = v_cache.reshape(nseq, r, kvw)
    cspec = pl.BlockSpec((sb, r, kvw), lambda i: (i, 0, 0))
    new = lambda col: pl.BlockSpec((None, sb, ts, kvw), lambda i: (4, i, 0, col))
    rep = D_HEADS // D_KV_HEADS
    by_row_group = sinks.reshape(D_KV_HEADS, rep).T.reshape(-1)
    sink_rows = jnp.broadcast_to(jnp.repeat(by_row_group, ts)[:, None], (D_HEADS * ts, 128))
    o, kn, vn = pl.pallas_call(
        _swa_sample_kernel,
        grid=(nseq // sb,),
        in_specs=[pl.BlockSpec((None, sb, ts, SLAB), lambda i: (3, i, 0, 0)), new(0), new(1), cspec, cspec,
                  pl.BlockSpec(sink_rows.shape, lambda i: (0, 0))],
        out_specs=[pl.BlockSpec((sb, ts, SLAB), lambda i: (i, 0, 0)), cspec, cspec],
        out_shape=[jax.ShapeDtypeStruct((nseq, ts, SLAB), F32)] + [jax.ShapeDtypeStruct(kc.shape, F32)] * 2,
        compiler_params=_cparams("parallel"),
        name="swa_sample",
    )(p4, p4, p4, kc, vc, sink_rows)
    return o.reshape(nseq * ts, SLAB), kn.reshape(k_cache.shape), vn.reshape(v_cache.shape)


def kernel(x_prompt, x_sample, state_a_C, state_a_n, state_a_m, state_a_conv, cache_b_k1, cache_b_v1, cache_b_k2, cache_b_v2, cache_b_k3, cache_b_v3, state_c_conv, state_c_ssm, cache_d_k, cache_d_v, e_norm_mix, e_w_in, e_b_gates, e_a_conv_w, e_a_conv_b, e_a_norm, e_w_out, o_norm_mix, o_w_in, o_c_conv_w, o_c_conv_b, o_c_dt_bias, o_c_A_log, o_c_D, o_c_norm, o_d_sinks, o_w_out, ffn_norm, w_gate, w_up, w_down, final_norm):
    depth = ffn_norm.shape[0]
    even_w = [_even_slabs(e_w_in[j]) for j in range((depth + 1) // 2)]
    odd_w = [_odd_slabs(o_w_in[j]) for j in range(depth // 2)]
    odd_wo = [jnp.concatenate([o_w_out[j][:C_DINNER], _group_aligned(o_w_out[j][C_DINNER:], 0)],
                              axis=0).astype(BF16) for j in range(depth // 2)]
    even_wo = [e_w_out[j].astype(BF16) for j in range((depth + 1) // 2)]
    wg16, wu16, wd16 = w_gate.astype(BF16), w_up.astype(BF16), w_down.astype(BF16)

    def trunk(x, pos, fresh, even_st, odd_st, ride=None, pre=None):
        nseq, t, _ = x.shape
        rode = {}
        assert t >= A_CONV - 1 and t >= C_CONV - 1
        cos_t, sin_t = _rope_tables(pos)
        x2 = x.reshape(nseq * t, D_MODEL)
        new_even, new_odd = [], []

        def tail(proj, s, keep, dil=1, lanes=slice(None)):
            p4 = proj.reshape(proj.shape[0], nseq, t, SLAB)
            if dil == 1:
                return p4[s, :, t - keep:, lanes]
            seg = PERM_TILE // dil
            last = p4[s, :, t - PERM_TILE:, lanes].reshape(nseq, dil, seg, -1)[:, :, seg - keep // dil:]
            return jnp.swapaxes(last, 1, 2).reshape(nseq, keep, -1)

        for layer in range(depth):
            j = layer // 2
            last = layer == depth - 1
            if layer % 2 == 0:
                if pre is not None and layer == 0:
                    pe = pre['proj']
                else:
                    pe = _norm_proj(x2, e_norm_mix[j], even_w[j], cos_t, sin_t, _even_kinds(fresh))
                st = {k: v[j] for k, v in even_st.items()}
                ya, c_new, n_new, m_new = _mlstm(pe, nseq, t, e_a_conv_w[j], e_a_conv_b[j], e_b_gates[j],
                                                 e_a_norm[j], st['a_conv'], st['a_C'], st['a_n'], st['a_m'])
                ns = {'a_C': c_new, 'a_n': n_new, 'a_m': m_new,
                      'a_conv': jnp.concatenate([tail(pe, 0, A_CONV - 1), tail(pe, 1, A_CONV - 1)], axis=-1)}
                parts = []
                for g in range(B_GROUPS):
                    kn, vn = f'b_k{g + 1}', f'b_v{g + 1}'
                    if fresh:
                        o_g, lse_g = _dilated_prompt(pe, nseq, t, g)
                        keep = min(B_WINDOWS[g], t)
                        ns[kn] = tail(pe, 7 + g, keep, B_DILATIONS[g]).reshape(nseq, keep, B_HEADS, HEAD_DIM)
                        ns[vn] = tail(pe, 10 + g, keep, B_DILATIONS[g]).reshape(nseq, keep, B_HEADS, HEAD_DIM)
                    elif pre is not None and layer == 0 and g in pre['groups']:
                        o_g, lse_g, ns[kn], ns[vn] = pre[g]
                    else:
                        o_g, lse_g, ns[kn], ns[vn] = _dilated_sample(pe, nseq, t, g, st[kn], st[vn])
                    parts += [o_g, lse_g]
                yb = _merge_groups(parts, fresh)
                if ride is not None and layer == 0:
                    x2, rode['k'] = _post_rider(
                        x2, ya, yb, even_wo[j], layer, ffn_norm, wg16, wu16, wd16, final_norm, last, 'k',
                        ride['groups'], ride['proj'], ride['kt'], None)
                else:
                    x2 = _post(x2, ya, yb, even_wo[j], layer, ffn_norm, wg16, wu16, wd16, final_norm, last)
                new_even.append(ns)
            else:
                po = _norm_proj(x2, o_norm_mix[j], odd_w[j], cos_t, sin_t, ODD_KINDS)
                st = {k: v[j] for k, v in odd_st.items()}
                yc, ssm_new = _ssd(po, nseq, t, o_c_conv_w[j], o_c_conv_b[j], o_c_dt_bias[j], o_c_A_log[j],
                                   o_c_D[j], o_c_norm[j], st['c_conv'], st['c_ssm'])
                ns = {'c_conv': jnp.concatenate([tail(po, 1, C_CONV - 1), tail(po, 2, C_CONV - 1)], axis=-1),
                      'c_ssm': ssm_new}
                kvw = D_KV_HEADS * HEAD_DIM
                if fresh:
                    yd = _swa_prompt(po, nseq, t, o_d_sinks[j])
                    keep = min(D_WINDOW, t)
                    ns['d_k'] = tail(po, 4, keep, lanes=slice(0, kvw)).reshape(nseq, keep, D_KV_HEADS, HEAD_DIM)
                    ns['d_v'] = tail(po, 4, keep, lanes=slice(kvw, 2 * kvw)).reshape(
                        nseq, keep, D_KV_HEADS, HEAD_DIM)
                else:
                    yd, ns['d_k'], ns['d_v'] = _swa_sample(po, nseq, t, o_d_sinks[j], st['d_k'], st['d_v'])
                if ride is not None and layer == 1:
                    x2, rode['v'] = _post_rider(
                        x2, yc, yd, odd_wo[j], layer, ffn_norm, wg16, wu16, wd16, final_norm, last, 'v',
                        ride['groups'], ride['proj'], ride['vt'], [kpl[1] for kpl in rode['k']])
                else:
                    x2 = _post(x2, yc, yd, odd_wo[j], layer, ffn_norm, wg16, wu16, wd16, final_norm, last)
                new_odd.append(ns)
        out = {k: jnp.stack([ns[k] for ns in new_even]) for k in new_even[0]}
        out.update({k: jnp.stack([ns[k] for ns in new_odd]) for k in new_odd[0]})
        return x2.reshape(nseq, t, D_MODEL), out, rode

    bp, tp, _ = x_prompt.shape
    bs, ts, _ = x_sample.shape
    n_even, n_odd = (depth + 1) // 2, depth // 2
    zeros = lambda *shape: jnp.zeros(shape, F32)
    even_p = {'a_C': zeros(n_even, bp, A_HEADS, A_DK, A_DV), 'a_n': zeros(n_even, bp, A_HEADS, A_DK),
              'a_m': zeros(n_even, bp, A_HEADS), 'a_conv': zeros(n_even, bp, A_CONV - 1, 2 * SLAB)}
    odd_p = {'c_conv': zeros(n_odd, bp, C_CONV - 1, 2 * SLAB),
             'c_ssm': zeros(n_odd, bp, C_HEADS, C_HEADDIM, C_DSTATE)}
    even_s = {'a_C': state_a_C, 'a_n': state_a_n, 'a_m': state_a_m, 'a_conv': state_a_conv,
              'b_k1': cache_b_k1, 'b_v1': cache_b_v1, 'b_k2': cache_b_k2, 'b_v2': cache_b_v2,
              'b_k3': cache_b_k3, 'b_v3': cache_b_v3}
    odd_s = {'c_conv': state_c_conv, 'c_ssm': state_c_ssm, 'd_k': cache_d_k, 'd_v': cache_d_v}
    pos_p = jnp.arange(tp, dtype=jnp.int32)
    pos_s = jnp.tile(PAST_LEN + jnp.arange(ts, dtype=jnp.int32), bs)

    groups = tuple(range(B_GROUPS))
    k_caches = [even_s[f'b_k{g + 1}'][0] for g in groups]
    v_caches = [even_s[f'b_v{g + 1}'][0] for g in groups]
    tile = bp * tp * RIDER_PHASES // bs
    can_ride = (depth >= 2 and bp * tp * RIDER_PHASES % bs == 0 and tile % 8 == 0 and 8 <= tile <= 1024
                and ts == 8 and all(c.shape[1] % 128 == 0 and c.shape[1] >= B_WINDOWS[g]
                                    for g, c in zip(groups, k_caches)))
    if can_ride:
        cos_s, sin_s = _rope_tables(pos_s)
        pe_s = _norm_proj(x_sample.reshape(bs * ts, D_MODEL), e_norm_mix[0], even_w[0], cos_s, sin_s,
                          _even_kinds(False))
        ride = {'groups': groups, 'proj': pe_s, 'kt': [_rows_minor(c) for c in k_caches],
                'vt': [_rows_minor(c) for c in v_caches]}
        y_p, pn, rode = trunk(x_prompt, pos_p, True, even_p, odd_p, ride=ride)
        pre = {'groups': groups, 'proj': pe_s}
        for k, g in enumerate(groups):
            (k_out, _, lse_g), (v_out, o_g) = rode['k'][k], rode['v'][k]
            pre[g] = (o_g, lse_g, _rows_major(k_out, B_HEADS), _rows_major(v_out, B_HEADS))
        y_s, sn, _ = trunk(x_sample, pos_s, False, even_s, odd_s, pre=pre)
    else:
        y_p, pn, _ = trunk(x_prompt, pos_p, True, even_p, odd_p)
        y_s, sn, _ = trunk(x_sample, pos_s, False, even_s, odd_s)

    keys = ('a_C', 'a_n', 'a_m', 'a_conv', 'b_k1', 'b_v1', 'b_k2', 'b_v2', 'b_k3', 'b_v3',
            'c_conv', 'c_ssm', 'd_k', 'd_v')
    states = []
    for k in keys:
        states += [pn[k], sn[k]]
    return (y_p, y_s, *states)
```

```python
import functools
import math

import jax
import jax.numpy as jnp
from jax import lax
from jax.experimental import pallas as pl
from jax.experimental.pallas import tpu as pltpu

F32 = jnp.float32
BF16 = jnp.bfloat16

D_MODEL = 1024
PAST_LEN = 16384
EPS = 1e-6
ROPE_THETA = 10000.0
HEAD_DIM = 64
SLAB = 512

A_HEADS = 4
A_DK = 128
A_DV = 128
A_CONV = 4
A_CHUNK = 64

B_HEADS = 8
B_WINDOWS = (128, 512, 2048)
B_DILATIONS = (1, 4, 16)
B_GROUPS = 3

C_HEADS = 8
C_HEADDIM = 64
C_DINNER = C_HEADS * C_HEADDIM
C_DSTATE = 128
C_NGROUPS = 2
C_CONV = 4
C_CHUNK = 128

D_HEADS = 8
D_KV_HEADS = 2
D_WINDOW = 128

NEG = -1e30
VMEM_LIMIT = 56 * 1024 * 1024


VMEM_LIMIT_RIDER = 60 * 1024 * 1024


def _cparams(*sem, vmem=VMEM_LIMIT):
    return pltpu.CompilerParams(dimension_semantics=sem, vmem_limit_bytes=vmem)


def _rot_half(y):
    lane = lax.broadcasted_iota(jnp.int32, y.shape, 1)
    fwd = pltpu.roll(y, SLAB - HEAD_DIM // 2, 1)
    bwd = pltpu.roll(y, HEAD_DIM // 2, 1)
    return jnp.where(lane % HEAD_DIM < HEAD_DIM // 2, fwd, bwd)


PERM_TILE = 2048
PROJ_ROWS = 512
PERM_STRIDE = 4


def _proj_kernel(x_ref, nw_ref, w_ref, cos_ref, sin_ref, o_ref, xn_ref, *ybuf, kinds):
    j = pl.program_id(1)
    tm = o_ref.shape[0]

    @pl.when(j == 0)
    def _():
        x = x_ref[...]
        ms = jnp.mean(x * x, axis=-1, keepdims=True)
        xn_ref[...] = (x * lax.rsqrt(ms + EPS) * nw_ref[...]).astype(BF16)

    rc = min(tm, PROJ_ROWS)

    def emit(rope_lanes, dil):
        seg = tm // dil
        cseg = rc // dil
        for c in range(tm // rc):
            rows = slice(c * rc, (c + 1) * rc)
            y = jnp.dot(xn_ref[rows, :], w_ref[...], preferred_element_type=F32)
            val = y
            if rope_lanes:
                cos = jnp.tile(cos_ref[rows, :], (1, SLAB // 128))
                sin = jnp.tile(sin_ref[rows, :], (1, SLAB // 128))
                val = y * cos + _rot_half(y) * sin
                if rope_lanes < SLAB:
                    lane = lax.broadcasted_iota(jnp.int32, y.shape, 1)
                    val = jnp.where(lane < rope_lanes, val, y)
            if dil == 1:
                o_ref[rows, :] = val
            else:
                for cb in range(SLAB // 128):
                    ybuf[0][cb, rows, :] = val[:, cb * 128:(cb + 1) * 128]
                if dil <= PERM_STRIDE:
                    for r in range(dil):
                        dst = slice(r * seg + c * cseg, r * seg + (c + 1) * cseg)
                        for cb in range(SLAB // 128):
                            o_ref[dst, cb * 128:(cb + 1) * 128] = ybuf[0][cb, pl.ds(c * rc + r, cseg, stride=dil), :]
                else:
                    d2 = dil // PERM_STRIDE
                    half = rc // PERM_STRIDE
                    for cb in range(SLAB // 128):
                        for r0 in range(PERM_STRIDE):
                            ybuf[1][cb, r0 * half:(r0 + 1) * half, :] = \
                                ybuf[0][cb, pl.ds(c * rc + r0, half, stride=PERM_STRIDE), :]
                    for r0 in range(PERM_STRIDE):
                        for r1 in range(d2):
                            r = r1 * PERM_STRIDE + r0
                            dst = slice(r * seg + c * cseg, r * seg + (c + 1) * cseg)
                            for cb in range(SLAB // 128):
                                o_ref[dst, cb * 128:(cb + 1) * 128] = \
                                    ybuf[1][cb, pl.ds(r0 * half + r1, cseg, stride=d2), :]

    for kind in sorted(set(kinds)):
        member = functools.reduce(jnp.logical_or, [j == s for s, k in enumerate(kinds) if k == kind])
        pl.when(member)(functools.partial(emit, *kind))


def _norm_proj(x, norm_w, w_slabs, cos_t, sin_t, kinds):
    n = x.shape[0]
    s = w_slabs.shape[0]
    p = cos_t.shape[0]
    perm = any(d > 1 for _, d in kinds)
    big = perm or (n % PERM_TILE == 0 and p % PERM_TILE == 0)
    tm = PERM_TILE if big else min(1024, n, p)
    assert n % tm == 0 and p % tm == 0 and len(kinds) == s
    npt = p // tm
    scratch = [pltpu.VMEM((tm, D_MODEL), BF16)]
    if perm:
        scratch.append(pltpu.VMEM((SLAB // 128, tm, 128), F32))
    if any(d > PERM_STRIDE for _, d in kinds):
        scratch.append(pltpu.VMEM((SLAB // 128, min(tm, PROJ_ROWS), 128), F32))
    return pl.pallas_call(
        functools.partial(_proj_kernel, kinds=tuple(kinds)),
        grid=(n // tm, s),
        in_specs=[
            pl.BlockSpec((tm, D_MODEL), lambda i, j: (i, 0)),
            pl.BlockSpec((1, D_MODEL), lambda i, j: (0, 0)),
            pl.BlockSpec((None, D_MODEL, SLAB), lambda i, j: (j, 0, 0)),
            pl.BlockSpec((tm, 128), lambda i, j: (i % npt, 0)),
            pl.BlockSpec((tm, 128), lambda i, j: (i % npt, 0)),
        ],
        out_specs=pl.BlockSpec((None, tm, SLAB), lambda i, j: (j, i, 0)),
        out_shape=jax.ShapeDtypeStruct((s, n, SLAB), F32),
        scratch_shapes=scratch,
        compiler_params=_cparams("parallel", "arbitrary"),
        name="norm_proj",
    )(x, norm_w.reshape(1, D_MODEL), w_slabs, cos_t, sin_t)


def _even_kinds(perm):
    dil = B_DILATIONS if perm else (1,) * B_GROUPS
    return ([(0, 1)] * 4 + [(SLAB, d) for d in dil] * 2 + [(0, d) for d in dil] + [(0, 1)])


ODD_KINDS = [(0, 1)] * 3 + [(SLAB, 1), (D_KV_HEADS * HEAD_DIM, 1)]


def _rope_tables(pos):
    half = HEAD_DIM // 2
    inv = ROPE_THETA ** (-jnp.arange(half, dtype=F32) / half)
    ang = pos.astype(F32)[:, None] * inv[None, :]
    cos, sin = jnp.cos(ang), jnp.sin(ang)
    return (jnp.concatenate([cos, cos, cos, cos], axis=-1),
            jnp.concatenate([-sin, sin, -sin, sin], axis=-1))


def _pad_cols(w, width):
    return jnp.pad(w, ((0, 0), (0, width - w.shape[1])))


def _even_slabs(w_in):
    main = w_in[:, :2048].reshape(D_MODEL, 4, SLAB)
    gates = _pad_cols(w_in[:, 2048:2056], SLAB)[:, None, :]
    qkv = w_in[:, 2056:].reshape(D_MODEL, 9, SLAB)
    return jnp.moveaxis(jnp.concatenate([main, qkv, gates], axis=1), 1, 0).astype(BF16)


def _group_aligned(w, axis):
    rep = D_HEADS // D_KV_HEADS
    shape = w.shape
    split = shape[:axis] + (D_KV_HEADS, rep, HEAD_DIM) + shape[axis + 1:]
    return jnp.swapaxes(w.reshape(split), axis, axis + 1).reshape(shape)


def _odd_slabs(w_in):
    main = w_in[:, :1536].reshape(D_MODEL, 3, SLAB)
    qd = _group_aligned(w_in[:, 1544:2056], 1)[:, None, :]
    kvd = _pad_cols(jnp.concatenate([w_in[:, 2056:2312], w_in[:, 1536:1544]], axis=1), SLAB)[:, None, :]
    return jnp.moveaxis(jnp.concatenate([main, qd, kvd], axis=1), 1, 0).astype(BF16)


def _split3(x):
    hi = x.astype(BF16)
    r1 = x - hi.astype(F32)
    mid = r1.astype(BF16)
    lo = (r1 - mid.astype(F32)).astype(BF16)
    return hi, mid, lo


def _cumsum_rows(x, tri):
    hi, mid, lo = _split3(x)
    acc = jnp.dot(tri, lo, preferred_element_type=F32)
    acc = acc + jnp.dot(tri, mid, preferred_element_type=F32)
    return acc + jnp.dot(tri, hi, preferred_element_type=F32)


def _transpose_slab(s):
    l = s.shape[0]
    if l < 128:
        s = jnp.concatenate([s, jnp.zeros((128 - l, 128), s.dtype)], axis=0)
    return s.T


def _conv_silu(buf_ref, x, w, b):
    tb = x.shape[0]
    buf_ref[8:8 + tb, :] = x
    y = b + w[3:4] * x
    for j in range(3):
        y = y + w[j:j + 1] * buf_ref[5 + j:5 + j + tb, :]
    buf_ref[0:8, :] = buf_ref[tb:tb + 8, :]
    return y * jax.nn.sigmoid(y)


def _log_sigmoid(x):
    return jnp.minimum(x, 0.0) - jnp.log1p(jnp.exp(-jnp.abs(x)))


def _dot_t(a, b):
    return lax.dot_general(a, b, (((1,), (1,)), ((), ())), preferred_element_type=F32)


def _tdot(a, b):
    return lax.dot_general(a, b, (((0,), (0,)), ((), ())), preferred_element_type=F32)


def _mlstm_kernel(q_ref, k_ref, v_ref, og_ref, g_ref, cw_ref, cb_ref, bg_ref, an_ref,
                  conv0_ref, c0_ref, n0_ref, m0_ref,
                  ya_ref, c_ref, n_ref, m_ref, qbuf, kbuf, *, chunk):
    sb, tb, _ = q_ref.shape
    tblk = pl.program_id(1)
    lane = lax.broadcasted_iota(jnp.int32, (chunk, 128), 1)
    lane1 = lax.broadcasted_iota(jnp.int32, (1, 128), 1)
    row = lax.broadcasted_iota(jnp.int32, (chunk, chunk), 0)
    col = lax.broadcasted_iota(jnp.int32, (chunk, chunk), 1)
    causal = col <= row
    tri = causal.astype(BF16)
    scale = A_DK ** -0.5

    @pl.when(tblk == 0)
    def _():
        c_ref[...] = c0_ref[...]
        n_ref[...] = n0_ref[...]
        m_ref[...] = m0_ref[...]
        for si in range(sb):
            qbuf[si, 0:8, :] = conv0_ref[si, :, 0:SLAB]
            kbuf[si, 0:8, :] = conv0_ref[si, :, SLAB:2 * SLAB]

    nchunk = tb // chunk
    seqs, chunks, heads = range(sb), range(nchunk), range(A_HEADS)
    rows = lambda c: slice(c * chunk, (c + 1) * chunk)
    cols = lambda h: slice(h * 128, (h + 1) * 128)
    inst = [(si, c, h) for si in seqs for c in chunks for h in heads]
    qc = [_conv_silu(qbuf.at[si], q_ref[si], cw_ref[:, 0:SLAB], cb_ref[:, 0:SLAB]) for si in seqs]
    kc = [_conv_silu(kbuf.at[si], k_ref[si], cw_ref[:, SLAB:2 * SLAB], cb_ref[:, SLAB:2 * SLAB]) * scale
          for si in seqs]
    gates = {(si, c): g_ref[si, rows(c), 0:128] + bg_ref[...] for si in seqs for c in chunks}
    bc = {k: _cumsum_rows(_log_sigmoid(g), tri) for k, g in gates.items()}
    st = {k: _transpose_slab(jnp.where(lane < A_HEADS, gates[k], bc[k])) for k in gates}
    qh = {(si, c, h): qc[si][rows(c), cols(h)] for si, c, h in inst}
    kh = {(si, c, h): kc[si][rows(c), cols(h)] for si, c, h in inst}
    qb = {k: v.astype(BF16) for k, v in qh.items()}
    vb = {(si, c, h): v_ref[si, rows(c), cols(h)].astype(BF16) for si, c, h in inst}
    b_t = {(si, c, h): bc[si, c][:, A_HEADS + h:A_HEADS + h + 1] for si, c, h in inst}
    dlog = {(si, c, h): jnp.where(causal, b_t[si, c, h] - st[si, c][A_HEADS + h:A_HEADS + h + 1, 0:chunk]
                                  + st[si, c][h:h + 1, 0:chunk], NEG) for si, c, h in inst}
    m_loc = {k: jnp.max(v, axis=1, keepdims=True) for k, v in dlog.items()}
    qk = {k: _dot_t(qb[k], kh[k].astype(BF16)) for k in inst}
    s_loc = {k: qk[k] * jnp.exp(dlog[k] - m_loc[k]) for k in inst}
    sv = {k: jnp.dot(s_loc[k].astype(BF16), vb[k], preferred_element_type=F32) for k in inst}
    rs = {k: jnp.sum(s_loc[k], axis=1, keepdims=True) for k in inst}
    b_end = {k: b_t[k][chunk - 1:chunk, :] for k in inst}
    g_col = {(si, c, h): b_end[si, c, h] - b_t[si, c, h] + gates[si, c][:, h:h + 1] for si, c, h in inst}
    g_max = {k: jnp.max(v, axis=0, keepdims=True) for k, v in g_col.items()}
    kw = {k: kh[k] * jnp.exp(g_col[k] - g_max[k]) for k in inst}
    u = {k: _tdot(kw[k].astype(BF16), vb[k]) for k in inst}
    nk = {k: jnp.sum(kw[k], axis=0, keepdims=True) for k in inst}
    c_st = {(si, h): c_ref[si, h] for si in seqs for h in heads}
    n_st = {(si, h): n_ref[si, h:h + 1, :] for si in seqs for h in heads}
    m_h = {(si, h): m_ref[si][:, h:h + 1] for si in seqs for h in heads}
    for c in chunks:
        cur = [(si, c, h) for si in seqs for h in heads]
        inter = {k: b_t[k] + m_h[k[0], k[2]] for k in cur}
        m_t = {k: jnp.maximum(inter[k], m_loc[k]) for k in cur}
        w_loc = {k: jnp.exp(m_loc[k] - m_t[k]) for k in cur}
        w_inter = {k: jnp.exp(inter[k] - m_t[k]) for k in cur}
        qc_st = {k: jnp.dot(qb[k], c_st[k[0], k[2]].astype(BF16), preferred_element_type=F32) for k in cur}
        qn = {k: jnp.sum(qh[k] * n_st[k[0], k[2]], axis=1, keepdims=True) for k in cur}
        num = {k: w_loc[k] * sv[k] + w_inter[k] * qc_st[k] for k in cur}
        den = {k: w_loc[k] * rs[k] + w_inter[k] * qn[k] for k in cur}
        hh = {k: num[k] / jnp.maximum(jnp.abs(den[k]), jnp.exp(-m_t[k])) for k in cur}
        m_new = {k: jnp.maximum(b_end[k] + m_h[k[0], k[2]], g_max[k]) for k in cur}
        decay = {k: jnp.exp(b_end[k] + m_h[k[0], k[2]] - m_new[k]) for k in cur}
        gain = {k: jnp.exp(g_max[k] - m_new[k]) for k in cur}
        for k in cur:
            sh = (k[0], k[2])
            c_st[sh] = decay[k] * c_st[sh] + gain[k] * u[k]
            n_st[sh] = decay[k] * n_st[sh] + gain[k] * nk[k]
            m_h[sh] = m_new[k]
        ms = {k: jnp.mean(hh[k] * hh[k], axis=-1, keepdims=True) for k in cur}
        for si, _, h in cur:
            hn = hh[si, c, h] * lax.rsqrt(ms[si, c, h] + EPS) * an_ref[:, cols(h)]
            ya_ref[si, rows(c), cols(h)] = hn * jax.nn.sigmoid(og_ref[si, rows(c), cols(h)])
    for si in seqs:
        m_vec = m_ref[si]
        for h in heads:
            c_ref[si, h] = c_st[si, h]
            n_ref[si, h:h + 1, :] = n_st[si, h]
            m_vec = jnp.where(lane1 == h, m_h[si, h], m_vec)
        m_ref[si] = m_vec


def _mlstm(proj, nseq, t, conv_w, conv_b, b_gates, a_norm, conv0, c0, n0, m0):
    chunk = A_CHUNK if t % A_CHUNK == 0 else t
    if t > chunk:
        sb, tb = 1, min(t, 4 * chunk)
    else:
        sb, tb = min(nseq, 8), t
    p4 = proj.reshape(proj.shape[0], nseq, t, SLAB)
    slab = lambda s: pl.BlockSpec((None, sb, tb, SLAB), lambda i, j, s=s: (s, i, j, 0))
    full = lambda a: pl.BlockSpec(a.shape, lambda i, j: (0,) * a.ndim)
    st = lambda a: pl.BlockSpec((sb,) + a.shape[1:], lambda i, j: (i,) + (0,) * (a.ndim - 1))
    bg = _pad_cols(b_gates.reshape(1, 2 * A_HEADS), 128)
    conv0p = jnp.pad(conv0, ((0, 0), (8 - (A_CONV - 1), 0), (0, 0)))
    m0p = _pad_cols(m0, 128).reshape(nseq, 1, 128)
    cb = conv_b.reshape(1, -1)
    an = a_norm.reshape(1, -1)
    ya, c, n, m = pl.pallas_call(
        functools.partial(_mlstm_kernel, chunk=chunk),
        grid=(nseq // sb, t // tb),
        in_specs=[slab(0), slab(1), slab(2), slab(3), slab(13),
                  full(conv_w), full(cb), full(bg), full(an),
                  st(conv0p), st(c0), st(n0), st(m0p)],
        out_specs=[pl.BlockSpec((sb, tb, SLAB), lambda i, j: (i, j, 0)), st(c0), st(n0), st(m0p)],
        out_shape=[jax.ShapeDtypeStruct((nseq, t, SLAB), F32),
                   jax.ShapeDtypeStruct(c0.shape, F32), jax.ShapeDtypeStruct(n0.shape, F32),
                   jax.ShapeDtypeStruct(m0p.shape, F32)],
        scratch_shapes=[pltpu.VMEM((sb, 8 + tb, SLAB), F32), pltpu.VMEM((sb, 8 + tb, SLAB), F32)],
        compiler_params=_cparams("parallel", "arbitrary"),
        name="mlstm",
    )(p4, p4, p4, p4, p4, conv_w, cb, bg, an, conv0p, c0, n0, m0p)
    return ya.reshape(nseq * t, SLAB), c, n, m[:, 0, :A_HEADS]


QBLK = 128


def _band_heads(q_m, k2, v2, valid, sinks):
    n = range(len(q_m))
    s = [jnp.where(valid[i], _dot_t(q_m[i], k2[i]), NEG) for i in n]
    m = [jnp.max(s[i], axis=1, keepdims=True) for i in n]
    if sinks is not None:
        m = [jnp.maximum(m[i], sinks[i]) for i in n]
    e = [jnp.exp(s[i] - m[i]) for i in n]
    l = [jnp.sum(e[i], axis=1, keepdims=True) for i in n]
    if sinks is not None:
        l = [l[i] + jnp.exp(sinks[i] - m[i]) for i in n]
    acc = [jnp.dot(e[i].astype(BF16), v2[i], preferred_element_type=F32) for i in n]
    return acc, m, l


def _band_masks(has_prev, window):
    a = lax.broadcasted_iota(jnp.int32, (QBLK, 2 * QBLK), 0)
    c = lax.broadcasted_iota(jnp.int32, (QBLK, 2 * QBLK), 1)
    rel = a - c + QBLK
    band = jnp.logical_and(rel >= 0, rel <= window)
    if has_prev is True:
        return band
    return jnp.logical_and(band, c >= jnp.where(has_prev, 0, QBLK))


def _key_rows(prev_ref, cur_ref, a, ls):
    if prev_ref.shape[0] == cur_ref.shape[0]:
        before = prev_ref[a * QBLK:(a + 1) * QBLK, ls]
    else:
        before = prev_ref[:, ls] if a == 0 else cur_ref[(a - 1) * QBLK:a * QBLK, ls]
    return jnp.concatenate([before, cur_ref[a * QBLK:(a + 1) * QBLK, ls]], axis=0).astype(BF16)


def _dilated_kernel(q_ref, kp_ref, kc_ref, vp_ref, vc_ref, o_ref, lse_ref):
    lane = lax.broadcasted_iota(jnp.int32, (1, 128), 1)
    low = lane < HEAD_DIM
    scale = HEAD_DIM ** -0.5
    chained = kp_ref.shape[0] != kc_ref.shape[0]
    subs = range(q_ref.shape[0] // QBLK)
    pairs = range(B_HEADS // 2)
    rows = lambda a: slice(a * QBLK, (a + 1) * QBLK)
    cols = lambda pr: slice(pr * 128, (pr + 1) * 128)
    valid = [_band_masks(True if (chained and a > 0) else pl.program_id(2) > 0, QBLK) for a in subs]
    qp = {(a, pr): q_ref[rows(a), cols(pr)] * scale for a in subs for pr in pairs}
    k2 = {(a, pr): _key_rows(kp_ref, kc_ref, a, cols(pr)) for a in subs for pr in pairs}
    v2 = {(a, pr): _key_rows(vp_ref, vc_ref, a, cols(pr)) for a in subs for pr in pairs}
    inst = [(a, pr, p) for a in subs for pr in pairs for p in range(2)]
    acc, m, l = _band_heads(
        [jnp.where(low if p == 0 else jnp.logical_not(low), qp[a, pr], 0.0).astype(BF16) for a, pr, p in inst],
        [k2[a, pr] for a, pr, _ in inst], [v2[a, pr] for a, pr, _ in inst], [valid[a] for a, _, _ in inst], None)
    o_h = [acc[i] / l[i] for i in range(len(inst))]
    lse_h = [m[i] + jnp.log(l[i]) for i in range(len(inst))]
    for i in range(0, len(inst), 2):
        a, pr, _ = inst[i]
        o_ref[rows(a), cols(pr)] = jnp.where(low, o_h[i], o_h[i + 1])
        lse_ref[rows(a), cols(pr)] = jnp.where(low, lse_h[i], lse_h[i + 1])


def _dilated_prompt(proj, b, t, g):
    d = B_DILATIONS[g]
    assert t % PERM_TILE == 0
    nb = PERM_TILE // d // QBLK
    per_tile = PERM_TILE // QBLK
    pv = proj.reshape(proj.shape[0], b, t, SLAB)

    def blk(i, r):
        return (i // nb) * per_tile + r * nb + i % nb

    if nb % 2 == 0:
        nq, classes, steps = 2, d, t // d // QBLK // 2
        cur_idx = lambda i, r: blk(2 * i, r) // 2
        prev_rows, prev_idx = QBLK, lambda i, r: blk(jnp.maximum(2 * i - 1, 0), r)
    elif nb == 1 and d % 2 == 0:
        nq, classes, steps = 2, d // 2, t // d // QBLK
        cur_idx = lambda i, r: blk(i, 2 * r) // 2
        prev_rows, prev_idx = 2 * QBLK, lambda i, r: blk(jnp.maximum(i - 1, 0), 2 * r) // 2
    else:
        nq, classes, steps = 1, d, t // d // QBLK
        cur_idx = blk
        prev_rows, prev_idx = QBLK, lambda i, r: blk(jnp.maximum(i - 1, 0), r)
    cur = lambda s: pl.BlockSpec((None, None, nq * QBLK, SLAB), lambda bi, r, i, s=s: (s, bi, cur_idx(i, r), 0))
    prev = lambda s: pl.BlockSpec((None, None, prev_rows, SLAB), lambda bi, r, i, s=s: (s, bi, prev_idx(i, r), 0))
    ospec = pl.BlockSpec((None, nq * QBLK, SLAB), lambda bi, r, i: (bi, cur_idx(i, r), 0))
    sq, sk, sv = 4 + g, 7 + g, 10 + g
    o, lse = pl.pallas_call(
        _dilated_kernel,
        grid=(b, classes, steps),
        in_specs=[cur(sq), prev(sk), cur(sk), prev(sv), cur(sv)],
        out_specs=[ospec, ospec],
        out_shape=[jax.ShapeDtypeStruct((b, t, SLAB), F32)] * 2,
        compiler_params=_cparams("parallel", "parallel", "arbitrary"),
        name=f"dilated_g{g}",
    )(pv, pv, pv, pv, pv)
    return o.reshape(b * t, SLAB), lse.reshape(b * t, SLAB)


def _tails_kernel(x_ref, o_ref, *buf, dil):
    tm = x_ref.shape[0]
    keep = o_ref.shape[1]
    if dil == 1:
        o_ref[...] = x_ref[tm - keep:tm, :].T
    else:
        seg = tm // dil
        for r in range(dil):
            buf[0][pl.ds(r, seg, stride=dil), :] = x_ref[r * seg:(r + 1) * seg, :]
        o_ref[...] = buf[0][tm - keep:tm, :].T


def _cache_tails(proj, nseq, t, g, keep):
    dil = B_DILATIONS[g]
    pv = proj.reshape(proj.shape[0], nseq, t, SLAB)
    last = t // PERM_TILE - 1
    out = pl.pallas_call(
        functools.partial(_tails_kernel, dil=dil),
        grid=(2, nseq, SLAB // 128),
        in_specs=[pl.BlockSpec((None, None, PERM_TILE, 128), lambda kv, b, cb: (7 + g + B_GROUPS * kv, b, last, cb))],
        out_specs=pl.BlockSpec((None, None, 128, keep), lambda kv, b, cb: (kv, b, cb, 0)),
        out_shape=jax.ShapeDtypeStruct((2, nseq, SLAB, keep), F32),
        scratch_shapes=[pltpu.VMEM((PERM_TILE, 128), F32)] if dil > 1 else [],
        compiler_params=_cparams("parallel", "parallel", "parallel"),
        name=f"cache_tails_g{g}",
    )(pv)
    return _rows_major(out[0], B_HEADS), _rows_major(out[1], B_HEADS)


def _merge_kernel(*refs, dils):
    ins, y_ref, bufs = refs[:6], refs[6], refs[7:]
    tm = y_ref.shape[0]
    vals = []
    nbuf = 0
    for k, ref in enumerate(ins):
        dil = dils[k // 2]
        if dil == 1:
            vals.append(ref[...])
        else:
            buf = bufs[nbuf]
            nbuf += 1
            seg = tm // dil
            for r in range(dil):
                buf[pl.ds(r, seg, stride=dil), :] = ref[r * seg:(r + 1) * seg, :]
            vals.append(buf[...])
    o1, l1, o2, l2, o3, l3 = vals
    m = jnp.maximum(jnp.maximum(l1, l2), l3)
    w1, w2, w3 = jnp.exp(l1 - m), jnp.exp(l2 - m), jnp.exp(l3 - m)
    y_ref[...] = (w1 * o1 + w2 * o2 + w3 * o3) / (w1 + w2 + w3)


def _merge_groups(parts, perm):
    n = parts[0].shape[0]
    dils = B_DILATIONS if perm else (1,) * B_GROUPS
    tm = PERM_TILE if perm else min(1024, n)
    spec = pl.BlockSpec((tm, 128), lambda i, j: (i, j))
    nbuf = 2 * sum(d > 1 for d in dils)
    return pl.pallas_call(
        functools.partial(_merge_kernel, dils=dils), grid=(n // tm, SLAB // 128), in_specs=[spec] * 6,
        out_specs=spec, out_shape=jax.ShapeDtypeStruct((n, SLAB), F32),
        scratch_shapes=[pltpu.VMEM((tm, 128), F32)] * nbuf,
        compiler_params=_cparams("parallel", "parallel"), name="merge_groups",
    )(*parts)


def _swa_kernel(q_ref, kp_ref, kc_ref, vp_ref, vc_ref, sink_ref, o_ref):
    lane = lax.broadcasted_iota(jnp.int32, (1, 128), 1)
    scale = HEAD_DIM ** -0.5
    rep = D_HEADS // D_KV_HEADS
    low = lane < HEAD_DIM
    allk = slice(None)
    subs = range(q_ref.shape[0] // QBLK)
    rows = lambda a: slice(a * QBLK, (a + 1) * QBLK)
    cols = lambda pr: slice(pr * 128, (pr + 1) * 128)
    valid = [_band_masks(pl.program_id(1) > 0 if a == 0 else True, D_WINDOW - 1) for a in subs]
    k2 = [_key_rows(kp_ref, kc_ref, a, allk) for a in subs]
    v2 = [_key_rows(vp_ref, vc_ref, a, allk) for a in subs]
    qp = {(a, pr): q_ref[rows(a), cols(pr)] * scale for a in subs for pr in range(rep)}
    inst = [(a, pr, g) for a in subs for pr in range(rep) for g in range(D_KV_HEADS)]
    acc, m, l = _band_heads(
        [jnp.where(low if g == 0 else jnp.logical_not(low), qp[a, pr], 0.0).astype(BF16) for a, pr, g in inst],
        [k2[a] for a, _, _ in inst], [v2[a] for a, _, _ in inst], [valid[a] for a, _, _ in inst],
        [sink_ref[g * rep + pr] for _, pr, g in inst])
    o_h = [acc[i] / l[i] for i in range(len(inst))]
    for i in range(0, len(inst), 2):
        a, pr, _ = inst[i]
        o_ref[rows(a), cols(pr)] = jnp.where(low, o_h[i], o_h[i + 1])


def _swa_prompt(proj, b, t, sinks):
    pv = proj.reshape(proj.shape[0], b, t, SLAB)
    nq = 2 if t % (2 * QBLK) == 0 else 1
    qspec = pl.BlockSpec((None, None, nq * QBLK, SLAB), lambda bi, i: (3, bi, i, 0))

    def kv(col, prev):
        if prev:
            return pl.BlockSpec((None, None, QBLK, 128), lambda bi, i: (4, bi, jnp.maximum(nq * i - 1, 0), col))
        return pl.BlockSpec((None, None, nq * QBLK, 128), lambda bi, i: (4, bi, i, col))

    o = pl.pallas_call(
        _swa_kernel,
        grid=(b, t // QBLK // nq),
        in_specs=[qspec, kv(0, True), kv(0, False), kv(1, True), kv(1, False),
                  pl.BlockSpec(memory_space=pltpu.SMEM)],
        out_specs=pl.BlockSpec((None, nq * QBLK, SLAB), lambda bi, i: (bi, i, 0)),
        out_shape=jax.ShapeDtypeStruct((b, t, SLAB), F32),
        compiler_params=_cparams("parallel", "arbitrary"),
        name="swa_prompt",
    )(pv, pv, pv, pv, pv, sinks)
    return o.reshape(b * t, SLAB)


def _softplus(x):
    return jnp.maximum(x, 0.0) + jnp.log1p(jnp.exp(-jnp.abs(x)))


def _ssd_kernel(z_ref, x_ref, bc_ref, dt_ref, cw_ref, cb_ref, dtb_ref, alog_ref, dl_ref, cn_ref,
                conv0_ref, h0_ref, y_ref, h_ref, xbuf, bcbuf, *, chunk):
    sb, tb, _ = x_ref.shape
    tblk = pl.program_id(1)
    lane1 = lax.broadcasted_iota(jnp.int32, (1, 128), 1)
    low = lane1 < C_HEADDIM
    row = lax.broadcasted_iota(jnp.int32, (chunk, chunk), 0)
    col = lax.broadcasted_iota(jnp.int32, (chunk, chunk), 1)
    causal = col <= row
    tri = causal.astype(BF16)
    srow_low = lax.broadcasted_iota(jnp.int32, (128, 1), 0) < C_HEADDIM
    hpg = C_HEADS // C_NGROUPS
    a_neg = -jnp.exp(alog_ref[...])

    @pl.when(tblk == 0)
    def _():
        h_ref[...] = h0_ref[...]
        for si in range(sb):
            xbuf[si, 0:8, :] = conv0_ref[si, :, 0:SLAB]
            bcbuf[si, 0:8, :] = conv0_ref[si, :, SLAB:2 * SLAB]

    seqs, chunks = range(sb), range(tb // chunk)
    ppg = hpg // 2
    npair = C_HEADS // 2
    rows = lambda c: slice(c * chunk, (c + 1) * chunk)
    cols = lambda pr: slice(pr * 128, (pr + 1) * 128)
    sc = [(si, c) for si in seqs for c in chunks]
    sch = [(si, c, h) for si, c in sc for h in range(C_HEADS)]
    scp = [(si, c, pr) for si, c in sc for pr in range(npair)]
    xc = [_conv_silu(xbuf.at[si], x_ref[si], cw_ref[:, 0:SLAB], cb_ref[:, 0:SLAB]) for si in seqs]
    bcc = [_conv_silu(bcbuf.at[si], bc_ref[si], cw_ref[:, SLAB:2 * SLAB], cb_ref[:, SLAB:2 * SLAB]) for si in seqs]
    dt = {(si, c): _softplus(dt_ref[si, rows(c), :] + dtb_ref[...]) for si, c in sc}
    cum = {k: _cumsum_rows(v * a_neg, tri) for k, v in dt.items()}
    cum_rows = {k: _transpose_slab(v) for k, v in cum.items()}
    dt_rows = {k: _transpose_slab(v) for k, v in dt.items()}
    bgm = {(si, c, g): bcc[si][rows(c), g * 128:(g + 1) * 128].astype(BF16) for si, c in sc
           for g in range(C_NGROUPS)}
    cgm = {(si, c, g): bcc[si][rows(c), 256 + g * 128:256 + (g + 1) * 128].astype(BF16) for si, c in sc
           for g in range(C_NGROUPS)}
    cbm = {k: _dot_t(cgm[k], bgm[k]) for k in bgm}
    x_pair = {(si, c, pr): xc[si][rows(c), cols(pr)] for si, c, pr in scp}
    cum_t = {(si, c, h): cum[si, c][:, h:h + 1] for si, c, h in sch}
    cum_e = {(si, c, h): cum[si, c][chunk - 1:chunk, h:h + 1] for si, c, h in sch}
    seg = {(si, c, h): jnp.where(causal, cum_t[si, c, h] - cum_rows[si, c][h:h + 1, 0:chunk], NEG)
           for si, c, h in sch}
    w = {(si, c, h): cbm[si, c, h // hpg] * jnp.exp(seg[si, c, h]) * dt_rows[si, c][h:h + 1, 0:chunk]
         for si, c, h in sch}
    xhalf = {(si, c, h): jnp.where(low if h % 2 == 0 else jnp.logical_not(low), x_pair[si, c, h // 2],
                                   0.0).astype(BF16) for si, c, h in sch}
    yh = {k: jnp.dot(w[k].astype(BF16), xhalf[k], preferred_element_type=F32) for k in sch}
    e_col = {k: jnp.exp(cum_t[k]) for k in sch}
    w_end = {(si, c, h): jnp.exp(cum_e[si, c, h] - cum_t[si, c, h]) * dt[si, c][:, h:h + 1] for si, c, h in sch}
    dec = {k: jnp.exp(cum_e[k]) for k in sch}
    y_intra = {(si, c, pr): yh[si, c, 2 * pr] + yh[si, c, 2 * pr + 1] + dl_ref[:, cols(pr)] * x_pair[si, c, pr]
               for si, c, pr in scp}
    xw = {(si, c, pr): x_pair[si, c, pr] * jnp.where(low, w_end[si, c, 2 * pr], w_end[si, c, 2 * pr + 1])
          for si, c, pr in scp}
    upd = {(si, c, pr): _tdot(xw[si, c, pr].astype(BF16), bgm[si, c, pr // ppg]) for si, c, pr in scp}
    e_sel = {(si, c, pr): jnp.where(low, e_col[si, c, 2 * pr], e_col[si, c, 2 * pr + 1]) for si, c, pr in scp}
    d_sel = {(si, c, pr): jnp.where(srow_low, dec[si, c, 2 * pr], dec[si, c, 2 * pr + 1]) for si, c, pr in scp}
    gate = {}
    for si, c, pr in scp:
        zz = z_ref[si, rows(c), cols(pr)]
        gate[si, c, pr] = zz * jax.nn.sigmoid(zz)
    hs = {(si, pr): h_ref[si, pr] for si in seqs for pr in range(npair)}
    for c in chunks:
        cur = [(si, c, pr) for si in seqs for pr in range(npair)]
        inter = {k: _dot_t(cgm[k[0], c, k[2] // ppg], hs[k[0], k[2]].astype(BF16)) * e_sel[k] for k in cur}
        for k in cur:
            hs[k[0], k[2]] = d_sel[k] * hs[k[0], k[2]] + upd[k]
        yz = {k: (y_intra[k] + inter[k]) * gate[k] for k in cur}
        sq = {k: jnp.sum(yz[k] * yz[k], axis=-1, keepdims=True) for k in cur}
        for si in seqs:
            for g in range(C_NGROUPS):
                prs = range(g * ppg, (g + 1) * ppg)
                inv = lax.rsqrt(sum(sq[si, c, pr] for pr in prs) / (hpg * C_HEADDIM) + EPS)
                for pr in prs:
                    y_ref[si, rows(c), cols(pr)] = yz[si, c, pr] * inv * cn_ref[:, cols(pr)]
    for si in seqs:
        for pr in range(npair):
            h_ref[si, pr] = hs[si, pr]


def _ssd(proj, nseq, t, conv_w, conv_b, dt_bias, a_log, d_skip, c_norm, conv0, h0):
    chunk = C_CHUNK if t % C_CHUNK == 0 else t
    if t > chunk:
        sb, tb = min(nseq, 2), min(t, 2 * chunk)
    else:
        sb, tb = min(nseq, 8), t
    p4 = proj.reshape(proj.shape[0], nseq, t, SLAB)
    slab = lambda s: pl.BlockSpec((None, sb, tb, SLAB), lambda i, j, s=s: (s, i, j, 0))
    full = lambda a: pl.BlockSpec(a.shape, lambda i, j: (0,) * a.ndim)
    st = lambda a: pl.BlockSpec((sb,) + a.shape[1:], lambda i, j: (i,) + (0,) * (a.ndim - 1))
    conv0p = jnp.pad(conv0, ((0, 0), (8 - (C_CONV - 1), 0), (0, 0)))
    hp = h0.reshape(nseq, C_HEADS // 2, 128, C_DSTATE)
    cb = conv_b.reshape(1, -1)
    dtb = _pad_cols(dt_bias.reshape(1, C_HEADS), 128)
    alog = _pad_cols(a_log.reshape(1, C_HEADS), 128)
    dl = jnp.repeat(d_skip, C_HEADDIM).reshape(1, C_DINNER)
    cn = c_norm.reshape(1, C_DINNER)
    y, h = pl.pallas_call(
        functools.partial(_ssd_kernel, chunk=chunk),
        grid=(nseq // sb, t // tb),
        in_specs=[slab(0), slab(1), slab(2),
                  pl.BlockSpec((None, sb, tb, 128), lambda i, j: (4, i, j, 2)),
                  full(conv_w), full(cb), full(dtb), full(alog), full(dl), full(cn), st(conv0p), st(hp)],
        out_specs=[pl.BlockSpec((sb, tb, SLAB), lambda i, j: (i, j, 0)), st(hp)],
        out_shape=[jax.ShapeDtypeStruct((nseq, t, SLAB), F32), jax.ShapeDtypeStruct(hp.shape, F32)],
        scratch_shapes=[pltpu.VMEM((sb, 8 + tb, SLAB), F32), pltpu.VMEM((sb, 8 + tb, SLAB), F32)],
        compiler_params=_cparams("parallel", "arbitrary"),
        name="ssd",
    )(p4, p4, p4, p4, conv_w, cb, dtb, alog, dl, cn, conv0p, hp)
    return y.reshape(nseq * t, SLAB), h.reshape(nseq, C_HEADS, C_HEADDIM, C_DSTATE)


FFN_CHUNK = 256


def _post_kernel(x_ref, ya_ref, yb_ref, wo_ref, fn_ref, wg_ref, wu_ref, wd_ref, on_ref, o_ref, act_ref, *, final):
    ymix = jnp.concatenate([ya_ref[...], yb_ref[...]], axis=-1).astype(BF16)
    x1 = x_ref[...] + jnp.dot(ymix, wo_ref[...], preferred_element_type=F32)
    ms = jnp.mean(x1 * x1, axis=-1, keepdims=True)
    hn = (x1 * lax.rsqrt(ms + EPS) * fn_ref[...]).astype(BF16)
    hidden = wg_ref.shape[1]
    for c in range(hidden // FFN_CHUNK):
        cs = slice(c * FFN_CHUNK, (c + 1) * FFN_CHUNK)
        gate = jnp.dot(hn, wg_ref[:, cs], preferred_element_type=F32)
        up = jnp.dot(hn, wu_ref[:, cs], preferred_element_type=F32)
        act_ref[:, cs] = (gate * jax.nn.sigmoid(gate) * up).astype(BF16)
    x2 = x1 + jnp.dot(act_ref[...], wd_ref[...], preferred_element_type=F32)
    if final:
        ms2 = jnp.mean(x2 * x2, axis=-1, keepdims=True)
        x2 = x2 * lax.rsqrt(ms2 + EPS) * on_ref[...]
    o_ref[...] = x2


def _post(x, ya, yb, w_out, layer, ffn_norm, w_gate, w_up, w_down, out_norm, final):
    n = x.shape[0]
    tm = min(1024, n)
    hidden = w_gate.shape[2]
    row = lambda w: pl.BlockSpec((tm, w), lambda i: (i, 0))
    res = lambda a: pl.BlockSpec(a.shape, lambda i: (0, 0), pipeline_mode=pl.Buffered(1))
    lay = lambda a: pl.BlockSpec((None,) + a.shape[1:], lambda i: (layer, 0, 0), pipeline_mode=pl.Buffered(1))
    fn = ffn_norm.reshape(-1, 1, D_MODEL)
    on = out_norm.reshape(1, D_MODEL)
    return pl.pallas_call(
        functools.partial(_post_kernel, final=final),
        grid=(n // tm,),
        in_specs=[row(D_MODEL), row(SLAB), row(SLAB), res(w_out), lay(fn), lay(w_gate), lay(w_up), lay(w_down),
                  res(on)],
        out_specs=row(D_MODEL),
        out_shape=jax.ShapeDtypeStruct((n, D_MODEL), F32),
        scratch_shapes=[pltpu.VMEM((tm, hidden), BF16)],
        compiler_params=_cparams("parallel"),
        name="post_ffn",
    )(x, ya, yb, w_out, fn, w_gate, w_up, w_down, on)


RIDER_PHASES = 4


def _rider_mask(ts, r, window, step):
    t = lax.broadcasted_iota(jnp.int32, (2 * ts, r + 128), 0) % ts
    c = lax.broadcasted_iota(jnp.int32, (2 * ts, r + 128), 1)
    rel = r + t - jnp.where(c < 128, c, c - 128 + ts)
    valid = jnp.logical_and(jnp.logical_and(rel >= 0, rel <= window), (rel & (step - 1)) == 0)
    return jnp.logical_and(valid, jnp.logical_or(c < ts, c >= 128))


def _shift_rows_minor(src_ref, new_t, rows, ts):
    r = src_ref.shape[1]
    lane = lax.broadcasted_iota(jnp.int32, (1, 128), 1)
    old = src_ref[rows, :]
    rolled = pltpu.roll(old, r - ts, 1)
    tail = jnp.where(lane < 128 - ts, rolled[:, r - 128:], new_t[rows, :])
    out = tail if r == 128 else jnp.concatenate([rolled[:, :r - 128], tail], axis=1)
    return old, out


def _new_rows_t(new_ref):
    ts = new_ref.shape[0]
    pad = jnp.zeros((128 - ts, SLAB), F32)
    return jnp.concatenate([pad, new_ref[...]], axis=0).T


def _rider_k(q_ref, kn_ref, kc_ref, ko_ref, p_ref, lse_ref, window, step):
    ts = q_ref.shape[0]
    r = kc_ref.shape[1]
    low = lax.broadcasted_iota(jnp.int32, (1, 128), 1) < HEAD_DIM
    valid = _rider_mask(ts, r, window, step)
    kn_t = _new_rows_t(kn_ref)
    scale = HEAD_DIM ** -0.5
    pairs = range(B_HEADS // 2)
    rows = lambda pr: slice(pr * 128, (pr + 1) * 128)
    s = []
    for pr in pairs:
        k_old, k_out = _shift_rows_minor(kc_ref, kn_t, rows(pr), ts)
        ko_ref[rows(pr), :] = k_out
        k_ext = jnp.concatenate([k_old[:, 0:128], k_out], axis=1).astype(BF16)
        qp = q_ref[:, rows(pr)] * scale
        q2 = jnp.concatenate([jnp.where(low, qp, 0.0), jnp.where(low, 0.0, qp)], axis=0).astype(BF16)
        s.append(jnp.where(valid, jnp.dot(q2, k_ext, preferred_element_type=F32), NEG))
    m = [jnp.max(s[pr], axis=1, keepdims=True) for pr in pairs]
    e = [jnp.exp(s[pr] - m[pr]) for pr in pairs]
    l = [jnp.sum(e[pr], axis=1, keepdims=True) for pr in pairs]
    for pr in pairs:
        p_ref[pr * 2 * ts:(pr + 1) * 2 * ts, :] = (e[pr] / l[pr]).astype(BF16)
        lse = m[pr] + jnp.log(l[pr])
        lse_ref[:, rows(pr)] = jnp.where(low, lse[0:ts], lse[ts:2 * ts])


def _rider_v(p_ref, vn_ref, vc_ref, vo_ref, o_ref):
    ts = vn_ref.shape[0]
    low = lax.broadcasted_iota(jnp.int32, (1, 128), 1) < HEAD_DIM
    vn_t = _new_rows_t(vn_ref)
    for pr in range(B_HEADS // 2):
        rows = slice(pr * 128, (pr + 1) * 128)
        v_old, v_out = _shift_rows_minor(vc_ref, vn_t, rows, ts)
        vo_ref[rows, :] = v_out
        v_ext = jnp.concatenate([v_old[:, 0:128], v_out], axis=1).astype(BF16)
        a = _dot_t(p_ref[pr * 2 * ts:(pr + 1) * 2 * ts, :], v_ext)
        o_ref[:, rows] = jnp.where(low, a[0:ts], a[ts:2 * ts])


def _post_rider_kernel(x_ref, ya_ref, yb_ref, wo_ref, fn_ref, wg_ref, wu_ref, wd_ref, on_ref, *rest,
                       final, side, groups):
    ng = len(groups)
    nout = 3 if side == 'k' else 2
    ins, o_ref, outs = rest[:3 * ng], rest[3 * ng], rest[3 * ng + 1:3 * ng + 1 + nout * ng]
    act_ref, hn_ref = rest[3 * ng + 1 + nout * ng:]
    x1_ref = o_ref

    def ride():
        for k, g in enumerate(groups):
            i3, o3 = ins[3 * k:3 * k + 3], outs[nout * k:nout * k + nout]
            if side == 'k':
                _rider_k(*i3, *o3, B_WINDOWS[g], B_DILATIONS[g])
            else:
                _rider_v(*i3, *o3)

    u = pl.program_id(1)
    nchunk = wg_ref.shape[1] // FFN_CHUNK
    per = -(-nchunk // (RIDER_PHASES - 1))
    first = nchunk - per * (RIDER_PHASES - 2)

    def hidden_chunks(lo, hi):
        hn = hn_ref[...]
        for c in range(lo, hi):
            cs = slice(c * FFN_CHUNK, (c + 1) * FFN_CHUNK)
            gate = jnp.dot(hn, wg_ref[:, cs], preferred_element_type=F32)
            up = jnp.dot(hn, wu_ref[:, cs], preferred_element_type=F32)
            act_ref[:, cs] = (gate * jax.nn.sigmoid(gate) * up).astype(BF16)

    def phase(k):
        def body():
            if k == 0:
                ymix = jnp.concatenate([ya_ref[...], yb_ref[...]], axis=-1).astype(BF16)
                x1 = x_ref[...] + jnp.dot(ymix, wo_ref[...], preferred_element_type=F32)
                ms = jnp.mean(x1 * x1, axis=-1, keepdims=True)
                x1_ref[...] = x1
                hn_ref[...] = (x1 * lax.rsqrt(ms + EPS) * fn_ref[...]).astype(BF16)
                hidden_chunks(0, first)
            elif k < RIDER_PHASES - 1:
                hidden_chunks(first + (k - 1) * per, first + k * per)
            else:
                x2 = x1_ref[...] + jnp.dot(act_ref[...], wd_ref[...], preferred_element_type=F32)
                if final:
                    ms2 = jnp.mean(x2 * x2, axis=-1, keepdims=True)
                    x2 = x2 * lax.rsqrt(ms2 + EPS) * on_ref[...]
                o_ref[...] = x2
            ride()
        return body

    for k in range(RIDER_PHASES):
        pl.when(u == k)(phase(k))


def _post_rider(x, ya, yb, w_out, layer, ffn_norm, w_gate, w_up, w_down, out_norm, final, side, groups, proj_s,
                caches_t, probs):
    n = x.shape[0]
    nseq = caches_t[0].shape[0]
    ts = proj_s.shape[1] // nseq
    tm = n * RIDER_PHASES // nseq
    hidden = w_gate.shape[2]
    row = lambda w, **kw: pl.BlockSpec((tm, w), lambda i, u: (i, 0), **kw)
    res = lambda a: pl.BlockSpec(a.shape, lambda i, u: (0, 0), pipeline_mode=pl.Buffered(1))
    lay = lambda a: pl.BlockSpec((None,) + a.shape[1:], lambda i, u: (layer, 0, 0), pipeline_mode=pl.Buffered(1))
    seq = lambda i, u: i * RIDER_PHASES + u
    new = lambda s: pl.BlockSpec((None, ts, SLAB), lambda i, u: (s, seq(i, u), 0))
    rows8 = pl.BlockSpec((ts, SLAB), lambda i, u: (seq(i, u), 0))
    fn = ffn_norm.reshape(-1, 1, D_MODEL)
    on = out_norm.reshape(1, D_MODEL)
    single = dict(pipeline_mode=pl.Buffered(1))
    common = [row(D_MODEL), row(SLAB, **single), row(SLAB, **single), res(w_out), lay(fn), lay(w_gate),
              lay(w_up), lay(w_down), res(on)]
    small = jax.ShapeDtypeStruct((nseq * ts, SLAB), F32)
    ins, args, outs, shapes = [], [], [row(D_MODEL)], [jax.ShapeDtypeStruct((n, D_MODEL), F32)]
    for k, g in enumerate(groups):
        r = caches_t[k].shape[2]
        cspec = pl.BlockSpec((None, SLAB, r), lambda i, u: (seq(i, u), 0, 0))
        pspec = pl.BlockSpec((None, B_HEADS * ts, r + 128), lambda i, u: (seq(i, u), 0, 0))
        cout = jax.ShapeDtypeStruct(caches_t[k].shape, F32)
        if side == 'k':
            ins += [new(4 + g), new(7 + g), cspec]
            args += [proj_s, proj_s, caches_t[k]]
            outs += [cspec, pspec, rows8]
            shapes += [cout, jax.ShapeDtypeStruct((nseq, B_HEADS * ts, r + 128), BF16), small]
        else:
            ins += [pspec, new(10 + g), cspec]
            args += [probs[k], proj_s, caches_t[k]]
            outs += [cspec, rows8]
            shapes += [cout, small]
    res_all = pl.pallas_call(
        functools.partial(_post_rider_kernel, final=final, side=side, groups=tuple(groups)),
        grid=(n // tm, RIDER_PHASES),
        in_specs=common + ins,
        out_specs=outs,
        out_shape=shapes,
        scratch_shapes=[pltpu.VMEM((tm, hidden), BF16), pltpu.VMEM((tm, D_MODEL), BF16)],
        compiler_params=_cparams("parallel", "arbitrary", vmem=VMEM_LIMIT_RIDER),
        name=f"post_ffn_ride_{side}",
    )(x, ya, yb, w_out, fn, w_gate, w_up, w_down, on, *args)
    nout = 3 if side == 'k' else 2
    return res_all[0], [res_all[1 + nout * k:1 + nout * (k + 1)] for k in range(len(groups))]


NEWPAD = 128


def _ext_mask(rows, r, ts, window, step):
    t = lax.broadcasted_iota(jnp.int32, (rows, r + NEWPAD), 0) % ts
    c = lax.broadcasted_iota(jnp.int32, (rows, r + NEWPAD), 1)
    rel = r + t - c
    ok = jnp.logical_and(jnp.logical_and(rel >= 0, rel <= window), c < r + ts)
    return jnp.logical_and(ok, (rel & (step - 1)) == 0)


def _shift_cache(dst_ref, src_ref, new, si, r, ts):
    dst_ref[si, 0:r - ts, :] = src_ref[si, ts:r, :]
    dst_ref[si, r - ts:r, :] = new


def _pad_new(x):
    return jnp.concatenate([x, jnp.zeros((NEWPAD - x.shape[0], x.shape[1]), x.dtype)], axis=0)


def _dil_sample_kernel(q_ref, kn_ref, vn_ref, kc_ref, vc_ref, o_ref, lse_ref, ko_ref, vo_ref,
                       *, window, step):
    sb, ts, _ = q_ref.shape
    r = kc_ref.shape[2]
    scale = HEAD_DIM ** -0.5
    lane = lax.broadcasted_iota(jnp.int32, (1, 128), 1)
    low = lane < HEAD_DIM
    t = lax.broadcasted_iota(jnp.int32, (2 * ts, r + 128), 0) % ts
    c = lax.broadcasted_iota(jnp.int32, (2 * ts, r + 128), 1)
    rel = r + t - jnp.where(c < 128, c, c - 128 + ts)
    valid = jnp.logical_and(jnp.logical_and(rel >= 0, rel <= window), (rel & (step - 1)) == 0)
    valid = jnp.logical_and(valid, jnp.logical_or(c < ts, c >= 128))
    pad = jnp.zeros((128 - ts, SLAB), F32)

    def shifted(src_ref, new_t, rows):
        old = src_ref[rows, :]
        rolled = pltpu.roll(old, r - ts, 1)
        tail = jnp.where(lane < 128 - ts, rolled[:, r - 128:], new_t[rows, :])
        out = tail if r == 128 else jnp.concatenate([rolled[:, :r - 128], tail], axis=1)
        return old, out

    for si, pr in [(si, pr) for si in range(sb) for pr in range(B_HEADS // 2)]:
        if pr == 0:
            kn_t = jnp.concatenate([pad, kn_ref[si]], axis=0).T
            vn_t = jnp.concatenate([pad, vn_ref[si]], axis=0).T
        rows = slice(pr * 128, (pr + 1) * 128)
        k_old, k_out = shifted(kc_ref.at[si], kn_t, rows)
        v_old, v_out = shifted(vc_ref.at[si], vn_t, rows)
        ko_ref[si, rows, :] = k_out
        vo_ref[si, rows, :] = v_out
        k_ext = jnp.concatenate([k_old[:, 0:128], k_out], axis=1).astype(BF16)
        v_ext = jnp.concatenate([v_old[:, 0:128], v_out], axis=1).astype(BF16)
        qp = q_ref[si, :, rows] * scale
        q2 = jnp.concatenate([jnp.where(low, qp, 0.0), jnp.where(low, 0.0, qp)], axis=0).astype(BF16)
        s = jnp.where(valid, jnp.dot(q2, k_ext, preferred_element_type=F32), NEG)
        m = jnp.max(s, axis=1, keepdims=True)
        e = jnp.exp(s - m)
        l = jnp.sum(e, axis=1, keepdims=True)
        a = _dot_t(e.astype(BF16), v_ext) / l
        lse = m + jnp.log(l)
        o_ref[si, :, rows] = jnp.where(low, a[0:ts], a[ts:2 * ts])
        lse_ref[si, :, rows] = jnp.where(low, lse[0:ts], lse[ts:2 * ts])


def _rows_minor(cache):
    nseq, r = cache.shape[:2]
    return jnp.transpose(cache, (0, 2, 3, 1)).reshape(nseq, -1, r)


def _rows_major(cache_t, heads):
    nseq, _, r = cache_t.shape
    return jnp.transpose(cache_t.reshape(nseq, heads, HEAD_DIM, r), (0, 3, 1, 2))


def _dilated_sample(proj, nseq, ts, g, k_cache, v_cache):
    r = k_cache.shape[1]
    assert ts % 8 == 0 and r % 128 == 0 and r >= B_WINDOWS[g]
    kt, vt = _rows_minor(k_cache), _rows_minor(v_cache)
    sb = max(1, min(nseq, 8, 1024 // r))
    p4 = proj.reshape(proj.shape[0], nseq, ts, SLAB)
    new = lambda s: pl.BlockSpec((None, sb, ts, SLAB), lambda i, s=s: (s, i, 0, 0))
    cspec = pl.BlockSpec((sb, SLAB, r), lambda i: (i, 0, 0))
    ospec = pl.BlockSpec((sb, ts, SLAB), lambda i: (i, 0, 0))
    o, lse, kn, vn = pl.pallas_call(
        functools.partial(_dil_sample_kernel, window=B_WINDOWS[g], step=B_DILATIONS[g]),
        grid=(nseq // sb,),
        in_specs=[new(4 + g), new(7 + g), new(10 + g), cspec, cspec],
        out_specs=[ospec, ospec, cspec, cspec],
        out_shape=[jax.ShapeDtypeStruct((nseq, ts, SLAB), F32)] * 2 + [jax.ShapeDtypeStruct(kt.shape, F32)] * 2,
        compiler_params=_cparams("parallel"),
        name=f"dilated_sample_g{g}",
    )(p4, p4, p4, kt, vt)
    return (o.reshape(nseq * ts, SLAB), lse.reshape(nseq * ts, SLAB),
            _rows_major(kn, B_HEADS), _rows_major(vn, B_HEADS))


def _swa_sample_kernel(q_ref, k_ref, v_ref, kc_ref, vc_ref, sink_ref, o_ref, kn_ref, vn_ref):
    sb, ts, _ = q_ref.shape
    r = kc_ref.shape[1]
    rows = D_HEADS * ts
    valid = _ext_mask(rows, r, ts, D_WINDOW - 1, 1)
    lane = lax.broadcasted_iota(jnp.int32, (1, 128), 1)
    low = lane < HEAD_DIM
    rep = D_HEADS // D_KV_HEADS
    scale = HEAD_DIM ** -0.5
    sink = sink_ref[:, 0:1]
    seqs = range(sb)

    def q_rows(si):
        pieces = []
        for pr in range(rep):
            base = q_ref[si, :, pr * 128:(pr + 1) * 128] * scale
            for g in range(D_KV_HEADS):
                pieces.append(jnp.where(low if g == 0 else jnp.logical_not(low), base, 0.0))
        return jnp.concatenate(pieces, axis=0).astype(BF16)

    kn = [k_ref[si] for si in seqs]
    vn = [v_ref[si] for si in seqs]
    qrows = [q_rows(si) for si in seqs]
    kext = [jnp.concatenate([kc_ref[si], _pad_new(kn[si])], axis=0).astype(BF16) for si in seqs]
    vext = [jnp.concatenate([vc_ref[si], _pad_new(vn[si])], axis=0).astype(BF16) for si in seqs]
    s = [jnp.where(valid, _dot_t(qrows[si], kext[si]), NEG) for si in seqs]
    m = [jnp.maximum(jnp.max(s[si], axis=1, keepdims=True), sink) for si in seqs]
    e = [jnp.exp(s[si] - m[si]) for si in seqs]
    l = [jnp.sum(e[si], axis=1, keepdims=True) + jnp.exp(sink - m[si]) for si in seqs]
    o = [jnp.dot(e[si].astype(BF16), vext[si], preferred_element_type=F32) / l[si] for si in seqs]
    for si in seqs:
        pairs = [jnp.where(low, o[si][2 * pr * ts:(2 * pr + 1) * ts], o[si][(2 * pr + 1) * ts:(2 * pr + 2) * ts])
                 for pr in range(rep)]
        o_ref[si] = jnp.concatenate(pairs, axis=1)
        _shift_cache(kn_ref, kc_ref, kn[si], si, r, ts)
        _shift_cache(vn_ref, vc_ref, vn[si], si, r, ts)


def _swa_sample(proj, nseq, ts, sinks, k_cache, v_cache):
    r = k_cache.shape[1]
    sb = min(nseq, 8)
    kvw = D_KV_HEADS * HEAD_DIM
    p4 = proj.reshape(proj.shape[0], nseq, ts, SLAB)
    kc = k_cache.reshape(nseq, r, kvw)
    vc = v_cache.reshape(nseq, r, kvw)
    cspec = pl.BlockSpec((sb, r, kvw), lambda i: (i, 0, 0))
    new = lambda col: pl.BlockSpec((None, sb, ts, kvw), lambda i: (4, i, 0, col))
    rep = D_HEADS // D_KV_HEADS
    by_row_group = sinks.reshape(D_KV_HEADS, rep).T.reshape(-1)
    sink_rows = jnp.broadcast_to(jnp.repeat(by_row_group, ts)[:, None], (D_HEADS * ts, 128))
    o, kn, vn = pl.pallas_call(
        _swa_sample_kernel,
        grid=(nseq // sb,),
        in_specs=[pl.BlockSpec((None, sb, ts, SLAB), lambda i: (3, i, 0, 0)), new(0), new(1), cspec, cspec,
                  pl.BlockSpec(sink_rows.shape, lambda i: (0, 0))],
        out_specs=[pl.BlockSpec((sb, ts, SLAB), lambda i: (i, 0, 0)), cspec, cspec],
        out_shape=[jax.ShapeDtypeStruct((nseq, ts, SLAB), F32)] + [jax.ShapeDtypeStruct(kc.shape, F32)] * 2,
        compiler_params=_cparams("parallel"),
        name="swa_sample",
    )(p4, p4, p4, kc, vc, sink_rows)
    return o.reshape(nseq * ts, SLAB), kn.reshape(k_cache.shape), vn.reshape(v_cache.shape)


def kernel(x_prompt, x_sample, state_a_C, state_a_n, state_a_m, state_a_conv, cache_b_k1, cache_b_v1, cache_b_k2, cache_b_v2, cache_b_k3, cache_b_v3, state_c_conv, state_c_ssm, cache_d_k, cache_d_v, e_norm_mix, e_w_in, e_b_gates, e_a_conv_w, e_a_conv_b, e_a_norm, e_w_out, o_norm_mix, o_w_in, o_c_conv_w, o_c_conv_b, o_c_dt_bias, o_c_A_log, o_c_D, o_c_norm, o_d_sinks, o_w_out, ffn_norm, w_gate, w_up, w_down, final_norm):
    depth = ffn_norm.shape[0]
    even_w = [_even_slabs(e_w_in[j]) for j in range((depth + 1) // 2)]
    odd_w = [_odd_slabs(o_w_in[j]) for j in range(depth // 2)]
    odd_wo = [jnp.concatenate([o_w_out[j][:C_DINNER], _group_aligned(o_w_out[j][C_DINNER:], 0)],
                              axis=0).astype(BF16) for j in range(depth // 2)]
    even_wo = [e_w_out[j].astype(BF16) for j in range((depth + 1) // 2)]
    wg16, wu16, wd16 = w_gate.astype(BF16), w_up.astype(BF16), w_down.astype(BF16)

    def trunk(x, pos, fresh, even_st, odd_st, ride=None, pre=None):
        nseq, t, _ = x.shape
        rode = {}
        assert t >= A_CONV - 1 and t >= C_CONV - 1
        cos_t, sin_t = _rope_tables(pos)
        x2 = x.reshape(nseq * t, D_MODEL)
        new_even, new_odd = [], []

        def tail(proj, s, keep, dil=1, lanes=slice(None)):
            p4 = proj.reshape(proj.shape[0], nseq, t, SLAB)
            if dil == 1:
                return p4[s, :, t - keep:, lanes]
            seg = PERM_TILE // dil
            last = p4[s, :, t - PERM_TILE:, lanes].reshape(nseq, dil, seg, -1)[:, :, seg - keep // dil:]
            return jnp.swapaxes(last, 1, 2).reshape(nseq, keep, -1)

        for layer in range(depth):
            j = layer // 2
            last = layer == depth - 1
            if layer % 2 == 0:
                if pre is not None and layer == 0:
                    pe = pre['proj']
                else:
                    pe = _norm_proj(x2, e_norm_mix[j], even_w[j], cos_t, sin_t, _even_kinds(fresh))
                st = {k: v[j] for k, v in even_st.items()}
                ya, c_new, n_new, m_new = _mlstm(pe, nseq, t, e_a_conv_w[j], e_a_conv_b[j], e_b_gates[j],
                                                 e_a_norm[j], st['a_conv'], st['a_C'], st['a_n'], st['a_m'])
                ns = {'a_C': c_new, 'a_n': n_new, 'a_m': m_new,
                      'a_conv': jnp.concatenate([tail(pe, 0, A_CONV - 1), tail(pe, 1, A_CONV - 1)], axis=-1)}
                parts = []
                for g in range(B_GROUPS):
                    kn, vn = f'b_k{g + 1}', f'b_v{g + 1}'
                    if fresh:
                        o_g, lse_g = _dilated_prompt(pe, nseq, t, g)
                        keep = min(B_WINDOWS[g], t)
                        if keep % 128 == 0 and keep <= PERM_TILE:
                            ns[kn], ns[vn] = _cache_tails(pe, nseq, t, g, keep)
                        else:
                            ns[kn] = tail(pe, 7 + g, keep, B_DILATIONS[g]).reshape(nseq, keep, B_HEADS, HEAD_DIM)
                            ns[vn] = tail(pe, 10 + g, keep, B_DILATIONS[g]).reshape(nseq, keep, B_HEADS, HEAD_DIM)
                    elif pre is not None and layer == 0 and g in pre['groups']:
                        o_g, lse_g, ns[kn], ns[vn] = pre[g]
                    else:
                        o_g, lse_g, ns[kn], ns[vn] = _dilated_sample(pe, nseq, t, g, st[kn], st[vn])
                    parts += [o_g, lse_g]
                yb = _merge_groups(parts, fresh)
                if ride is not None and layer == 0:
                    x2, rode['k'] = _post_rider(
                        x2, ya, yb, even_wo[j], layer, ffn_norm, wg16, wu16, wd16, final_norm, last, 'k',
                        ride['groups'], ride['proj'], ride['kt'], None)
                else:
                    x2 = _post(x2, ya, yb, even_wo[j], layer, ffn_norm, wg16, wu16, wd16, final_norm, last)
                new_even.append(ns)
            else:
                po = _norm_proj(x2, o_norm_mix[j], odd_w[j], cos_t, sin_t, ODD_KINDS)
                st = {k: v[j] for k, v in odd_st.items()}
                yc, ssm_new = _ssd(po, nseq, t, o_c_conv_w[j], o_c_conv_b[j], o_c_dt_bias[j], o_c_A_log[j],
                                   o_c_D[j], o_c_norm[j], st['c_conv'], st['c_ssm'])
                ns = {'c_conv': jnp.concatenate([tail(po, 1, C_CONV - 1), tail(po, 2, C_CONV - 1)], axis=-1),
                      'c_ssm': ssm_new}
                kvw = D_KV_HEADS * HEAD_DIM
                if fresh:
                    yd = _swa_prompt(po, nseq, t, o_d_sinks[j])
                    keep = min(D_WINDOW, t)
                    ns['d_k'] = tail(po, 4, keep, lanes=slice(0, kvw)).reshape(nseq, keep, D_KV_HEADS, HEAD_DIM)
                    ns['d_v'] = tail(po, 4, keep, lanes=slice(kvw, 2 * kvw)).reshape(
                        nseq, keep, D_KV_HEADS, HEAD_DIM)
                else:
                    yd, ns['d_k'], ns['d_v'] = _swa_sample(po, nseq, t, o_d_sinks[j], st['d_k'], st['d_v'])
                if ride is not None and layer == 1:
                    x2, rode['v'] = _post_rider(
                        x2, yc, yd, odd_wo[j], layer, ffn_norm, wg16, wu16, wd16, final_norm, last, 'v',
                        ride['groups'], ride['proj'], ride['vt'], [kpl[1] for kpl in rode['k']])
                else:
                    x2 = _post(x2, yc, yd, odd_wo[j], layer, ffn_norm, wg16, wu16, wd16, final_norm, last)
                new_odd.append(ns)
        out = {k: jnp.stack([ns[k] for ns in new_even]) for k in new_even[0]}
        out.update({k: jnp.stack([ns[k] for ns in new_odd]) for k in new_odd[0]})
        return x2.reshape(nseq, t, D_MODEL), out, rode

    bp, tp, _ = x_prompt.shape
    bs, ts, _ = x_sample.shape
    n_even, n_odd = (depth + 1) // 2, depth // 2
    zeros = lambda *shape: jnp.zeros(shape, F32)
    even_p = {'a_C': zeros(n_even, bp, A_HEADS, A_DK, A_DV), 'a_n': zeros(n_even, bp, A_HEADS, A_DK),
              'a_m': zeros(n_even, bp, A_HEADS), 'a_conv': zeros(n_even, bp, A_CONV - 1, 2 * SLAB)}
    odd_p = {'c_conv': zeros(n_odd, bp, C_CONV - 1, 2 * SLAB),
             'c_ssm': zeros(n_odd, bp, C_HEADS, C_HEADDIM, C_DSTATE)}
    even_s = {'a_C': state_a_C, 'a_n': state_a_n, 'a_m': state_a_m, 'a_conv': state_a_conv,
              'b_k1': cache_b_k1, 'b_v1': cache_b_v1, 'b_k2': cache_b_k2, 'b_v2': cache_b_v2,
              'b_k3': cache_b_k3, 'b_v3': cache_b_v3}
    odd_s = {'c_conv': state_c_conv, 'c_ssm': state_c_ssm, 'd_k': cache_d_k, 'd_v': cache_d_v}
    pos_p = jnp.arange(tp, dtype=jnp.int32)
    pos_s = jnp.tile(PAST_LEN + jnp.arange(ts, dtype=jnp.int32), bs)

    groups = tuple(range(B_GROUPS))
    k_caches = [even_s[f'b_k{g + 1}'][0] for g in groups]
    v_caches = [even_s[f'b_v{g + 1}'][0] for g in groups]
    tile = bp * tp * RIDER_PHASES // bs
    can_ride = (depth >= 2 and bp * tp * RIDER_PHASES % bs == 0 and tile % 8 == 0 and 8 <= tile <= 1024
                and ts == 8 and all(c.shape[1] % 128 == 0 and c.shape[1] >= B_WINDOWS[g]
                                    for g, c in zip(groups, k_caches)))
    if can_ride:
        cos_s, sin_s = _rope_tables(pos_s)
        pe_s = _norm_proj(x_sample.reshape(bs * ts, D_MODEL), e_norm_mix[0], even_w[0], cos_s, sin_s,
                          _even_kinds(False))
        ride = {'groups': groups, 'proj': pe_s, 'kt': [_rows_minor(c) for c in k_caches],
                'vt': [_rows_minor(c) for c in v_caches]}
        y_p, pn, rode = trunk(x_prompt, pos_p, True, even_p, odd_p, ride=ride)
        pre = {'groups': groups, 'proj': pe_s}
        for k, g in enumerate(groups):
            (k_out, _, lse_g), (v_out, o_g) = rode['k'][k], rode['v'][k]
            pre[g] = (o_g, lse_g, _rows_major(k_out, B_HEADS), _rows_major(v_out, B_HEADS))
        y_s, sn, _ = trunk(x_sample, pos_s, False, even_s, odd_s, pre=pre)
    else:
        y_p, pn, _ = trunk(x_prompt, pos_p, True, even_p, odd_p)
        y_s, sn, _ = trunk(x_sample, pos_s, False, even_s, odd_s)

    keys = ('a_C', 'a_n', 'a_m', 'a_conv', 'b_k1', 'b_v1', 'b_k2', 'b_v2', 'b_k3', 'b_v3',
            'c_conv', 'c_ssm', 'd_k', 'd_v')
    states = []
    for k in keys:
        states += [pn[k], sn[k]]
    return (y_p, y_s, *states)
```

```python
import functools
import math

import jax
import jax.numpy as jnp
from jax import lax
from jax.experimental import pallas as pl
from jax.experimental.pallas import tpu as pltpu

F32 = jnp.float32
BF16 = jnp.bfloat16

D_MODEL = 1024
PAST_LEN = 16384
EPS = 1e-6
ROPE_THETA = 10000.0
HEAD_DIM = 64
SLAB = 512

A_HEADS = 4
A_DK = 128
A_DV = 128
A_CONV = 4
A_CHUNK = 64

B_HEADS = 8
B_WINDOWS = (128, 512, 2048)
B_DILATIONS = (1, 4, 16)
B_GROUPS = 3

C_HEADS = 8
C_HEADDIM = 64
C_DINNER = C_HEADS * C_HEADDIM
C_DSTATE = 128
C_NGROUPS = 2
C_CONV = 4
C_CHUNK = 128

D_HEADS = 8
D_KV_HEADS = 2
D_WINDOW = 128

NEG = -1e30
VMEM_LIMIT = 56 * 1024 * 1024


VMEM_LIMIT_RIDER = 60 * 1024 * 1024


def _cparams(*sem, vmem=VMEM_LIMIT):
    return pltpu.CompilerParams(dimension_semantics=sem, vmem_limit_bytes=vmem)


def _rot_half(y):
    lane = lax.broadcasted_iota(jnp.int32, y.shape, 1)
    fwd = pltpu.roll(y, SLAB - HEAD_DIM // 2, 1)
    bwd = pltpu.roll(y, HEAD_DIM // 2, 1)
    return jnp.where(lane % HEAD_DIM < HEAD_DIM // 2, fwd, bwd)


PERM_TILE = 2048
PROJ_ROWS = 512
PERM_STRIDE = 4


def _proj_kernel(x_ref, nw_ref, w_ref, cos_ref, sin_ref, o_ref, xn_ref, *ybuf, kinds):
    j = pl.program_id(1)
    tm = o_ref.shape[0]

    @pl.when(j == 0)
    def _():
        x = x_ref[...]
        ms = jnp.mean(x * x, axis=-1, keepdims=True)
        xn_ref[...] = (x * lax.rsqrt(ms + EPS) * nw_ref[...]).astype(BF16)

    rc = min(tm, PROJ_ROWS)

    def emit(rope_lanes, dil):
        seg = tm // dil
        cseg = rc // dil
        for c in range(tm // rc):
            rows = slice(c * rc, (c + 1) * rc)
            y = jnp.dot(xn_ref[rows, :], w_ref[...], preferred_element_type=F32)
            val = y
            if rope_lanes:
                cos = jnp.tile(cos_ref[rows, :], (1, SLAB // 128))
                sin = jnp.tile(sin_ref[rows, :], (1, SLAB // 128))
                val = y * cos + _rot_half(y) * sin
                if rope_lanes < SLAB:
                    lane = lax.broadcasted_iota(jnp.int32, y.shape, 1)
                    val = jnp.where(lane < rope_lanes, val, y)
            if dil == 1:
                o_ref[rows, :] = val
            else:
                for cb in range(SLAB // 128):
                    ybuf[0][cb, rows, :] = val[:, cb * 128:(cb + 1) * 128]
                if dil <= PERM_STRIDE:
                    for r in range(dil):
                        dst = slice(r * seg + c * cseg, r * seg + (c + 1) * cseg)
                        for cb in range(SLAB // 128):
                            o_ref[dst, cb * 128:(cb + 1) * 128] = ybuf[0][cb, pl.ds(c * rc + r, cseg, stride=dil), :]
                else:
                    d2 = dil // PERM_STRIDE
                    half = rc // PERM_STRIDE
                    for cb in range(SLAB // 128):
                        for r0 in range(PERM_STRIDE):
                            ybuf[1][cb, r0 * half:(r0 + 1) * half, :] = \
                                ybuf[0][cb, pl.ds(c * rc + r0, half, stride=PERM_STRIDE), :]
                    for r0 in range(PERM_STRIDE):
                        for r1 in range(d2):
                            r = r1 * PERM_STRIDE + r0
                            dst = slice(r * seg + c * cseg, r * seg + (c + 1) * cseg)
                            for cb in range(SLAB // 128):
                                o_ref[dst, cb * 128:(cb + 1) * 128] = \
                                    ybuf[1][cb, pl.ds(r0 * half + r1, cseg, stride=d2), :]

    for kind in sorted(set(kinds)):
        member = functools.reduce(jnp.logical_or, [j == s for s, k in enumerate(kinds) if k == kind])
        pl.when(member)(functools.partial(emit, *kind))


def _norm_proj(x, norm_w, w_slabs, cos_t, sin_t, kinds):
    n = x.shape[0]
    s = w_slabs.shape[0]
    p = cos_t.shape[0]
    perm = any(d > 1 for _, d in kinds)
    big = perm or (n % PERM_TILE == 0 and p % PERM_TILE == 0)
    tm = PERM_TILE if big else min(1024, n, p)
    assert n % tm == 0 and p % tm == 0 and len(kinds) == s
    npt = p // tm
    scratch = [pltpu.VMEM((tm, D_MODEL), BF16)]
    if perm:
        scratch.append(pltpu.VMEM((SLAB // 128, tm, 128), F32))
    if any(d > PERM_STRIDE for _, d in kinds):
        scratch.append(pltpu.VMEM((SLAB // 128, min(tm, PROJ_ROWS), 128), F32))
    return pl.pallas_call(
        functools.partial(_proj_kernel, kinds=tuple(kinds)),
        grid=(n // tm, s),
        in_specs=[
            pl.BlockSpec((tm, D_MODEL), lambda i, j: (i, 0)),
            pl.BlockSpec((1, D_MODEL), lambda i, j: (0, 0)),
            pl.BlockSpec((None, D_MODEL, SLAB), lambda i, j: (j, 0, 0)),
            pl.BlockSpec((tm, 128), lambda i, j: (i % npt, 0)),
            pl.BlockSpec((tm, 128), lambda i, j: (i % npt, 0)),
        ],
        out_specs=pl.BlockSpec((None, tm, SLAB), lambda i, j: (j, i, 0)),
        out_shape=jax.ShapeDtypeStruct((s, n, SLAB), F32),
        scratch_shapes=scratch,
        compiler_params=_cparams("parallel", "arbitrary"),
        name="norm_proj",
    )(x, norm_w.reshape(1, D_MODEL), w_slabs, cos_t, sin_t)


def _even_kinds(perm):
    dil = B_DILATIONS if perm else (1,) * B_GROUPS
    return ([(0, 1)] * 4 + [(SLAB, d) for d in dil] * 2 + [(0, d) for d in dil] + [(0, 1)])


ODD_KINDS = [(0, 1)] * 3 + [(SLAB, 1), (D_KV_HEADS * HEAD_DIM, 1)]


def _rope_tables(pos):
    half = HEAD_DIM // 2
    inv = ROPE_THETA ** (-jnp.arange(half, dtype=F32) / half)
    ang = pos.astype(F32)[:, None] * inv[None, :]
    cos, sin = jnp.cos(ang), jnp.sin(ang)
    return (jnp.concatenate([cos, cos, cos, cos], axis=-1),
            jnp.concatenate([-sin, sin, -sin, sin], axis=-1))


def _pad_cols(w, width):
    return jnp.pad(w, ((0, 0), (0, width - w.shape[1])))


def _even_slabs(w_in):
    main = w_in[:, :2048].reshape(D_MODEL, 4, SLAB)
    gates = _pad_cols(w_in[:, 2048:2056], SLAB)[:, None, :]
    qkv = w_in[:, 2056:].reshape(D_MODEL, 9, SLAB)
    return jnp.moveaxis(jnp.concatenate([main, qkv, gates], axis=1), 1, 0).astype(BF16)


def _group_aligned(w, axis):
    rep = D_HEADS // D_KV_HEADS
    shape = w.shape
    split = shape[:axis] + (D_KV_HEADS, rep, HEAD_DIM) + shape[axis + 1:]
    return jnp.swapaxes(w.reshape(split), axis, axis + 1).reshape(shape)


def _odd_slabs(w_in):
    main = w_in[:, :1536].reshape(D_MODEL, 3, SLAB)
    qd = _group_aligned(w_in[:, 1544:2056], 1)[:, None, :]
    kvd = _pad_cols(jnp.concatenate([w_in[:, 2056:2312], w_in[:, 1536:1544]], axis=1), SLAB)[:, None, :]
    return jnp.moveaxis(jnp.concatenate([main, qd, kvd], axis=1), 1, 0).astype(BF16)


def _split3(x):
    hi = x.astype(BF16)
    r1 = x - hi.astype(F32)
    mid = r1.astype(BF16)
    lo = (r1 - mid.astype(F32)).astype(BF16)
    return hi, mid, lo


def _cumsum_rows(x, tri):
    hi, mid, lo = _split3(x)
    acc = jnp.dot(tri, lo, preferred_element_type=F32)
    acc = acc + jnp.dot(tri, mid, preferred_element_type=F32)
    return acc + jnp.dot(tri, hi, preferred_element_type=F32)


def _transpose_slab(s):
    l = s.shape[0]
    if l < 128:
        s = jnp.concatenate([s, jnp.zeros((128 - l, 128), s.dtype)], axis=0)
    return s.T


def _conv_silu(buf_ref, x, w, b):
    tb = x.shape[0]
    buf_ref[8:8 + tb, :] = x
    y = b + w[3:4] * x
    for j in range(3):
        y = y + w[j:j + 1] * buf_ref[5 + j:5 + j + tb, :]
    buf_ref[0:8, :] = buf_ref[tb:tb + 8, :]
    return y * jax.nn.sigmoid(y)


def _log_sigmoid(x):
    return jnp.minimum(x, 0.0) - jnp.log1p(jnp.exp(-jnp.abs(x)))


def _dot_t(a, b):
    return lax.dot_general(a, b, (((1,), (1,)), ((), ())), preferred_element_type=F32)


def _tdot(a, b):
    return lax.dot_general(a, b, (((0,), (0,)), ((), ())), preferred_element_type=F32)


def _mlstm_kernel(q_ref, k_ref, v_ref, og_ref, g_ref, cw_ref, cb_ref, bg_ref, an_ref,
                  conv0_ref, c0_ref, n0_ref, m0_ref,
                  ya_ref, c_ref, n_ref, m_ref, qbuf, kbuf, *, chunk):
    sb, tb, _ = q_ref.shape
    tblk = pl.program_id(1)
    lane = lax.broadcasted_iota(jnp.int32, (chunk, 128), 1)
    lane1 = lax.broadcasted_iota(jnp.int32, (1, 128), 1)
    row = lax.broadcasted_iota(jnp.int32, (chunk, chunk), 0)
    col = lax.broadcasted_iota(jnp.int32, (chunk, chunk), 1)
    causal = col <= row
    tri = causal.astype(BF16)
    scale = A_DK ** -0.5

    @pl.when(tblk == 0)
    def _():
        c_ref[...] = c0_ref[...]
        n_ref[...] = n0_ref[...]
        m_ref[...] = m0_ref[...]
        for si in range(sb):
            qbuf[si, 0:8, :] = conv0_ref[si, :, 0:SLAB]
            kbuf[si, 0:8, :] = conv0_ref[si, :, SLAB:2 * SLAB]

    nchunk = tb // chunk
    seqs, chunks, heads = range(sb), range(nchunk), range(A_HEADS)
    rows = lambda c: slice(c * chunk, (c + 1) * chunk)
    cols = lambda h: slice(h * 128, (h + 1) * 128)
    inst = [(si, c, h) for si in seqs for c in chunks for h in heads]
    qc = [_conv_silu(qbuf.at[si], q_ref[si], cw_ref[:, 0:SLAB], cb_ref[:, 0:SLAB]) for si in seqs]
    kc = [_conv_silu(kbuf.at[si], k_ref[si], cw_ref[:, SLAB:2 * SLAB], cb_ref[:, SLAB:2 * SLAB]) * scale
          for si in seqs]
    gates = {(si, c): g_ref[si, rows(c), 0:128] + bg_ref[...] for si in seqs for c in chunks}
    bc = {k: _cumsum_rows(_log_sigmoid(g), tri) for k, g in gates.items()}
    st = {k: _transpose_slab(jnp.where(lane < A_HEADS, gates[k], bc[k])) for k in gates}
    qh = {(si, c, h): qc[si][rows(c), cols(h)] for si, c, h in inst}
    kh = {(si, c, h): kc[si][rows(c), cols(h)] for si, c, h in inst}
    qb = {k: v.astype(BF16) for k, v in qh.items()}
    vb = {(si, c, h): v_ref[si, rows(c), cols(h)].astype(BF16) for si, c, h in inst}
    b_t = {(si, c, h): bc[si, c][:, A_HEADS + h:A_HEADS + h + 1] for si, c, h in inst}
    dlog = {(si, c, h): jnp.where(causal, b_t[si, c, h] - st[si, c][A_HEADS + h:A_HEADS + h + 1, 0:chunk]
                                  + st[si, c][h:h + 1, 0:chunk], NEG) for si, c, h in inst}
    m_loc = {k: jnp.max(v, axis=1, keepdims=True) for k, v in dlog.items()}
    qk = {k: _dot_t(qb[k], kh[k].astype(BF16)) for k in inst}
    s_loc = {k: qk[k] * jnp.exp(dlog[k] - m_loc[k]) for k in inst}
    sv = {k: jnp.dot(s_loc[k].astype(BF16), vb[k], preferred_element_type=F32) for k in inst}
    rs = {k: jnp.sum(s_loc[k], axis=1, keepdims=True) for k in inst}
    b_end = {k: b_t[k][chunk - 1:chunk, :] for k in inst}
    g_col = {(si, c, h): b_end[si, c, h] - b_t[si, c, h] + gates[si, c][:, h:h + 1] for si, c, h in inst}
    g_max = {k: jnp.max(v, axis=0, keepdims=True) for k, v in g_col.items()}
    kw = {k: kh[k] * jnp.exp(g_col[k] - g_max[k]) for k in inst}
    u = {k: _tdot(kw[k].astype(BF16), vb[k]) for k in inst}
    nk = {k: jnp.sum(kw[k], axis=0, keepdims=True) for k in inst}
    c_st = {(si, h): c_ref[si, h] for si in seqs for h in heads}
    n_st = {(si, h): n_ref[si, h:h + 1, :] for si in seqs for h in heads}
    m_h = {(si, h): m_ref[si][:, h:h + 1] for si in seqs for h in heads}
    for c in chunks:
        cur = [(si, c, h) for si in seqs for h in heads]
        inter = {k: b_t[k] + m_h[k[0], k[2]] for k in cur}
        m_t = {k: jnp.maximum(inter[k], m_loc[k]) for k in cur}
        w_loc = {k: jnp.exp(m_loc[k] - m_t[k]) for k in cur}
        w_inter = {k: jnp.exp(inter[k] - m_t[k]) for k in cur}
        qc_st = {k: jnp.dot(qb[k], c_st[k[0], k[2]].astype(BF16), preferred_element_type=F32) for k in cur}
        qn = {k: jnp.sum(qh[k] * n_st[k[0], k[2]], axis=1, keepdims=True) for k in cur}
        num = {k: w_loc[k] * sv[k] + w_inter[k] * qc_st[k] for k in cur}
        den = {k: w_loc[k] * rs[k] + w_inter[k] * qn[k] for k in cur}
        hh = {k: num[k] / jnp.maximum(jnp.abs(den[k]), jnp.exp(-m_t[k])) for k in cur}
        m_new = {k: jnp.maximum(b_end[k] + m_h[k[0], k[2]], g_max[k]) for k in cur}
        decay = {k: jnp.exp(b_end[k] + m_h[k[0], k[2]] - m_new[k]) for k in cur}
        gain = {k: jnp.exp(g_max[k] - m_new[k]) for k in cur}
        for k in cur:
            sh = (k[0], k[2])
            c_st[sh] = decay[k] * c_st[sh] + gain[k] * u[k]
            n_st[sh] = decay[k] * n_st[sh] + gain[k] * nk[k]
            m_h[sh] = m_new[k]
        ms = {k: jnp.mean(hh[k] * hh[k], axis=-1, keepdims=True) for k in cur}
        for si, _, h in cur:
            hn = hh[si, c, h] * lax.rsqrt(ms[si, c, h] + EPS) * an_ref[:, cols(h)]
            ya_ref[si, rows(c), cols(h)] = hn * jax.nn.sigmoid(og_ref[si, rows(c), cols(h)])
    for si in seqs:
        m_vec = m_ref[si]
        for h in heads:
            c_ref[si, h] = c_st[si, h]
            n_ref[si, h:h + 1, :] = n_st[si, h]
            m_vec = jnp.where(lane1 == h, m_h[si, h], m_vec)
        m_ref[si] = m_vec


def _mlstm(proj, nseq, t, conv_w, conv_b, b_gates, a_norm, conv0, c0, n0, m0):
    chunk = A_CHUNK if t % A_CHUNK == 0 else t
    if t > chunk:
        sb, tb = min(nseq, 2), min(t, 8 * chunk)
    else:
        sb, tb = min(nseq, 8), t
    p4 = proj.reshape(proj.shape[0], nseq, t, SLAB)
    slab = lambda s: pl.BlockSpec((None, sb, tb, SLAB), lambda i, j, s=s: (s, i, j, 0))
    full = lambda a: pl.BlockSpec(a.shape, lambda i, j: (0,) * a.ndim)
    st = lambda a: pl.BlockSpec((sb,) + a.shape[1:], lambda i, j: (i,) + (0,) * (a.ndim - 1))
    bg = _pad_cols(b_gates.reshape(1, 2 * A_HEADS), 128)
    conv0p = jnp.pad(conv0, ((0, 0), (8 - (A_CONV - 1), 0), (0, 0)))
    m0p = _pad_cols(m0, 128).reshape(nseq, 1, 128)
    cb = conv_b.reshape(1, -1)
    an = a_norm.reshape(1, -1)
    ya, c, n, m = pl.pallas_call(
        functools.partial(_mlstm_kernel, chunk=chunk),
        grid=(nseq // sb, t // tb),
        in_specs=[slab(0), slab(1), slab(2), slab(3), slab(13),
                  full(conv_w), full(cb), full(bg), full(an),
                  st(conv0p), st(c0), st(n0), st(m0p)],
        out_specs=[pl.BlockSpec((sb, tb, SLAB), lambda i, j: (i, j, 0)), st(c0), st(n0), st(m0p)],
        out_shape=[jax.ShapeDtypeStruct((nseq, t, SLAB), F32),
                   jax.ShapeDtypeStruct(c0.shape, F32), jax.ShapeDtypeStruct(n0.shape, F32),
                   jax.ShapeDtypeStruct(m0p.shape, F32)],
        scratch_shapes=[pltpu.VMEM((sb, 8 + tb, SLAB), F32), pltpu.VMEM((sb, 8 + tb, SLAB), F32)],
        compiler_params=_cparams("parallel", "arbitrary"),
        name="mlstm",
    )(p4, p4, p4, p4, p4, conv_w, cb, bg, an, conv0p, c0, n0, m0p)
    return ya.reshape(nseq * t, SLAB), c, n, m[:, 0, :A_HEADS]


QBLK = 128


def _band_heads(q_m, k2, v2, valid, sinks):
    n = range(len(q_m))
    s = [jnp.where(valid[i], _dot_t(q_m[i], k2[i]), NEG) for i in n]
    m = [jnp.max(s[i], axis=1, keepdims=True) for i in n]
    if sinks is not None:
        m = [jnp.maximum(m[i], sinks[i]) for i in n]
    e = [jnp.exp(s[i] - m[i]) for i in n]
    l = [jnp.sum(e[i], axis=1, keepdims=True) for i in n]
    if sinks is not None:
        l = [l[i] + jnp.exp(sinks[i] - m[i]) for i in n]
    acc = [jnp.dot(e[i].astype(BF16), v2[i], preferred_element_type=F32) for i in n]
    return acc, m, l


def _band_masks(has_prev, window):
    a = lax.broadcasted_iota(jnp.int32, (QBLK, 2 * QBLK), 0)
    c = lax.broadcasted_iota(jnp.int32, (QBLK, 2 * QBLK), 1)
    rel = a - c + QBLK
    band = jnp.logical_and(rel >= 0, rel <= window)
    if has_prev is True:
        return band
    return jnp.logical_and(band, c >= jnp.where(has_prev, 0, QBLK))


def _key_rows(prev_ref, cur_ref, a, ls):
    if prev_ref.shape[0] == cur_ref.shape[0]:
        before = prev_ref[a * QBLK:(a + 1) * QBLK, ls]
    else:
        before = prev_ref[:, ls] if a == 0 else cur_ref[(a - 1) * QBLK:a * QBLK, ls]
    return jnp.concatenate([before, cur_ref[a * QBLK:(a + 1) * QBLK, ls]], axis=0).astype(BF16)


def _dilated_kernel(q_ref, kp_ref, kc_ref, vp_ref, vc_ref, o_ref, lse_ref):
    lane = lax.broadcasted_iota(jnp.int32, (1, 128), 1)
    low = lane < HEAD_DIM
    scale = HEAD_DIM ** -0.5
    chained = kp_ref.shape[0] != kc_ref.shape[0]
    subs = range(q_ref.shape[0] // QBLK)
    pairs = range(B_HEADS // 2)
    rows = lambda a: slice(a * QBLK, (a + 1) * QBLK)
    cols = lambda pr: slice(pr * 128, (pr + 1) * 128)
    valid = [_band_masks(True if (chained and a > 0) else pl.program_id(2) > 0, QBLK) for a in subs]
    qp = {(a, pr): q_ref[rows(a), cols(pr)] * scale for a in subs for pr in pairs}
    k2 = {(a, pr): _key_rows(kp_ref, kc_ref, a, cols(pr)) for a in subs for pr in pairs}
    v2 = {(a, pr): _key_rows(vp_ref, vc_ref, a, cols(pr)) for a in subs for pr in pairs}
    inst = [(a, pr, p) for a in subs for pr in pairs for p in range(2)]
    acc, m, l = _band_heads(
        [jnp.where(low if p == 0 else jnp.logical_not(low), qp[a, pr], 0.0).astype(BF16) for a, pr, p in inst],
        [k2[a, pr] for a, pr, _ in inst], [v2[a, pr] for a, pr, _ in inst], [valid[a] for a, _, _ in inst], None)
    o_h = [acc[i] / l[i] for i in range(len(inst))]
    lse_h = [m[i] + jnp.log(l[i]) for i in range(len(inst))]
    for i in range(0, len(inst), 2):
        a, pr, _ = inst[i]
        o_ref[rows(a), cols(pr)] = jnp.where(low, o_h[i], o_h[i + 1])
        lse_ref[rows(a), cols(pr)] = jnp.where(low, lse_h[i], lse_h[i + 1])


def _dilated_prompt(proj, b, t, g):
    d = B_DILATIONS[g]
    assert t % PERM_TILE == 0
    nb = PERM_TILE // d // QBLK
    per_tile = PERM_TILE // QBLK
    pv = proj.reshape(proj.shape[0], b, t, SLAB)

    def blk(i, r):
        return (i // nb) * per_tile + r * nb + i % nb

    if nb % 2 == 0:
        nq, classes, steps = 2, d, t // d // QBLK // 2
        cur_idx = lambda i, r: blk(2 * i, r) // 2
        prev_rows, prev_idx = QBLK, lambda i, r: blk(jnp.maximum(2 * i - 1, 0), r)
    elif nb == 1 and d % 2 == 0:
        nq, classes, steps = 2, d // 2, t // d // QBLK
        cur_idx = lambda i, r: blk(i, 2 * r) // 2
        prev_rows, prev_idx = 2 * QBLK, lambda i, r: blk(jnp.maximum(i - 1, 0), 2 * r) // 2
    else:
        nq, classes, steps = 1, d, t // d // QBLK
        cur_idx = blk
        prev_rows, prev_idx = QBLK, lambda i, r: blk(jnp.maximum(i - 1, 0), r)
    cur = lambda s: pl.BlockSpec((None, None, nq * QBLK, SLAB), lambda bi, r, i, s=s: (s, bi, cur_idx(i, r), 0))
    prev = lambda s: pl.BlockSpec((None, None, prev_rows, SLAB), lambda bi, r, i, s=s: (s, bi, prev_idx(i, r), 0))
    ospec = pl.BlockSpec((None, nq * QBLK, SLAB), lambda bi, r, i: (bi, cur_idx(i, r), 0))
    sq, sk, sv = 4 + g, 7 + g, 10 + g
    o, lse = pl.pallas_call(
        _dilated_kernel,
        grid=(b, classes, steps),
        in_specs=[cur(sq), prev(sk), cur(sk), prev(sv), cur(sv)],
        out_specs=[ospec, ospec],
        out_shape=[jax.ShapeDtypeStruct((b, t, SLAB), F32)] * 2,
        compiler_params=_cparams("parallel", "parallel", "arbitrary"),
        name=f"dilated_g{g}",
    )(pv, pv, pv, pv, pv)
    return o.reshape(b * t, SLAB), lse.reshape(b * t, SLAB)


def _tails_kernel(x_ref, o_ref, *buf, dil):
    tm = x_ref.shape[0]
    keep = o_ref.shape[1]
    if dil == 1:
        o_ref[...] = x_ref[tm - keep:tm, :].T
    else:
        seg = tm // dil
        for r in range(dil):
            buf[0][pl.ds(r, seg, stride=dil), :] = x_ref[r * seg:(r + 1) * seg, :]
        o_ref[...] = buf[0][tm - keep:tm, :].T


def _cache_tails(proj, nseq, t, g, keep):
    dil = B_DILATIONS[g]
    pv = proj.reshape(proj.shape[0], nseq, t, SLAB)
    rows = keep if dil == 1 else PERM_TILE
    last = t // rows - 1
    out = pl.pallas_call(
        functools.partial(_tails_kernel, dil=dil),
        grid=(2, nseq, SLAB // 128),
        in_specs=[pl.BlockSpec((None, None, rows, 128), lambda kv, b, cb: (7 + g + B_GROUPS * kv, b, last, cb))],
        out_specs=pl.BlockSpec((None, None, 128, keep), lambda kv, b, cb: (kv, b, cb, 0)),
        out_shape=jax.ShapeDtypeStruct((2, nseq, SLAB, keep), F32),
        scratch_shapes=[pltpu.VMEM((PERM_TILE, 128), F32)] if dil > 1 else [],
        compiler_params=_cparams("parallel", "parallel", "parallel"),
        name=f"cache_tails_g{g}",
    )(pv)
    return _rows_major(out[0], B_HEADS), _rows_major(out[1], B_HEADS)


def _merge_kernel(*refs, dils):
    ins, y_ref, bufs = refs[:6], refs[6], refs[7:]
    tm = y_ref.shape[0]
    vals = []
    nbuf = 0
    for k, ref in enumerate(ins):
        dil = dils[k // 2]
        if dil == 1:
            vals.append(ref[...])
        else:
            buf = bufs[nbuf]
            nbuf += 1
            seg = tm // dil
            for r in range(dil):
                buf[pl.ds(r, seg, stride=dil), :] = ref[r * seg:(r + 1) * seg, :]
            vals.append(buf[...])
    o1, l1, o2, l2, o3, l3 = vals
    m = jnp.maximum(jnp.maximum(l1, l2), l3)
    w1, w2, w3 = jnp.exp(l1 - m), jnp.exp(l2 - m), jnp.exp(l3 - m)
    y_ref[...] = (w1 * o1 + w2 * o2 + w3 * o3) / (w1 + w2 + w3)


def _merge_groups(parts, perm):
    n = parts[0].shape[0]
    dils = B_DILATIONS if perm else (1,) * B_GROUPS
    tm = PERM_TILE if perm else min(1024, n)
    spec = pl.BlockSpec((tm, 128), lambda i, j: (i, j))
    nbuf = 2 * sum(d > 1 for d in dils)
    return pl.pallas_call(
        functools.partial(_merge_kernel, dils=dils), grid=(n // tm, SLAB // 128), in_specs=[spec] * 6,
        out_specs=spec, out_shape=jax.ShapeDtypeStruct((n, SLAB), F32),
        scratch_shapes=[pltpu.VMEM((tm, 128), F32)] * nbuf,
        compiler_params=_cparams("parallel", "parallel"), name="merge_groups",
    )(*parts)


def _swa_kernel(q_ref, kp_ref, kc_ref, vp_ref, vc_ref, sink_ref, o_ref):
    lane = lax.broadcasted_iota(jnp.int32, (1, 128), 1)
    scale = HEAD_DIM ** -0.5
    rep = D_HEADS // D_KV_HEADS
    low = lane < HEAD_DIM
    allk = slice(None)
    subs = range(q_ref.shape[0] // QBLK)
    rows = lambda a: slice(a * QBLK, (a + 1) * QBLK)
    cols = lambda pr: slice(pr * 128, (pr + 1) * 128)
    valid = [_band_masks(pl.program_id(1) > 0 if a == 0 else True, D_WINDOW - 1) for a in subs]
    k2 = [_key_rows(kp_ref, kc_ref, a, allk) for a in subs]
    v2 = [_key_rows(vp_ref, vc_ref, a, allk) for a in subs]
    qp = {(a, pr): q_ref[rows(a), cols(pr)] * scale for a in subs for pr in range(rep)}
    inst = [(a, pr, g) for a in subs for pr in range(rep) for g in range(D_KV_HEADS)]
    acc, m, l = _band_heads(
        [jnp.where(low if g == 0 else jnp.logical_not(low), qp[a, pr], 0.0).astype(BF16) for a, pr, g in inst],
        [k2[a] for a, _, _ in inst], [v2[a] for a, _, _ in inst], [valid[a] for a, _, _ in inst],
        [sink_ref[g * rep + pr] for _, pr, g in inst])
    o_h = [acc[i] / l[i] for i in range(len(inst))]
    for i in range(0, len(inst), 2):
        a, pr, _ = inst[i]
        o_ref[rows(a), cols(pr)] = jnp.where(low, o_h[i], o_h[i + 1])


def _swa_prompt(proj, b, t, sinks):
    pv = proj.reshape(proj.shape[0], b, t, SLAB)
    nq = 2 if t % (2 * QBLK) == 0 else 1
    qspec = pl.BlockSpec((None, None, nq * QBLK, SLAB), lambda bi, i: (3, bi, i, 0))

    def kv(col, prev):
        if prev:
            return pl.BlockSpec((None, None, QBLK, 128), lambda bi, i: (4, bi, jnp.maximum(nq * i - 1, 0), col))
        return pl.BlockSpec((None, None, nq * QBLK, 128), lambda bi, i: (4, bi, i, col))

    o = pl.pallas_call(
        _swa_kernel,
        grid=(b, t // QBLK // nq),
        in_specs=[qspec, kv(0, True), kv(0, False), kv(1, True), kv(1, False),
                  pl.BlockSpec(memory_space=pltpu.SMEM)],
        out_specs=pl.BlockSpec((None, nq * QBLK, SLAB), lambda bi, i: (bi, i, 0)),
        out_shape=jax.ShapeDtypeStruct((b, t, SLAB), F32),
        compiler_params=_cparams("parallel", "arbitrary"),
        name="swa_prompt",
    )(pv, pv, pv, pv, pv, sinks)
    return o.reshape(b * t, SLAB)


def _softplus(x):
    return jnp.maximum(x, 0.0) + jnp.log1p(jnp.exp(-jnp.abs(x)))


def _ssd_kernel(z_ref, x_ref, bc_ref, dt_ref, cw_ref, cb_ref, dtb_ref, alog_ref, dl_ref, cn_ref,
                conv0_ref, h0_ref, y_ref, h_ref, xbuf, bcbuf, *, chunk):
    sb, tb, _ = x_ref.shape
    tblk = pl.program_id(1)
    lane1 = lax.broadcasted_iota(jnp.int32, (1, 128), 1)
    low = lane1 < C_HEADDIM
    row = lax.broadcasted_iota(jnp.int32, (chunk, chunk), 0)
    col = lax.broadcasted_iota(jnp.int32, (chunk, chunk), 1)
    causal = col <= row
    tri = causal.astype(BF16)
    srow_low = lax.broadcasted_iota(jnp.int32, (128, 1), 0) < C_HEADDIM
    hpg = C_HEADS // C_NGROUPS
    a_neg = -jnp.exp(alog_ref[...])

    @pl.when(tblk == 0)
    def _():
        h_ref[...] = h0_ref[...]
        for si in range(sb):
            xbuf[si, 0:8, :] = conv0_ref[si, :, 0:SLAB]
            bcbuf[si, 0:8, :] = conv0_ref[si, :, SLAB:2 * SLAB]

    seqs, chunks = range(sb), range(tb // chunk)
    ppg = hpg // 2
    npair = C_HEADS // 2
    rows = lambda c: slice(c * chunk, (c + 1) * chunk)
    cols = lambda pr: slice(pr * 128, (pr + 1) * 128)
    sc = [(si, c) for si in seqs for c in chunks]
    sch = [(si, c, h) for si, c in sc for h in range(C_HEADS)]
    scp = [(si, c, pr) for si, c in sc for pr in range(npair)]
    xc = [_conv_silu(xbuf.at[si], x_ref[si], cw_ref[:, 0:SLAB], cb_ref[:, 0:SLAB]) for si in seqs]
    bcc = [_conv_silu(bcbuf.at[si], bc_ref[si], cw_ref[:, SLAB:2 * SLAB], cb_ref[:, SLAB:2 * SLAB]) for si in seqs]
    dt = {(si, c): _softplus(dt_ref[si, rows(c), :] + dtb_ref[...]) for si, c in sc}
    cum = {k: _cumsum_rows(v * a_neg, tri) for k, v in dt.items()}
    cum_rows = {k: _transpose_slab(v) for k, v in cum.items()}
    dt_rows = {k: _transpose_slab(v) for k, v in dt.items()}
    bgm = {(si, c, g): bcc[si][rows(c), g * 128:(g + 1) * 128].astype(BF16) for si, c in sc
           for g in range(C_NGROUPS)}
    cgm = {(si, c, g): bcc[si][rows(c), 256 + g * 128:256 + (g + 1) * 128].astype(BF16) for si, c in sc
           for g in range(C_NGROUPS)}
    cbm = {k: _dot_t(cgm[k], bgm[k]) for k in bgm}
    x_pair = {(si, c, pr): xc[si][rows(c), cols(pr)] for si, c, pr in scp}
    cum_t = {(si, c, h): cum[si, c][:, h:h + 1] for si, c, h in sch}
    cum_e = {(si, c, h): cum[si, c][chunk - 1:chunk, h:h + 1] for si, c, h in sch}
    seg = {(si, c, h): jnp.where(causal, cum_t[si, c, h] - cum_rows[si, c][h:h + 1, 0:chunk], NEG)
           for si, c, h in sch}
    w = {(si, c, h): cbm[si, c, h // hpg] * jnp.exp(seg[si, c, h]) * dt_rows[si, c][h:h + 1, 0:chunk]
         for si, c, h in sch}
    xhalf = {(si, c, h): jnp.where(low if h % 2 == 0 else jnp.logical_not(low), x_pair[si, c, h // 2],
                                   0.0).astype(BF16) for si, c, h in sch}
    yh = {k: jnp.dot(w[k].astype(BF16), xhalf[k], preferred_element_type=F32) for k in sch}
    e_col = {k: jnp.exp(cum_t[k]) for k in sch}
    w_end = {(si, c, h): jnp.exp(cum_e[si, c, h] - cum_t[si, c, h]) * dt[si, c][:, h:h + 1] for si, c, h in sch}
    dec = {k: jnp.exp(cum_e[k]) for k in sch}
    y_intra = {(si, c, pr): yh[si, c, 2 * pr] + yh[si, c, 2 * pr + 1] + dl_ref[:, cols(pr)] * x_pair[si, c, pr]
               for si, c, pr in scp}
    xw = {(si, c, pr): x_pair[si, c, pr] * jnp.where(low, w_end[si, c, 2 * pr], w_end[si, c, 2 * pr + 1])
          for si, c, pr in scp}
    upd = {(si, c, pr): _tdot(xw[si, c, pr].astype(BF16), bgm[si, c, pr // ppg]) for si, c, pr in scp}
    e_sel = {(si, c, pr): jnp.where(low, e_col[si, c, 2 * pr], e_col[si, c, 2 * pr + 1]) for si, c, pr in scp}
    d_sel = {(si, c, pr): jnp.where(srow_low, dec[si, c, 2 * pr], dec[si, c, 2 * pr + 1]) for si, c, pr in scp}
    gate = {}
    for si, c, pr in scp:
        zz = z_ref[si, rows(c), cols(pr)]
        gate[si, c, pr] = zz * jax.nn.sigmoid(zz)
    hs = {(si, pr): h_ref[si, pr] for si in seqs for pr in range(npair)}
    for c in chunks:
        cur = [(si, c, pr) for si in seqs for pr in range(npair)]
        inter = {k: _dot_t(cgm[k[0], c, k[2] // ppg], hs[k[0], k[2]].astype(BF16)) * e_sel[k] for k in cur}
        for k in cur:
            hs[k[0], k[2]] = d_sel[k] * hs[k[0], k[2]] + upd[k]
        yz = {k: (y_intra[k] + inter[k]) * gate[k] for k in cur}
        sq = {k: jnp.sum(yz[k] * yz[k], axis=-1, keepdims=True) for k in cur}
        for si in seqs:
            for g in range(C_NGROUPS):
                prs = range(g * ppg, (g + 1) * ppg)
                inv = lax.rsqrt(sum(sq[si, c, pr] for pr in prs) / (hpg * C_HEADDIM) + EPS)
                for pr in prs:
                    y_ref[si, rows(c), cols(pr)] = yz[si, c, pr] * inv * cn_ref[:, cols(pr)]
    for si in seqs:
        for pr in range(npair):
            h_ref[si, pr] = hs[si, pr]


def _ssd(proj, nseq, t, conv_w, conv_b, dt_bias, a_log, d_skip, c_norm, conv0, h0):
    chunk = C_CHUNK if t % C_CHUNK == 0 else t
    if t > chunk:
        sb, tb = min(nseq, 2), min(t, 4 * chunk)
    else:
        sb, tb = min(nseq, 8), t
    p4 = proj.reshape(proj.shape[0], nseq, t, SLAB)
    slab = lambda s: pl.BlockSpec((None, sb, tb, SLAB), lambda i, j, s=s: (s, i, j, 0))
    full = lambda a: pl.BlockSpec(a.shape, lambda i, j: (0,) * a.ndim)
    st = lambda a: pl.BlockSpec((sb,) + a.shape[1:], lambda i, j: (i,) + (0,) * (a.ndim - 1))
    conv0p = jnp.pad(conv0, ((0, 0), (8 - (C_CONV - 1), 0), (0, 0)))
    hp = h0.reshape(nseq, C_HEADS // 2, 128, C_DSTATE)
    cb = conv_b.reshape(1, -1)
    dtb = _pad_cols(dt_bias.reshape(1, C_HEADS), 128)
    alog = _pad_cols(a_log.reshape(1, C_HEADS), 128)
    dl = jnp.repeat(d_skip, C_HEADDIM).reshape(1, C_DINNER)
    cn = c_norm.reshape(1, C_DINNER)
    y, h = pl.pallas_call(
        functools.partial(_ssd_kernel, chunk=chunk),
        grid=(nseq // sb, t // tb),
        in_specs=[slab(0), slab(1), slab(2),
                  pl.BlockSpec((None, sb, tb, 128), lambda i, j: (4, i, j, 2)),
                  full(conv_w), full(cb), full(dtb), full(alog), full(dl), full(cn), st(conv0p), st(hp)],
        out_specs=[pl.BlockSpec((sb, tb, SLAB), lambda i, j: (i, j, 0)), st(hp)],
        out_shape=[jax.ShapeDtypeStruct((nseq, t, SLAB), F32), jax.ShapeDtypeStruct(hp.shape, F32)],
        scratch_shapes=[pltpu.VMEM((sb, 8 + tb, SLAB), F32), pltpu.VMEM((sb, 8 + tb, SLAB), F32)],
        compiler_params=_cparams("parallel", "arbitrary"),
        name="ssd",
    )(p4, p4, p4, p4, conv_w, cb, dtb, alog, dl, cn, conv0p, hp)
    return y.reshape(nseq * t, SLAB), h.reshape(nseq, C_HEADS, C_HEADDIM, C_DSTATE)


FFN_CHUNK = 256


def _post_kernel(x_ref, ya_ref, yb_ref, wo_ref, fn_ref, wg_ref, wu_ref, wd_ref, on_ref, o_ref, act_ref, *, final):
    ymix = jnp.concatenate([ya_ref[...], yb_ref[...]], axis=-1).astype(BF16)
    x1 = x_ref[...] + jnp.dot(ymix, wo_ref[...], preferred_element_type=F32)
    ms = jnp.mean(x1 * x1, axis=-1, keepdims=True)
    hn = (x1 * lax.rsqrt(ms + EPS) * fn_ref[...]).astype(BF16)
    hidden = wg_ref.shape[1]
    for c in range(hidden // FFN_CHUNK):
        cs = slice(c * FFN_CHUNK, (c + 1) * FFN_CHUNK)
        gate = jnp.dot(hn, wg_ref[:, cs], preferred_element_type=F32)
        up = jnp.dot(hn, wu_ref[:, cs], preferred_element_type=F32)
        act_ref[:, cs] = (gate * jax.nn.sigmoid(gate) * up).astype(BF16)
    x2 = x1 + jnp.dot(act_ref[...], wd_ref[...], preferred_element_type=F32)
    if final:
        ms2 = jnp.mean(x2 * x2, axis=-1, keepdims=True)
        x2 = x2 * lax.rsqrt(ms2 + EPS) * on_ref[...]
    o_ref[...] = x2


def _post(x, ya, yb, w_out, layer, ffn_norm, w_gate, w_up, w_down, out_norm, final):
    n = x.shape[0]
    tm = min(1024, n)
    hidden = w_gate.shape[2]
    row = lambda w: pl.BlockSpec((tm, w), lambda i: (i, 0))
    res = lambda a: pl.BlockSpec(a.shape, lambda i: (0, 0), pipeline_mode=pl.Buffered(1))
    lay = lambda a: pl.BlockSpec((None,) + a.shape[1:], lambda i: (layer, 0, 0), pipeline_mode=pl.Buffered(1))
    fn = ffn_norm.reshape(-1, 1, D_MODEL)
    on = out_norm.reshape(1, D_MODEL)
    return pl.pallas_call(
        functools.partial(_post_kernel, final=final),
        grid=(n // tm,),
        in_specs=[row(D_MODEL), row(SLAB), row(SLAB), res(w_out), lay(fn), lay(w_gate), lay(w_up), lay(w_down),
                  res(on)],
        out_specs=row(D_MODEL),
        out_shape=jax.ShapeDtypeStruct((n, D_MODEL), F32),
        scratch_shapes=[pltpu.VMEM((tm, hidden), BF16)],
        compiler_params=_cparams("parallel"),
        name="post_ffn",
    )(x, ya, yb, w_out, fn, w_gate, w_up, w_down, on)


RIDER_PHASES = 4


def _rider_mask(ts, r, window, step):
    t = lax.broadcasted_iota(jnp.int32, (2 * ts, r + 128), 0) % ts
    c = lax.broadcasted_iota(jnp.int32, (2 * ts, r + 128), 1)
    rel = r + t - jnp.where(c < 128, c, c - 128 + ts)
    valid = jnp.logical_and(jnp.logical_and(rel >= 0, rel <= window), (rel & (step - 1)) == 0)
    return jnp.logical_and(valid, jnp.logical_or(c < ts, c >= 128))


def _shift_rows_minor(src_ref, new_t, rows, ts):
    r = src_ref.shape[1]
    lane = lax.broadcasted_iota(jnp.int32, (1, 128), 1)
    old = src_ref[rows, :]
    rolled = pltpu.roll(old, r - ts, 1)
    tail = jnp.where(lane < 128 - ts, rolled[:, r - 128:], new_t[rows, :])
    out = tail if r == 128 else jnp.concatenate([rolled[:, :r - 128], tail], axis=1)
    return old, out


def _new_rows_t(new_ref):
    ts = new_ref.shape[0]
    pad = jnp.zeros((128 - ts, SLAB), F32)
    return jnp.concatenate([pad, new_ref[...]], axis=0).T


def _rider_k(q_ref, kn_ref, kc_ref, ko_ref, p_ref, lse_ref, window, step):
    ts = q_ref.shape[0]
    r = kc_ref.shape[1]
    low = lax.broadcasted_iota(jnp.int32, (1, 128), 1) < HEAD_DIM
    valid = _rider_mask(ts, r, window, step)
    kn_t = _new_rows_t(kn_ref)
    scale = HEAD_DIM ** -0.5
    pairs = range(B_HEADS // 2)
    rows = lambda pr: slice(pr * 128, (pr + 1) * 128)
    s = []
    for pr in pairs:
        k_old, k_out = _shift_rows_minor(kc_ref, kn_t, rows(pr), ts)
        ko_ref[rows(pr), :] = k_out
        k_ext = jnp.concatenate([k_old[:, 0:128], k_out], axis=1).astype(BF16)
        qp = q_ref[:, rows(pr)] * scale
        q2 = jnp.concatenate([jnp.where(low, qp, 0.0), jnp.where(low, 0.0, qp)], axis=0).astype(BF16)
        s.append(jnp.where(valid, jnp.dot(q2, k_ext, preferred_element_type=F32), NEG))
    m = [jnp.max(s[pr], axis=1, keepdims=True) for pr in pairs]
    e = [jnp.exp(s[pr] - m[pr]) for pr in pairs]
    l = [jnp.sum(e[pr], axis=1, keepdims=True) for pr in pairs]
    for pr in pairs:
        p_ref[pr * 2 * ts:(pr + 1) * 2 * ts, :] = (e[pr] / l[pr]).astype(BF16)
        lse = m[pr] + jnp.log(l[pr])
        lse_ref[:, rows(pr)] = jnp.where(low, lse[0:ts], lse[ts:2 * ts])


def _rider_v(p_ref, vn_ref, vc_ref, vo_ref, o_ref):
    ts = vn_ref.shape[0]
    low = lax.broadcasted_iota(jnp.int32, (1, 128), 1) < HEAD_DIM
    vn_t = _new_rows_t(vn_ref)
    for pr in range(B_HEADS // 2):
        rows = slice(pr * 128, (pr + 1) * 128)
        v_old, v_out = _shift_rows_minor(vc_ref, vn_t, rows, ts)
        vo_ref[rows, :] = v_out
        v_ext = jnp.concatenate([v_old[:, 0:128], v_out], axis=1).astype(BF16)
        a = _dot_t(p_ref[pr * 2 * ts:(pr + 1) * 2 * ts, :], v_ext)
        o_ref[:, rows] = jnp.where(low, a[0:ts], a[ts:2 * ts])


def _post_rider_kernel(x_ref, ya_ref, yb_ref, wo_ref, fn_ref, wg_ref, wu_ref, wd_ref, on_ref, *rest,
                       final, side, groups):
    ng = len(groups)
    nout = 3 if side == 'k' else 2
    ins, o_ref, outs = rest[:3 * ng], rest[3 * ng], rest[3 * ng + 1:3 * ng + 1 + nout * ng]
    act_ref, hn_ref = rest[3 * ng + 1 + nout * ng:]
    x1_ref = o_ref

    def ride():
        for k, g in enumerate(groups):
            i3, o3 = ins[3 * k:3 * k + 3], outs[nout * k:nout * k + nout]
            if side == 'k':
                _rider_k(*i3, *o3, B_WINDOWS[g], B_DILATIONS[g])
            else:
                _rider_v(*i3, *o3)

    u = pl.program_id(1)
    nchunk = wg_ref.shape[1] // FFN_CHUNK
    per = -(-nchunk // (RIDER_PHASES - 1))
    first = nchunk - per * (RIDER_PHASES - 2)

    def hidden_chunks(lo, hi):
        hn = hn_ref[...]
        for c in range(lo, hi):
            cs = slice(c * FFN_CHUNK, (c + 1) * FFN_CHUNK)
            gate = jnp.dot(hn, wg_ref[:, cs], preferred_element_type=F32)
            up = jnp.dot(hn, wu_ref[:, cs], preferred_element_type=F32)
            act_ref[:, cs] = (gate * jax.nn.sigmoid(gate) * up).astype(BF16)

    def phase(k):
        def body():
            if k == 0:
                ymix = jnp.concatenate([ya_ref[...], yb_ref[...]], axis=-1).astype(BF16)
                x1 = x_ref[...] + jnp.dot(ymix, wo_ref[...], preferred_element_type=F32)
                ms = jnp.mean(x1 * x1, axis=-1, keepdims=True)
                x1_ref[...] = x1
                hn_ref[...] = (x1 * lax.rsqrt(ms + EPS) * fn_ref[...]).astype(BF16)
                hidden_chunks(0, first)
            elif k < RIDER_PHASES - 1:
                hidden_chunks(first + (k - 1) * per, first + k * per)
            else:
                x2 = x1_ref[...] + jnp.dot(act_ref[...], wd_ref[...], preferred_element_type=F32)
                if final:
                    ms2 = jnp.mean(x2 * x2, axis=-1, keepdims=True)
                    x2 = x2 * lax.rsqrt(ms2 + EPS) * on_ref[...]
                o_ref[...] = x2
            ride()
        return body

    for k in range(RIDER_PHASES):
        pl.when(u == k)(phase(k))


def _post_rider(x, ya, yb, w_out, layer, ffn_norm, w_gate, w_up, w_down, out_norm, final, side, groups, proj_s,
                caches_t, probs):
    n = x.shape[0]
    nseq = caches_t[0].shape[0]
    ts = proj_s.shape[1] // nseq
    tm = n * RIDER_PHASES // nseq
    hidden = w_gate.shape[2]
    row = lambda w, **kw: pl.BlockSpec((tm, w), lambda i, u: (i, 0), **kw)
    res = lambda a: pl.BlockSpec(a.shape, lambda i, u: (0, 0), pipeline_mode=pl.Buffered(1))
    lay = lambda a: pl.BlockSpec((None,) + a.shape[1:], lambda i, u: (layer, 0, 0), pipeline_mode=pl.Buffered(1))
    seq = lambda i, u: i * RIDER_PHASES + u
    new = lambda s: pl.BlockSpec((None, ts, SLAB), lambda i, u: (s, seq(i, u), 0))
    rows8 = pl.BlockSpec((ts, SLAB), lambda i, u: (seq(i, u), 0))
    fn = ffn_norm.reshape(-1, 1, D_MODEL)
    on = out_norm.reshape(1, D_MODEL)
    single = dict(pipeline_mode=pl.Buffered(1))
    common = [row(D_MODEL), row(SLAB, **single), row(SLAB, **single), res(w_out), lay(fn), lay(w_gate),
              lay(w_up), lay(w_down), res(on)]
    small = jax.ShapeDtypeStruct((nseq * ts, SLAB), F32)
    ins, args, outs, shapes = [], [], [row(D_MODEL)], [jax.ShapeDtypeStruct((n, D_MODEL), F32)]
    for k, g in enumerate(groups):
        r = caches_t[k].shape[2]
        cspec = pl.BlockSpec((None, SLAB, r), lambda i, u: (seq(i, u), 0, 0))
        pspec = pl.BlockSpec((None, B_HEADS * ts, r + 128), lambda i, u: (seq(i, u), 0, 0))
        cout = jax.ShapeDtypeStruct(caches_t[k].shape, F32)
        if side == 'k':
            ins += [new(4 + g), new(7 + g), cspec]
            args += [proj_s, proj_s, caches_t[k]]
            outs += [cspec, pspec, rows8]
            shapes += [cout, jax.ShapeDtypeStruct((nseq, B_HEADS * ts, r + 128), BF16), small]
        else:
            ins += [pspec, new(10 + g), cspec]
            args += [probs[k], proj_s, caches_t[k]]
            outs += [cspec, rows8]
            shapes += [cout, small]
    res_all = pl.pallas_call(
        functools.partial(_post_rider_kernel, final=final, side=side, groups=tuple(groups)),
        grid=(n // tm, RIDER_PHASES),
        in_specs=common + ins,
        out_specs=outs,
        out_shape=shapes,
        scratch_shapes=[pltpu.VMEM((tm, hidden), BF16), pltpu.VMEM((tm, D_MODEL), BF16)],
        compiler_params=_cparams("parallel", "arbitrary", vmem=VMEM_LIMIT_RIDER),
        name=f"post_ffn_ride_{side}",
    )(x, ya, yb, w_out, fn, w_gate, w_up, w_down, on, *args)
    nout = 3 if side == 'k' else 2
    return res_all[0], [res_all[1 + nout * k:1 + nout * (k + 1)] for k in range(len(groups))]


NEWPAD = 128


def _ext_mask(rows, r, ts, window, step):
    t = lax.broadcasted_iota(jnp.int32, (rows, r + NEWPAD), 0) % ts
    c = lax.broadcasted_iota(jnp.int32, (rows, r + NEWPAD), 1)
    rel = r + t - c
    ok = jnp.logical_and(jnp.logical_and(rel >= 0, rel <= window), c < r + ts)
    return jnp.logical_and(ok, (rel & (step - 1)) == 0)


def _shift_cache(dst_ref, src_ref, new, si, r, ts):
    dst_ref[si, 0:r - ts, :] = src_ref[si, ts:r, :]
    dst_ref[si, r - ts:r, :] = new


def _pad_new(x):
    return jnp.concatenate([x, jnp.zeros((NEWPAD - x.shape[0], x.shape[1]), x.dtype)], axis=0)


def _dil_sample_kernel(q_ref, kn_ref, vn_ref, kc_ref, vc_ref, o_ref, lse_ref, ko_ref, vo_ref,
                       *, window, step):
    sb, ts, _ = q_ref.shape
    r = kc_ref.shape[2]
    scale = HEAD_DIM ** -0.5
    lane = lax.broadcasted_iota(jnp.int32, (1, 128), 1)
    low = lane < HEAD_DIM
    t = lax.broadcasted_iota(jnp.int32, (2 * ts, r + 128), 0) % ts
    c = lax.broadcasted_iota(jnp.int32, (2 * ts, r + 128), 1)
    rel = r + t - jnp.where(c < 128, c, c - 128 + ts)
    valid = jnp.logical_and(jnp.logical_and(rel >= 0, rel <= window), (rel & (step - 1)) == 0)
    valid = jnp.logical_and(valid, jnp.logical_or(c < ts, c >= 128))
    pad = jnp.zeros((128 - ts, SLAB), F32)

    def shifted(src_ref, new_t, rows):
        old = src_ref[rows, :]
        rolled = pltpu.roll(old, r - ts, 1)
        tail = jnp.where(lane < 128 - ts, rolled[:, r - 128:], new_t[rows, :])
        out = tail if r == 128 else jnp.concatenate([rolled[:, :r - 128], tail], axis=1)
        return old, out

    for si, pr in [(si, pr) for si in range(sb) for pr in range(B_HEADS // 2)]:
        if pr == 0:
            kn_t = jnp.concatenate([pad, kn_ref[si]], axis=0).T
            vn_t = jnp.concatenate([pad, vn_ref[si]], axis=0).T
        rows = slice(pr * 128, (pr + 1) * 128)
        k_old, k_out = shifted(kc_ref.at[si], kn_t, rows)
        v_old, v_out = shifted(vc_ref.at[si], vn_t, rows)
        ko_ref[si, rows, :] = k_out
        vo_ref[si, rows, :] = v_out
        k_ext = jnp.concatenate([k_old[:, 0:128], k_out], axis=1).astype(BF16)
        v_ext = jnp.concatenate([v_old[:, 0:128], v_out], axis=1).astype(BF16)
        qp = q_ref[si, :, rows] * scale
        q2 = jnp.concatenate([jnp.where(low, qp, 0.0), jnp.where(low, 0.0, qp)], axis=0).astype(BF16)
        s = jnp.where(valid, jnp.dot(q2, k_ext, preferred_element_type=F32), NEG)
        m = jnp.max(s, axis=1, keepdims=True)
        e = jnp.exp(s - m)
        l = jnp.sum(e, axis=1, keepdims=True)
        a = _dot_t(e.astype(BF16), v_ext) / l
        lse = m + jnp.log(l)
        o_ref[si, :, rows] = jnp.where(low, a[0:ts], a[ts:2 * ts])
        lse_ref[si, :, rows] = jnp.where(low, lse[0:ts], lse[ts:2 * ts])


def _rows_minor(cache):
    nseq, r = cache.shape[:2]
    return jnp.transpose(cache, (0, 2, 3, 1)).reshape(nseq, -1, r)


def _rows_major(cache_t, heads):
    nseq, _, r = cache_t.shape
    return jnp.transpose(cache_t.reshape(nseq, heads, HEAD_DIM, r), (0, 3, 1, 2))


def _dilated_sample(proj, nseq, ts, g, k_cache, v_cache):
    r = k_cache.shape[1]
    assert ts % 8 == 0 and r % 128 == 0 and r >= B_WINDOWS[g]
    kt, vt = _rows_minor(k_cache), _rows_minor(v_cache)
    sb = max(1, min(nseq, 8, 1024 // r))
    p4 = proj.reshape(proj.shape[0], nseq, ts, SLAB)
    new = lambda s: pl.BlockSpec((None, sb, ts, SLAB), lambda i, s=s: (s, i, 0, 0))
    cspec = pl.BlockSpec((sb, SLAB, r), lambda i: (i, 0, 0))
    ospec = pl.BlockSpec((sb, ts, SLAB), lambda i: (i, 0, 0))
    o, lse, kn, vn = pl.pallas_call(
        functools.partial(_dil_sample_kernel, window=B_WINDOWS[g], step=B_DILATIONS[g]),
        grid=(nseq // sb,),
        in_specs=[new(4 + g), new(7 + g), new(10 + g), cspec, cspec],
        out_specs=[ospec, ospec, cspec, cspec],
        out_shape=[jax.ShapeDtypeStruct((nseq, ts, SLAB), F32)] * 2 + [jax.ShapeDtypeStruct(kt.shape, F32)] * 2,
        compiler_params=_cparams("parallel"),
        name=f"dilated_sample_g{g}",
    )(p4, p4, p4, kt, vt)
    return (o.reshape(nseq * ts, SLAB), lse.reshape(nseq * ts, SLAB),
            _rows_major(kn, B_HEADS), _rows_major(vn, B_HEADS))


def _swa_sample_kernel(q_ref, k_ref, v_ref, kc_ref, vc_ref, sink_ref, o_ref, kn_ref, vn_ref):
    sb, ts, _ = q_ref.shape
    r = kc_ref.shape[1]
    rows = D_HEADS * ts
    valid = _ext_mask(rows, r, ts, D_WINDOW - 1, 1)
    lane = lax.broadcasted_iota(jnp.int32, (1, 128), 1)
    low = lane < HEAD_DIM
    rep = D_HEADS // D_KV_HEADS
    scale = HEAD_DIM ** -0.5
    sink = sink_ref[:, 0:1]
    seqs = range(sb)

    def q_rows(si):
        pieces = []
        for pr in range(rep):
            base = q_ref[si, :, pr * 128:(pr + 1) * 128] * scale
            for g in range(D_KV_HEADS):
                pieces.append(jnp.where(low if g == 0 else jnp.logical_not(low), base, 0.0))
        return jnp.concatenate(pieces, axis=0).astype(BF16)

    kn = [k_ref[si] for si in seqs]
    vn = [v_ref[si] for si in seqs]
    qrows = [q_rows(si) for si in seqs]
    kext = [jnp.concatenate([kc_ref[si], _pad_new(kn[si])], axis=0).astype(BF16) for si in seqs]
    vext = [jnp.concatenate([vc_ref[si], _pad_new(vn[si])], axis=0).astype(BF16) for si in seqs]
    s = [jnp.where(valid, _dot_t(qrows[si], kext[si]), NEG) for si in seqs]
    m = [jnp.maximum(jnp.max(s[si], axis=1, keepdims=True), sink) for si in seqs]
    e = [jnp.exp(s[si] - m[si]) for si in seqs]
    l = [jnp.sum(e[si], axis=1, keepdims=True) + jnp.exp(sink - m[si]) for si in seqs]
    o = [jnp.dot(e[si].astype(BF16), vext[si], preferred_element_type=F32) / l[si] for si in seqs]
    for si in seqs:
        pairs = [jnp.where(low, o[si][2 * pr * ts:(2 * pr + 1) * ts], o[si][(2 * pr + 1) * ts:(2 * pr + 2) * ts])
                 for pr in range(rep)]
        o_ref[si] = jnp.concatenate(pairs, axis=1)
        _shift_cache(kn_ref, kc_ref, kn[si], si, r, ts)
        _shift_cache(vn_ref, vc_ref, vn[si], si, r, ts)


def _swa_sample(proj, nseq, ts, sinks, k_cache, v_cache):
    r = k_cache.shape[1]
    sb = min(nseq, 8)
    kvw = D_KV_HEADS * HEAD_DIM
    p4 = proj.reshape(proj.shape[0], nseq, ts, SLAB)
    kc = k_cache.reshape(nseq, r, kvw)
    vc = v_cache.reshape(nseq, r, kvw)
    cspec = pl.BlockSpec((sb, r, kvw), lambda i: (i, 0, 0))
    new = lambda col: pl.BlockSpec((None, sb, ts, kvw), lambda i: (4, i, 0, col))
    rep = D_HEADS // D_KV_HEADS
    by_row_group = sinks.reshape(D_KV_HEADS, rep).T.reshape(-1)
    sink_rows = jnp.broadcast_to(jnp.repeat(by_row_group, ts)[:, None], (D_HEADS * ts, 128))
    o, kn, vn = pl.pallas_call(
        _swa_sample_kernel,
        grid=(nseq // sb,),
        in_specs=[pl.BlockSpec((None, sb, ts, SLAB), lambda i: (3, i, 0, 0)), new(0), new(1), cspec, cspec,
                  pl.BlockSpec(sink_rows.shape, lambda i: (0, 0))],
        out_specs=[pl.BlockSpec((sb, ts, SLAB), lambda i: (i, 0, 0)), cspec, cspec],
        out_shape=[jax.ShapeDtypeStruct((nseq, ts, SLAB), F32)] + [jax.ShapeDtypeStruct(kc.shape, F32)] * 2,
        compiler_params=_cparams("parallel"),
        name="swa_sample",
    )(p4, p4, p4, kc, vc, sink_rows)
    return o.reshape(nseq * ts, SLAB), kn.reshape(k_cache.shape), vn.reshape(v_cache.shape)


def kernel(x_prompt, x_sample, state_a_C, state_a_n, state_a_m, state_a_conv, cache_b_k1, cache_b_v1, cache_b_k2, cache_b_v2, cache_b_k3, cache_b_v3, state_c_conv, state_c_ssm, cache_d_k, cache_d_v, e_norm_mix, e_w_in, e_b_gates, e_a_conv_w, e_a_conv_b, e_a_norm, e_w_out, o_norm_mix, o_w_in, o_c_conv_w, o_c_conv_b, o_c_dt_bias, o_c_A_log, o_c_D, o_c_norm, o_d_sinks, o_w_out, ffn_norm, w_gate, w_up, w_down, final_norm):
    depth = ffn_norm.shape[0]
    even_w = [_even_slabs(e_w_in[j]) for j in range((depth + 1) // 2)]
    odd_w = [_odd_slabs(o_w_in[j]) for j in range(depth // 2)]
    odd_wo = [jnp.concatenate([o_w_out[j][:C_DINNER], _group_aligned(o_w_out[j][C_DINNER:], 0)],
                              axis=0).astype(BF16) for j in range(depth // 2)]
    even_wo = [e_w_out[j].astype(BF16) for j in range((depth + 1) // 2)]
    wg16, wu16, wd16 = w_gate.astype(BF16), w_up.astype(BF16), w_down.astype(BF16)

    def trunk(x, pos, fresh, even_st, odd_st, ride=None, pre=None):
        nseq, t, _ = x.shape
        rode = {}
        assert t >= A_CONV - 1 and t >= C_CONV - 1
        cos_t, sin_t = _rope_tables(pos)
        x2 = x.reshape(nseq * t, D_MODEL)
        new_even, new_odd = [], []

        def tail(proj, s, keep, dil=1, lanes=slice(None)):
            p4 = proj.reshape(proj.shape[0], nseq, t, SLAB)
            if dil == 1:
                return p4[s, :, t - keep:, lanes]
            seg = PERM_TILE // dil
            last = p4[s, :, t - PERM_TILE:, lanes].reshape(nseq, dil, seg, -1)[:, :, seg - keep // dil:]
            return jnp.swapaxes(last, 1, 2).reshape(nseq, keep, -1)

        for layer in range(depth):
            j = layer // 2
            last = layer == depth - 1
            if layer % 2 == 0:
                if pre is not None and layer == 0:
                    pe = pre['proj']
                else:
                    pe = _norm_proj(x2, e_norm_mix[j], even_w[j], cos_t, sin_t, _even_kinds(fresh))
                st = {k: v[j] for k, v in even_st.items()}
                ya, c_new, n_new, m_new = _mlstm(pe, nseq, t, e_a_conv_w[j], e_a_conv_b[j], e_b_gates[j],
                                                 e_a_norm[j], st['a_conv'], st['a_C'], st['a_n'], st['a_m'])
                ns = {'a_C': c_new, 'a_n': n_new, 'a_m': m_new,
                      'a_conv': jnp.concatenate([tail(pe, 0, A_CONV - 1), tail(pe, 1, A_CONV - 1)], axis=-1)}
                parts = []
                for g in range(B_GROUPS):
                    kn, vn = f'b_k{g + 1}', f'b_v{g + 1}'
                    if fresh:
                        o_g, lse_g = _dilated_prompt(pe, nseq, t, g)
                        keep = min(B_WINDOWS[g], t)
                        if keep % 128 == 0 and keep <= PERM_TILE:
                            ns[kn], ns[vn] = _cache_tails(pe, nseq, t, g, keep)
                        else:
                            ns[kn] = tail(pe, 7 + g, keep, B_DILATIONS[g]).reshape(nseq, keep, B_HEADS, HEAD_DIM)
                            ns[vn] = tail(pe, 10 + g, keep, B_DILATIONS[g]).reshape(nseq, keep, B_HEADS, HEAD_DIM)
                    elif pre is not None and layer == 0 and g in pre['groups']:
                        o_g, lse_g, ns[kn], ns[vn] = pre[g]
                    else:
                        o_g, lse_g, ns[kn], ns[vn] = _dilated_sample(pe, nseq, t, g, st[kn], st[vn])
                    parts += [o_g, lse_g]
                yb = _merge_groups(parts, fresh)
                if ride is not None and layer == 0:
                    x2, rode['k'] = _post_rider(
                        x2, ya, yb, even_wo[j], layer, ffn_norm, wg16, wu16, wd16, final_norm, last, 'k',
                        ride['groups'], ride['proj'], ride['kt'], None)
                else:
                    x2 = _post(x2, ya, yb, even_wo[j], layer, ffn_norm, wg16, wu16, wd16, final_norm, last)
                new_even.append(ns)
            else:
                po = _norm_proj(x2, o_norm_mix[j], odd_w[j], cos_t, sin_t, ODD_KINDS)
                st = {k: v[j] for k, v in odd_st.items()}
                yc, ssm_new = _ssd(po, nseq, t, o_c_conv_w[j], o_c_conv_b[j], o_c_dt_bias[j], o_c_A_log[j],
                                   o_c_D[j], o_c_norm[j], st['c_conv'], st['c_ssm'])
                ns = {'c_conv': jnp.concatenate([tail(po, 1, C_CONV - 1), tail(po, 2, C_CONV - 1)], axis=-1),
                      'c_ssm': ssm_new}
                kvw = D_KV_HEADS * HEAD_DIM
                if fresh:
                    yd = _swa_prompt(po, nseq, t, o_d_sinks[j])
                    keep = min(D_WINDOW, t)
                    ns['d_k'] = tail(po, 4, keep, lanes=slice(0, kvw)).reshape(nseq, keep, D_KV_HEADS, HEAD_DIM)
                    ns['d_v'] = tail(po, 4, keep, lanes=slice(kvw, 2 * kvw)).reshape(
                        nseq, keep, D_KV_HEADS, HEAD_DIM)
                else:
                    yd, ns['d_k'], ns['d_v'] = _swa_sample(po, nseq, t, o_d_sinks[j], st['d_k'], st['d_v'])
                if ride is not None and layer == 1:
                    x2, rode['v'] = _post_rider(
                        x2, yc, yd, odd_wo[j], layer, ffn_norm, wg16, wu16, wd16, final_norm, last, 'v',
                        ride['groups'], ride['proj'], ride['vt'], [kpl[1] for kpl in rode['k']])
                else:
                    x2 = _post(x2, yc, yd, odd_wo[j], layer, ffn_norm, wg16, wu16, wd16, final_norm, last)
                new_odd.append(ns)
        out = {k: jnp.stack([ns[k] for ns in new_even]) for k in new_even[0]}
        out.update({k: jnp.stack([ns[k] for ns in new_odd]) for k in new_odd[0]})
        return x2.reshape(nseq, t, D_MODEL), out, rode

    bp, tp, _ = x_prompt.shape
    bs, ts, _ = x_sample.shape
    n_even, n_odd = (depth + 1) // 2, depth // 2
    zeros = lambda *shape: jnp.zeros(shape, F32)
    even_p = {'a_C': zeros(n_even, bp, A_HEADS, A_DK, A_DV), 'a_n': zeros(n_even, bp, A_HEADS, A_DK),
              'a_m': zeros(n_even, bp, A_HEADS), 'a_conv': zeros(n_even, bp, A_CONV - 1, 2 * SLAB)}
    odd_p = {'c_conv': zeros(n_odd, bp, C_CONV - 1, 2 * SLAB),
             'c_ssm': zeros(n_odd, bp, C_HEADS, C_HEADDIM, C_DSTATE)}
    even_s = {'a_C': state_a_C, 'a_n': state_a_n, 'a_m': state_a_m, 'a_conv': state_a_conv,
              'b_k1': cache_b_k1, 'b_v1': cache_b_v1, 'b_k2': cache_b_k2, 'b_v2': cache_b_v2,
              'b_k3': cache_b_k3, 'b_v3': cache_b_v3}
    odd_s = {'c_conv': state_c_conv, 'c_ssm': state_c_ssm, 'd_k': cache_d_k, 'd_v': cache_d_v}
    pos_p = jnp.arange(tp, dtype=jnp.int32)
    pos_s = jnp.tile(PAST_LEN + jnp.arange(ts, dtype=jnp.int32), bs)

    groups = tuple(range(B_GROUPS))
    k_caches = [even_s[f'b_k{g + 1}'][0] for g in groups]
    v_caches = [even_s[f'b_v{g + 1}'][0] for g in groups]
    tile = bp * tp * RIDER_PHASES // bs
    can_ride = (depth >= 2 and bp * tp * RIDER_PHASES % bs == 0 and tile % 8 == 0 and 8 <= tile <= 1024
                and ts == 8 and all(c.shape[1] % 128 == 0 and c.shape[1] >= B_WINDOWS[g]
                                    for g, c in zip(groups, k_caches)))
    if can_ride:
        cos_s, sin_s = _rope_tables(pos_s)
        pe_s = _norm_proj(x_sample.reshape(bs * ts, D_MODEL), e_norm_mix[0], even_w[0], cos_s, sin_s,
                          _even_kinds(False))
        ride = {'groups': groups, 'proj': pe_s, 'kt': [_rows_minor(c) for c in k_caches],
                'vt': [_rows_minor(c) for c in v_caches]}
        y_p, pn, rode = trunk(x_prompt, pos_p, True, even_p, odd_p, ride=ride)
        pre = {'groups': groups, 'proj': pe_s}
        for k, g in enumerate(groups):
            (k_out, _, lse_g), (v_out, o_g) = rode['k'][k], rode['v'][k]
            pre[g] = (o_g, lse_g, _rows_major(k_out, B_HEADS), _rows_major(v_out, B_HEADS))
        y_s, sn, _ = trunk(x_sample, pos_s, False, even_s, odd_s, pre=pre)
    else:
        y_p, pn, _ = trunk(x_prompt, pos_p, True, even_p, odd_p)
        y_s, sn, _ = trunk(x_sample, pos_s, False, even_s, odd_s)

    keys = ('a_C', 'a_n', 'a_m', 'a_conv', 'b_k1', 'b_v1', 'b_k2', 'b_v2', 'b_k3', 'b_v3',
            'c_conv', 'c_ssm', 'd_k', 'd_v')
    states = []
    for k in keys:
        states += [pn[k], sn[k]]
    return (y_p, y_s, *states)
```

```python
import functools
import math

import jax
import jax.numpy as jnp
from jax import lax
from jax.experimental import pallas as pl
from jax.experimental.pallas import tpu as pltpu

F32 = jnp.float32
BF16 = jnp.bfloat16

D_MODEL = 1024
PAST_LEN = 16384
EPS = 1e-6
ROPE_THETA = 10000.0
HEAD_DIM = 64
SLAB = 512

A_HEADS = 4
A_DK = 128
A_DV = 128
A_CONV = 4
A_CHUNK = 64

B_HEADS = 8
B_WINDOWS = (128, 512, 2048)
B_DILATIONS = (1, 4, 16)
B_GROUPS = 3

C_HEADS = 8
C_HEADDIM = 64
C_DINNER = C_HEADS * C_HEADDIM
C_DSTATE = 128
C_NGROUPS = 2
C_CONV = 4
C_CHUNK = 128

D_HEADS = 8
D_KV_HEADS = 2
D_WINDOW = 128

NEG = -1e30
VMEM_LIMIT = 56 * 1024 * 1024


VMEM_LIMIT_RIDER = 60 * 1024 * 1024


def _mix_dtype(rows):
    return BF16 if rows % 16 == 0 else F32


def _cparams(*sem, vmem=VMEM_LIMIT):
    return pltpu.CompilerParams(dimension_semantics=sem, vmem_limit_bytes=vmem)


def _rot_half(y):
    lane = lax.broadcasted_iota(jnp.int32, y.shape, 1)
    fwd = pltpu.roll(y, SLAB - HEAD_DIM // 2, 1)
    bwd = pltpu.roll(y, HEAD_DIM // 2, 1)
    return jnp.where(lane % HEAD_DIM < HEAD_DIM // 2, fwd, bwd)


PERM_TILE = 2048
PROJ_ROWS = 512
PERM_STRIDE = 4


def _proj_kernel(x_ref, nw_ref, w_ref, cos_ref, sin_ref, o_ref, xn_ref, *ybuf, kinds):
    j = pl.program_id(1)
    tm = o_ref.shape[0]

    @pl.when(j == 0)
    def _():
        x = x_ref[...]
        ms = jnp.mean(x * x, axis=-1, keepdims=True)
        xn_ref[...] = (x * lax.rsqrt(ms + EPS) * nw_ref[...]).astype(BF16)

    rc = min(tm, PROJ_ROWS)

    def emit(rope_lanes, dil):
        seg = tm // dil
        cseg = rc // dil
        for c in range(tm // rc):
            rows = slice(c * rc, (c + 1) * rc)
            y = jnp.dot(xn_ref[rows, :], w_ref[...], preferred_element_type=F32)
            val = y
            if rope_lanes:
                cos = jnp.tile(cos_ref[rows, :], (1, SLAB // 128))
                sin = jnp.tile(sin_ref[rows, :], (1, SLAB // 128))
                val = y * cos + _rot_half(y) * sin
                if rope_lanes < SLAB:
                    lane = lax.broadcasted_iota(jnp.int32, y.shape, 1)
                    val = jnp.where(lane < rope_lanes, val, y)
            if dil == 1:
                o_ref[rows, :] = val
            else:
                for cb in range(SLAB // 128):
                    ybuf[0][cb, rows, :] = val[:, cb * 128:(cb + 1) * 128]
                if dil <= PERM_STRIDE:
                    for r in range(dil):
                        dst = slice(r * seg + c * cseg, r * seg + (c + 1) * cseg)
                        for cb in range(SLAB // 128):
                            o_ref[dst, cb * 128:(cb + 1) * 128] = ybuf[0][cb, pl.ds(c * rc + r, cseg, stride=dil), :]
                else:
                    d2 = dil // PERM_STRIDE
                    half = rc // PERM_STRIDE
                    for cb in range(SLAB // 128):
                        for r0 in range(PERM_STRIDE):
                            ybuf[1][cb, r0 * half:(r0 + 1) * half, :] = \
                                ybuf[0][cb, pl.ds(c * rc + r0, half, stride=PERM_STRIDE), :]
                    for r0 in range(PERM_STRIDE):
                        for r1 in range(d2):
                            r = r1 * PERM_STRIDE + r0
                            dst = slice(r * seg + c * cseg, r * seg + (c + 1) * cseg)
                            for cb in range(SLAB // 128):
                                o_ref[dst, cb * 128:(cb + 1) * 128] = \
                                    ybuf[1][cb, pl.ds(r0 * half + r1, cseg, stride=d2), :]

    for kind in sorted(set(kinds)):
        member = functools.reduce(jnp.logical_or, [j == s for s, k in enumerate(kinds) if k == kind])
        pl.when(member)(functools.partial(emit, *kind))


def _norm_proj(x, norm_w, w_slabs, cos_t, sin_t, kinds):
    n = x.shape[0]
    s = w_slabs.shape[0]
    p = cos_t.shape[0]
    perm = any(d > 1 for _, d in kinds)
    big = perm or (n % PERM_TILE == 0 and p % PERM_TILE == 0)
    tm = PERM_TILE if big else min(1024, n, p)
    assert n % tm == 0 and p % tm == 0 and len(kinds) == s
    npt = p // tm
    scratch = [pltpu.VMEM((tm, D_MODEL), BF16)]
    if perm:
        scratch.append(pltpu.VMEM((SLAB // 128, tm, 128), F32))
    if any(d > PERM_STRIDE for _, d in kinds):
        scratch.append(pltpu.VMEM((SLAB // 128, min(tm, PROJ_ROWS), 128), F32))
    return pl.pallas_call(
        functools.partial(_proj_kernel, kinds=tuple(kinds)),
        grid=(n // tm, s),
        in_specs=[
            pl.BlockSpec((tm, D_MODEL), lambda i, j: (i, 0)),
            pl.BlockSpec((1, D_MODEL), lambda i, j: (0, 0)),
            pl.BlockSpec((None, D_MODEL, SLAB), lambda i, j: (j, 0, 0)),
            pl.BlockSpec((tm, 128), lambda i, j: (i % npt, 0)),
            pl.BlockSpec((tm, 128), lambda i, j: (i % npt, 0)),
        ],
        out_specs=pl.BlockSpec((None, tm, SLAB), lambda i, j: (j, i, 0)),
        out_shape=jax.ShapeDtypeStruct((s, n, SLAB), F32),
        scratch_shapes=scratch,
        compiler_params=_cparams("parallel", "arbitrary"),
        name="norm_proj",
    )(x, norm_w.reshape(1, D_MODEL), w_slabs, cos_t, sin_t)


def _even_kinds(perm):
    dil = B_DILATIONS if perm else (1,) * B_GROUPS
    return ([(0, 1)] * 4 + [(SLAB, d) for d in dil] * 2 + [(0, d) for d in dil] + [(0, 1)])


ODD_KINDS = [(0, 1)] * 3 + [(SLAB, 1), (D_KV_HEADS * HEAD_DIM, 1)]


def _rope_tables(pos):
    half = HEAD_DIM // 2
    inv = ROPE_THETA ** (-jnp.arange(half, dtype=F32) / half)
    ang = pos.astype(F32)[:, None] * inv[None, :]
    cos, sin = jnp.cos(ang), jnp.sin(ang)
    return (jnp.concatenate([cos, cos, cos, cos], axis=-1),
            jnp.concatenate([-sin, sin, -sin, sin], axis=-1))


def _pad_cols(w, width):
    return jnp.pad(w, ((0, 0), (0, width - w.shape[1])))


def _even_slabs(w_in):
    main = w_in[:, :2048].reshape(D_MODEL, 4, SLAB)
    gates = _pad_cols(w_in[:, 2048:2056], SLAB)[:, None, :]
    qkv = w_in[:, 2056:].reshape(D_MODEL, 9, SLAB)
    return jnp.moveaxis(jnp.concatenate([main, qkv, gates], axis=1), 1, 0).astype(BF16)


def _group_aligned(w, axis):
    rep = D_HEADS // D_KV_HEADS
    shape = w.shape
    split = shape[:axis] + (D_KV_HEADS, rep, HEAD_DIM) + shape[axis + 1:]
    return jnp.swapaxes(w.reshape(split), axis, axis + 1).reshape(shape)


def _odd_slabs(w_in):
    main = w_in[:, :1536].reshape(D_MODEL, 3, SLAB)
    qd = _group_aligned(w_in[:, 1544:2056], 1)[:, None, :]
    kvd = _pad_cols(jnp.concatenate([w_in[:, 2056:2312], w_in[:, 1536:1544]], axis=1), SLAB)[:, None, :]
    return jnp.moveaxis(jnp.concatenate([main, qd, kvd], axis=1), 1, 0).astype(BF16)


def _split3(x):
    hi = x.astype(BF16)
    r1 = x - hi.astype(F32)
    mid = r1.astype(BF16)
    lo = (r1 - mid.astype(F32)).astype(BF16)
    return hi, mid, lo


def _cumsum_rows(x, tri):
    hi, mid, lo = _split3(x)
    acc = jnp.dot(tri, lo, preferred_element_type=F32)
    acc = acc + jnp.dot(tri, mid, preferred_element_type=F32)
    return acc + jnp.dot(tri, hi, preferred_element_type=F32)


def _transpose_slab(s):
    l = s.shape[0]
    if l < 128:
        s = jnp.concatenate([s, jnp.zeros((128 - l, 128), s.dtype)], axis=0)
    return s.T


def _conv_silu(buf_ref, x, w, b):
    tb = x.shape[0]
    buf_ref[8:8 + tb, :] = x
    y = b + w[3:4] * x
    for j in range(3):
        y = y + w[j:j + 1] * buf_ref[5 + j:5 + j + tb, :]
    buf_ref[0:8, :] = buf_ref[tb:tb + 8, :]
    return y * jax.nn.sigmoid(y)


def _log_sigmoid(x):
    return jnp.minimum(x, 0.0) - jnp.log1p(jnp.exp(-jnp.abs(x)))


def _dot_t(a, b):
    return lax.dot_general(a, b, (((1,), (1,)), ((), ())), preferred_element_type=F32)


def _tdot(a, b):
    return lax.dot_general(a, b, (((0,), (0,)), ((), ())), preferred_element_type=F32)


def _mlstm_kernel(q_ref, k_ref, v_ref, og_ref, g_ref, cw_ref, cb_ref, bg_ref, an_ref,
                  conv0_ref, c0_ref, n0_ref, m0_ref,
                  ya_ref, c_ref, n_ref, m_ref, qbuf, kbuf, *, chunk):
    sb, tb, _ = q_ref.shape
    tblk = pl.program_id(1)
    lane = lax.broadcasted_iota(jnp.int32, (chunk, 128), 1)
    lane1 = lax.broadcasted_iota(jnp.int32, (1, 128), 1)
    row = lax.broadcasted_iota(jnp.int32, (chunk, chunk), 0)
    col = lax.broadcasted_iota(jnp.int32, (chunk, chunk), 1)
    causal = col <= row
    tri = causal.astype(BF16)
    scale = A_DK ** -0.5

    @pl.when(tblk == 0)
    def _():
        c_ref[...] = c0_ref[...]
        n_ref[...] = n0_ref[...]
        m_ref[...] = m0_ref[...]
        for si in range(sb):
            qbuf[si, 0:8, :] = conv0_ref[si, :, 0:SLAB]
            kbuf[si, 0:8, :] = conv0_ref[si, :, SLAB:2 * SLAB]

    nchunk = tb // chunk
    seqs, chunks, heads = range(sb), range(nchunk), range(A_HEADS)
    rows = lambda c: slice(c * chunk, (c + 1) * chunk)
    cols = lambda h: slice(h * 128, (h + 1) * 128)
    inst = [(si, c, h) for si in seqs for c in chunks for h in heads]
    qc = [_conv_silu(qbuf.at[si], q_ref[si], cw_ref[:, 0:SLAB], cb_ref[:, 0:SLAB]) for si in seqs]
    kc = [_conv_silu(kbuf.at[si], k_ref[si], cw_ref[:, SLAB:2 * SLAB], cb_ref[:, SLAB:2 * SLAB]) * scale
          for si in seqs]
    gates = {(si, c): g_ref[si, rows(c), 0:128] + bg_ref[...] for si in seqs for c in chunks}
    bc = {k: _cumsum_rows(_log_sigmoid(g), tri) for k, g in gates.items()}
    st = {k: _transpose_slab(jnp.where(lane < A_HEADS, gates[k], bc[k])) for k in gates}
    qh = {(si, c, h): qc[si][rows(c), cols(h)] for si, c, h in inst}
    kh = {(si, c, h): kc[si][rows(c), cols(h)] for si, c, h in inst}
    qb = {k: v.astype(BF16) for k, v in qh.items()}
    vb = {(si, c, h): v_ref[si, rows(c), cols(h)].astype(BF16) for si, c, h in inst}
    b_t = {(si, c, h): bc[si, c][:, A_HEADS + h:A_HEADS + h + 1] for si, c, h in inst}
    dlog = {(si, c, h): jnp.where(causal, b_t[si, c, h] - st[si, c][A_HEADS + h:A_HEADS + h + 1, 0:chunk]
                                  + st[si, c][h:h + 1, 0:chunk], NEG) for si, c, h in inst}
    m_loc = {k: jnp.max(v, axis=1, keepdims=True) for k, v in dlog.items()}
    qk = {k: _dot_t(qb[k], kh[k].astype(BF16)) for k in inst}
    s_loc = {k: qk[k] * jnp.exp(dlog[k] - m_loc[k]) for k in inst}
    sv = {k: jnp.dot(s_loc[k].astype(BF16), vb[k], preferred_element_type=F32) for k in inst}
    rs = {k: jnp.sum(s_loc[k], axis=1, keepdims=True) for k in inst}
    b_end = {k: b_t[k][chunk - 1:chunk, :] for k in inst}
    g_col = {(si, c, h): b_end[si, c, h] - b_t[si, c, h] + gates[si, c][:, h:h + 1] for si, c, h in inst}
    g_max = {k: jnp.max(v, axis=0, keepdims=True) for k, v in g_col.items()}
    kw = {k: kh[k] * jnp.exp(g_col[k] - g_max[k]) for k in inst}
    u = {k: _tdot(kw[k].astype(BF16), vb[k]) for k in inst}
    nk = {k: jnp.sum(kw[k], axis=0, keepdims=True) for k in inst}
    c_st = {(si, h): c_ref[si, h] for si in seqs for h in heads}
    n_st = {(si, h): n_ref[si, h:h + 1, :] for si in seqs for h in heads}
    m_h = {(si, h): m_ref[si][:, h:h + 1] for si in seqs for h in heads}
    for c in chunks:
        cur = [(si, c, h) for si in seqs for h in heads]
        inter = {k: b_t[k] + m_h[k[0], k[2]] for k in cur}
        m_t = {k: jnp.maximum(inter[k], m_loc[k]) for k in cur}
        w_loc = {k: jnp.exp(m_loc[k] - m_t[k]) for k in cur}
        w_inter = {k: jnp.exp(inter[k] - m_t[k]) for k in cur}
        qc_st = {k: jnp.dot(qb[k], c_st[k[0], k[2]].astype(BF16), preferred_element_type=F32) for k in cur}
        qn = {k: jnp.sum(qh[k] * n_st[k[0], k[2]], axis=1, keepdims=True) for k in cur}
        num = {k: w_loc[k] * sv[k] + w_inter[k] * qc_st[k] for k in cur}
        den = {k: w_loc[k] * rs[k] + w_inter[k] * qn[k] for k in cur}
        hh = {k: num[k] / jnp.maximum(jnp.abs(den[k]), jnp.exp(-m_t[k])) for k in cur}
        m_new = {k: jnp.maximum(b_end[k] + m_h[k[0], k[2]], g_max[k]) for k in cur}
        decay = {k: jnp.exp(b_end[k] + m_h[k[0], k[2]] - m_new[k]) for k in cur}
        gain = {k: jnp.exp(g_max[k] - m_new[k]) for k in cur}
        for k in cur:
            sh = (k[0], k[2])
            c_st[sh] = decay[k] * c_st[sh] + gain[k] * u[k]
            n_st[sh] = decay[k] * n_st[sh] + gain[k] * nk[k]
            m_h[sh] = m_new[k]
        ms = {k: jnp.mean(hh[k] * hh[k], axis=-1, keepdims=True) for k in cur}
        for si, _, h in cur:
            hn = hh[si, c, h] * lax.rsqrt(ms[si, c, h] + EPS) * an_ref[:, cols(h)]
            ya_ref[si, rows(c), cols(h)] = (hn * jax.nn.sigmoid(og_ref[si, rows(c), cols(h)])).astype(ya_ref.dtype)
    for si in seqs:
        m_vec = m_ref[si]
        for h in heads:
            c_ref[si, h] = c_st[si, h]
            n_ref[si, h:h + 1, :] = n_st[si, h]
            m_vec = jnp.where(lane1 == h, m_h[si, h], m_vec)
        m_ref[si] = m_vec


def _mlstm(proj, nseq, t, conv_w, conv_b, b_gates, a_norm, conv0, c0, n0, m0):
    chunk = A_CHUNK if t % A_CHUNK == 0 else t
    if t > chunk:
        sb, tb = min(nseq, 2), min(t, 8 * chunk)
    else:
        sb, tb = min(nseq, 8), t
    p4 = proj.reshape(proj.shape[0], nseq, t, SLAB)
    slab = lambda s: pl.BlockSpec((None, sb, tb, SLAB), lambda i, j, s=s: (s, i, j, 0))
    full = lambda a: pl.BlockSpec(a.shape, lambda i, j: (0,) * a.ndim)
    st = lambda a: pl.BlockSpec((sb,) + a.shape[1:], lambda i, j: (i,) + (0,) * (a.ndim - 1))
    bg = _pad_cols(b_gates.reshape(1, 2 * A_HEADS), 128)
    conv0p = jnp.pad(conv0, ((0, 0), (8 - (A_CONV - 1), 0), (0, 0)))
    m0p = _pad_cols(m0, 128).reshape(nseq, 1, 128)
    cb = conv_b.reshape(1, -1)
    an = a_norm.reshape(1, -1)
    ya, c, n, m = pl.pallas_call(
        functools.partial(_mlstm_kernel, chunk=chunk),
        grid=(nseq // sb, t // tb),
        in_specs=[slab(0), slab(1), slab(2), slab(3), slab(13),
                  full(conv_w), full(cb), full(bg), full(an),
                  st(conv0p), st(c0), st(n0), st(m0p)],
        out_specs=[pl.BlockSpec((sb, tb, SLAB), lambda i, j: (i, j, 0)), st(c0), st(n0), st(m0p)],
        out_shape=[jax.ShapeDtypeStruct((nseq, t, SLAB), _mix_dtype(t)),
                   jax.ShapeDtypeStruct(c0.shape, F32), jax.ShapeDtypeStruct(n0.shape, F32),
                   jax.ShapeDtypeStruct(m0p.shape, F32)],
        scratch_shapes=[pltpu.VMEM((sb, 8 + tb, SLAB), F32), pltpu.VMEM((sb, 8 + tb, SLAB), F32)],
        compiler_params=_cparams("parallel", "arbitrary"),
        name="mlstm",
    )(p4, p4, p4, p4, p4, conv_w, cb, bg, an, conv0p, c0, n0, m0p)
    return ya.reshape(nseq * t, SLAB), c, n, m[:, 0, :A_HEADS]


QBLK = 128


def _band_heads(q_m, k2, v2, valid, sinks):
    n = range(len(q_m))
    s = [jnp.where(valid[i], _dot_t(q_m[i], k2[i]), NEG) for i in n]
    m = [jnp.max(s[i], axis=1, keepdims=True) for i in n]
    if sinks is not None:
        m = [jnp.maximum(m[i], sinks[i]) for i in n]
    e = [jnp.exp(s[i] - m[i]) for i in n]
    l = [jnp.sum(e[i], axis=1, keepdims=True) for i in n]
    if sinks is not None:
        l = [l[i] + jnp.exp(sinks[i] - m[i]) for i in n]
    acc = [jnp.dot(e[i].astype(BF16), v2[i], preferred_element_type=F32) for i in n]
    return acc, m, l


def _band_masks(has_prev, window):
    a = lax.broadcasted_iota(jnp.int32, (QBLK, 2 * QBLK), 0)
    c = lax.broadcasted_iota(jnp.int32, (QBLK, 2 * QBLK), 1)
    rel = a - c + QBLK
    band = jnp.logical_and(rel >= 0, rel <= window)
    if has_prev is True:
        return band
    return jnp.logical_and(band, c >= jnp.where(has_prev, 0, QBLK))


def _key_rows(prev_ref, cur_ref, a, ls):
    if prev_ref.shape[0] == cur_ref.shape[0]:
        before = prev_ref[a * QBLK:(a + 1) * QBLK, ls]
    else:
        before = prev_ref[:, ls] if a == 0 else cur_ref[(a - 1) * QBLK:a * QBLK, ls]
    return jnp.concatenate([before, cur_ref[a * QBLK:(a + 1) * QBLK, ls]], axis=0).astype(BF16)


def _dilated_kernel(q_ref, kp_ref, kc_ref, vp_ref, vc_ref, o_ref, lse_ref):
    lane = lax.broadcasted_iota(jnp.int32, (1, 128), 1)
    low = lane < HEAD_DIM
    scale = HEAD_DIM ** -0.5
    chained = kp_ref.shape[0] != kc_ref.shape[0]
    subs = range(q_ref.shape[0] // QBLK)
    pairs = range(B_HEADS // 2)
    rows = lambda a: slice(a * QBLK, (a + 1) * QBLK)
    cols = lambda pr: slice(pr * 128, (pr + 1) * 128)
    valid = [_band_masks(True if (chained and a > 0) else pl.program_id(2) > 0, QBLK) for a in subs]
    qp = {(a, pr): q_ref[rows(a), cols(pr)] * scale for a in subs for pr in pairs}
    k2 = {(a, pr): _key_rows(kp_ref, kc_ref, a, cols(pr)) for a in subs for pr in pairs}
    v2 = {(a, pr): _key_rows(vp_ref, vc_ref, a, cols(pr)) for a in subs for pr in pairs}
    inst = [(a, pr, p) for a in subs for pr in pairs for p in range(2)]
    acc, m, l = _band_heads(
        [jnp.where(low if p == 0 else jnp.logical_not(low), qp[a, pr], 0.0).astype(BF16) for a, pr, p in inst],
        [k2[a, pr] for a, pr, _ in inst], [v2[a, pr] for a, pr, _ in inst], [valid[a] for a, _, _ in inst], None)
    o_h = [acc[i] / l[i] for i in range(len(inst))]
    lse_h = [m[i] + jnp.log(l[i]) for i in range(len(inst))]
    for i in range(0, len(inst), 2):
        a, pr, _ = inst[i]
        o_ref[rows(a), cols(pr)] = jnp.where(low, o_h[i], o_h[i + 1])
        lse_ref[rows(a), cols(pr)] = jnp.where(low, lse_h[i], lse_h[i + 1])


def _dilated_prompt(proj, b, t, g):
    d = B_DILATIONS[g]
    assert t % PERM_TILE == 0
    nb = PERM_TILE // d // QBLK
    per_tile = PERM_TILE // QBLK
    pv = proj.reshape(proj.shape[0], b, t, SLAB)

    def blk(i, r):
        return (i // nb) * per_tile + r * nb + i % nb

    if nb % 2 == 0:
        nq, classes, steps = 2, d, t // d // QBLK // 2
        cur_idx = lambda i, r: blk(2 * i, r) // 2
        prev_rows, prev_idx = QBLK, lambda i, r: blk(jnp.maximum(2 * i - 1, 0), r)
    elif nb == 1 and d % 2 == 0:
        nq, classes, steps = 2, d // 2, t // d // QBLK
        cur_idx = lambda i, r: blk(i, 2 * r) // 2
        prev_rows, prev_idx = 2 * QBLK, lambda i, r: blk(jnp.maximum(i - 1, 0), 2 * r) // 2
    else:
        nq, classes, steps = 1, d, t // d // QBLK
        cur_idx = blk
        prev_rows, prev_idx = QBLK, lambda i, r: blk(jnp.maximum(i - 1, 0), r)
    cur = lambda s: pl.BlockSpec((None, None, nq * QBLK, SLAB), lambda bi, r, i, s=s: (s, bi, cur_idx(i, r), 0))
    prev = lambda s: pl.BlockSpec((None, None, prev_rows, SLAB), lambda bi, r, i, s=s: (s, bi, prev_idx(i, r), 0))
    ospec = pl.BlockSpec((None, nq * QBLK, SLAB), lambda bi, r, i: (bi, cur_idx(i, r), 0))
    sq, sk, sv = 4 + g, 7 + g, 10 + g
    o, lse = pl.pallas_call(
        _dilated_kernel,
        grid=(b, classes, steps),
        in_specs=[cur(sq), prev(sk), cur(sk), prev(sv), cur(sv)],
        out_specs=[ospec, ospec],
        out_shape=[jax.ShapeDtypeStruct((b, t, SLAB), F32)] * 2,
        compiler_params=_cparams("parallel", "parallel", "arbitrary"),
        name=f"dilated_g{g}",
    )(pv, pv, pv, pv, pv)
    return o.reshape(b * t, SLAB), lse.reshape(b * t, SLAB)


def _tails_kernel(x_ref, o_ref, *buf, dil):
    tm = x_ref.shape[0]
    keep = o_ref.shape[1]
    if dil == 1:
        o_ref[...] = x_ref[tm - keep:tm, :].T
    else:
        seg = tm // dil
        for r in range(dil):
            buf[0][pl.ds(r, seg, stride=dil), :] = x_ref[r * seg:(r + 1) * seg, :]
        o_ref[...] = buf[0][tm - keep:tm, :].T


def _cache_tails(proj, nseq, t, g, keep):
    dil = B_DILATIONS[g]
    pv = proj.reshape(proj.shape[0], nseq, t, SLAB)
    rows = keep if dil == 1 else PERM_TILE
    last = t // rows - 1
    out = pl.pallas_call(
        functools.partial(_tails_kernel, dil=dil),
        grid=(2, nseq, SLAB // 128),
        in_specs=[pl.BlockSpec((None, None, rows, 128), lambda kv, b, cb: (7 + g + B_GROUPS * kv, b, last, cb))],
        out_specs=pl.BlockSpec((None, None, 128, keep), lambda kv, b, cb: (kv, b, cb, 0)),
        out_shape=jax.ShapeDtypeStruct((2, nseq, SLAB, keep), F32),
        scratch_shapes=[pltpu.VMEM((PERM_TILE, 128), F32)] if dil > 1 else [],
        compiler_params=_cparams("parallel", "parallel", "parallel"),
        name=f"cache_tails_g{g}",
    )(pv)
    return _rows_major(out[0], B_HEADS), _rows_major(out[1], B_HEADS)


def _merge_kernel(*refs, dils):
    ins, y_ref, bufs = refs[:6], refs[6], refs[7:]
    tm = y_ref.shape[0]
    vals = []
    nbuf = 0
    for k, ref in enumerate(ins):
        dil = dils[k // 2]
        if dil == 1:
            vals.append(ref[...])
        else:
            buf = bufs[nbuf]
            nbuf += 1
            seg = tm // dil
            for r in range(dil):
                buf[pl.ds(r, seg, stride=dil), :] = ref[r * seg:(r + 1) * seg, :]
            vals.append(buf[...])
    o1, l1, o2, l2, o3, l3 = vals
    m = jnp.maximum(jnp.maximum(l1, l2), l3)
    w1, w2, w3 = jnp.exp(l1 - m), jnp.exp(l2 - m), jnp.exp(l3 - m)
    y_ref[...] = ((w1 * o1 + w2 * o2 + w3 * o3) / (w1 + w2 + w3)).astype(y_ref.dtype)


def _merge_groups(parts, perm):
    n = parts[0].shape[0]
    dils = B_DILATIONS if perm else (1,) * B_GROUPS
    tm = PERM_TILE if perm else min(1024, n)
    spec = pl.BlockSpec((tm, 128), lambda i, j: (i, j))
    nbuf = 2 * sum(d > 1 for d in dils)
    return pl.pallas_call(
        functools.partial(_merge_kernel, dils=dils), grid=(n // tm, SLAB // 128), in_specs=[spec] * 6,
        out_specs=spec, out_shape=jax.ShapeDtypeStruct((n, SLAB), _mix_dtype(tm)),
        scratch_shapes=[pltpu.VMEM((tm, 128), F32)] * nbuf,
        compiler_params=_cparams("parallel", "parallel"), name="merge_groups",
    )(*parts)


def _swa_kernel(q_ref, kp_ref, kc_ref, vp_ref, vc_ref, sink_ref, o_ref):
    lane = lax.broadcasted_iota(jnp.int32, (1, 128), 1)
    scale = HEAD_DIM ** -0.5
    rep = D_HEADS // D_KV_HEADS
    low = lane < HEAD_DIM
    allk = slice(None)
    subs = range(q_ref.shape[0] // QBLK)
    rows = lambda a: slice(a * QBLK, (a + 1) * QBLK)
    cols = lambda pr: slice(pr * 128, (pr + 1) * 128)
    valid = [_band_masks(pl.program_id(1) > 0 if a == 0 else True, D_WINDOW - 1) for a in subs]
    k2 = [_key_rows(kp_ref, kc_ref, a, allk) for a in subs]
    v2 = [_key_rows(vp_ref, vc_ref, a, allk) for a in subs]
    qp = {(a, pr): q_ref[rows(a), cols(pr)] * scale for a in subs for pr in range(rep)}
    inst = [(a, pr, g) for a in subs for pr in range(rep) for g in range(D_KV_HEADS)]
    acc, m, l = _band_heads(
        [jnp.where(low if g == 0 else jnp.logical_not(low), qp[a, pr], 0.0).astype(BF16) for a, pr, g in inst],
        [k2[a] for a, _, _ in inst], [v2[a] for a, _, _ in inst], [valid[a] for a, _, _ in inst],
        [sink_ref[g * rep + pr] for _, pr, g in inst])
    o_h = [acc[i] / l[i] for i in range(len(inst))]
    for i in range(0, len(inst), 2):
        a, pr, _ = inst[i]
        o_ref[rows(a), cols(pr)] = jnp.where(low, o_h[i], o_h[i + 1]).astype(o_ref.dtype)


def _swa_prompt(proj, b, t, sinks):
    pv = proj.reshape(proj.shape[0], b, t, SLAB)
    nq = 2 if t % (2 * QBLK) == 0 else 1
    qspec = pl.BlockSpec((None, None, nq * QBLK, SLAB), lambda bi, i: (3, bi, i, 0))

    def kv(col, prev):
        if prev:
            return pl.BlockSpec((None, None, QBLK, 128), lambda bi, i: (4, bi, jnp.maximum(nq * i - 1, 0), col))
        return pl.BlockSpec((None, None, nq * QBLK, 128), lambda bi, i: (4, bi, i, col))

    o = pl.pallas_call(
        _swa_kernel,
        grid=(b, t // QBLK // nq),
        in_specs=[qspec, kv(0, True), kv(0, False), kv(1, True), kv(1, False),
                  pl.BlockSpec(memory_space=pltpu.SMEM)],
        out_specs=pl.BlockSpec((None, nq * QBLK, SLAB), lambda bi, i: (bi, i, 0)),
        out_shape=jax.ShapeDtypeStruct((b, t, SLAB), _mix_dtype(t)),
        compiler_params=_cparams("parallel", "arbitrary"),
        name="swa_prompt",
    )(pv, pv, pv, pv, pv, sinks)
    return o.reshape(b * t, SLAB)


def _softplus(x):
    return jnp.maximum(x, 0.0) + jnp.log1p(jnp.exp(-jnp.abs(x)))


def _ssd_kernel(z_ref, x_ref, bc_ref, dt_ref, cw_ref, cb_ref, dtb_ref, alog_ref, dl_ref, cn_ref,
                conv0_ref, h0_ref, y_ref, h_ref, xbuf, bcbuf, *, chunk):
    sb, tb, _ = x_ref.shape
    tblk = pl.program_id(1)
    lane1 = lax.broadcasted_iota(jnp.int32, (1, 128), 1)
    low = lane1 < C_HEADDIM
    row = lax.broadcasted_iota(jnp.int32, (chunk, chunk), 0)
    col = lax.broadcasted_iota(jnp.int32, (chunk, chunk), 1)
    causal = col <= row
    tri = causal.astype(BF16)
    srow_low = lax.broadcasted_iota(jnp.int32, (128, 1), 0) < C_HEADDIM
    hpg = C_HEADS // C_NGROUPS
    a_neg = -jnp.exp(alog_ref[...])

    @pl.when(tblk == 0)
    def _():
        h_ref[...] = h0_ref[...]
        for si in range(sb):
            xbuf[si, 0:8, :] = conv0_ref[si, :, 0:SLAB]
            bcbuf[si, 0:8, :] = conv0_ref[si, :, SLAB:2 * SLAB]

    seqs, chunks = range(sb), range(tb // chunk)
    ppg = hpg // 2
    npair = C_HEADS // 2
    rows = lambda c: slice(c * chunk, (c + 1) * chunk)
    cols = lambda pr: slice(pr * 128, (pr + 1) * 128)
    sc = [(si, c) for si in seqs for c in chunks]
    sch = [(si, c, h) for si, c in sc for h in range(C_HEADS)]
    scp = [(si, c, pr) for si, c in sc for pr in range(npair)]
    xc = [_conv_silu(xbuf.at[si], x_ref[si], cw_ref[:, 0:SLAB], cb_ref[:, 0:SLAB]) for si in seqs]
    bcc = [_conv_silu(bcbuf.at[si], bc_ref[si], cw_ref[:, SLAB:2 * SLAB], cb_ref[:, SLAB:2 * SLAB]) for si in seqs]
    dt = {(si, c): _softplus(dt_ref[si, rows(c), :] + dtb_ref[...]) for si, c in sc}
    cum = {k: _cumsum_rows(v * a_neg, tri) for k, v in dt.items()}
    cum_rows = {k: _transpose_slab(v) for k, v in cum.items()}
    dt_rows = {k: _transpose_slab(v) for k, v in dt.items()}
    bgm = {(si, c, g): bcc[si][rows(c), g * 128:(g + 1) * 128].astype(BF16) for si, c in sc
           for g in range(C_NGROUPS)}
    cgm = {(si, c, g): bcc[si][rows(c), 256 + g * 128:256 + (g + 1) * 128].astype(BF16) for si, c in sc
           for g in range(C_NGROUPS)}
    cbm = {k: _dot_t(cgm[k], bgm[k]) for k in bgm}
    x_pair = {(si, c, pr): xc[si][rows(c), cols(pr)] for si, c, pr in scp}
    cum_t = {(si, c, h): cum[si, c][:, h:h + 1] for si, c, h in sch}
    cum_e = {(si, c, h): cum[si, c][chunk - 1:chunk, h:h + 1] for si, c, h in sch}
    seg = {(si, c, h): jnp.where(causal, cum_t[si, c, h] - cum_rows[si, c][h:h + 1, 0:chunk], NEG)
           for si, c, h in sch}
    w = {(si, c, h): cbm[si, c, h // hpg] * jnp.exp(seg[si, c, h]) * dt_rows[si, c][h:h + 1, 0:chunk]
         for si, c, h in sch}
    xhalf = {(si, c, h): jnp.where(low if h % 2 == 0 else jnp.logical_not(low), x_pair[si, c, h // 2],
                                   0.0).astype(BF16) for si, c, h in sch}
    yh = {k: jnp.dot(w[k].astype(BF16), xhalf[k], preferred_element_type=F32) for k in sch}
    e_col = {k: jnp.exp(cum_t[k]) for k in sch}
    w_end = {(si, c, h): jnp.exp(cum_e[si, c, h] - cum_t[si, c, h]) * dt[si, c][:, h:h + 1] for si, c, h in sch}
    dec = {k: jnp.exp(cum_e[k]) for k in sch}
    y_intra = {(si, c, pr): yh[si, c, 2 * pr] + yh[si, c, 2 * pr + 1] + dl_ref[:, cols(pr)] * x_pair[si, c, pr]
               for si, c, pr in scp}
    xw = {(si, c, pr): x_pair[si, c, pr] * jnp.where(low, w_end[si, c, 2 * pr], w_end[si, c, 2 * pr + 1])
          for si, c, pr in scp}
    upd = {(si, c, pr): _tdot(xw[si, c, pr].astype(BF16), bgm[si, c, pr // ppg]) for si, c, pr in scp}
    e_sel = {(si, c, pr): jnp.where(low, e_col[si, c, 2 * pr], e_col[si, c, 2 * pr + 1]) for si, c, pr in scp}
    d_sel = {(si, c, pr): jnp.where(srow_low, dec[si, c, 2 * pr], dec[si, c, 2 * pr + 1]) for si, c, pr in scp}
    gate = {}
    for si, c, pr in scp:
        zz = z_ref[si, rows(c), cols(pr)]
        gate[si, c, pr] = zz * jax.nn.sigmoid(zz)
    hs = {(si, pr): h_ref[si, pr] for si in seqs for pr in range(npair)}
    for c in chunks:
        cur = [(si, c, pr) for si in seqs for pr in range(npair)]
        inter = {k: _dot_t(cgm[k[0], c, k[2] // ppg], hs[k[0], k[2]].astype(BF16)) * e_sel[k] for k in cur}
        for k in cur:
            hs[k[0], k[2]] = d_sel[k] * hs[k[0], k[2]] + upd[k]
        yz = {k: (y_intra[k] + inter[k]) * gate[k] for k in cur}
        sq = {k: jnp.sum(yz[k] * yz[k], axis=-1, keepdims=True) for k in cur}
        for si in seqs:
            for g in range(C_NGROUPS):
                prs = range(g * ppg, (g + 1) * ppg)
                inv = lax.rsqrt(sum(sq[si, c, pr] for pr in prs) / (hpg * C_HEADDIM) + EPS)
                for pr in prs:
                    y_ref[si, rows(c), cols(pr)] = (yz[si, c, pr] * inv * cn_ref[:, cols(pr)]).astype(y_ref.dtype)
    for si in seqs:
        for pr in range(npair):
            h_ref[si, pr] = hs[si, pr]


def _ssd(proj, nseq, t, conv_w, conv_b, dt_bias, a_log, d_skip, c_norm, conv0, h0):
    chunk = C_CHUNK if t % C_CHUNK == 0 else t
    if t > chunk:
        sb, tb = min(nseq, 2), min(t, 4 * chunk)
    else:
        sb, tb = min(nseq, 8), t
    p4 = proj.reshape(proj.shape[0], nseq, t, SLAB)
    slab = lambda s: pl.BlockSpec((None, sb, tb, SLAB), lambda i, j, s=s: (s, i, j, 0))
    full = lambda a: pl.BlockSpec(a.shape, lambda i, j: (0,) * a.ndim)
    st = lambda a: pl.BlockSpec((sb,) + a.shape[1:], lambda i, j: (i,) + (0,) * (a.ndim - 1))
    conv0p = jnp.pad(conv0, ((0, 0), (8 - (C_CONV - 1), 0), (0, 0)))
    hp = h0.reshape(nseq, C_HEADS // 2, 128, C_DSTATE)
    cb = conv_b.reshape(1, -1)
    dtb = _pad_cols(dt_bias.reshape(1, C_HEADS), 128)
    alog = _pad_cols(a_log.reshape(1, C_HEADS), 128)
    dl = jnp.repeat(d_skip, C_HEADDIM).reshape(1, C_DINNER)
    cn = c_norm.reshape(1, C_DINNER)
    y, h = pl.pallas_call(
        functools.partial(_ssd_kernel, chunk=chunk),
        grid=(nseq // sb, t // tb),
        in_specs=[slab(0), slab(1), slab(2),
                  pl.BlockSpec((None, sb, tb, 128), lambda i, j: (4, i, j, 2)),
                  full(conv_w), full(cb), full(dtb), full(alog), full(dl), full(cn), st(conv0p), st(hp)],
        out_specs=[pl.BlockSpec((sb, tb, SLAB), lambda i, j: (i, j, 0)), st(hp)],
        out_shape=[jax.ShapeDtypeStruct((nseq, t, SLAB), _mix_dtype(t)), jax.ShapeDtypeStruct(hp.shape, F32)],
        scratch_shapes=[pltpu.VMEM((sb, 8 + tb, SLAB), F32), pltpu.VMEM((sb, 8 + tb, SLAB), F32)],
        compiler_params=_cparams("parallel", "arbitrary"),
        name="ssd",
    )(p4, p4, p4, p4, conv_w, cb, dtb, alog, dl, cn, conv0p, hp)
    return y.reshape(nseq * t, SLAB), h.reshape(nseq, C_HEADS, C_HEADDIM, C_DSTATE)


FFN_CHUNK = 256


def _post_kernel(x_ref, ya_ref, yb_ref, wo_ref, fn_ref, wg_ref, wu_ref, wd_ref, on_ref, o_ref, act_ref, *, final):
    ymix = jnp.concatenate([ya_ref[...], yb_ref[...]], axis=-1).astype(BF16)
    x1 = x_ref[...] + jnp.dot(ymix, wo_ref[...], preferred_element_type=F32)
    ms = jnp.mean(x1 * x1, axis=-1, keepdims=True)
    hn = (x1 * lax.rsqrt(ms + EPS) * fn_ref[...]).astype(BF16)
    hidden = wg_ref.shape[1]
    for c in range(hidden // FFN_CHUNK):
        cs = slice(c * FFN_CHUNK, (c + 1) * FFN_CHUNK)
        gate = jnp.dot(hn, wg_ref[:, cs], preferred_element_type=F32)
        up = jnp.dot(hn, wu_ref[:, cs], preferred_element_type=F32)
        act_ref[:, cs] = (gate * jax.nn.sigmoid(gate) * up).astype(BF16)
    x2 = x1 + jnp.dot(act_ref[...], wd_ref[...], preferred_element_type=F32)
    if final:
        ms2 = jnp.mean(x2 * x2, axis=-1, keepdims=True)
        x2 = x2 * lax.rsqrt(ms2 + EPS) * on_ref[...]
    o_ref[...] = x2


def _post(x, ya, yb, w_out, layer, ffn_norm, w_gate, w_up, w_down, out_norm, final):
    n = x.shape[0]
    tm = min(1024, n)
    hidden = w_gate.shape[2]
    row = lambda w: pl.BlockSpec((tm, w), lambda i: (i, 0))
    res = lambda a: pl.BlockSpec(a.shape, lambda i: (0, 0), pipeline_mode=pl.Buffered(1))
    lay = lambda a: pl.BlockSpec((None,) + a.shape[1:], lambda i: (layer, 0, 0), pipeline_mode=pl.Buffered(1))
    fn = ffn_norm.reshape(-1, 1, D_MODEL)
    on = out_norm.reshape(1, D_MODEL)
    return pl.pallas_call(
        functools.partial(_post_kernel, final=final),
        grid=(n // tm,),
        in_specs=[row(D_MODEL), row(SLAB), row(SLAB), res(w_out), lay(fn), lay(w_gate), lay(w_up), lay(w_down),
                  res(on)],
        out_specs=row(D_MODEL),
        out_shape=jax.ShapeDtypeStruct((n, D_MODEL), F32),
        scratch_shapes=[pltpu.VMEM((tm, hidden), BF16)],
        compiler_params=_cparams("parallel"),
        name="post_ffn",
    )(x, ya, yb, w_out, fn, w_gate, w_up, w_down, on)


RIDER_PHASES = 4


def _rider_mask(ts, r, window, step):
    t = lax.broadcasted_iota(jnp.int32, (2 * ts, r + 128), 0) % ts
    c = lax.broadcasted_iota(jnp.int32, (2 * ts, r + 128), 1)
    rel = r + t - jnp.where(c < 128, c, c - 128 + ts)
    valid = jnp.logical_and(jnp.logical_and(rel >= 0, rel <= window), (rel & (step - 1)) == 0)
    return jnp.logical_and(valid, jnp.logical_or(c < ts, c >= 128))


def _shift_rows_minor(src_ref, new_t, rows, ts):
    r = src_ref.shape[1]
    lane = lax.broadcasted_iota(jnp.int32, (1, 128), 1)
    old = src_ref[rows, :]
    rolled = pltpu.roll(old, r - ts, 1)
    tail = jnp.where(lane < 128 - ts, rolled[:, r - 128:], new_t[rows, :])
    out = tail if r == 128 else jnp.concatenate([rolled[:, :r - 128], tail], axis=1)
    return old, out


def _new_rows_t(new_ref):
    ts = new_ref.shape[0]
    pad = jnp.zeros((128 - ts, SLAB), F32)
    return jnp.concatenate([pad, new_ref[...]], axis=0).T


def _rider_k(q_ref, kn_ref, kc_ref, ko_ref, p_ref, lse_ref, window, step):
    ts = q_ref.shape[0]
    r = kc_ref.shape[1]
    low = lax.broadcasted_iota(jnp.int32, (1, 128), 1) < HEAD_DIM
    valid = _rider_mask(ts, r, window, step)
    kn_t = _new_rows_t(kn_ref)
    scale = HEAD_DIM ** -0.5
    pairs = range(B_HEADS // 2)
    rows = lambda pr: slice(pr * 128, (pr + 1) * 128)
    s = []
    for pr in pairs:
        k_old, k_out = _shift_rows_minor(kc_ref, kn_t, rows(pr), ts)
        ko_ref[rows(pr), :] = k_out
        k_ext = jnp.concatenate([k_old[:, 0:128], k_out], axis=1).astype(BF16)
        qp = q_ref[:, rows(pr)] * scale
        q2 = jnp.concatenate([jnp.where(low, qp, 0.0), jnp.where(low, 0.0, qp)], axis=0).astype(BF16)
        s.append(jnp.where(valid, jnp.dot(q2, k_ext, preferred_element_type=F32), NEG))
    m = [jnp.max(s[pr], axis=1, keepdims=True) for pr in pairs]
    e = [jnp.exp(s[pr] - m[pr]) for pr in pairs]
    l = [jnp.sum(e[pr], axis=1, keepdims=True) for pr in pairs]
    for pr in pairs:
        p_ref[pr * 2 * ts:(pr + 1) * 2 * ts, :] = (e[pr] / l[pr]).astype(BF16)
        lse = m[pr] + jnp.log(l[pr])
        lse_ref[:, rows(pr)] = jnp.where(low, lse[0:ts], lse[ts:2 * ts])


def _rider_v(p_ref, vn_ref, vc_ref, vo_ref, o_ref):
    ts = vn_ref.shape[0]
    low = lax.broadcasted_iota(jnp.int32, (1, 128), 1) < HEAD_DIM
    vn_t = _new_rows_t(vn_ref)
    for pr in range(B_HEADS // 2):
        rows = slice(pr * 128, (pr + 1) * 128)
        v_old, v_out = _shift_rows_minor(vc_ref, vn_t, rows, ts)
        vo_ref[rows, :] = v_out
        v_ext = jnp.concatenate([v_old[:, 0:128], v_out], axis=1).astype(BF16)
        a = _dot_t(p_ref[pr * 2 * ts:(pr + 1) * 2 * ts, :], v_ext)
        o_ref[:, rows] = jnp.where(low, a[0:ts], a[ts:2 * ts])


def _post_rider_kernel(x_ref, ya_ref, yb_ref, wo_ref, fn_ref, wg_ref, wu_ref, wd_ref, on_ref, *rest,
                       final, side, groups):
    ng = len(groups)
    nout = 3 if side == 'k' else 2
    ins, o_ref, outs = rest[:3 * ng], rest[3 * ng], rest[3 * ng + 1:3 * ng + 1 + nout * ng]
    act_ref, hn_ref = rest[3 * ng + 1 + nout * ng:]
    x1_ref = o_ref

    def ride():
        for k, g in enumerate(groups):
            i3, o3 = ins[3 * k:3 * k + 3], outs[nout * k:nout * k + nout]
            if side == 'k':
                _rider_k(*i3, *o3, B_WINDOWS[g], B_DILATIONS[g])
            else:
                _rider_v(*i3, *o3)

    u = pl.program_id(1)
    nchunk = wg_ref.shape[1] // FFN_CHUNK
    per = -(-nchunk // (RIDER_PHASES - 1))
    first = nchunk - per * (RIDER_PHASES - 2)

    def hidden_chunks(lo, hi):
        hn = hn_ref[...]
        for c in range(lo, hi):
            cs = slice(c * FFN_CHUNK, (c + 1) * FFN_CHUNK)
            gate = jnp.dot(hn, wg_ref[:, cs], preferred_element_type=F32)
            up = jnp.dot(hn, wu_ref[:, cs], preferred_element_type=F32)
            act_ref[:, cs] = (gate * jax.nn.sigmoid(gate) * up).astype(BF16)

    def phase(k):
        def body():
            if k == 0:
                ymix = jnp.concatenate([ya_ref[...], yb_ref[...]], axis=-1).astype(BF16)
                x1 = x_ref[...] + jnp.dot(ymix, wo_ref[...], preferred_element_type=F32)
                ms = jnp.mean(x1 * x1, axis=-1, keepdims=True)
                x1_ref[...] = x1
                hn_ref[...] = (x1 * lax.rsqrt(ms + EPS) * fn_ref[...]).astype(BF16)
                hidden_chunks(0, first)
            elif k < RIDER_PHASES - 1:
                hidden_chunks(first + (k - 1) * per, first + k * per)
            else:
                x2 = x1_ref[...] + jnp.dot(act_ref[...], wd_ref[...], preferred_element_type=F32)
                if final:
                    ms2 = jnp.mean(x2 * x2, axis=-1, keepdims=True)
                    x2 = x2 * lax.rsqrt(ms2 + EPS) * on_ref[...]
                o_ref[...] = x2
            ride()
        return body

    for k in range(RIDER_PHASES):
        pl.when(u == k)(phase(k))


def _post_rider(x, ya, yb, w_out, layer, ffn_norm, w_gate, w_up, w_down, out_norm, final, side, groups, proj_s,
                caches_t, probs):
    n = x.shape[0]
    nseq = caches_t[0].shape[0]
    ts = proj_s.shape[1] // nseq
    tm = n * RIDER_PHASES // nseq
    hidden = w_gate.shape[2]
    row = lambda w, **kw: pl.BlockSpec((tm, w), lambda i, u: (i, 0), **kw)
    res = lambda a: pl.BlockSpec(a.shape, lambda i, u: (0, 0), pipeline_mode=pl.Buffered(1))
    lay = lambda a: pl.BlockSpec((None,) + a.shape[1:], lambda i, u: (layer, 0, 0), pipeline_mode=pl.Buffered(1))
    seq = lambda i, u: i * RIDER_PHASES + u
    new = lambda s: pl.BlockSpec((None, ts, SLAB), lambda i, u: (s, seq(i, u), 0))
    rows8 = pl.BlockSpec((ts, SLAB), lambda i, u: (seq(i, u), 0))
    fn = ffn_norm.reshape(-1, 1, D_MODEL)
    on = out_norm.reshape(1, D_MODEL)
    common = [row(D_MODEL), row(SLAB), row(SLAB), res(w_out), lay(fn), lay(w_gate),
              lay(w_up), lay(w_down), res(on)]
    small = jax.ShapeDtypeStruct((nseq * ts, SLAB), F32)
    ins, args, outs, shapes = [], [], [row(D_MODEL)], [jax.ShapeDtypeStruct((n, D_MODEL), F32)]
    for k, g in enumerate(groups):
        r = caches_t[k].shape[2]
        cspec = pl.BlockSpec((None, SLAB, r), lambda i, u: (seq(i, u), 0, 0))
        pspec = pl.BlockSpec((None, B_HEADS * ts, r + 128), lambda i, u: (seq(i, u), 0, 0))
        cout = jax.ShapeDtypeStruct(caches_t[k].shape, F32)
        if side == 'k':
            ins += [new(4 + g), new(7 + g), cspec]
            args += [proj_s, proj_s, caches_t[k]]
            outs += [cspec, pspec, rows8]
            shapes += [cout, jax.ShapeDtypeStruct((nseq, B_HEADS * ts, r + 128), BF16), small]
        else:
            ins += [pspec, new(10 + g), cspec]
            args += [probs[k], proj_s, caches_t[k]]
            outs += [cspec, rows8]
            shapes += [cout, small]
    res_all = pl.pallas_call(
        functools.partial(_post_rider_kernel, final=final, side=side, groups=tuple(groups)),
        grid=(n // tm, RIDER_PHASES),
        in_specs=common + ins,
        out_specs=outs,
        out_shape=shapes,
        scratch_shapes=[pltpu.VMEM((tm, hidden), BF16), pltpu.VMEM((tm, D_MODEL), BF16)],
        compiler_params=_cparams("parallel", "arbitrary", vmem=VMEM_LIMIT_RIDER),
        name=f"post_ffn_ride_{side}",
    )(x, ya, yb, w_out, fn, w_gate, w_up, w_down, on, *args)
    nout = 3 if side == 'k' else 2
    return res_all[0], [res_all[1 + nout * k:1 + nout * (k + 1)] for k in range(len(groups))]


NEWPAD = 128


def _ext_mask(rows, r, ts, window, step):
    t = lax.broadcasted_iota(jnp.int32, (rows, r + NEWPAD), 0) % ts
    c = lax.broadcasted_iota(jnp.int32, (rows, r + NEWPAD), 1)
    rel = r + t - c
    ok = jnp.logical_and(jnp.logical_and(rel >= 0, rel <= window), c < r + ts)
    return jnp.logical_and(ok, (rel & (step - 1)) == 0)


def _shift_cache(dst_ref, src_ref, new, si, r, ts):
    dst_ref[si, 0:r - ts, :] = src_ref[si, ts:r, :]
    dst_ref[si, r - ts:r, :] = new


def _pad_new(x):
    return jnp.concatenate([x, jnp.zeros((NEWPAD - x.shape[0], x.shape[1]), x.dtype)], axis=0)


def _dil_sample_kernel(q_ref, kn_ref, vn_ref, kc_ref, vc_ref, o_ref, lse_ref, ko_ref, vo_ref,
                       *, window, step):
    sb, ts, _ = q_ref.shape
    r = kc_ref.shape[2]
    scale = HEAD_DIM ** -0.5
    lane = lax.broadcasted_iota(jnp.int32, (1, 128), 1)
    low = lane < HEAD_DIM
    t = lax.broadcasted_iota(jnp.int32, (2 * ts, r + 128), 0) % ts
    c = lax.broadcasted_iota(jnp.int32, (2 * ts, r + 128), 1)
    rel = r + t - jnp.where(c < 128, c, c - 128 + ts)
    valid = jnp.logical_and(jnp.logical_and(rel >= 0, rel <= window), (rel & (step - 1)) == 0)
    valid = jnp.logical_and(valid, jnp.logical_or(c < ts, c >= 128))
    pad = jnp.zeros((128 - ts, SLAB), F32)

    def shifted(src_ref, new_t, rows):
        old = src_ref[rows, :]
        rolled = pltpu.roll(old, r - ts, 1)
        tail = jnp.where(lane < 128 - ts, rolled[:, r - 128:], new_t[rows, :])
        out = tail if r == 128 else jnp.concatenate([rolled[:, :r - 128], tail], axis=1)
        return old, out

    for si, pr in [(si, pr) for si in range(sb) for pr in range(B_HEADS // 2)]:
        if pr == 0:
            kn_t = jnp.concatenate([pad, kn_ref[si]], axis=0).T
            vn_t = jnp.concatenate([pad, vn_ref[si]], axis=0).T
        rows = slice(pr * 128, (pr + 1) * 128)
        k_old, k_out = shifted(kc_ref.at[si], kn_t, rows)
        v_old, v_out = shifted(vc_ref.at[si], vn_t, rows)
        ko_ref[si, rows, :] = k_out
        vo_ref[si, rows, :] = v_out
        k_ext = jnp.concatenate([k_old[:, 0:128], k_out], axis=1).astype(BF16)
        v_ext = jnp.concatenate([v_old[:, 0:128], v_out], axis=1).astype(BF16)
        qp = q_ref[si, :, rows] * scale
        q2 = jnp.concatenate([jnp.where(low, qp, 0.0), jnp.where(low, 0.0, qp)], axis=0).astype(BF16)
        s = jnp.where(valid, jnp.dot(q2, k_ext, preferred_element_type=F32), NEG)
        m = jnp.max(s, axis=1, keepdims=True)
        e = jnp.exp(s - m)
        l = jnp.sum(e, axis=1, keepdims=True)
        a = _dot_t(e.astype(BF16), v_ext) / l
        lse = m + jnp.log(l)
        o_ref[si, :, rows] = jnp.where(low, a[0:ts], a[ts:2 * ts])
        lse_ref[si, :, rows] = jnp.where(low, lse[0:ts], lse[ts:2 * ts])


def _rows_minor(cache):
    nseq, r = cache.shape[:2]
    return jnp.transpose(cache, (0, 2, 3, 1)).reshape(nseq, -1, r)


def _rows_major(cache_t, heads):
    nseq, _, r = cache_t.shape
    return jnp.transpose(cache_t.reshape(nseq, heads, HEAD_DIM, r), (0, 3, 1, 2))


def _dilated_sample(proj, nseq, ts, g, k_cache, v_cache):
    r = k_cache.shape[1]
    assert ts % 8 == 0 and r % 128 == 0 and r >= B_WINDOWS[g]
    kt, vt = _rows_minor(k_cache), _rows_minor(v_cache)
    sb = max(1, min(nseq, 8, 1024 // r))
    p4 = proj.reshape(proj.shape[0], nseq, ts, SLAB)
    new = lambda s: pl.BlockSpec((None, sb, ts, SLAB), lambda i, s=s: (s, i, 0, 0))
    cspec = pl.BlockSpec((sb, SLAB, r), lambda i: (i, 0, 0))
    ospec = pl.BlockSpec((sb, ts, SLAB), lambda i: (i, 0, 0))
    o, lse, kn, vn = pl.pallas_call(
        functools.partial(_dil_sample_kernel, window=B_WINDOWS[g], step=B_DILATIONS[g]),
        grid=(nseq // sb,),
        in_specs=[new(4 + g), new(7 + g), new(10 + g), cspec, cspec],
        out_specs=[ospec, ospec, cspec, cspec],
        out_shape=[jax.ShapeDtypeStruct((nseq, ts, SLAB), F32)] * 2 + [jax.ShapeDtypeStruct(kt.shape, F32)] * 2,
        compiler_params=_cparams("parallel"),
        name=f"dilated_sample_g{g}",
    )(p4, p4, p4, kt, vt)
    return (o.reshape(nseq * ts, SLAB), lse.reshape(nseq * ts, SLAB),
            _rows_major(kn, B_HEADS), _rows_major(vn, B_HEADS))


def _swa_sample_kernel(q_ref, k_ref, v_ref, kc_ref, vc_ref, sink_ref, o_ref, kn_ref, vn_ref):
    sb, ts, _ = q_ref.shape
    r = kc_ref.shape[1]
    rows = D_HEADS * ts
    valid = _ext_mask(rows, r, ts, D_WINDOW - 1, 1)
    lane = lax.broadcasted_iota(jnp.int32, (1, 128), 1)
    low = lane < HEAD_DIM
    rep = D_HEADS // D_KV_HEADS
    scale = HEAD_DIM ** -0.5
    sink = sink_ref[:, 0:1]
    seqs = range(sb)

    def q_rows(si):
        pieces = []
        for pr in range(rep):
            base = q_ref[si, :, pr * 128:(pr + 1) * 128] * scale
            for g in range(D_KV_HEADS):
                pieces.append(jnp.where(low if g == 0 else jnp.logical_not(low), base, 0.0))
        return jnp.concatenate(pieces, axis=0).astype(BF16)

    kn = [k_ref[si] for si in seqs]
    vn = [v_ref[si] for si in seqs]
    qrows = [q_rows(si) for si in seqs]
    kext = [jnp.concatenate([kc_ref[si], _pad_new(kn[si])], axis=0).astype(BF16) for si in seqs]
    vext = [jnp.concatenate([vc_ref[si], _pad_new(vn[si])], axis=0).astype(BF16) for si in seqs]
    s = [jnp.where(valid, _dot_t(qrows[si], kext[si]), NEG) for si in seqs]
    m = [jnp.maximum(jnp.max(s[si], axis=1, keepdims=True), sink) for si in seqs]
    e = [jnp.exp(s[si] - m[si]) for si in seqs]
    l = [jnp.sum(e[si], axis=1, keepdims=True) + jnp.exp(sink - m[si]) for si in seqs]
    o = [jnp.dot(e[si].astype(BF16), vext[si], preferred_element_type=F32) / l[si] for si in seqs]
    for si in seqs:
        pairs = [jnp.where(low, o[si][2 * pr * ts:(2 * pr + 1) * ts], o[si][(2 * pr + 1) * ts:(2 * pr + 2) * ts])
                 for pr in range(rep)]
        o_ref[si] = jnp.concatenate(pairs, axis=1)
        _shift_cache(kn_ref, kc_ref, kn[si], si, r, ts)
        _shift_cache(vn_ref, vc_ref, vn[si], si, r, ts)


def _swa_sample(proj, nseq, ts, sinks, k_cache, v_cache):
    r = k_cache.shape[1]
    sb = min(nseq, 8)
    kvw = D_KV_HEADS * HEAD_DIM
    p4 = proj.reshape(proj.shape[0], nseq, ts, SLAB)
    kc = k_cache.reshape(nseq, r, kvw)
    vc = v_cache.reshape(nseq, r, kvw)
    cspec = pl.BlockSpec((sb, r, kvw), lambda i: (i, 0, 0))
    new = lambda col: pl.BlockSpec((None, sb, ts, kvw), lambda i: (4, i, 0, col))
    rep = D_HEADS // D_KV_HEADS
    by_row_group = sinks.reshape(D_KV_HEADS, rep).T.reshape(-1)
    sink_rows = jnp.broadcast_to(jnp.repeat(by_row_group, ts)[:, None], (D_HEADS * ts, 128))
    o, kn, vn = pl.pallas_call(
        _swa_sample_kernel,
        grid=(nseq // sb,),
        in_specs=[pl.BlockSpec((None, sb, ts, SLAB), lambda i: (3, i, 0, 0)), new(0), new(1), cspec, cspec,
                  pl.BlockSpec(sink_rows.shape, lambda i: (0, 0))],
        out_specs=[pl.BlockSpec((sb, ts, SLAB), lambda i: (i, 0, 0)), cspec, cspec],
        out_shape=[jax.ShapeDtypeStruct((nseq, ts, SLAB), F32)] + [jax.ShapeDtypeStruct(kc.shape, F32)] * 2,
        compiler_params=_cparams("parallel"),
        name="swa_sample",
    )(p4, p4, p4, kc, vc, sink_rows)
    return o.reshape(nseq * ts, SLAB), kn.reshape(k_cache.shape), vn.reshape(v_cache.shape)


def kernel(x_prompt, x_sample, state_a_C, state_a_n, state_a_m, state_a_conv, cache_b_k1, cache_b_v1, cache_b_k2, cache_b_v2, cache_b_k3, cache_b_v3, state_c_conv, state_c_ssm, cache_d_k, cache_d_v, e_norm_mix, e_w_in, e_b_gates, e_a_conv_w, e_a_conv_b, e_a_norm, e_w_out, o_norm_mix, o_w_in, o_c_conv_w, o_c_conv_b, o_c_dt_bias, o_c_A_log, o_c_D, o_c_norm, o_d_sinks, o_w_out, ffn_norm, w_gate, w_up, w_down, final_norm):
    depth = ffn_norm.shape[0]
    even_w = [_even_slabs(e_w_in[j]) for j in range((depth + 1) // 2)]
    odd_w = [_odd_slabs(o_w_in[j]) for j in range(depth // 2)]
    odd_wo = [jnp.concatenate([o_w_out[j][:C_DINNER], _group_aligned(o_w_out[j][C_DINNER:], 0)],
                              axis=0).astype(BF16) for j in range(depth // 2)]
    even_wo = [e_w_out[j].astype(BF16) for j in range((depth + 1) // 2)]
    wg16, wu16, wd16 = w_gate.astype(BF16), w_up.astype(BF16), w_down.astype(BF16)

    def trunk(x, pos, fresh, even_st, odd_st, ride=None, pre=None):
        nseq, t, _ = x.shape
        rode = {}
        assert t >= A_CONV - 1 and t >= C_CONV - 1
        cos_t, sin_t = _rope_tables(pos)
        x2 = x.reshape(nseq * t, D_MODEL)
        new_even, new_odd = [], []

        def tail(proj, s, keep, dil=1, lanes=slice(None)):
            p4 = proj.reshape(proj.shape[0], nseq, t, SLAB)
            if dil == 1:
                return p4[s, :, t - keep:, lanes]
            seg = PERM_TILE // dil
            last = p4[s, :, t - PERM_TILE:, lanes].reshape(nseq, dil, seg, -1)[:, :, seg - keep // dil:]
            return jnp.swapaxes(last, 1, 2).reshape(nseq, keep, -1)

        for layer in range(depth):
            j = layer // 2
            last = layer == depth - 1
            if layer % 2 == 0:
                if pre is not None and layer == 0:
                    pe = pre['proj']
                else:
                    pe = _norm_proj(x2, e_norm_mix[j], even_w[j], cos_t, sin_t, _even_kinds(fresh))
                st = {k: v[j] for k, v in even_st.items()}
                ya, c_new, n_new, m_new = _mlstm(pe, nseq, t, e_a_conv_w[j], e_a_conv_b[j], e_b_gates[j],
                                                 e_a_norm[j], st['a_conv'], st['a_C'], st['a_n'], st['a_m'])
                ns = {'a_C': c_new, 'a_n': n_new, 'a_m': m_new,
                      'a_conv': jnp.concatenate([tail(pe, 0, A_CONV - 1), tail(pe, 1, A_CONV - 1)], axis=-1)}
                parts = []
                for g in range(B_GROUPS):
                    kn, vn = f'b_k{g + 1}', f'b_v{g + 1}'
                    if fresh:
                        o_g, lse_g = _dilated_prompt(pe, nseq, t, g)
                        keep = min(B_WINDOWS[g], t)
                        if keep % 128 == 0 and keep <= PERM_TILE:
                            ns[kn], ns[vn] = _cache_tails(pe, nseq, t, g, keep)
                        else:
                            ns[kn] = tail(pe, 7 + g, keep, B_DILATIONS[g]).reshape(nseq, keep, B_HEADS, HEAD_DIM)
                            ns[vn] = tail(pe, 10 + g, keep, B_DILATIONS[g]).reshape(nseq, keep, B_HEADS, HEAD_DIM)
                    elif pre is not None and layer == 0 and g in pre['groups']:
                        o_g, lse_g, ns[kn], ns[vn] = pre[g]
                    else:
                        o_g, lse_g, ns[kn], ns[vn] = _dilated_sample(pe, nseq, t, g, st[kn], st[vn])
                    parts += [o_g, lse_g]
                yb = _merge_groups(parts, fresh)
                if ride is not None and layer == 0:
                    x2, rode['k'] = _post_rider(
                        x2, ya, yb, even_wo[j], layer, ffn_norm, wg16, wu16, wd16, final_norm, last, 'k',
                        ride['groups'], ride['proj'], ride['kt'], None)
                else:
                    x2 = _post(x2, ya, yb, even_wo[j], layer, ffn_norm, wg16, wu16, wd16, final_norm, last)
                new_even.append(ns)
            else:
                po = _norm_proj(x2, o_norm_mix[j], odd_w[j], cos_t, sin_t, ODD_KINDS)
                st = {k: v[j] for k, v in odd_st.items()}
                yc, ssm_new = _ssd(po, nseq, t, o_c_conv_w[j], o_c_conv_b[j], o_c_dt_bias[j], o_c_A_log[j],
                                   o_c_D[j], o_c_norm[j], st['c_conv'], st['c_ssm'])
                ns = {'c_conv': jnp.concatenate([tail(po, 1, C_CONV - 1), tail(po, 2, C_CONV - 1)], axis=-1),
                      'c_ssm': ssm_new}
                kvw = D_KV_HEADS * HEAD_DIM
                if fresh:
                    yd = _swa_prompt(po, nseq, t, o_d_sinks[j])
                    keep = min(D_WINDOW, t)
                    ns['d_k'] = tail(po, 4, keep, lanes=slice(0, kvw)).reshape(nseq, keep, D_KV_HEADS, HEAD_DIM)
                    ns['d_v'] = tail(po, 4, keep, lanes=slice(kvw, 2 * kvw)).reshape(
                        nseq, keep, D_KV_HEADS, HEAD_DIM)
                else:
                    yd, ns['d_k'], ns['d_v'] = _swa_sample(po, nseq, t, o_d_sinks[j], st['d_k'], st['d_v'])
                if ride is not None and layer == 1:
                    x2, rode['v'] = _post_rider(
                        x2, yc, yd, odd_wo[j], layer, ffn_norm, wg16, wu16, wd16, final_norm, last, 'v',
                        ride['groups'], ride['proj'], ride['vt'], [kpl[1] for kpl in rode['k']])
                else:
                    x2 = _post(x2, yc, yd, odd_wo[j], layer, ffn_norm, wg16, wu16, wd16, final_norm, last)
                new_odd.append(ns)
        out = {k: jnp.stack([ns[k] for ns in new_even]) for k in new_even[0]}
        out.update({k: jnp.stack([ns[k] for ns in new_odd]) for k in new_odd[0]})
        return x2.reshape(nseq, t, D_MODEL), out, rode

    bp, tp, _ = x_prompt.shape
    bs, ts, _ = x_sample.shape
    n_even, n_odd = (depth + 1) // 2, depth // 2
    zeros = lambda *shape: jnp.zeros(shape, F32)
    even_p = {'a_C': zeros(n_even, bp, A_HEADS, A_DK, A_DV), 'a_n': zeros(n_even, bp, A_HEADS, A_DK),
              'a_m': zeros(n_even, bp, A_HEADS), 'a_conv': zeros(n_even, bp, A_CONV - 1, 2 * SLAB)}
    odd_p = {'c_conv': zeros(n_odd, bp, C_CONV - 1, 2 * SLAB),
             'c_ssm': zeros(n_odd, bp, C_HEADS, C_HEADDIM, C_DSTATE)}
    even_s = {'a_C': state_a_C, 'a_n': state_a_n, 'a_m': state_a_m, 'a_conv': state_a_conv,
              'b_k1': cache_b_k1, 'b_v1': cache_b_v1, 'b_k2': cache_b_k2, 'b_v2': cache_b_v2,
              'b_k3': cache_b_k3, 'b_v3': cache_b_v3}
    odd_s = {'c_conv': state_c_conv, 'c_ssm': state_c_ssm, 'd_k': cache_d_k, 'd_v': cache_d_v}
    pos_p = jnp.arange(tp, dtype=jnp.int32)
    pos_s = jnp.tile(PAST_LEN + jnp.arange(ts, dtype=jnp.int32), bs)

    groups = tuple(range(B_GROUPS))
    k_caches = [even_s[f'b_k{g + 1}'][0] for g in groups]
    v_caches = [even_s[f'b_v{g + 1}'][0] for g in groups]
    tile = bp * tp * RIDER_PHASES // bs
    can_ride = (depth >= 2 and bp * tp * RIDER_PHASES % bs == 0 and tile % 8 == 0 and 8 <= tile <= 1024
                and ts == 8 and all(c.shape[1] % 128 == 0 and c.shape[1] >= B_WINDOWS[g]
                                    for g, c in zip(groups, k_caches)))
    if can_ride:
        cos_s, sin_s = _rope_tables(pos_s)
        pe_s = _norm_proj(x_sample.reshape(bs * ts, D_MODEL), e_norm_mix[0], even_w[0], cos_s, sin_s,
                          _even_kinds(False))
        ride = {'groups': groups, 'proj': pe_s, 'kt': [_rows_minor(c) for c in k_caches],
                'vt': [_rows_minor(c) for c in v_caches]}
        y_p, pn, rode = trunk(x_prompt, pos_p, True, even_p, odd_p, ride=ride)
        pre = {'groups': groups, 'proj': pe_s}
        for k, g in enumerate(groups):
            (k_out, _, lse_g), (v_out, o_g) = rode['k'][k], rode['v'][k]
            pre[g] = (o_g, lse_g, _rows_major(k_out, B_HEADS), _rows_major(v_out, B_HEADS))
        y_s, sn, _ = trunk(x_sample, pos_s, False, even_s, odd_s, pre=pre)
    else:
        y_p, pn, _ = trunk(x_prompt, pos_p, True, even_p, odd_p)
        y_s, sn, _ = trunk(x_sample, pos_s, False, even_s, odd_s)

    keys = ('a_C', 'a_n', 'a_m', 'a_conv', 'b_k1', 'b_v1', 'b_k2', 'b_v2', 'b_k3', 'b_v3',
            'c_conv', 'c_ssm', 'd_k', 'd_v')
    states = []
    for k in keys:
        states += [pn[k], sn[k]]
    return (y_p, y_s, *states)
```

```python
import functools
import math

import jax
import jax.numpy as jnp
from jax import lax
from jax.experimental import pallas as pl
from jax.experimental.pallas import tpu as pltpu

F32 = jnp.float32
BF16 = jnp.bfloat16

D_MODEL = 1024
PAST_LEN = 16384
EPS = 1e-6
ROPE_THETA = 10000.0
HEAD_DIM = 64
SLAB = 512

A_HEADS = 4
A_DK = 128
A_DV = 128
A_CONV = 4
A_CHUNK = 64

B_HEADS = 8
B_WINDOWS = (128, 512, 2048)
B_DILATIONS = (1, 4, 16)
B_GROUPS = 3

C_HEADS = 8
C_HEADDIM = 64
C_DINNER = C_HEADS * C_HEADDIM
C_DSTATE = 128
C_NGROUPS = 2
C_CONV = 4
C_CHUNK = 128

D_HEADS = 8
D_KV_HEADS = 2
D_WINDOW = 128

NEG = -1e30
VMEM_LIMIT = 56 * 1024 * 1024


VMEM_LIMIT_RIDER = 60 * 1024 * 1024


def _mix_dtype(rows):
    return BF16 if rows % 16 == 0 else F32


def _cparams(*sem, vmem=VMEM_LIMIT):
    return pltpu.CompilerParams(dimension_semantics=sem, vmem_limit_bytes=vmem)


def _rot_half(y):
    lane = lax.broadcasted_iota(jnp.int32, y.shape, 1)
    fwd = pltpu.roll(y, SLAB - HEAD_DIM // 2, 1)
    bwd = pltpu.roll(y, HEAD_DIM // 2, 1)
    return jnp.where(lane % HEAD_DIM < HEAD_DIM // 2, fwd, bwd)


PERM_TILE = 2048
PROJ_ROWS = 256
PERM_STRIDE = 4


def _proj_kernel(x_ref, nw_ref, w_ref, cos_ref, sin_ref, o_ref, xn_ref, *ybuf, kinds):
    j = pl.program_id(1)
    tm = o_ref.shape[0]

    @pl.when(j == 0)
    def _():
        x = x_ref[...]
        ms = jnp.mean(x * x, axis=-1, keepdims=True)
        xn_ref[...] = (x * lax.rsqrt(ms + EPS) * nw_ref[...]).astype(BF16)

    rc = min(tm, PROJ_ROWS)

    def emit(rope_lanes, dil):
        seg = tm // dil
        cseg = rc // dil
        for c in range(tm // rc):
            rows = slice(c * rc, (c + 1) * rc)
            y = jnp.dot(xn_ref[rows, :], w_ref[...], preferred_element_type=F32)
            val = y
            if rope_lanes:
                cos = jnp.tile(cos_ref[rows, :], (1, SLAB // 128))
                sin = jnp.tile(sin_ref[rows, :], (1, SLAB // 128))
                val = y * cos + _rot_half(y) * sin
                if rope_lanes < SLAB:
                    lane = lax.broadcasted_iota(jnp.int32, y.shape, 1)
                    val = jnp.where(lane < rope_lanes, val, y)
            if dil == 1:
                o_ref[rows, :] = val
            else:
                for cb in range(SLAB // 128):
                    ybuf[0][cb, rows, :] = val[:, cb * 128:(cb + 1) * 128]
                if dil <= PERM_STRIDE:
                    for r in range(dil):
                        dst = slice(r * seg + c * cseg, r * seg + (c + 1) * cseg)
                        for cb in range(SLAB // 128):
                            o_ref[dst, cb * 128:(cb + 1) * 128] = ybuf[0][cb, pl.ds(c * rc + r, cseg, stride=dil), :]
                else:
                    d2 = dil // PERM_STRIDE
                    half = rc // PERM_STRIDE
                    for cb in range(SLAB // 128):
                        for r0 in range(PERM_STRIDE):
                            ybuf[1][cb, r0 * half:(r0 + 1) * half, :] = \
                                ybuf[0][cb, pl.ds(c * rc + r0, half, stride=PERM_STRIDE), :]
                    for r0 in range(PERM_STRIDE):
                        for r1 in range(d2):
                            r = r1 * PERM_STRIDE + r0
                            dst = slice(r * seg + c * cseg, r * seg + (c + 1) * cseg)
                            for cb in range(SLAB // 128):
                                o_ref[dst, cb * 128:(cb + 1) * 128] = \
                                    ybuf[1][cb, pl.ds(r0 * half + r1, cseg, stride=d2), :]

    for kind in sorted(set(kinds)):
        member = functools.reduce(jnp.logical_or, [j == s for s, k in enumerate(kinds) if k == kind])
        pl.when(member)(functools.partial(emit, *kind))


def _norm_proj(x, norm_w, w_slabs, cos_t, sin_t, kinds):
    n = x.shape[0]
    s = w_slabs.shape[0]
    p = cos_t.shape[0]
    perm = any(d > 1 for _, d in kinds)
    big = perm or (n % PERM_TILE == 0 and p % PERM_TILE == 0)
    tm = PERM_TILE if big else min(1024, n, p)
    assert n % tm == 0 and p % tm == 0 and len(kinds) == s
    npt = p // tm
    scratch = [pltpu.VMEM((tm, D_MODEL), BF16)]
    if perm:
        scratch.append(pltpu.VMEM((SLAB // 128, tm, 128), F32))
    if any(d > PERM_STRIDE for _, d in kinds):
        scratch.append(pltpu.VMEM((SLAB // 128, min(tm, PROJ_ROWS), 128), F32))
    return pl.pallas_call(
        functools.partial(_proj_kernel, kinds=tuple(kinds)),
        grid=(n // tm, s),
        in_specs=[
            pl.BlockSpec((tm, D_MODEL), lambda i, j: (i, 0)),
            pl.BlockSpec((1, D_MODEL), lambda i, j: (0, 0)),
            pl.BlockSpec((None, D_MODEL, SLAB), lambda i, j: (j, 0, 0)),
            pl.BlockSpec((tm, 128), lambda i, j: (i % npt, 0)),
            pl.BlockSpec((tm, 128), lambda i, j: (i % npt, 0)),
        ],
        out_specs=pl.BlockSpec((None, tm, SLAB), lambda i, j: (j, i, 0)),
        out_shape=jax.ShapeDtypeStruct((s, n, SLAB), F32),
        scratch_shapes=scratch,
        compiler_params=_cparams("parallel", "arbitrary"),
        name="norm_proj",
    )(x, norm_w.reshape(1, D_MODEL), w_slabs, cos_t, sin_t)


def _even_kinds(perm):
    dil = B_DILATIONS if perm else (1,) * B_GROUPS
    return ([(0, 1)] * 4 + [(SLAB, d) for d in dil] * 2 + [(0, d) for d in dil] + [(0, 1)])


ODD_KINDS = [(0, 1)] * 3 + [(SLAB, 1), (D_KV_HEADS * HEAD_DIM, 1)]


def _rope_tables(pos):
    half = HEAD_DIM // 2
    inv = ROPE_THETA ** (-jnp.arange(half, dtype=F32) / half)
    ang = pos.astype(F32)[:, None] * inv[None, :]
    cos, sin = jnp.cos(ang), jnp.sin(ang)
    return (jnp.concatenate([cos, cos, cos, cos], axis=-1),
            jnp.concatenate([-sin, sin, -sin, sin], axis=-1))


def _pad_cols(w, width):
    return jnp.pad(w, ((0, 0), (0, width - w.shape[1])))


def _even_slabs(w_in):
    main = w_in[:, :2048].reshape(D_MODEL, 4, SLAB)
    gates = _pad_cols(w_in[:, 2048:2056], SLAB)[:, None, :]
    qkv = w_in[:, 2056:].reshape(D_MODEL, 9, SLAB)
    return jnp.moveaxis(jnp.concatenate([main, qkv, gates], axis=1), 1, 0).astype(BF16)


def _group_aligned(w, axis):
    rep = D_HEADS // D_KV_HEADS
    shape = w.shape
    split = shape[:axis] + (D_KV_HEADS, rep, HEAD_DIM) + shape[axis + 1:]
    return jnp.swapaxes(w.reshape(split), axis, axis + 1).reshape(shape)


def _odd_slabs(w_in):
    main = w_in[:, :1536].reshape(D_MODEL, 3, SLAB)
    qd = _group_aligned(w_in[:, 1544:2056], 1)[:, None, :]
    kvd = _pad_cols(jnp.concatenate([w_in[:, 2056:2312], w_in[:, 1536:1544]], axis=1), SLAB)[:, None, :]
    return jnp.moveaxis(jnp.concatenate([main, qd, kvd], axis=1), 1, 0).astype(BF16)


def _split3(x):
    hi = x.astype(BF16)
    r1 = x - hi.astype(F32)
    mid = r1.astype(BF16)
    lo = (r1 - mid.astype(F32)).astype(BF16)
    return hi, mid, lo


def _cumsum_rows(x, tri):
    hi, mid, lo = _split3(x)
    acc = jnp.dot(tri, lo, preferred_element_type=F32)
    acc = acc + jnp.dot(tri, mid, preferred_element_type=F32)
    return acc + jnp.dot(tri, hi, preferred_element_type=F32)


def _transpose_slab(s):
    l = s.shape[0]
    if l < 128:
        s = jnp.concatenate([s, jnp.zeros((128 - l, 128), s.dtype)], axis=0)
    return s.T


def _conv_silu(buf_ref, x, w, b):
    tb = x.shape[0]
    buf_ref[8:8 + tb, :] = x
    y = b + w[3:4] * x
    for j in range(3):
        y = y + w[j:j + 1] * buf_ref[5 + j:5 + j + tb, :]
    buf_ref[0:8, :] = buf_ref[tb:tb + 8, :]
    return y * jax.nn.sigmoid(y)


def _log_sigmoid(x):
    return jnp.minimum(x, 0.0) - jnp.log1p(jnp.exp(-jnp.abs(x)))


def _dot_t(a, b):
    return lax.dot_general(a, b, (((1,), (1,)), ((), ())), preferred_element_type=F32)


def _tdot(a, b):
    return lax.dot_general(a, b, (((0,), (0,)), ((), ())), preferred_element_type=F32)


def _mlstm_kernel(q_ref, k_ref, v_ref, og_ref, g_ref, cw_ref, cb_ref, bg_ref, an_ref,
                  conv0_ref, c0_ref, n0_ref, m0_ref,
                  ya_ref, c_ref, n_ref, m_ref, qbuf, kbuf, *, chunk):
    sb, tb, _ = q_ref.shape
    tblk = pl.program_id(1)
    lane = lax.broadcasted_iota(jnp.int32, (chunk, 128), 1)
    lane1 = lax.broadcasted_iota(jnp.int32, (1, 128), 1)
    row = lax.broadcasted_iota(jnp.int32, (chunk, chunk), 0)
    col = lax.broadcasted_iota(jnp.int32, (chunk, chunk), 1)
    causal = col <= row
    tri = causal.astype(BF16)
    scale = A_DK ** -0.5

    @pl.when(tblk == 0)
    def _():
        c_ref[...] = c0_ref[...]
        n_ref[...] = n0_ref[...]
        m_ref[...] = m0_ref[...]
        for si in range(sb):
            qbuf[si, 0:8, :] = conv0_ref[si, :, 0:SLAB]
            kbuf[si, 0:8, :] = conv0_ref[si, :, SLAB:2 * SLAB]

    nchunk = tb // chunk
    seqs, chunks, heads = range(sb), range(nchunk), range(A_HEADS)
    rows = lambda c: slice(c * chunk, (c + 1) * chunk)
    cols = lambda h: slice(h * 128, (h + 1) * 128)
    inst = [(si, c, h) for si in seqs for c in chunks for h in heads]
    qc = [_conv_silu(qbuf.at[si], q_ref[si], cw_ref[:, 0:SLAB], cb_ref[:, 0:SLAB]) for si in seqs]
    kc = [_conv_silu(kbuf.at[si], k_ref[si], cw_ref[:, SLAB:2 * SLAB], cb_ref[:, SLAB:2 * SLAB]) * scale
          for si in seqs]
    gates = {(si, c): g_ref[si, rows(c), 0:128] + bg_ref[...] for si in seqs for c in chunks}
    bc = {k: _cumsum_rows(_log_sigmoid(g), tri) for k, g in gates.items()}
    st = {k: _transpose_slab(jnp.where(lane < A_HEADS, gates[k], bc[k])) for k in gates}
    qh = {(si, c, h): qc[si][rows(c), cols(h)] for si, c, h in inst}
    kh = {(si, c, h): kc[si][rows(c), cols(h)] for si, c, h in inst}
    qb = {k: v.astype(BF16) for k, v in qh.items()}
    vb = {(si, c, h): v_ref[si, rows(c), cols(h)].astype(BF16) for si, c, h in inst}
    b_t = {(si, c, h): bc[si, c][:, A_HEADS + h:A_HEADS + h + 1] for si, c, h in inst}
    dlog = {(si, c, h): jnp.where(causal, b_t[si, c, h] - st[si, c][A_HEADS + h:A_HEADS + h + 1, 0:chunk]
                                  + st[si, c][h:h + 1, 0:chunk], NEG) for si, c, h in inst}
    m_loc = {k: jnp.max(v, axis=1, keepdims=True) for k, v in dlog.items()}
    qk = {k: _dot_t(qb[k], kh[k].astype(BF16)) for k in inst}
    s_loc = {k: qk[k] * jnp.exp(dlog[k] - m_loc[k]) for k in inst}
    sv = {k: jnp.dot(s_loc[k].astype(BF16), vb[k], preferred_element_type=F32) for k in inst}
    rs = {k: jnp.sum(s_loc[k], axis=1, keepdims=True) for k in inst}
    b_end = {k: b_t[k][chunk - 1:chunk, :] for k in inst}
    g_col = {(si, c, h): b_end[si, c, h] - b_t[si, c, h] + gates[si, c][:, h:h + 1] for si, c, h in inst}
    g_max = {k: jnp.max(v, axis=0, keepdims=True) for k, v in g_col.items()}
    kw = {k: kh[k] * jnp.exp(g_col[k] - g_max[k]) for k in inst}
    u = {k: _tdot(kw[k].astype(BF16), vb[k]) for k in inst}
    nk = {k: jnp.sum(kw[k], axis=0, keepdims=True) for k in inst}
    c_st = {(si, h): c_ref[si, h] for si in seqs for h in heads}
    n_st = {(si, h): n_ref[si, h:h + 1, :] for si in seqs for h in heads}
    m_h = {(si, h): m_ref[si][:, h:h + 1] for si in seqs for h in heads}
    for c in chunks:
        cur = [(si, c, h) for si in seqs for h in heads]
        inter = {k: b_t[k] + m_h[k[0], k[2]] for k in cur}
        m_t = {k: jnp.maximum(inter[k], m_loc[k]) for k in cur}
        w_loc = {k: jnp.exp(m_loc[k] - m_t[k]) for k in cur}
        w_inter = {k: jnp.exp(inter[k] - m_t[k]) for k in cur}
        qc_st = {k: jnp.dot(qb[k], c_st[k[0], k[2]].astype(BF16), preferred_element_type=F32) for k in cur}
        qn = {k: jnp.sum(qh[k] * n_st[k[0], k[2]], axis=1, keepdims=True) for k in cur}
        num = {k: w_loc[k] * sv[k] + w_inter[k] * qc_st[k] for k in cur}
        den = {k: w_loc[k] * rs[k] + w_inter[k] * qn[k] for k in cur}
        hh = {k: num[k] / jnp.maximum(jnp.abs(den[k]), jnp.exp(-m_t[k])) for k in cur}
        m_new = {k: jnp.maximum(b_end[k] + m_h[k[0], k[2]], g_max[k]) for k in cur}
        decay = {k: jnp.exp(b_end[k] + m_h[k[0], k[2]] - m_new[k]) for k in cur}
        gain = {k: jnp.exp(g_max[k] - m_new[k]) for k in cur}
        for k in cur:
            sh = (k[0], k[2])
            c_st[sh] = decay[k] * c_st[sh] + gain[k] * u[k]
            n_st[sh] = decay[k] * n_st[sh] + gain[k] * nk[k]
            m_h[sh] = m_new[k]
        ms = {k: jnp.mean(hh[k] * hh[k], axis=-1, keepdims=True) for k in cur}
        for si, _, h in cur:
            hn = hh[si, c, h] * lax.rsqrt(ms[si, c, h] + EPS) * an_ref[:, cols(h)]
            ya_ref[si, rows(c), cols(h)] = (hn * jax.nn.sigmoid(og_ref[si, rows(c), cols(h)])).astype(ya_ref.dtype)
    for si in seqs:
        m_vec = m_ref[si]
        for h in heads:
            c_ref[si, h] = c_st[si, h]
            n_ref[si, h:h + 1, :] = n_st[si, h]
            m_vec = jnp.where(lane1 == h, m_h[si, h], m_vec)
        m_ref[si] = m_vec


def _mlstm(proj, nseq, t, conv_w, conv_b, b_gates, a_norm, conv0, c0, n0, m0):
    chunk = A_CHUNK if t % A_CHUNK == 0 else t
    if t > chunk:
        sb, tb = min(nseq, 2), min(t, 8 * chunk)
    else:
        sb, tb = min(nseq, 8), t
    p4 = proj.reshape(proj.shape[0], nseq, t, SLAB)
    slab = lambda s: pl.BlockSpec((None, sb, tb, SLAB), lambda i, j, s=s: (s, i, j, 0))
    full = lambda a: pl.BlockSpec(a.shape, lambda i, j: (0,) * a.ndim)
    st = lambda a: pl.BlockSpec((sb,) + a.shape[1:], lambda i, j: (i,) + (0,) * (a.ndim - 1))
    bg = _pad_cols(b_gates.reshape(1, 2 * A_HEADS), 128)
    conv0p = jnp.pad(conv0, ((0, 0), (8 - (A_CONV - 1), 0), (0, 0)))
    m0p = _pad_cols(m0, 128).reshape(nseq, 1, 128)
    cb = conv_b.reshape(1, -1)
    an = a_norm.reshape(1, -1)
    ya, c, n, m = pl.pallas_call(
        functools.partial(_mlstm_kernel, chunk=chunk),
        grid=(nseq // sb, t // tb),
        in_specs=[slab(0), slab(1), slab(2), slab(3), slab(13),
                  full(conv_w), full(cb), full(bg), full(an),
                  st(conv0p), st(c0), st(n0), st(m0p)],
        out_specs=[pl.BlockSpec((sb, tb, SLAB), lambda i, j: (i, j, 0)), st(c0), st(n0), st(m0p)],
        out_shape=[jax.ShapeDtypeStruct((nseq, t, SLAB), _mix_dtype(t)),
                   jax.ShapeDtypeStruct(c0.shape, F32), jax.ShapeDtypeStruct(n0.shape, F32),
                   jax.ShapeDtypeStruct(m0p.shape, F32)],
        scratch_shapes=[pltpu.VMEM((sb, 8 + tb, SLAB), F32), pltpu.VMEM((sb, 8 + tb, SLAB), F32)],
        compiler_params=_cparams("parallel", "arbitrary"),
        name="mlstm",
    )(p4, p4, p4, p4, p4, conv_w, cb, bg, an, conv0p, c0, n0, m0p)
    return ya.reshape(nseq * t, SLAB), c, n, m[:, 0, :A_HEADS]


QBLK = 128


def _band_heads(q_m, k2, v2, valid, sinks):
    n = range(len(q_m))
    s = [jnp.where(valid[i], _dot_t(q_m[i], k2[i]), NEG) for i in n]
    m = [jnp.max(s[i], axis=1, keepdims=True) for i in n]
    if sinks is not None:
        m = [jnp.maximum(m[i], sinks[i]) for i in n]
    e = [jnp.exp(s[i] - m[i]) for i in n]
    l = [jnp.sum(e[i], axis=1, keepdims=True) for i in n]
    if sinks is not None:
        l = [l[i] + jnp.exp(sinks[i] - m[i]) for i in n]
    acc = [jnp.dot(e[i].astype(BF16), v2[i], preferred_element_type=F32) for i in n]
    return acc, m, l


def _band_masks(has_prev, window):
    a = lax.broadcasted_iota(jnp.int32, (QBLK, 2 * QBLK), 0)
    c = lax.broadcasted_iota(jnp.int32, (QBLK, 2 * QBLK), 1)
    rel = a - c + QBLK
    band = jnp.logical_and(rel >= 0, rel <= window)
    if has_prev is True:
        return band
    return jnp.logical_and(band, c >= jnp.where(has_prev, 0, QBLK))


def _key_rows(prev_ref, cur_ref, a, ls):
    if prev_ref.shape[0] == cur_ref.shape[0]:
        before = prev_ref[a * QBLK:(a + 1) * QBLK, ls]
    else:
        before = prev_ref[:, ls] if a == 0 else cur_ref[(a - 1) * QBLK:a * QBLK, ls]
    return jnp.concatenate([before, cur_ref[a * QBLK:(a + 1) * QBLK, ls]], axis=0).astype(BF16)


def _dilated_kernel(q_ref, kp_ref, kc_ref, vp_ref, vc_ref, o_ref, lse_ref):
    lane = lax.broadcasted_iota(jnp.int32, (1, 128), 1)
    low = lane < HEAD_DIM
    scale = HEAD_DIM ** -0.5
    chained = kp_ref.shape[0] != kc_ref.shape[0]
    subs = range(q_ref.shape[0] // QBLK)
    pairs = range(B_HEADS // 2)
    rows = lambda a: slice(a * QBLK, (a + 1) * QBLK)
    cols = lambda pr: slice(pr * 128, (pr + 1) * 128)
    valid = [_band_masks(True if (chained and a > 0) else pl.program_id(2) > 0, QBLK) for a in subs]
    qp = {(a, pr): q_ref[rows(a), cols(pr)] * scale for a in subs for pr in pairs}
    k2 = {(a, pr): _key_rows(kp_ref, kc_ref, a, cols(pr)) for a in subs for pr in pairs}
    v2 = {(a, pr): _key_rows(vp_ref, vc_ref, a, cols(pr)) for a in subs for pr in pairs}
    inst = [(a, pr, p) for a in subs for pr in pairs for p in range(2)]
    acc, m, l = _band_heads(
        [jnp.where(low if p == 0 else jnp.logical_not(low), qp[a, pr], 0.0).astype(BF16) for a, pr, p in inst],
        [k2[a, pr] for a, pr, _ in inst], [v2[a, pr] for a, pr, _ in inst], [valid[a] for a, _, _ in inst], None)
    o_h = [acc[i] / l[i] for i in range(len(inst))]
    lse_h = [m[i] + jnp.log(l[i]) for i in range(len(inst))]
    for i in range(0, len(inst), 2):
        a, pr, _ = inst[i]
        o_ref[rows(a), cols(pr)] = jnp.where(low, o_h[i], o_h[i + 1])
        lse_ref[rows(a), cols(pr)] = jnp.where(low, lse_h[i], lse_h[i + 1])


def _dilated_prompt(proj, b, t, g):
    d = B_DILATIONS[g]
    assert t % PERM_TILE == 0
    nb = PERM_TILE // d // QBLK
    per_tile = PERM_TILE // QBLK
    pv = proj.reshape(proj.shape[0], b, t, SLAB)

    def blk(i, r):
        return (i // nb) * per_tile + r * nb + i % nb

    if nb % 2 == 0:
        nq, classes, steps = 2, d, t // d // QBLK // 2
        cur_idx = lambda i, r: blk(2 * i, r) // 2
        prev_rows, prev_idx = QBLK, lambda i, r: blk(jnp.maximum(2 * i - 1, 0), r)
    elif nb == 1 and d % 2 == 0:
        nq, classes, steps = 2, d // 2, t // d // QBLK
        cur_idx = lambda i, r: blk(i, 2 * r) // 2
        prev_rows, prev_idx = 2 * QBLK, lambda i, r: blk(jnp.maximum(i - 1, 0), 2 * r) // 2
    else:
        nq, classes, steps = 1, d, t // d // QBLK
        cur_idx = blk
        prev_rows, prev_idx = QBLK, lambda i, r: blk(jnp.maximum(i - 1, 0), r)
    cur = lambda s: pl.BlockSpec((None, None, nq * QBLK, SLAB), lambda bi, r, i, s=s: (s, bi, cur_idx(i, r), 0))
    prev = lambda s: pl.BlockSpec((None, None, prev_rows, SLAB), lambda bi, r, i, s=s: (s, bi, prev_idx(i, r), 0))
    ospec = pl.BlockSpec((None, nq * QBLK, SLAB), lambda bi, r, i: (bi, cur_idx(i, r), 0))
    sq, sk, sv = 4 + g, 7 + g, 10 + g
    o, lse = pl.pallas_call(
        _dilated_kernel,
        grid=(b, classes, steps),
        in_specs=[cur(sq), prev(sk), cur(sk), prev(sv), cur(sv)],
        out_specs=[ospec, ospec],
        out_shape=[jax.ShapeDtypeStruct((b, t, SLAB), F32)] * 2,
        compiler_params=_cparams("parallel", "parallel", "arbitrary"),
        name=f"dilated_g{g}",
    )(pv, pv, pv, pv, pv)
    return o.reshape(b * t, SLAB), lse.reshape(b * t, SLAB)


def _tails_kernel(x_ref, o_ref, *buf, dil):
    tm = x_ref.shape[0]
    keep = o_ref.shape[1]
    if dil == 1:
        o_ref[...] = x_ref[tm - keep:tm, :].T
    else:
        seg = tm // dil
        for r in range(dil):
            buf[0][pl.ds(r, seg, stride=dil), :] = x_ref[r * seg:(r + 1) * seg, :]
        o_ref[...] = buf[0][tm - keep:tm, :].T


def _cache_tails(proj, nseq, t, g, keep):
    dil = B_DILATIONS[g]
    pv = proj.reshape(proj.shape[0], nseq, t, SLAB)
    rows = keep if dil == 1 else PERM_TILE
    last = t // rows - 1
    out = pl.pallas_call(
        functools.partial(_tails_kernel, dil=dil),
        grid=(2, nseq, SLAB // 128),
        in_specs=[pl.BlockSpec((None, None, rows, 128), lambda kv, b, cb: (7 + g + B_GROUPS * kv, b, last, cb))],
        out_specs=pl.BlockSpec((None, None, 128, keep), lambda kv, b, cb: (kv, b, cb, 0)),
        out_shape=jax.ShapeDtypeStruct((2, nseq, SLAB, keep), F32),
        scratch_shapes=[pltpu.VMEM((PERM_TILE, 128), F32)] if dil > 1 else [],
        compiler_params=_cparams("parallel", "parallel", "parallel"),
        name=f"cache_tails_g{g}",
    )(pv)
    return _rows_major(out[0], B_HEADS), _rows_major(out[1], B_HEADS)


def _merge_kernel(*refs, dils):
    ins, y_ref, bufs = refs[:6], refs[6], refs[7:]
    tm = y_ref.shape[0]
    vals = []
    nbuf = 0
    for k, ref in enumerate(ins):
        dil = dils[k // 2]
        if dil == 1:
            vals.append(ref[...])
        else:
            buf = bufs[nbuf]
            nbuf += 1
            seg = tm // dil
            for r in range(dil):
                buf[pl.ds(r, seg, stride=dil), :] = ref[r * seg:(r + 1) * seg, :]
            vals.append(buf[...])
    o1, l1, o2, l2, o3, l3 = vals
    m = jnp.maximum(jnp.maximum(l1, l2), l3)
    w1, w2, w3 = jnp.exp(l1 - m), jnp.exp(l2 - m), jnp.exp(l3 - m)
    y_ref[...] = ((w1 * o1 + w2 * o2 + w3 * o3) / (w1 + w2 + w3)).astype(y_ref.dtype)


def _merge_groups(parts, perm):
    n = parts[0].shape[0]
    dils = B_DILATIONS if perm else (1,) * B_GROUPS
    tm = PERM_TILE if perm else min(1024, n)
    spec = pl.BlockSpec((tm, 128), lambda i, j: (i, j))
    nbuf = 2 * sum(d > 1 for d in dils)
    return pl.pallas_call(
        functools.partial(_merge_kernel, dils=dils), grid=(n // tm, SLAB // 128), in_specs=[spec] * 6,
        out_specs=spec, out_shape=jax.ShapeDtypeStruct((n, SLAB), _mix_dtype(tm)),
        scratch_shapes=[pltpu.VMEM((tm, 128), F32)] * nbuf,
        compiler_params=_cparams("parallel", "parallel"), name="merge_groups",
    )(*parts)


def _swa_kernel(q_ref, kp_ref, kc_ref, vp_ref, vc_ref, sink_ref, o_ref):
    lane = lax.broadcasted_iota(jnp.int32, (1, 128), 1)
    scale = HEAD_DIM ** -0.5
    rep = D_HEADS // D_KV_HEADS
    low = lane < HEAD_DIM
    allk = slice(None)
    subs = range(q_ref.shape[0] // QBLK)
    rows = lambda a: slice(a * QBLK, (a + 1) * QBLK)
    cols = lambda pr: slice(pr * 128, (pr + 1) * 128)
    valid = [_band_masks(pl.program_id(1) > 0 if a == 0 else True, D_WINDOW - 1) for a in subs]
    k2 = [_key_rows(kp_ref, kc_ref, a, allk) for a in subs]
    v2 = [_key_rows(vp_ref, vc_ref, a, allk) for a in subs]
    qp = {(a, pr): q_ref[rows(a), cols(pr)] * scale for a in subs for pr in range(rep)}
    inst = [(a, pr, g) for a in subs for pr in range(rep) for g in range(D_KV_HEADS)]
    acc, m, l = _band_heads(
        [jnp.where(low if g == 0 else jnp.logical_not(low), qp[a, pr], 0.0).astype(BF16) for a, pr, g in inst],
        [k2[a] for a, _, _ in inst], [v2[a] for a, _, _ in inst], [valid[a] for a, _, _ in inst],
        [sink_ref[g * rep + pr] for _, pr, g in inst])
    o_h = [acc[i] / l[i] for i in range(len(inst))]
    for i in range(0, len(inst), 2):
        a, pr, _ = inst[i]
        o_ref[rows(a), cols(pr)] = jnp.where(low, o_h[i], o_h[i + 1]).astype(o_ref.dtype)


def _swa_prompt(proj, b, t, sinks):
    pv = proj.reshape(proj.shape[0], b, t, SLAB)
    nq = 2 if t % (2 * QBLK) == 0 else 1
    qspec = pl.BlockSpec((None, None, nq * QBLK, SLAB), lambda bi, i: (3, bi, i, 0))

    def kv(col, prev):
        if prev:
            return pl.BlockSpec((None, None, QBLK, 128), lambda bi, i: (4, bi, jnp.maximum(nq * i - 1, 0), col))
        return pl.BlockSpec((None, None, nq * QBLK, 128), lambda bi, i: (4, bi, i, col))

    o = pl.pallas_call(
        _swa_kernel,
        grid=(b, t // QBLK // nq),
        in_specs=[qspec, kv(0, True), kv(0, False), kv(1, True), kv(1, False),
                  pl.BlockSpec(memory_space=pltpu.SMEM)],
        out_specs=pl.BlockSpec((None, nq * QBLK, SLAB), lambda bi, i: (bi, i, 0)),
        out_shape=jax.ShapeDtypeStruct((b, t, SLAB), _mix_dtype(t)),
        compiler_params=_cparams("parallel", "arbitrary"),
        name="swa_prompt",
    )(pv, pv, pv, pv, pv, sinks)
    return o.reshape(b * t, SLAB)


def _softplus(x):
    return jnp.maximum(x, 0.0) + jnp.log1p(jnp.exp(-jnp.abs(x)))


def _ssd_kernel(z_ref, x_ref, bc_ref, dt_ref, cw_ref, cb_ref, dtb_ref, alog_ref, dl_ref, cn_ref,
                conv0_ref, h0_ref, y_ref, h_ref, xbuf, bcbuf, *, chunk):
    sb, tb, _ = x_ref.shape
    tblk = pl.program_id(1)
    lane1 = lax.broadcasted_iota(jnp.int32, (1, 128), 1)
    low = lane1 < C_HEADDIM
    row = lax.broadcasted_iota(jnp.int32, (chunk, chunk), 0)
    col = lax.broadcasted_iota(jnp.int32, (chunk, chunk), 1)
    causal = col <= row
    tri = causal.astype(BF16)
    srow_low = lax.broadcasted_iota(jnp.int32, (128, 1), 0) < C_HEADDIM
    hpg = C_HEADS // C_NGROUPS
    a_neg = -jnp.exp(alog_ref[...])

    @pl.when(tblk == 0)
    def _():
        h_ref[...] = h0_ref[...]
        for si in range(sb):
            xbuf[si, 0:8, :] = conv0_ref[si, :, 0:SLAB]
            bcbuf[si, 0:8, :] = conv0_ref[si, :, SLAB:2 * SLAB]

    seqs, chunks = range(sb), range(tb // chunk)
    ppg = hpg // 2
    npair = C_HEADS // 2
    rows = lambda c: slice(c * chunk, (c + 1) * chunk)
    cols = lambda pr: slice(pr * 128, (pr + 1) * 128)
    sc = [(si, c) for si in seqs for c in chunks]
    sch = [(si, c, h) for si, c in sc for h in range(C_HEADS)]
    scp = [(si, c, pr) for si, c in sc for pr in range(npair)]
    xc = [_conv_silu(xbuf.at[si], x_ref[si], cw_ref[:, 0:SLAB], cb_ref[:, 0:SLAB]) for si in seqs]
    bcc = [_conv_silu(bcbuf.at[si], bc_ref[si], cw_ref[:, SLAB:2 * SLAB], cb_ref[:, SLAB:2 * SLAB]) for si in seqs]
    dt = {(si, c): _softplus(dt_ref[si, rows(c), :] + dtb_ref[...]) for si, c in sc}
    cum = {k: _cumsum_rows(v * a_neg, tri) for k, v in dt.items()}
    cum_rows = {k: _transpose_slab(v) for k, v in cum.items()}
    dt_rows = {k: _transpose_slab(v) for k, v in dt.items()}
    bgm = {(si, c, g): bcc[si][rows(c), g * 128:(g + 1) * 128].astype(BF16) for si, c in sc
           for g in range(C_NGROUPS)}
    cgm = {(si, c, g): bcc[si][rows(c), 256 + g * 128:256 + (g + 1) * 128].astype(BF16) for si, c in sc
           for g in range(C_NGROUPS)}
    cbm = {k: _dot_t(cgm[k], bgm[k]) for k in bgm}
    x_pair = {(si, c, pr): xc[si][rows(c), cols(pr)] for si, c, pr in scp}
    cum_t = {(si, c, h): cum[si, c][:, h:h + 1] for si, c, h in sch}
    cum_e = {(si, c, h): cum[si, c][chunk - 1:chunk, h:h + 1] for si, c, h in sch}
    seg = {(si, c, h): jnp.where(causal, cum_t[si, c, h] - cum_rows[si, c][h:h + 1, 0:chunk], NEG)
           for si, c, h in sch}
    w = {(si, c, h): cbm[si, c, h // hpg] * jnp.exp(seg[si, c, h]) * dt_rows[si, c][h:h + 1, 0:chunk]
         for si, c, h in sch}
    xhalf = {(si, c, h): jnp.where(low if h % 2 == 0 else jnp.logical_not(low), x_pair[si, c, h // 2],
                                   0.0).astype(BF16) for si, c, h in sch}
    yh = {k: jnp.dot(w[k].astype(BF16), xhalf[k], preferred_element_type=F32) for k in sch}
    e_col = {k: jnp.exp(cum_t[k]) for k in sch}
    w_end = {(si, c, h): jnp.exp(cum_e[si, c, h] - cum_t[si, c, h]) * dt[si, c][:, h:h + 1] for si, c, h in sch}
    dec = {k: jnp.exp(cum_e[k]) for k in sch}
    y_intra = {(si, c, pr): yh[si, c, 2 * pr] + yh[si, c, 2 * pr + 1] + dl_ref[:, cols(pr)] * x_pair[si, c, pr]
               for si, c, pr in scp}
    xw = {(si, c, pr): x_pair[si, c, pr] * jnp.where(low, w_end[si, c, 2 * pr], w_end[si, c, 2 * pr + 1])
          for si, c, pr in scp}
    upd = {(si, c, pr): _tdot(xw[si, c, pr].astype(BF16), bgm[si, c, pr // ppg]) for si, c, pr in scp}
    e_sel = {(si, c, pr): jnp.where(low, e_col[si, c, 2 * pr], e_col[si, c, 2 * pr + 1]) for si, c, pr in scp}
    d_sel = {(si, c, pr): jnp.where(srow_low, dec[si, c, 2 * pr], dec[si, c, 2 * pr + 1]) for si, c, pr in scp}
    gate = {}
    for si, c, pr in scp:
        zz = z_ref[si, rows(c), cols(pr)]
        gate[si, c, pr] = zz * jax.nn.sigmoid(zz)
    hs = {(si, pr): h_ref[si, pr] for si in seqs for pr in range(npair)}
    for c in chunks:
        cur = [(si, c, pr) for si in seqs for pr in range(npair)]
        inter = {k: _dot_t(cgm[k[0], c, k[2] // ppg], hs[k[0], k[2]].astype(BF16)) * e_sel[k] for k in cur}
        for k in cur:
            hs[k[0], k[2]] = d_sel[k] * hs[k[0], k[2]] + upd[k]
        yz = {k: (y_intra[k] + inter[k]) * gate[k] for k in cur}
        sq = {k: jnp.sum(yz[k] * yz[k], axis=-1, keepdims=True) for k in cur}
        for si in seqs:
            for g in range(C_NGROUPS):
                prs = range(g * ppg, (g + 1) * ppg)
                inv = lax.rsqrt(sum(sq[si, c, pr] for pr in prs) / (hpg * C_HEADDIM) + EPS)
                for pr in prs:
                    y_ref[si, rows(c), cols(pr)] = (yz[si, c, pr] * inv * cn_ref[:, cols(pr)]).astype(y_ref.dtype)
    for si in seqs:
        for pr in range(npair):
            h_ref[si, pr] = hs[si, pr]


def _ssd(proj, nseq, t, conv_w, conv_b, dt_bias, a_log, d_skip, c_norm, conv0, h0):
    chunk = C_CHUNK if t % C_CHUNK == 0 else t
    if t > chunk:
        sb, tb = min(nseq, 2), min(t, 4 * chunk)
    else:
        sb, tb = min(nseq, 8), t
    p4 = proj.reshape(proj.shape[0], nseq, t, SLAB)
    slab = lambda s: pl.BlockSpec((None, sb, tb, SLAB), lambda i, j, s=s: (s, i, j, 0))
    full = lambda a: pl.BlockSpec(a.shape, lambda i, j: (0,) * a.ndim)
    st = lambda a: pl.BlockSpec((sb,) + a.shape[1:], lambda i, j: (i,) + (0,) * (a.ndim - 1))
    conv0p = jnp.pad(conv0, ((0, 0), (8 - (C_CONV - 1), 0), (0, 0)))
    hp = h0.reshape(nseq, C_HEADS // 2, 128, C_DSTATE)
    cb = conv_b.reshape(1, -1)
    dtb = _pad_cols(dt_bias.reshape(1, C_HEADS), 128)
    alog = _pad_cols(a_log.reshape(1, C_HEADS), 128)
    dl = jnp.repeat(d_skip, C_HEADDIM).reshape(1, C_DINNER)
    cn = c_norm.reshape(1, C_DINNER)
    y, h = pl.pallas_call(
        functools.partial(_ssd_kernel, chunk=chunk),
        grid=(nseq // sb, t // tb),
        in_specs=[slab(0), slab(1), slab(2),
                  pl.BlockSpec((None, sb, tb, 128), lambda i, j: (4, i, j, 2)),
                  full(conv_w), full(cb), full(dtb), full(alog), full(dl), full(cn), st(conv0p), st(hp)],
        out_specs=[pl.BlockSpec((sb, tb, SLAB), lambda i, j: (i, j, 0)), st(hp)],
        out_shape=[jax.ShapeDtypeStruct((nseq, t, SLAB), _mix_dtype(t)), jax.ShapeDtypeStruct(hp.shape, F32)],
        scratch_shapes=[pltpu.VMEM((sb, 8 + tb, SLAB), F32), pltpu.VMEM((sb, 8 + tb, SLAB), F32)],
        compiler_params=_cparams("parallel", "arbitrary"),
        name="ssd",
    )(p4, p4, p4, p4, conv_w, cb, dtb, alog, dl, cn, conv0p, hp)
    return y.reshape(nseq * t, SLAB), h.reshape(nseq, C_HEADS, C_HEADDIM, C_DSTATE)


FFN_CHUNK = 256


def _post_kernel(x_ref, ya_ref, yb_ref, wo_ref, fn_ref, wg_ref, wu_ref, wd_ref, on_ref, o_ref, act_ref, *, final):
    ymix = jnp.concatenate([ya_ref[...], yb_ref[...]], axis=-1).astype(BF16)
    x1 = x_ref[...] + jnp.dot(ymix, wo_ref[...], preferred_element_type=F32)
    ms = jnp.mean(x1 * x1, axis=-1, keepdims=True)
    hn = (x1 * lax.rsqrt(ms + EPS) * fn_ref[...]).astype(BF16)
    hidden = wg_ref.shape[1]
    for c in range(hidden // FFN_CHUNK):
        cs = slice(c * FFN_CHUNK, (c + 1) * FFN_CHUNK)
        gate = jnp.dot(hn, wg_ref[:, cs], preferred_element_type=F32)
        up = jnp.dot(hn, wu_ref[:, cs], preferred_element_type=F32)
        act_ref[:, cs] = (gate * jax.nn.sigmoid(gate) * up).astype(BF16)
    x2 = x1 + jnp.dot(act_ref[...], wd_ref[...], preferred_element_type=F32)
    if final:
        ms2 = jnp.mean(x2 * x2, axis=-1, keepdims=True)
        x2 = x2 * lax.rsqrt(ms2 + EPS) * on_ref[...]
    o_ref[...] = x2


def _post(x, ya, yb, w_out, layer, ffn_norm, w_gate, w_up, w_down, out_norm, final):
    n = x.shape[0]
    tm = min(1024, n)
    hidden = w_gate.shape[2]
    row = lambda w: pl.BlockSpec((tm, w), lambda i: (i, 0))
    res = lambda a: pl.BlockSpec(a.shape, lambda i: (0, 0), pipeline_mode=pl.Buffered(1))
    lay = lambda a: pl.BlockSpec((None,) + a.shape[1:], lambda i: (layer, 0, 0), pipeline_mode=pl.Buffered(1))
    fn = ffn_norm.reshape(-1, 1, D_MODEL)
    on = out_norm.reshape(1, D_MODEL)
    return pl.pallas_call(
        functools.partial(_post_kernel, final=final),
        grid=(n // tm,),
        in_specs=[row(D_MODEL), row(SLAB), row(SLAB), res(w_out), lay(fn), lay(w_gate), lay(w_up), lay(w_down),
                  res(on)],
        out_specs=row(D_MODEL),
        out_shape=jax.ShapeDtypeStruct((n, D_MODEL), F32),
        scratch_shapes=[pltpu.VMEM((tm, hidden), BF16)],
        compiler_params=_cparams("parallel"),
        name="post_ffn",
    )(x, ya, yb, w_out, fn, w_gate, w_up, w_down, on)


RIDER_PHASES = 4


def _rider_mask(ts, r, window, step):
    t = lax.broadcasted_iota(jnp.int32, (2 * ts, r + 128), 0) % ts
    c = lax.broadcasted_iota(jnp.int32, (2 * ts, r + 128), 1)
    rel = r + t - jnp.where(c < 128, c, c - 128 + ts)
    valid = jnp.logical_and(jnp.logical_and(rel >= 0, rel <= window), (rel & (step - 1)) == 0)
    return jnp.logical_and(valid, jnp.logical_or(c < ts, c >= 128))


def _shift_rows_minor(src_ref, new_t, rows, ts):
    r = src_ref.shape[1]
    lane = lax.broadcasted_iota(jnp.int32, (1, 128), 1)
    old = src_ref[rows, :]
    rolled = pltpu.roll(old, r - ts, 1)
    tail = jnp.where(lane < 128 - ts, rolled[:, r - 128:], new_t[rows, :])
    out = tail if r == 128 else jnp.concatenate([rolled[:, :r - 128], tail], axis=1)
    return old, out


def _new_rows_t(new_ref):
    ts = new_ref.shape[0]
    pad = jnp.zeros((128 - ts, SLAB), F32)
    return jnp.concatenate([pad, new_ref[...]], axis=0).T


def _rider_k(q_ref, kn_ref, kc_ref, ko_ref, p_ref, lse_ref, window, step):
    ts = q_ref.shape[0]
    r = kc_ref.shape[1]
    low = lax.broadcasted_iota(jnp.int32, (1, 128), 1) < HEAD_DIM
    valid = _rider_mask(ts, r, window, step)
    kn_t = _new_rows_t(kn_ref)
    scale = HEAD_DIM ** -0.5
    pairs = range(B_HEADS // 2)
    rows = lambda pr: slice(pr * 128, (pr + 1) * 128)
    s = []
    for pr in pairs:
        k_old, k_out = _shift_rows_minor(kc_ref, kn_t, rows(pr), ts)
        ko_ref[rows(pr), :] = k_out
        k_ext = jnp.concatenate([k_old[:, 0:128], k_out], axis=1).astype(BF16)
        qp = q_ref[:, rows(pr)] * scale
        q2 = jnp.concatenate([jnp.where(low, qp, 0.0), jnp.where(low, 0.0, qp)], axis=0).astype(BF16)
        s.append(jnp.where(valid, jnp.dot(q2, k_ext, preferred_element_type=F32), NEG))
    m = [jnp.max(s[pr], axis=1, keepdims=True) for pr in pairs]
    e = [jnp.exp(s[pr] - m[pr]) for pr in pairs]
    l = [jnp.sum(e[pr], axis=1, keepdims=True) for pr in pairs]
    for pr in pairs:
        p_ref[pr * 2 * ts:(pr + 1) * 2 * ts, :] = (e[pr] / l[pr]).astype(BF16)
        lse = m[pr] + jnp.log(l[pr])
        lse_ref[:, rows(pr)] = jnp.where(low, lse[0:ts], lse[ts:2 * ts])


def _rider_v(p_ref, vn_ref, vc_ref, vo_ref, o_ref):
    ts = vn_ref.shape[0]
    low = lax.broadcasted_iota(jnp.int32, (1, 128), 1) < HEAD_DIM
    vn_t = _new_rows_t(vn_ref)
    for pr in range(B_HEADS // 2):
        rows = slice(pr * 128, (pr + 1) * 128)
        v_old, v_out = _shift_rows_minor(vc_ref, vn_t, rows, ts)
        vo_ref[rows, :] = v_out
        v_ext = jnp.concatenate([v_old[:, 0:128], v_out], axis=1).astype(BF16)
        a = _dot_t(p_ref[pr * 2 * ts:(pr + 1) * 2 * ts, :], v_ext)
        o_ref[:, rows] = jnp.where(low, a[0:ts], a[ts:2 * ts])


def _post_rider_kernel(x_ref, ya_ref, yb_ref, wo_ref, fn_ref, wg_ref, wu_ref, wd_ref, on_ref, *rest,
                       final, side, groups):
    ng = len(groups)
    nout = 3 if side == 'k' else 2
    ins, o_ref, outs = rest[:3 * ng], rest[3 * ng], rest[3 * ng + 1:3 * ng + 1 + nout * ng]
    act_ref, hn_ref = rest[3 * ng + 1 + nout * ng:]
    x1_ref = o_ref

    def ride():
        for k, g in enumerate(groups):
            i3, o3 = ins[3 * k:3 * k + 3], outs[nout * k:nout * k + nout]
            if side == 'k':
                _rider_k(*i3, *o3, B_WINDOWS[g], B_DILATIONS[g])
            else:
                _rider_v(*i3, *o3)

    u = pl.program_id(1)
    nchunk = wg_ref.shape[1] // FFN_CHUNK
    per = -(-nchunk // (RIDER_PHASES - 1))
    first = nchunk - per * (RIDER_PHASES - 2)

    def hidden_chunks(lo, hi):
        hn = hn_ref[...]
        for c in range(lo, hi):
            cs = slice(c * FFN_CHUNK, (c + 1) * FFN_CHUNK)
            gate = jnp.dot(hn, wg_ref[:, cs], preferred_element_type=F32)
            up = jnp.dot(hn, wu_ref[:, cs], preferred_element_type=F32)
            act_ref[:, cs] = (gate * jax.nn.sigmoid(gate) * up).astype(BF16)

    def phase(k):
        def body():
            if k == 0:
                ymix = jnp.concatenate([ya_ref[...], yb_ref[...]], axis=-1).astype(BF16)
                x1 = x_ref[...] + jnp.dot(ymix, wo_ref[...], preferred_element_type=F32)
                ms = jnp.mean(x1 * x1, axis=-1, keepdims=True)
                x1_ref[...] = x1
                hn_ref[...] = (x1 * lax.rsqrt(ms + EPS) * fn_ref[...]).astype(BF16)
                hidden_chunks(0, first)
            elif k < RIDER_PHASES - 1:
                hidden_chunks(first + (k - 1) * per, first + k * per)
            else:
                x2 = x1_ref[...] + jnp.dot(act_ref[...], wd_ref[...], preferred_element_type=F32)
                if final:
                    ms2 = jnp.mean(x2 * x2, axis=-1, keepdims=True)
                    x2 = x2 * lax.rsqrt(ms2 + EPS) * on_ref[...]
                o_ref[...] = x2
            ride()
        return body

    for k in range(RIDER_PHASES):
        pl.when(u == k)(phase(k))


def _post_rider(x, ya, yb, w_out, layer, ffn_norm, w_gate, w_up, w_down, out_norm, final, side, groups, proj_s,
                caches_t, probs):
    n = x.shape[0]
    nseq = caches_t[0].shape[0]
    ts = proj_s.shape[1] // nseq
    tm = n * RIDER_PHASES // nseq
    hidden = w_gate.shape[2]
    row = lambda w, **kw: pl.BlockSpec((tm, w), lambda i, u: (i, 0), **kw)
    res = lambda a: pl.BlockSpec(a.shape, lambda i, u: (0, 0), pipeline_mode=pl.Buffered(1))
    lay = lambda a: pl.BlockSpec((None,) + a.shape[1:], lambda i, u: (layer, 0, 0), pipeline_mode=pl.Buffered(1))
    seq = lambda i, u: i * RIDER_PHASES + u
    new = lambda s: pl.BlockSpec((None, ts, SLAB), lambda i, u: (s, seq(i, u), 0))
    rows8 = pl.BlockSpec((ts, SLAB), lambda i, u: (seq(i, u), 0))
    fn = ffn_norm.reshape(-1, 1, D_MODEL)
    on = out_norm.reshape(1, D_MODEL)
    common = [row(D_MODEL), row(SLAB), row(SLAB), res(w_out), lay(fn), lay(w_gate),
              lay(w_up), lay(w_down), res(on)]
    small = jax.ShapeDtypeStruct((nseq * ts, SLAB), F32)
    ins, args, outs, shapes = [], [], [row(D_MODEL)], [jax.ShapeDtypeStruct((n, D_MODEL), F32)]
    for k, g in enumerate(groups):
        r = caches_t[k].shape[2]
        cspec = pl.BlockSpec((None, SLAB, r), lambda i, u: (seq(i, u), 0, 0))
        pspec = pl.BlockSpec((None, B_HEADS * ts, r + 128), lambda i, u: (seq(i, u), 0, 0))
        cout = jax.ShapeDtypeStruct(caches_t[k].shape, F32)
        if side == 'k':
            ins += [new(4 + g), new(7 + g), cspec]
            args += [proj_s, proj_s, caches_t[k]]
            outs += [cspec, pspec, rows8]
            shapes += [cout, jax.ShapeDtypeStruct((nseq, B_HEADS * ts, r + 128), BF16), small]
        else:
            ins += [pspec, new(10 + g), cspec]
            args += [probs[k], proj_s, caches_t[k]]
            outs += [cspec, rows8]
            shapes += [cout, small]
    res_all = pl.pallas_call(
        functools.partial(_post_rider_kernel, final=final, side=side, groups=tuple(groups)),
        grid=(n // tm, RIDER_PHASES),
        in_specs=common + ins,
        out_specs=outs,
        out_shape=shapes,
        scratch_shapes=[pltpu.VMEM((tm, hidden), BF16), pltpu.VMEM((tm, D_MODEL), BF16)],
        compiler_params=_cparams("parallel", "arbitrary", vmem=VMEM_LIMIT_RIDER),
        name=f"post_ffn_ride_{side}",
    )(x, ya, yb, w_out, fn, w_gate, w_up, w_down, on, *args)
    nout = 3 if side == 'k' else 2
    return res_all[0], [res_all[1 + nout * k:1 + nout * (k + 1)] for k in range(len(groups))]


NEWPAD = 128


def _ext_mask(rows, r, ts, window, step):
    t = lax.broadcasted_iota(jnp.int32, (rows, r + NEWPAD), 0) % ts
    c = lax.broadcasted_iota(jnp.int32, (rows, r + NEWPAD), 1)
    rel = r + t - c
    ok = jnp.logical_and(jnp.logical_and(rel >= 0, rel <= window), c < r + ts)
    return jnp.logical_and(ok, (rel & (step - 1)) == 0)


def _shift_cache(dst_ref, src_ref, new, si, r, ts):
    dst_ref[si, 0:r - ts, :] = src_ref[si, ts:r, :]
    dst_ref[si, r - ts:r, :] = new


def _pad_new(x):
    return jnp.concatenate([x, jnp.zeros((NEWPAD - x.shape[0], x.shape[1]), x.dtype)], axis=0)


def _dil_sample_kernel(q_ref, kn_ref, vn_ref, kc_ref, vc_ref, o_ref, lse_ref, ko_ref, vo_ref,
                       *, window, step):
    sb, ts, _ = q_ref.shape
    r = kc_ref.shape[2]
    scale = HEAD_DIM ** -0.5
    lane = lax.broadcasted_iota(jnp.int32, (1, 128), 1)
    low = lane < HEAD_DIM
    t = lax.broadcasted_iota(jnp.int32, (2 * ts, r + 128), 0) % ts
    c = lax.broadcasted_iota(jnp.int32, (2 * ts, r + 128), 1)
    rel = r + t - jnp.where(c < 128, c, c - 128 + ts)
    valid = jnp.logical_and(jnp.logical_and(rel >= 0, rel <= window), (rel & (step - 1)) == 0)
    valid = jnp.logical_and(valid, jnp.logical_or(c < ts, c >= 128))
    pad = jnp.zeros((128 - ts, SLAB), F32)

    def shifted(src_ref, new_t, rows):
        old = src_ref[rows, :]
        rolled = pltpu.roll(old, r - ts, 1)
        tail = jnp.where(lane < 128 - ts, rolled[:, r - 128:], new_t[rows, :])
        out = tail if r == 128 else jnp.concatenate([rolled[:, :r - 128], tail], axis=1)
        return old, out

    for si, pr in [(si, pr) for si in range(sb) for pr in range(B_HEADS // 2)]:
        if pr == 0:
            kn_t = jnp.concatenate([pad, kn_ref[si]], axis=0).T
            vn_t = jnp.concatenate([pad, vn_ref[si]], axis=0).T
        rows = slice(pr * 128, (pr + 1) * 128)
        k_old, k_out = shifted(kc_ref.at[si], kn_t, rows)
        v_old, v_out = shifted(vc_ref.at[si], vn_t, rows)
        ko_ref[si, rows, :] = k_out
        vo_ref[si, rows, :] = v_out
        k_ext = jnp.concatenate([k_old[:, 0:128], k_out], axis=1).astype(BF16)
        v_ext = jnp.concatenate([v_old[:, 0:128], v_out], axis=1).astype(BF16)
        qp = q_ref[si, :, rows] * scale
        q2 = jnp.concatenate([jnp.where(low, qp, 0.0), jnp.where(low, 0.0, qp)], axis=0).astype(BF16)
        s = jnp.where(valid, jnp.dot(q2, k_ext, preferred_element_type=F32), NEG)
        m = jnp.max(s, axis=1, keepdims=True)
        e = jnp.exp(s - m)
        l = jnp.sum(e, axis=1, keepdims=True)
        a = _dot_t(e.astype(BF16), v_ext) / l
        lse = m + jnp.log(l)
        o_ref[si, :, rows] = jnp.where(low, a[0:ts], a[ts:2 * ts])
        lse_ref[si, :, rows] = jnp.where(low, lse[0:ts], lse[ts:2 * ts])


def _rows_minor(cache):
    nseq, r = cache.shape[:2]
    return jnp.transpose(cache, (0, 2, 3, 1)).reshape(nseq, -1, r)


def _rows_major(cache_t, heads):
    nseq, _, r = cache_t.shape
    return jnp.transpose(cache_t.reshape(nseq, heads, HEAD_DIM, r), (0, 3, 1, 2))


def _dilated_sample(proj, nseq, ts, g, k_cache, v_cache):
    r = k_cache.shape[1]
    assert ts % 8 == 0 and r % 128 == 0 and r >= B_WINDOWS[g]
    kt, vt = _rows_minor(k_cache), _rows_minor(v_cache)
    sb = max(1, min(nseq, 8, 1024 // r))
    p4 = proj.reshape(proj.shape[0], nseq, ts, SLAB)
    new = lambda s: pl.BlockSpec((None, sb, ts, SLAB), lambda i, s=s: (s, i, 0, 0))
    cspec = pl.BlockSpec((sb, SLAB, r), lambda i: (i, 0, 0))
    ospec = pl.BlockSpec((sb, ts, SLAB), lambda i: (i, 0, 0))
    o, lse, kn, vn = pl.pallas_call(
        functools.partial(_dil_sample_kernel, window=B_WINDOWS[g], step=B_DILATIONS[g]),
        grid=(nseq // sb,),
        in_specs=[new(4 + g), new(7 + g), new(10 + g), cspec, cspec],
        out_specs=[ospec, ospec, cspec, cspec],
        out_shape=[jax.ShapeDtypeStruct((nseq, ts, SLAB), F32)] * 2 + [jax.ShapeDtypeStruct(kt.shape, F32)] * 2,
        compiler_params=_cparams("parallel"),
        name=f"dilated_sample_g{g}",
    )(p4, p4, p4, kt, vt)
    return (o.reshape(nseq * ts, SLAB), lse.reshape(nseq * ts, SLAB),
            _rows_major(kn, B_HEADS), _rows_major(vn, B_HEADS))


def _swa_sample_kernel(q_ref, k_ref, v_ref, kc_ref, vc_ref, sink_ref, o_ref, kn_ref, vn_ref):
    sb, ts, _ = q_ref.shape
    r = kc_ref.shape[1]
    rows = D_HEADS * ts
    valid = _ext_mask(rows, r, ts, D_WINDOW - 1, 1)
    lane = lax.broadcasted_iota(jnp.int32, (1, 128), 1)
    low = lane < HEAD_DIM
    rep = D_HEADS // D_KV_HEADS
    scale = HEAD_DIM ** -0.5
    sink = sink_ref[:, 0:1]
    seqs = range(sb)

    def q_rows(si):
        pieces = []
        for pr in range(rep):
            base = q_ref[si, :, pr * 128:(pr + 1) * 128] * scale
            for g in range(D_KV_HEADS):
                pieces.append(jnp.where(low if g == 0 else jnp.logical_not(low), base, 0.0))
        return jnp.concatenate(pieces, axis=0).astype(BF16)

    kn = [k_ref[si] for si in seqs]
    vn = [v_ref[si] for si in seqs]
    qrows = [q_rows(si) for si in seqs]
    kext = [jnp.concatenate([kc_ref[si], _pad_new(kn[si])], axis=0).astype(BF16) for si in seqs]
    vext = [jnp.concatenate([vc_ref[si], _pad_new(vn[si])], axis=0).astype(BF16) for si in seqs]
    s = [jnp.where(valid, _dot_t(qrows[si], kext[si]), NEG) for si in seqs]
    m = [jnp.maximum(jnp.max(s[si], axis=1, keepdims=True), sink) for si in seqs]
    e = [jnp.exp(s[si] - m[si]) for si in seqs]
    l = [jnp.sum(e[si], axis=1, keepdims=True) + jnp.exp(sink - m[si]) for si in seqs]
    o = [jnp.dot(e[si].astype(BF16), vext[si], preferred_element_type=F32) / l[si] for si in seqs]
    for si in seqs:
        pairs = [jnp.where(low, o[si][2 * pr * ts:(2 * pr + 1) * ts], o[si][(2 * pr + 1) * ts:(2 * pr + 2) * ts])
                 for pr in range(rep)]
        o_ref[si] = jnp.concatenate(pairs, axis=1)
        _shift_cache(kn_ref, kc_ref, kn[si], si, r, ts)
        _shift_cache(vn_ref, vc_ref, vn[si], si, r, ts)


def _swa_sample(proj, nseq, ts, sinks, k_cache, v_cache):
    r = k_cache.shape[1]
    sb = min(nseq, 8)
    kvw = D_KV_HEADS * HEAD_DIM
    p4 = proj.reshape(proj.shape[0], nseq, ts, SLAB)
    kc = k_cache.reshape(nseq, r, kvw)
    vc = v_cache.reshape(nseq, r, kvw)
    cspec = pl.BlockSpec((sb, r, kvw), lambda i: (i, 0, 0))
    new = lambda col: pl.BlockSpec((None, sb, ts, kvw), lambda i: (4, i, 0, col))
    rep = D_HEADS // D_KV_HEADS
    by_row_group = sinks.reshape(D_KV_HEADS, rep).T.reshape(-1)
    sink_rows = jnp.broadcast_to(jnp.repeat(by_row_group, ts)[:, None], (D_HEADS * ts, 128))
    o, kn, vn = pl.pallas_call(
        _swa_sample_kernel,
        grid=(nseq // sb,),
        in_specs=[pl.BlockSpec((None, sb, ts, SLAB), lambda i: (3, i, 0, 0)), new(0), new(1), cspec, cspec,
                  pl.BlockSpec(sink_rows.shape, lambda i: (0, 0))],
        out_specs=[pl.BlockSpec((sb, ts, SLAB), lambda i: (i, 0, 0)), cspec, cspec],
        out_shape=[jax.ShapeDtypeStruct((nseq, ts, SLAB), F32)] + [jax.ShapeDtypeStruct(kc.shape, F32)] * 2,
        compiler_params=_cparams("parallel"),
        name="swa_sample",
    )(p4, p4, p4, kc, vc, sink_rows)
    return o.reshape(nseq * ts, SLAB), kn.reshape(k_cache.shape), vn.reshape(v_cache.shape)


def kernel(x_prompt, x_sample, state_a_C, state_a_n, state_a_m, state_a_conv, cache_b_k1, cache_b_v1, cache_b_k2, cache_b_v2, cache_b_k3, cache_b_v3, state_c_conv, state_c_ssm, cache_d_k, cache_d_v, e_norm_mix, e_w_in, e_b_gates, e_a_conv_w, e_a_conv_b, e_a_norm, e_w_out, o_norm_mix, o_w_in, o_c_conv_w, o_c_conv_b, o_c_dt_bias, o_c_A_log, o_c_D, o_c_norm, o_d_sinks, o_w_out, ffn_norm, w_gate, w_up, w_down, final_norm):
    depth = ffn_norm.shape[0]
    even_w = [_even_slabs(e_w_in[j]) for j in range((depth + 1) // 2)]
    odd_w = [_odd_slabs(o_w_in[j]) for j in range(depth // 2)]
    odd_wo = [jnp.concatenate([o_w_out[j][:C_DINNER], _group_aligned(o_w_out[j][C_DINNER:], 0)],
                              axis=0).astype(BF16) for j in range(depth // 2)]
    even_wo = [e_w_out[j].astype(BF16) for j in range((depth + 1) // 2)]
    wg16, wu16, wd16 = w_gate.astype(BF16), w_up.astype(BF16), w_down.astype(BF16)

    def trunk(x, pos, fresh, even_st, odd_st, ride=None, pre=None):
        nseq, t, _ = x.shape
        rode = {}
        assert t >= A_CONV - 1 and t >= C_CONV - 1
        cos_t, sin_t = _rope_tables(pos)
        x2 = x.reshape(nseq * t, D_MODEL)
        new_even, new_odd = [], []

        def tail(proj, s, keep, dil=1, lanes=slice(None)):
            p4 = proj.reshape(proj.shape[0], nseq, t, SLAB)
            if dil == 1:
                return p4[s, :, t - keep:, lanes]
            seg = PERM_TILE // dil
            last = p4[s, :, t - PERM_TILE:, lanes].reshape(nseq, dil, seg, -1)[:, :, seg - keep // dil:]
            return jnp.swapaxes(last, 1, 2).reshape(nseq, keep, -1)

        for layer in range(depth):
            j = layer // 2
            last = layer == depth - 1
            if layer % 2 == 0:
                if pre is not None and layer == 0:
                    pe = pre['proj']
                else:
                    pe = _norm_proj(x2, e_norm_mix[j], even_w[j], cos_t, sin_t, _even_kinds(fresh))
                st = {k: v[j] for k, v in even_st.items()}
                ya, c_new, n_new, m_new = _mlstm(pe, nseq, t, e_a_conv_w[j], e_a_conv_b[j], e_b_gates[j],
                                                 e_a_norm[j], st['a_conv'], st['a_C'], st['a_n'], st['a_m'])
                ns = {'a_C': c_new, 'a_n': n_new, 'a_m': m_new,
                      'a_conv': jnp.concatenate([tail(pe, 0, A_CONV - 1), tail(pe, 1, A_CONV - 1)], axis=-1)}
                parts = []
                for g in range(B_GROUPS):
                    kn, vn = f'b_k{g + 1}', f'b_v{g + 1}'
                    if fresh:
                        o_g, lse_g = _dilated_prompt(pe, nseq, t, g)
                        keep = min(B_WINDOWS[g], t)
                        if keep % 128 == 0 and keep <= PERM_TILE:
                            ns[kn], ns[vn] = _cache_tails(pe, nseq, t, g, keep)
                        else:
                            ns[kn] = tail(pe, 7 + g, keep, B_DILATIONS[g]).reshape(nseq, keep, B_HEADS, HEAD_DIM)
                            ns[vn] = tail(pe, 10 + g, keep, B_DILATIONS[g]).reshape(nseq, keep, B_HEADS, HEAD_DIM)
                    elif pre is not None and layer == 0 and g in pre['groups']:
                        o_g, lse_g, ns[kn], ns[vn] = pre[g]
                    else:
                        o_g, lse_g, ns[kn], ns[vn] = _dilated_sample(pe, nseq, t, g, st[kn], st[vn])
                    parts += [o_g, lse_g]
                yb = _merge_groups(parts, fresh)
                if ride is not None and layer == 0:
                    x2, rode['k'] = _post_rider(
                        x2, ya, yb, even_wo[j], layer, ffn_norm, wg16, wu16, wd16, final_norm, last, 'k',
                        ride['groups'], ride['proj'], ride['kt'], None)
                else:
                    x2 = _post(x2, ya, yb, even_wo[j], layer, ffn_norm, wg16, wu16, wd16, final_norm, last)
                new_even.append(ns)
            else:
                po = _norm_proj(x2, o_norm_mix[j], odd_w[j], cos_t, sin_t, ODD_KINDS)
                st = {k: v[j] for k, v in odd_st.items()}
                yc, ssm_new = _ssd(po, nseq, t, o_c_conv_w[j], o_c_conv_b[j], o_c_dt_bias[j], o_c_A_log[j],
                                   o_c_D[j], o_c_norm[j], st['c_conv'], st['c_ssm'])
                ns = {'c_conv': jnp.concatenate([tail(po, 1, C_CONV - 1), tail(po, 2, C_CONV - 1)], axis=-1),
                      'c_ssm': ssm_new}
                kvw = D_KV_HEADS * HEAD_DIM
                if fresh:
                    yd = _swa_prompt(po, nseq, t, o_d_sinks[j])
                    keep = min(D_WINDOW, t)
                    ns['d_k'] = tail(po, 4, keep, lanes=slice(0, kvw)).reshape(nseq, keep, D_KV_HEADS, HEAD_DIM)
                    ns['d_v'] = tail(po, 4, keep, lanes=slice(kvw, 2 * kvw)).reshape(
                        nseq, keep, D_KV_HEADS, HEAD_DIM)
                else:
                    yd, ns['d_k'], ns['d_v'] = _swa_sample(po, nseq, t, o_d_sinks[j], st['d_k'], st['d_v'])
                if ride is not None and layer == 1:
                    x2, rode['v'] = _post_rider(
                        x2, yc, yd, odd_wo[j], layer, ffn_norm, wg16, wu16, wd16, final_norm, last, 'v',
                        ride['groups'], ride['proj'], ride['vt'], [kpl[1] for kpl in rode['k']])
                else:
                    x2 = _post(x2, yc, yd, odd_wo[j], layer, ffn_norm, wg16, wu16, wd16, final_norm, last)
                new_odd.append(ns)
        out = {k: jnp.stack([ns[k] for ns in new_even]) for k in new_even[0]}
        out.update({k: jnp.stack([ns[k] for ns in new_odd]) for k in new_odd[0]})
        return x2.reshape(nseq, t, D_MODEL), out, rode

    bp, tp, _ = x_prompt.shape
    bs, ts, _ = x_sample.shape
    n_even, n_odd = (depth + 1) // 2, depth // 2
    zeros = lambda *shape: jnp.zeros(shape, F32)
    even_p = {'a_C': zeros(n_even, bp, A_HEADS, A_DK, A_DV), 'a_n': zeros(n_even, bp, A_HEADS, A_DK),
              'a_m': zeros(n_even, bp, A_HEADS), 'a_conv': zeros(n_even, bp, A_CONV - 1, 2 * SLAB)}
    odd_p = {'c_conv': zeros(n_odd, bp, C_CONV - 1, 2 * SLAB),
             'c_ssm': zeros(n_odd, bp, C_HEADS, C_HEADDIM, C_DSTATE)}
    even_s = {'a_C': state_a_C, 'a_n': state_a_n, 'a_m': state_a_m, 'a_conv': state_a_conv,
              'b_k1': cache_b_k1, 'b_v1': cache_b_v1, 'b_k2': cache_b_k2, 'b_v2': cache_b_v2,
              'b_k3': cache_b_k3, 'b_v3': cache_b_v3}
    odd_s = {'c_conv': state_c_conv, 'c_ssm': state_c_ssm, 'd_k': cache_d_k, 'd_v': cache_d_v}
    pos_p = jnp.arange(tp, dtype=jnp.int32)
    pos_s = jnp.tile(PAST_LEN + jnp.arange(ts, dtype=jnp.int32), bs)

    groups = tuple(range(B_GROUPS))
    k_caches = [even_s[f'b_k{g + 1}'][0] for g in groups]
    v_caches = [even_s[f'b_v{g + 1}'][0] for g in groups]
    tile = bp * tp * RIDER_PHASES // bs
    can_ride = (depth >= 2 and bp * tp * RIDER_PHASES % bs == 0 and tile % 8 == 0 and 8 <= tile <= 1024
                and ts == 8 and all(c.shape[1] % 128 == 0 and c.shape[1] >= B_WINDOWS[g]
                                    for g, c in zip(groups, k_caches)))
    if can_ride:
        cos_s, sin_s = _rope_tables(pos_s)
        pe_s = _norm_proj(x_sample.reshape(bs * ts, D_MODEL), e_norm_mix[0], even_w[0], cos_s, sin_s,
                          _even_kinds(False))
        ride = {'groups': groups, 'proj': pe_s, 'kt': [_rows_minor(c) for c in k_caches],
                'vt': [_rows_minor(c) for c in v_caches]}
        y_p, pn, rode = trunk(x_prompt, pos_p, True, even_p, odd_p, ride=ride)
        pre = {'groups': groups, 'proj': pe_s}
        for k, g in enumerate(groups):
            (k_out, _, lse_g), (v_out, o_g) = rode['k'][k], rode['v'][k]
            pre[g] = (o_g, lse_g, _rows_major(k_out, B_HEADS), _rows_major(v_out, B_HEADS))
        y_s, sn, _ = trunk(x_sample, pos_s, False, even_s, odd_s, pre=pre)
    else:
        y_p, pn, _ = trunk(x_prompt, pos_p, True, even_p, odd_p)
        y_s, sn, _ = trunk(x_sample, pos_s, False, even_s, odd_s)

    keys = ('a_C', 'a_n', 'a_m', 'a_conv', 'b_k1', 'b_v1', 'b_k2', 'b_v2', 'b_k3', 'b_v3',
            'c_conv', 'c_ssm', 'd_k', 'd_v')
    states = []
    for k in keys:
        states += [pn[k], sn[k]]
    return (y_p, y_s, *states)
```

```python
import functools
import math

import jax
import jax.numpy as jnp
from jax import lax
from jax.experimental import pallas as pl
from jax.experimental.pallas import tpu as pltpu

F32 = jnp.float32
BF16 = jnp.bfloat16

D_MODEL = 1024
PAST_LEN = 16384
EPS = 1e-6
ROPE_THETA = 10000.0
HEAD_DIM = 64
SLAB = 512

A_HEADS = 4
A_DK = 128
A_DV = 128
A_CONV = 4
A_CHUNK = 64

B_HEADS = 8
B_WINDOWS = (128, 512, 2048)
B_DILATIONS = (1, 4, 16)
B_GROUPS = 3

C_HEADS = 8
C_HEADDIM = 64
C_DINNER = C_HEADS * C_HEADDIM
C_DSTATE = 128
C_NGROUPS = 2
C_CONV = 4
C_CHUNK = 128

D_HEADS = 8
D_KV_HEADS = 2
D_WINDOW = 128

NEG = -1e30
VMEM_LIMIT = 56 * 1024 * 1024


VMEM_LIMIT_RIDER = 60 * 1024 * 1024


def _mix_dtype(rows):
    return BF16 if rows % 16 == 0 else F32


def _cparams(*sem, vmem=VMEM_LIMIT):
    return pltpu.CompilerParams(dimension_semantics=sem, vmem_limit_bytes=vmem)


def _rot_half(y):
    lane = lax.broadcasted_iota(jnp.int32, y.shape, 1)
    fwd = pltpu.roll(y, SLAB - HEAD_DIM // 2, 1)
    bwd = pltpu.roll(y, HEAD_DIM // 2, 1)
    return jnp.where(lane % HEAD_DIM < HEAD_DIM // 2, fwd, bwd)


PERM_TILE = 2048
PROJ_ROWS = 256
PERM_STRIDE = 4


def _proj_kernel(x_ref, nw_ref, w_ref, cos_ref, sin_ref, o_ref, xn_ref, *ybuf, kinds):
    j = pl.program_id(1)
    tm = o_ref.shape[0]

    @pl.when(j == 0)
    def _():
        x = x_ref[...]
        ms = jnp.mean(x * x, axis=-1, keepdims=True)
        xn_ref[...] = (x * lax.rsqrt(ms + EPS) * nw_ref[...]).astype(BF16)

    rc = min(tm, PROJ_ROWS)

    def emit(rope_lanes, dil):
        seg = tm // dil
        cseg = rc // dil
        for c in range(tm // rc):
            rows = slice(c * rc, (c + 1) * rc)
            y = jnp.dot(xn_ref[rows, :], w_ref[...], preferred_element_type=F32)
            val = y
            if rope_lanes:
                cos = jnp.tile(cos_ref[rows, :], (1, SLAB // 128))
                sin = jnp.tile(sin_ref[rows, :], (1, SLAB // 128))
                val = y * cos + _rot_half(y) * sin
                if rope_lanes < SLAB:
                    lane = lax.broadcasted_iota(jnp.int32, y.shape, 1)
                    val = jnp.where(lane < rope_lanes, val, y)
            if dil == 1:
                o_ref[rows, :] = val
            else:
                for cb in range(SLAB // 128):
                    ybuf[0][cb, rows, :] = val[:, cb * 128:(cb + 1) * 128]
                if dil <= PERM_STRIDE:
                    for r in range(dil):
                        dst = slice(r * seg + c * cseg, r * seg + (c + 1) * cseg)
                        for cb in range(SLAB // 128):
                            o_ref[dst, cb * 128:(cb + 1) * 128] = ybuf[0][cb, pl.ds(c * rc + r, cseg, stride=dil), :]
                else:
                    d2 = dil // PERM_STRIDE
                    half = rc // PERM_STRIDE
                    for cb in range(SLAB // 128):
                        for r0 in range(PERM_STRIDE):
                            ybuf[1][cb, r0 * half:(r0 + 1) * half, :] = \
                                ybuf[0][cb, pl.ds(c * rc + r0, half, stride=PERM_STRIDE), :]
                    for r0 in range(PERM_STRIDE):
                        for r1 in range(d2):
                            r = r1 * PERM_STRIDE + r0
                            dst = slice(r * seg + c * cseg, r * seg + (c + 1) * cseg)
                            for cb in range(SLAB // 128):
                                o_ref[dst, cb * 128:(cb + 1) * 128] = \
                                    ybuf[1][cb, pl.ds(r0 * half + r1, cseg, stride=d2), :]

    for kind in sorted(set(kinds)):
        member = functools.reduce(jnp.logical_or, [j == s for s, k in enumerate(kinds) if k == kind])
        pl.when(member)(functools.partial(emit, *kind))


def _norm_proj(x, norm_w, w_slabs, cos_t, sin_t, kinds):
    n = x.shape[0]
    s = w_slabs.shape[0]
    p = cos_t.shape[0]
    perm = any(d > 1 for _, d in kinds)
    big = perm or (n % PERM_TILE == 0 and p % PERM_TILE == 0)
    tm = PERM_TILE if big else min(1024, n, p)
    assert n % tm == 0 and p % tm == 0 and len(kinds) == s
    npt = p // tm
    scratch = [pltpu.VMEM((tm, D_MODEL), BF16)]
    if perm:
        scratch.append(pltpu.VMEM((SLAB // 128, tm, 128), F32))
    if any(d > PERM_STRIDE for _, d in kinds):
        scratch.append(pltpu.VMEM((SLAB // 128, min(tm, PROJ_ROWS), 128), F32))
    return pl.pallas_call(
        functools.partial(_proj_kernel, kinds=tuple(kinds)),
        grid=(n // tm, s),
        in_specs=[
            pl.BlockSpec((tm, D_MODEL), lambda i, j: (i, 0)),
            pl.BlockSpec((1, D_MODEL), lambda i, j: (0, 0)),
            pl.BlockSpec((None, D_MODEL, SLAB), lambda i, j: (j, 0, 0)),
            pl.BlockSpec((tm, 128), lambda i, j: (i % npt, 0)),
            pl.BlockSpec((tm, 128), lambda i, j: (i % npt, 0)),
        ],
        out_specs=pl.BlockSpec((None, tm, SLAB), lambda i, j: (j, i, 0)),
        out_shape=jax.ShapeDtypeStruct((s, n, SLAB), F32),
        scratch_shapes=scratch,
        compiler_params=_cparams("parallel", "arbitrary"),
        name="norm_proj",
    )(x, norm_w.reshape(1, D_MODEL), w_slabs, cos_t, sin_t)


def _even_kinds(perm):
    dil = B_DILATIONS if perm else (1,) * B_GROUPS
    return ([(0, 1)] * 4 + [(SLAB, d) for d in dil] * 2 + [(0, d) for d in dil] + [(0, 1)])


ODD_KINDS = [(0, 1)] * 3 + [(SLAB, 1), (D_KV_HEADS * HEAD_DIM, 1)]


def _rope_tables(pos):
    half = HEAD_DIM // 2
    inv = ROPE_THETA ** (-jnp.arange(half, dtype=F32) / half)
    ang = pos.astype(F32)[:, None] * inv[None, :]
    cos, sin = jnp.cos(ang), jnp.sin(ang)
    return (jnp.concatenate([cos, cos, cos, cos], axis=-1),
            jnp.concatenate([-sin, sin, -sin, sin], axis=-1))


def _pad_cols(w, width):
    return jnp.pad(w, ((0, 0), (0, width - w.shape[1])))


def _even_slabs(w_in):
    main = w_in[:, :2048].reshape(D_MODEL, 4, SLAB)
    gates = _pad_cols(w_in[:, 2048:2056], SLAB)[:, None, :]
    qkv = w_in[:, 2056:].reshape(D_MODEL, 9, SLAB)
    return jnp.moveaxis(jnp.concatenate([main, qkv, gates], axis=1), 1, 0).astype(BF16)


def _group_aligned(w, axis):
    rep = D_HEADS // D_KV_HEADS
    shape = w.shape
    split = shape[:axis] + (D_KV_HEADS, rep, HEAD_DIM) + shape[axis + 1:]
    return jnp.swapaxes(w.reshape(split), axis, axis + 1).reshape(shape)


def _odd_slabs(w_in):
    main = w_in[:, :1536].reshape(D_MODEL, 3, SLAB)
    qd = _group_aligned(w_in[:, 1544:2056], 1)[:, None, :]
    kvd = _pad_cols(jnp.concatenate([w_in[:, 2056:2312], w_in[:, 1536:1544]], axis=1), SLAB)[:, None, :]
    return jnp.moveaxis(jnp.concatenate([main, qd, kvd], axis=1), 1, 0).astype(BF16)


def _split3(x):
    hi = x.astype(BF16)
    r1 = x - hi.astype(F32)
    mid = r1.astype(BF16)
    lo = (r1 - mid.astype(F32)).astype(BF16)
    return hi, mid, lo


def _cumsum_rows(x, tri):
    hi, mid, lo = _split3(x)
    acc = jnp.dot(tri, lo, preferred_element_type=F32)
    acc = acc + jnp.dot(tri, mid, preferred_element_type=F32)
    return acc + jnp.dot(tri, hi, preferred_element_type=F32)


def _transpose_slab(s):
    l = s.shape[0]
    if l < 128:
        s = jnp.concatenate([s, jnp.zeros((128 - l, 128), s.dtype)], axis=0)
    return s.T


def _conv_silu(buf_ref, x, w, b):
    tb = x.shape[0]
    buf_ref[8:8 + tb, :] = x
    y = b + w[3:4] * x
    for j in range(3):
        y = y + w[j:j + 1] * buf_ref[5 + j:5 + j + tb, :]
    buf_ref[0:8, :] = buf_ref[tb:tb + 8, :]
    return y * jax.nn.sigmoid(y)


def _log_sigmoid(x):
    return jnp.minimum(x, 0.0) - jnp.log1p(jnp.exp(-jnp.abs(x)))


def _dot_t(a, b):
    return lax.dot_general(a, b, (((1,), (1,)), ((), ())), preferred_element_type=F32)


def _tdot(a, b):
    return lax.dot_general(a, b, (((0,), (0,)), ((), ())), preferred_element_type=F32)


def _mlstm_kernel(q_ref, k_ref, v_ref, og_ref, g_ref, cw_ref, cb_ref, bg_ref, an_ref,
                  conv0_ref, c0_ref, n0_ref, m0_ref,
                  ya_ref, c_ref, n_ref, m_ref, qbuf, kbuf, *, chunk):
    sb, tb, _ = q_ref.shape
    tblk = pl.program_id(1)
    lane = lax.broadcasted_iota(jnp.int32, (chunk, 128), 1)
    lane1 = lax.broadcasted_iota(jnp.int32, (1, 128), 1)
    row = lax.broadcasted_iota(jnp.int32, (chunk, chunk), 0)
    col = lax.broadcasted_iota(jnp.int32, (chunk, chunk), 1)
    causal = col <= row
    tri = causal.astype(BF16)
    scale = A_DK ** -0.5

    @pl.when(tblk == 0)
    def _():
        c_ref[...] = c0_ref[...]
        n_ref[...] = n0_ref[...]
        m_ref[...] = m0_ref[...]
        for si in range(sb):
            qbuf[si, 0:8, :] = conv0_ref[si, :, 0:SLAB]
            kbuf[si, 0:8, :] = conv0_ref[si, :, SLAB:2 * SLAB]

    nchunk = tb // chunk
    seqs, chunks, heads = range(sb), range(nchunk), range(A_HEADS)
    rows = lambda c: slice(c * chunk, (c + 1) * chunk)
    cols = lambda h: slice(h * 128, (h + 1) * 128)
    inst = [(si, c, h) for si in seqs for c in chunks for h in heads]
    qc = [_conv_silu(qbuf.at[si], q_ref[si], cw_ref[:, 0:SLAB], cb_ref[:, 0:SLAB]) for si in seqs]
    kc = [_conv_silu(kbuf.at[si], k_ref[si], cw_ref[:, SLAB:2 * SLAB], cb_ref[:, SLAB:2 * SLAB]) * scale
          for si in seqs]
    gates = {(si, c): g_ref[si, rows(c), 0:128] + bg_ref[...] for si in seqs for c in chunks}
    bc = {k: _cumsum_rows(_log_sigmoid(g), tri) for k, g in gates.items()}
    st = {k: _transpose_slab(jnp.where(lane < A_HEADS, gates[k], bc[k])) for k in gates}
    qh = {(si, c, h): qc[si][rows(c), cols(h)] for si, c, h in inst}
    kh = {(si, c, h): kc[si][rows(c), cols(h)] for si, c, h in inst}
    qb = {k: v.astype(BF16) for k, v in qh.items()}
    vb = {(si, c, h): v_ref[si, rows(c), cols(h)].astype(BF16) for si, c, h in inst}
    b_t = {(si, c, h): bc[si, c][:, A_HEADS + h:A_HEADS + h + 1] for si, c, h in inst}
    dlog = {(si, c, h): jnp.where(causal, b_t[si, c, h] - st[si, c][A_HEADS + h:A_HEADS + h + 1, 0:chunk]
                                  + st[si, c][h:h + 1, 0:chunk], NEG) for si, c, h in inst}
    m_loc = {k: jnp.max(v, axis=1, keepdims=True) for k, v in dlog.items()}
    qk = {k: _dot_t(qb[k], kh[k].astype(BF16)) for k in inst}
    s_loc = {k: qk[k] * jnp.exp(dlog[k] - m_loc[k]) for k in inst}
    sv = {k: jnp.dot(s_loc[k].astype(BF16), vb[k], preferred_element_type=F32) for k in inst}
    rs = {k: jnp.sum(s_loc[k], axis=1, keepdims=True) for k in inst}
    b_end = {k: b_t[k][chunk - 1:chunk, :] for k in inst}
    g_col = {(si, c, h): b_end[si, c, h] - b_t[si, c, h] + gates[si, c][:, h:h + 1] for si, c, h in inst}
    g_max = {k: jnp.max(v, axis=0, keepdims=True) for k, v in g_col.items()}
    kw = {k: kh[k] * jnp.exp(g_col[k] - g_max[k]) for k in inst}
    u = {k: _tdot(kw[k].astype(BF16), vb[k]) for k in inst}
    nk = {k: jnp.sum(kw[k], axis=0, keepdims=True) for k in inst}
    c_st = {(si, h): c_ref[si, h] for si in seqs for h in heads}
    n_st = {(si, h): n_ref[si, h:h + 1, :] for si in seqs for h in heads}
    m_h = {(si, h): m_ref[si][:, h:h + 1] for si in seqs for h in heads}
    for c in chunks:
        cur = [(si, c, h) for si in seqs for h in heads]
        inter = {k: b_t[k] + m_h[k[0], k[2]] for k in cur}
        m_t = {k: jnp.maximum(inter[k], m_loc[k]) for k in cur}
        w_loc = {k: jnp.exp(m_loc[k] - m_t[k]) for k in cur}
        w_inter = {k: jnp.exp(inter[k] - m_t[k]) for k in cur}
        qc_st = {k: jnp.dot(qb[k], c_st[k[0], k[2]].astype(BF16), preferred_element_type=F32) for k in cur}
        qn = {k: jnp.sum(qh[k] * n_st[k[0], k[2]], axis=1, keepdims=True) for k in cur}
        num = {k: w_loc[k] * sv[k] + w_inter[k] * qc_st[k] for k in cur}
        den = {k: w_loc[k] * rs[k] + w_inter[k] * qn[k] for k in cur}
        hh = {k: num[k] / jnp.maximum(jnp.abs(den[k]), jnp.exp(-m_t[k])) for k in cur}
        m_new = {k: jnp.maximum(b_end[k] + m_h[k[0], k[2]], g_max[k]) for k in cur}
        decay = {k: jnp.exp(b_end[k] + m_h[k[0], k[2]] - m_new[k]) for k in cur}
        gain = {k: jnp.exp(g_max[k] - m_new[k]) for k in cur}
        for k in cur:
            sh = (k[0], k[2])
            c_st[sh] = decay[k] * c_st[sh] + gain[k] * u[k]
            n_st[sh] = decay[k] * n_st[sh] + gain[k] * nk[k]
            m_h[sh] = m_new[k]
        ms = {k: jnp.mean(hh[k] * hh[k], axis=-1, keepdims=True) for k in cur}
        for si, _, h in cur:
            hn = hh[si, c, h] * lax.rsqrt(ms[si, c, h] + EPS) * an_ref[:, cols(h)]
            ya_ref[si, rows(c), cols(h)] = (hn * jax.nn.sigmoid(og_ref[si, rows(c), cols(h)])).astype(ya_ref.dtype)
    for si in seqs:
        m_vec = m_ref[si]
        for h in heads:
            c_ref[si, h] = c_st[si, h]
            n_ref[si, h:h + 1, :] = n_st[si, h]
            m_vec = jnp.where(lane1 == h, m_h[si, h], m_vec)
        m_ref[si] = m_vec


def _mlstm(proj, nseq, t, conv_w, conv_b, b_gates, a_norm, conv0, c0, n0, m0):
    chunk = A_CHUNK if t % A_CHUNK == 0 else t
    if t > chunk:
        sb, tb = min(nseq, 2), min(t, 8 * chunk)
    else:
        sb, tb = min(nseq, 8), t
    p4 = proj.reshape(proj.shape[0], nseq, t, SLAB)
    slab = lambda s: pl.BlockSpec((None, sb, tb, SLAB), lambda i, j, s=s: (s, i, j, 0))
    full = lambda a: pl.BlockSpec(a.shape, lambda i, j: (0,) * a.ndim)
    st = lambda a: pl.BlockSpec((sb,) + a.shape[1:], lambda i, j: (i,) + (0,) * (a.ndim - 1))
    bg = _pad_cols(b_gates.reshape(1, 2 * A_HEADS), 128)
    conv0p = jnp.pad(conv0, ((0, 0), (8 - (A_CONV - 1), 0), (0, 0)))
    m0p = _pad_cols(m0, 128).reshape(nseq, 1, 128)
    cb = conv_b.reshape(1, -1)
    an = a_norm.reshape(1, -1)
    ya, c, n, m = pl.pallas_call(
        functools.partial(_mlstm_kernel, chunk=chunk),
        grid=(nseq // sb, t // tb),
        in_specs=[slab(0), slab(1), slab(2), slab(3), slab(13),
                  full(conv_w), full(cb), full(bg), full(an),
                  st(conv0p), st(c0), st(n0), st(m0p)],
        out_specs=[pl.BlockSpec((sb, tb, SLAB), lambda i, j: (i, j, 0)), st(c0), st(n0), st(m0p)],
        out_shape=[jax.ShapeDtypeStruct((nseq, t, SLAB), _mix_dtype(t)),
                   jax.ShapeDtypeStruct(c0.shape, F32), jax.ShapeDtypeStruct(n0.shape, F32),
                   jax.ShapeDtypeStruct(m0p.shape, F32)],
        scratch_shapes=[pltpu.VMEM((sb, 8 + tb, SLAB), F32), pltpu.VMEM((sb, 8 + tb, SLAB), F32)],
        compiler_params=_cparams("parallel", "arbitrary"),
        name="mlstm",
    )(p4, p4, p4, p4, p4, conv_w, cb, bg, an, conv0p, c0, n0, m0p)
    return ya.reshape(nseq * t, SLAB), c, n, m[:, 0, :A_HEADS]


QBLK = 128


def _band_heads(q_m, k2, v2, valid, sinks):
    n = range(len(q_m))
    s = [jnp.where(valid[i], _dot_t(q_m[i], k2[i]), NEG) for i in n]
    m = [jnp.max(s[i], axis=1, keepdims=True) for i in n]
    if sinks is not None:
        m = [jnp.maximum(m[i], sinks[i]) for i in n]
    e = [jnp.exp(s[i] - m[i]) for i in n]
    l = [jnp.sum(e[i], axis=1, keepdims=True) for i in n]
    if sinks is not None:
        l = [l[i] + jnp.exp(sinks[i] - m[i]) for i in n]
    acc = [jnp.dot(e[i].astype(BF16), v2[i], preferred_element_type=F32) for i in n]
    return acc, m, l


def _band_masks(has_prev, window):
    a = lax.broadcasted_iota(jnp.int32, (QBLK, 2 * QBLK), 0)
    c = lax.broadcasted_iota(jnp.int32, (QBLK, 2 * QBLK), 1)
    rel = a - c + QBLK
    band = jnp.logical_and(rel >= 0, rel <= window)
    if has_prev is True:
        return band
    return jnp.logical_and(band, c >= jnp.where(has_prev, 0, QBLK))


def _key_rows(prev_ref, cur_ref, a, ls):
    if prev_ref.shape[0] == cur_ref.shape[0]:
        before = prev_ref[a * QBLK:(a + 1) * QBLK, ls]
    else:
        before = prev_ref[:, ls] if a == 0 else cur_ref[(a - 1) * QBLK:a * QBLK, ls]
    return jnp.concatenate([before, cur_ref[a * QBLK:(a + 1) * QBLK, ls]], axis=0).astype(BF16)


def _dilated_kernel(q_ref, kp_ref, kc_ref, vp_ref, vc_ref, o_ref, lse_ref):
    lane = lax.broadcasted_iota(jnp.int32, (1, 128), 1)
    low = lane < HEAD_DIM
    scale = HEAD_DIM ** -0.5
    chained = kp_ref.shape[0] != kc_ref.shape[0]
    subs = range(q_ref.shape[0] // QBLK)
    pairs = range(B_HEADS // 2)
    rows = lambda a: slice(a * QBLK, (a + 1) * QBLK)
    cols = lambda pr: slice(pr * 128, (pr + 1) * 128)
    valid = [_band_masks(True if (chained and a > 0) else pl.program_id(2) > 0, QBLK) for a in subs]
    qp = {(a, pr): q_ref[rows(a), cols(pr)] * scale for a in subs for pr in pairs}
    k2 = {(a, pr): _key_rows(kp_ref, kc_ref, a, cols(pr)) for a in subs for pr in pairs}
    v2 = {(a, pr): _key_rows(vp_ref, vc_ref, a, cols(pr)) for a in subs for pr in pairs}
    inst = [(a, pr, p) for a in subs for pr in pairs for p in range(2)]
    acc, m, l = _band_heads(
        [jnp.where(low if p == 0 else jnp.logical_not(low), qp[a, pr], 0.0).astype(BF16) for a, pr, p in inst],
        [k2[a, pr] for a, pr, _ in inst], [v2[a, pr] for a, pr, _ in inst], [valid[a] for a, _, _ in inst], None)
    o_h = [acc[i] / l[i] for i in range(len(inst))]
    lse_h = [m[i] + jnp.log(l[i]) for i in range(len(inst))]
    for i in range(0, len(inst), 2):
        a, pr, _ = inst[i]
        o_ref[rows(a), cols(pr)] = jnp.where(low, o_h[i], o_h[i + 1])
        lse_ref[rows(a), cols(pr)] = jnp.where(low, lse_h[i], lse_h[i + 1])


def _dilated_prompt(proj, b, t, g):
    d = B_DILATIONS[g]
    assert t % PERM_TILE == 0
    nb = PERM_TILE // d // QBLK
    per_tile = PERM_TILE // QBLK
    pv = proj.reshape(proj.shape[0], b, t, SLAB)

    def blk(i, r):
        return (i // nb) * per_tile + r * nb + i % nb

    if nb % 2 == 0:
        nq = 4 if nb % 4 == 0 else 2
        classes, steps = d, t // d // QBLK // nq
        cur_idx = lambda i, r: blk(nq * i, r) // nq
        prev_rows, prev_idx = QBLK, lambda i, r: blk(jnp.maximum(nq * i - 1, 0), r)
    elif nb == 1 and d % 2 == 0:
        nq, classes, steps = 2, d // 2, t // d // QBLK
        cur_idx = lambda i, r: blk(i, 2 * r) // 2
        prev_rows, prev_idx = 2 * QBLK, lambda i, r: blk(jnp.maximum(i - 1, 0), 2 * r) // 2
    else:
        nq, classes, steps = 1, d, t // d // QBLK
        cur_idx = blk
        prev_rows, prev_idx = QBLK, lambda i, r: blk(jnp.maximum(i - 1, 0), r)
    cur = lambda s: pl.BlockSpec((None, None, nq * QBLK, SLAB), lambda bi, r, i, s=s: (s, bi, cur_idx(i, r), 0))
    prev = lambda s: pl.BlockSpec((None, None, prev_rows, SLAB), lambda bi, r, i, s=s: (s, bi, prev_idx(i, r), 0))
    ospec = pl.BlockSpec((None, nq * QBLK, SLAB), lambda bi, r, i: (bi, cur_idx(i, r), 0))
    sq, sk, sv = 4 + g, 7 + g, 10 + g
    o, lse = pl.pallas_call(
        _dilated_kernel,
        grid=(b, classes, steps),
        in_specs=[cur(sq), prev(sk), cur(sk), prev(sv), cur(sv)],
        out_specs=[ospec, ospec],
        out_shape=[jax.ShapeDtypeStruct((b, t, SLAB), F32)] * 2,
        compiler_params=_cparams("parallel", "parallel", "arbitrary"),
        name=f"dilated_g{g}",
    )(pv, pv, pv, pv, pv)
    return o.reshape(b * t, SLAB), lse.reshape(b * t, SLAB)


def _tails_kernel(x_ref, o_ref, *buf, dil):
    tm = x_ref.shape[0]
    keep = o_ref.shape[1]
    if dil == 1:
        o_ref[...] = x_ref[tm - keep:tm, :].T
    else:
        seg = tm // dil
        for r in range(dil):
            buf[0][pl.ds(r, seg, stride=dil), :] = x_ref[r * seg:(r + 1) * seg, :]
        o_ref[...] = buf[0][tm - keep:tm, :].T


def _cache_tails(proj, nseq, t, g, keep):
    dil = B_DILATIONS[g]
    pv = proj.reshape(proj.shape[0], nseq, t, SLAB)
    rows = keep if dil == 1 else PERM_TILE
    last = t // rows - 1
    out = pl.pallas_call(
        functools.partial(_tails_kernel, dil=dil),
        grid=(2, nseq, SLAB // 128),
        in_specs=[pl.BlockSpec((None, None, rows, 128), lambda kv, b, cb: (7 + g + B_GROUPS * kv, b, last, cb))],
        out_specs=pl.BlockSpec((None, None, 128, keep), lambda kv, b, cb: (kv, b, cb, 0)),
        out_shape=jax.ShapeDtypeStruct((2, nseq, SLAB, keep), F32),
        scratch_shapes=[pltpu.VMEM((PERM_TILE, 128), F32)] if dil > 1 else [],
        compiler_params=_cparams("parallel", "parallel", "parallel"),
        name=f"cache_tails_g{g}",
    )(pv)
    return _rows_major(out[0], B_HEADS), _rows_major(out[1], B_HEADS)


def _merge_kernel(*refs, dils):
    ins, y_ref, bufs = refs[:6], refs[6], refs[7:]
    tm = y_ref.shape[0]
    vals = []
    nbuf = 0
    for k, ref in enumerate(ins):
        dil = dils[k // 2]
        if dil == 1:
            vals.append(ref[...])
        else:
            buf = bufs[nbuf]
            nbuf += 1
            seg = tm // dil
            for r in range(dil):
                buf[pl.ds(r, seg, stride=dil), :] = ref[r * seg:(r + 1) * seg, :]
            vals.append(buf[...])
    o1, l1, o2, l2, o3, l3 = vals
    m = jnp.maximum(jnp.maximum(l1, l2), l3)
    w1, w2, w3 = jnp.exp(l1 - m), jnp.exp(l2 - m), jnp.exp(l3 - m)
    y_ref[...] = ((w1 * o1 + w2 * o2 + w3 * o3) / (w1 + w2 + w3)).astype(y_ref.dtype)


def _merge_groups(parts, perm):
    n = parts[0].shape[0]
    dils = B_DILATIONS if perm else (1,) * B_GROUPS
    tm = PERM_TILE if perm else min(1024, n)
    spec = pl.BlockSpec((tm, 128), lambda i, j: (i, j))
    nbuf = 2 * sum(d > 1 for d in dils)
    return pl.pallas_call(
        functools.partial(_merge_kernel, dils=dils), grid=(n // tm, SLAB // 128), in_specs=[spec] * 6,
        out_specs=spec, out_shape=jax.ShapeDtypeStruct((n, SLAB), _mix_dtype(tm)),
        scratch_shapes=[pltpu.VMEM((tm, 128), F32)] * nbuf,
        compiler_params=_cparams("parallel", "parallel"), name="merge_groups",
    )(*parts)


def _swa_kernel(q_ref, kp_ref, kc_ref, vp_ref, vc_ref, sink_ref, o_ref):
    lane = lax.broadcasted_iota(jnp.int32, (1, 128), 1)
    scale = HEAD_DIM ** -0.5
    rep = D_HEADS // D_KV_HEADS
    low = lane < HEAD_DIM
    allk = slice(None)
    subs = range(q_ref.shape[0] // QBLK)
    rows = lambda a: slice(a * QBLK, (a + 1) * QBLK)
    cols = lambda pr: slice(pr * 128, (pr + 1) * 128)
    valid = [_band_masks(pl.program_id(1) > 0 if a == 0 else True, D_WINDOW - 1) for a in subs]
    k2 = [_key_rows(kp_ref, kc_ref, a, allk) for a in subs]
    v2 = [_key_rows(vp_ref, vc_ref, a, allk) for a in subs]
    qp = {(a, pr): q_ref[rows(a), cols(pr)] * scale for a in subs for pr in range(rep)}
    inst = [(a, pr, g) for a in subs for pr in range(rep) for g in range(D_KV_HEADS)]
    acc, m, l = _band_heads(
        [jnp.where(low if g == 0 else jnp.logical_not(low), qp[a, pr], 0.0).astype(BF16) for a, pr, g in inst],
        [k2[a] for a, _, _ in inst], [v2[a] for a, _, _ in inst], [valid[a] for a, _, _ in inst],
        [sink_ref[g * rep + pr] for _, pr, g in inst])
    o_h = [acc[i] / l[i] for i in range(len(inst))]
    for i in range(0, len(inst), 2):
        a, pr, _ = inst[i]
        o_ref[rows(a), cols(pr)] = jnp.where(low, o_h[i], o_h[i + 1]).astype(o_ref.dtype)


def _swa_prompt(proj, b, t, sinks):
    pv = proj.reshape(proj.shape[0], b, t, SLAB)
    nq = next(k for k in (4, 2, 1) if t % (k * QBLK) == 0)
    qspec = pl.BlockSpec((None, None, nq * QBLK, SLAB), lambda bi, i: (3, bi, i, 0))

    def kv(col, prev):
        if prev:
            return pl.BlockSpec((None, None, QBLK, 128), lambda bi, i: (4, bi, jnp.maximum(nq * i - 1, 0), col))
        return pl.BlockSpec((None, None, nq * QBLK, 128), lambda bi, i: (4, bi, i, col))

    o = pl.pallas_call(
        _swa_kernel,
        grid=(b, t // QBLK // nq),
        in_specs=[qspec, kv(0, True), kv(0, False), kv(1, True), kv(1, False),
                  pl.BlockSpec(memory_space=pltpu.SMEM)],
        out_specs=pl.BlockSpec((None, nq * QBLK, SLAB), lambda bi, i: (bi, i, 0)),
        out_shape=jax.ShapeDtypeStruct((b, t, SLAB), _mix_dtype(t)),
        compiler_params=_cparams("parallel", "arbitrary"),
        name="swa_prompt",
    )(pv, pv, pv, pv, pv, sinks)
    return o.reshape(b * t, SLAB)


def _softplus(x):
    return jnp.maximum(x, 0.0) + jnp.log1p(jnp.exp(-jnp.abs(x)))


def _ssd_kernel(z_ref, x_ref, bc_ref, dt_ref, cw_ref, cb_ref, dtb_ref, alog_ref, dl_ref, cn_ref,
                conv0_ref, h0_ref, y_ref, h_ref, xbuf, bcbuf, *, chunk):
    sb, tb, _ = x_ref.shape
    tblk = pl.program_id(1)
    lane1 = lax.broadcasted_iota(jnp.int32, (1, 128), 1)
    low = lane1 < C_HEADDIM
    row = lax.broadcasted_iota(jnp.int32, (chunk, chunk), 0)
    col = lax.broadcasted_iota(jnp.int32, (chunk, chunk), 1)
    causal = col <= row
    tri = causal.astype(BF16)
    srow_low = lax.broadcasted_iota(jnp.int32, (128, 1), 0) < C_HEADDIM
    hpg = C_HEADS // C_NGROUPS
    a_neg = -jnp.exp(alog_ref[...])

    @pl.when(tblk == 0)
    def _():
        h_ref[...] = h0_ref[...]
        for si in range(sb):
            xbuf[si, 0:8, :] = conv0_ref[si, :, 0:SLAB]
            bcbuf[si, 0:8, :] = conv0_ref[si, :, SLAB:2 * SLAB]

    seqs, chunks = range(sb), range(tb // chunk)
    ppg = hpg // 2
    npair = C_HEADS // 2
    rows = lambda c: slice(c * chunk, (c + 1) * chunk)
    cols = lambda pr: slice(pr * 128, (pr + 1) * 128)
    sc = [(si, c) for si in seqs for c in chunks]
    sch = [(si, c, h) for si, c in sc for h in range(C_HEADS)]
    scp = [(si, c, pr) for si, c in sc for pr in range(npair)]
    xc = [_conv_silu(xbuf.at[si], x_ref[si], cw_ref[:, 0:SLAB], cb_ref[:, 0:SLAB]) for si in seqs]
    bcc = [_conv_silu(bcbuf.at[si], bc_ref[si], cw_ref[:, SLAB:2 * SLAB], cb_ref[:, SLAB:2 * SLAB]) for si in seqs]
    dt = {(si, c): _softplus(dt_ref[si, rows(c), :] + dtb_ref[...]) for si, c in sc}
    cum = {k: _cumsum_rows(v * a_neg, tri) for k, v in dt.items()}
    cum_rows = {k: _transpose_slab(v) for k, v in cum.items()}
    dt_rows = {k: _transpose_slab(v) for k, v in dt.items()}
    bgm = {(si, c, g): bcc[si][rows(c), g * 128:(g + 1) * 128].astype(BF16) for si, c in sc
           for g in range(C_NGROUPS)}
    cgm = {(si, c, g): bcc[si][rows(c), 256 + g * 128:256 + (g + 1) * 128].astype(BF16) for si, c in sc
           for g in range(C_NGROUPS)}
    cbm = {k: _dot_t(cgm[k], bgm[k]) for k in bgm}
    x_pair = {(si, c, pr): xc[si][rows(c), cols(pr)] for si, c, pr in scp}
    cum_t = {(si, c, h): cum[si, c][:, h:h + 1] for si, c, h in sch}
    cum_e = {(si, c, h): cum[si, c][chunk - 1:chunk, h:h + 1] for si, c, h in sch}
    seg = {(si, c, h): jnp.where(causal, cum_t[si, c, h] - cum_rows[si, c][h:h + 1, 0:chunk], NEG)
           for si, c, h in sch}
    w = {(si, c, h): cbm[si, c, h // hpg] * jnp.exp(seg[si, c, h]) * dt_rows[si, c][h:h + 1, 0:chunk]
         for si, c, h in sch}
    xhalf = {(si, c, h): jnp.where(low if h % 2 == 0 else jnp.logical_not(low), x_pair[si, c, h // 2],
                                   0.0).astype(BF16) for si, c, h in sch}
    yh = {k: jnp.dot(w[k].astype(BF16), xhalf[k], preferred_element_type=F32) for k in sch}
    e_col = {k: jnp.exp(cum_t[k]) for k in sch}
    w_end = {(si, c, h): jnp.exp(cum_e[si, c, h] - cum_t[si, c, h]) * dt[si, c][:, h:h + 1] for si, c, h in sch}
    dec = {k: jnp.exp(cum_e[k]) for k in sch}
    y_intra = {(si, c, pr): yh[si, c, 2 * pr] + yh[si, c, 2 * pr + 1] + dl_ref[:, cols(pr)] * x_pair[si, c, pr]
               for si, c, pr in scp}
    xw = {(si, c, pr): x_pair[si, c, pr] * jnp.where(low, w_end[si, c, 2 * pr], w_end[si, c, 2 * pr + 1])
          for si, c, pr in scp}
    upd = {(si, c, pr): _tdot(xw[si, c, pr].astype(BF16), bgm[si, c, pr // ppg]) for si, c, pr in scp}
    e_sel = {(si, c, pr): jnp.where(low, e_col[si, c, 2 * pr], e_col[si, c, 2 * pr + 1]) for si, c, pr in scp}
    d_sel = {(si, c, pr): jnp.where(srow_low, dec[si, c, 2 * pr], dec[si, c, 2 * pr + 1]) for si, c, pr in scp}
    gate = {}
    for si, c, pr in scp:
        zz = z_ref[si, rows(c), cols(pr)]
        gate[si, c, pr] = zz * jax.nn.sigmoid(zz)
    hs = {(si, pr): h_ref[si, pr] for si in seqs for pr in range(npair)}
    for c in chunks:
        cur = [(si, c, pr) for si in seqs for pr in range(npair)]
        inter = {k: _dot_t(cgm[k[0], c, k[2] // ppg], hs[k[0], k[2]].astype(BF16)) * e_sel[k] for k in cur}
        for k in cur:
            hs[k[0], k[2]] = d_sel[k] * hs[k[0], k[2]] + upd[k]
        yz = {k: (y_intra[k] + inter[k]) * gate[k] for k in cur}
        sq = {k: jnp.sum(yz[k] * yz[k], axis=-1, keepdims=True) for k in cur}
        for si in seqs:
            for g in range(C_NGROUPS):
                prs = range(g * ppg, (g + 1) * ppg)
                inv = lax.rsqrt(sum(sq[si, c, pr] for pr in prs) / (hpg * C_HEADDIM) + EPS)
                for pr in prs:
                    y_ref[si, rows(c), cols(pr)] = (yz[si, c, pr] * inv * cn_ref[:, cols(pr)]).astype(y_ref.dtype)
    for si in seqs:
        for pr in range(npair):
            h_ref[si, pr] = hs[si, pr]


def _ssd(proj, nseq, t, conv_w, conv_b, dt_bias, a_log, d_skip, c_norm, conv0, h0):
    chunk = C_CHUNK if t % C_CHUNK == 0 else t
    if t > chunk:
        sb, tb = min(nseq, 2), min(t, 4 * chunk)
    else:
        sb, tb = min(nseq, 8), t
    p4 = proj.reshape(proj.shape[0], nseq, t, SLAB)
    slab = lambda s: pl.BlockSpec((None, sb, tb, SLAB), lambda i, j, s=s: (s, i, j, 0))
    full = lambda a: pl.BlockSpec(a.shape, lambda i, j: (0,) * a.ndim)
    st = lambda a: pl.BlockSpec((sb,) + a.shape[1:], lambda i, j: (i,) + (0,) * (a.ndim - 1))
    conv0p = jnp.pad(conv0, ((0, 0), (8 - (C_CONV - 1), 0), (0, 0)))
    hp = h0.reshape(nseq, C_HEADS // 2, 128, C_DSTATE)
    cb = conv_b.reshape(1, -1)
    dtb = _pad_cols(dt_bias.reshape(1, C_HEADS), 128)
    alog = _pad_cols(a_log.reshape(1, C_HEADS), 128)
    dl = jnp.repeat(d_skip, C_HEADDIM).reshape(1, C_DINNER)
    cn = c_norm.reshape(1, C_DINNER)
    y, h = pl.pallas_call(
        functools.partial(_ssd_kernel, chunk=chunk),
        grid=(nseq // sb, t // tb),
        in_specs=[slab(0), slab(1), slab(2),
                  pl.BlockSpec((None, sb, tb, 128), lambda i, j: (4, i, j, 2)),
                  full(conv_w), full(cb), full(dtb), full(alog), full(dl), full(cn), st(conv0p), st(hp)],
        out_specs=[pl.BlockSpec((sb, tb, SLAB), lambda i, j: (i, j, 0)), st(hp)],
        out_shape=[jax.ShapeDtypeStruct((nseq, t, SLAB), _mix_dtype(t)), jax.ShapeDtypeStruct(hp.shape, F32)],
        scratch_shapes=[pltpu.VMEM((sb, 8 + tb, SLAB), F32), pltpu.VMEM((sb, 8 + tb, SLAB), F32)],
        compiler_params=_cparams("parallel", "arbitrary"),
        name="ssd",
    )(p4, p4, p4, p4, conv_w, cb, dtb, alog, dl, cn, conv0p, hp)
    return y.reshape(nseq * t, SLAB), h.reshape(nseq, C_HEADS, C_HEADDIM, C_DSTATE)


FFN_CHUNK = 256


def _post_kernel(x_ref, ya_ref, yb_ref, wo_ref, fn_ref, wg_ref, wu_ref, wd_ref, on_ref, o_ref, act_ref, *, final):
    ymix = jnp.concatenate([ya_ref[...], yb_ref[...]], axis=-1).astype(BF16)
    x1 = x_ref[...] + jnp.dot(ymix, wo_ref[...], preferred_element_type=F32)
    ms = jnp.mean(x1 * x1, axis=-1, keepdims=True)
    hn = (x1 * lax.rsqrt(ms + EPS) * fn_ref[...]).astype(BF16)
    hidden = wg_ref.shape[1]
    for c in range(hidden // FFN_CHUNK):
        cs = slice(c * FFN_CHUNK, (c + 1) * FFN_CHUNK)
        gate = jnp.dot(hn, wg_ref[:, cs], preferred_element_type=F32)
        up = jnp.dot(hn, wu_ref[:, cs], preferred_element_type=F32)
        act_ref[:, cs] = (gate * jax.nn.sigmoid(gate) * up).astype(BF16)
    x2 = x1 + jnp.dot(act_ref[...], wd_ref[...], preferred_element_type=F32)
    if final:
        ms2 = jnp.mean(x2 * x2, axis=-1, keepdims=True)
        x2 = x2 * lax.rsqrt(ms2 + EPS) * on_ref[...]
    o_ref[...] = x2


def _post(x, ya, yb, w_out, layer, ffn_norm, w_gate, w_up, w_down, out_norm, final):
    n = x.shape[0]
    tm = min(1024, n)
    hidden = w_gate.shape[2]
    row = lambda w: pl.BlockSpec((tm, w), lambda i: (i, 0))
    res = lambda a: pl.BlockSpec(a.shape, lambda i: (0, 0), pipeline_mode=pl.Buffered(1))
    lay = lambda a: pl.BlockSpec((None,) + a.shape[1:], lambda i: (layer, 0, 0), pipeline_mode=pl.Buffered(1))
    fn = ffn_norm.reshape(-1, 1, D_MODEL)
    on = out_norm.reshape(1, D_MODEL)
    return pl.pallas_call(
        functools.partial(_post_kernel, final=final),
        grid=(n // tm,),
        in_specs=[row(D_MODEL), row(SLAB), row(SLAB), res(w_out), lay(fn), lay(w_gate), lay(w_up), lay(w_down),
                  res(on)],
        out_specs=row(D_MODEL),
        out_shape=jax.ShapeDtypeStruct((n, D_MODEL), F32),
        scratch_shapes=[pltpu.VMEM((tm, hidden), BF16)],
        compiler_params=_cparams("parallel"),
        name="post_ffn",
    )(x, ya, yb, w_out, fn, w_gate, w_up, w_down, on)


RIDER_PHASES = 4


def _rider_mask(ts, r, window, step):
    t = lax.broadcasted_iota(jnp.int32, (2 * ts, r + 128), 0) % ts
    c = lax.broadcasted_iota(jnp.int32, (2 * ts, r + 128), 1)
    rel = r + t - jnp.where(c < 128, c, c - 128 + ts)
    valid = jnp.logical_and(jnp.logical_and(rel >= 0, rel <= window), (rel & (step - 1)) == 0)
    return jnp.logical_and(valid, jnp.logical_or(c < ts, c >= 128))


def _shift_rows_minor(src_ref, new_t, rows, ts):
    r = src_ref.shape[1]
    lane = lax.broadcasted_iota(jnp.int32, (1, 128), 1)
    old = src_ref[rows, :]
    rolled = pltpu.roll(old, r - ts, 1)
    tail = jnp.where(lane < 128 - ts, rolled[:, r - 128:], new_t[rows, :])
    out = tail if r == 128 else jnp.concatenate([rolled[:, :r - 128], tail], axis=1)
    return old, out


def _new_rows_t(new_ref):
    ts = new_ref.shape[0]
    pad = jnp.zeros((128 - ts, SLAB), F32)
    return jnp.concatenate([pad, new_ref[...]], axis=0).T


def _rider_k(q_ref, kn_ref, kc_ref, ko_ref, p_ref, lse_ref, window, step):
    ts = q_ref.shape[0]
    r = kc_ref.shape[1]
    low = lax.broadcasted_iota(jnp.int32, (1, 128), 1) < HEAD_DIM
    valid = _rider_mask(ts, r, window, step)
    kn_t = _new_rows_t(kn_ref)
    scale = HEAD_DIM ** -0.5
    pairs = range(B_HEADS // 2)
    rows = lambda pr: slice(pr * 128, (pr + 1) * 128)
    s = []
    for pr in pairs:
        k_old, k_out = _shift_rows_minor(kc_ref, kn_t, rows(pr), ts)
        ko_ref[rows(pr), :] = k_out
        k_ext = jnp.concatenate([k_old[:, 0:128], k_out], axis=1).astype(BF16)
        qp = q_ref[:, rows(pr)] * scale
        q2 = jnp.concatenate([jnp.where(low, qp, 0.0), jnp.where(low, 0.0, qp)], axis=0).astype(BF16)
        s.append(jnp.where(valid, jnp.dot(q2, k_ext, preferred_element_type=F32), NEG))
    m = [jnp.max(s[pr], axis=1, keepdims=True) for pr in pairs]
    e = [jnp.exp(s[pr] - m[pr]) for pr in pairs]
    l = [jnp.sum(e[pr], axis=1, keepdims=True) for pr in pairs]
    for pr in pairs:
        p_ref[pr * 2 * ts:(pr + 1) * 2 * ts, :] = (e[pr] / l[pr]).astype(BF16)
        lse = m[pr] + jnp.log(l[pr])
        lse_ref[:, rows(pr)] = jnp.where(low, lse[0:ts], lse[ts:2 * ts])


def _rider_v(p_ref, vn_ref, vc_ref, vo_ref, o_ref):
    ts = vn_ref.shape[0]
    low = lax.broadcasted_iota(jnp.int32, (1, 128), 1) < HEAD_DIM
    vn_t = _new_rows_t(vn_ref)
    for pr in range(B_HEADS // 2):
        rows = slice(pr * 128, (pr + 1) * 128)
        v_old, v_out = _shift_rows_minor(vc_ref, vn_t, rows, ts)
        vo_ref[rows, :] = v_out
        v_ext = jnp.concatenate([v_old[:, 0:128], v_out], axis=1).astype(BF16)
        a = _dot_t(p_ref[pr * 2 * ts:(pr + 1) * 2 * ts, :], v_ext)
        o_ref[:, rows] = jnp.where(low, a[0:ts], a[ts:2 * ts])


def _post_rider_kernel(x_ref, ya_ref, yb_ref, wo_ref, fn_ref, wg_ref, wu_ref, wd_ref, on_ref, *rest,
                       final, side, groups):
    ng = len(groups)
    nout = 3 if side == 'k' else 2
    ins, o_ref, outs = rest[:3 * ng], rest[3 * ng], rest[3 * ng + 1:3 * ng + 1 + nout * ng]
    act_ref, hn_ref = rest[3 * ng + 1 + nout * ng:]
    x1_ref = o_ref

    def ride():
        for k, g in enumerate(groups):
            i3, o3 = ins[3 * k:3 * k + 3], outs[nout * k:nout * k + nout]
            if side == 'k':
                _rider_k(*i3, *o3, B_WINDOWS[g], B_DILATIONS[g])
            else:
                _rider_v(*i3, *o3)

    u = pl.program_id(1)
    nchunk = wg_ref.shape[1] // FFN_CHUNK
    per = -(-nchunk // (RIDER_PHASES - 1))
    first = nchunk - per * (RIDER_PHASES - 2)

    def hidden_chunks(lo, hi):
        hn = hn_ref[...]
        for c in range(lo, hi):
            cs = slice(c * FFN_CHUNK, (c + 1) * FFN_CHUNK)
            gate = jnp.dot(hn, wg_ref[:, cs], preferred_element_type=F32)
            up = jnp.dot(hn, wu_ref[:, cs], preferred_element_type=F32)
            act_ref[:, cs] = (gate * jax.nn.sigmoid(gate) * up).astype(BF16)

    def phase(k):
        def body():
            if k == 0:
                ymix = jnp.concatenate([ya_ref[...], yb_ref[...]], axis=-1).astype(BF16)
                x1 = x_ref[...] + jnp.dot(ymix, wo_ref[...], preferred_element_type=F32)
                ms = jnp.mean(x1 * x1, axis=-1, keepdims=True)
                x1_ref[...] = x1
                hn_ref[...] = (x1 * lax.rsqrt(ms + EPS) * fn_ref[...]).astype(BF16)
                hidden_chunks(0, first)
            elif k < RIDER_PHASES - 1:
                hidden_chunks(first + (k - 1) * per, first + k * per)
            else:
                x2 = x1_ref[...] + jnp.dot(act_ref[...], wd_ref[...], preferred_element_type=F32)
                if final:
                    ms2 = jnp.mean(x2 * x2, axis=-1, keepdims=True)
                    x2 = x2 * lax.rsqrt(ms2 + EPS) * on_ref[...]
                o_ref[...] = x2
            ride()
        return body

    for k in range(RIDER_PHASES):
        pl.when(u == k)(phase(k))


def _post_rider(x, ya, yb, w_out, layer, ffn_norm, w_gate, w_up, w_down, out_norm, final, side, groups, proj_s,
                caches_t, probs):
    n = x.shape[0]
    nseq = caches_t[0].shape[0]
    ts = proj_s.shape[1] // nseq
    tm = n * RIDER_PHASES // nseq
    hidden = w_gate.shape[2]
    row = lambda w, **kw: pl.BlockSpec((tm, w), lambda i, u: (i, 0), **kw)
    res = lambda a: pl.BlockSpec(a.shape, lambda i, u: (0, 0), pipeline_mode=pl.Buffered(1))
    lay = lambda a: pl.BlockSpec((None,) + a.shape[1:], lambda i, u: (layer, 0, 0), pipeline_mode=pl.Buffered(1))
    seq = lambda i, u: i * RIDER_PHASES + u
    new = lambda s: pl.BlockSpec((None, ts, SLAB), lambda i, u: (s, seq(i, u), 0))
    rows8 = pl.BlockSpec((ts, SLAB), lambda i, u: (seq(i, u), 0))
    fn = ffn_norm.reshape(-1, 1, D_MODEL)
    on = out_norm.reshape(1, D_MODEL)
    common = [row(D_MODEL), row(SLAB), row(SLAB), res(w_out), lay(fn), lay(w_gate),
              lay(w_up), lay(w_down), res(on)]
    small = jax.ShapeDtypeStruct((nseq * ts, SLAB), F32)
    ins, args, outs, shapes = [], [], [row(D_MODEL)], [jax.ShapeDtypeStruct((n, D_MODEL), F32)]
    for k, g in enumerate(groups):
        r = caches_t[k].shape[2]
        cspec = pl.BlockSpec((None, SLAB, r), lambda i, u: (seq(i, u), 0, 0))
        pspec = pl.BlockSpec((None, B_HEADS * ts, r + 128), lambda i, u: (seq(i, u), 0, 0))
        cout = jax.ShapeDtypeStruct(caches_t[k].shape, F32)
        if side == 'k':
            ins += [new(4 + g), new(7 + g), cspec]
            args += [proj_s, proj_s, caches_t[k]]
            outs += [cspec, pspec, rows8]
            shapes += [cout, jax.ShapeDtypeStruct((nseq, B_HEADS * ts, r + 128), BF16), small]
        else:
            ins += [pspec, new(10 + g), cspec]
            args += [probs[k], proj_s, caches_t[k]]
            outs += [cspec, rows8]
            shapes += [cout, small]
    res_all = pl.pallas_call(
        functools.partial(_post_rider_kernel, final=final, side=side, groups=tuple(groups)),
        grid=(n // tm, RIDER_PHASES),
        in_specs=common + ins,
        out_specs=outs,
        out_shape=shapes,
        scratch_shapes=[pltpu.VMEM((tm, hidden), BF16), pltpu.VMEM((tm, D_MODEL), BF16)],
        compiler_params=_cparams("parallel", "arbitrary", vmem=VMEM_LIMIT_RIDER),
        name=f"post_ffn_ride_{side}",
    )(x, ya, yb, w_out, fn, w_gate, w_up, w_down, on, *args)
    nout = 3 if side == 'k' else 2
    return res_all[0], [res_all[1 + nout * k:1 + nout * (k + 1)] for k in range(len(groups))]


NEWPAD = 128


def _ext_mask(rows, r, ts, window, step):
    t = lax.broadcasted_iota(jnp.int32, (rows, r + NEWPAD), 0) % ts
    c = lax.broadcasted_iota(jnp.int32, (rows, r + NEWPAD), 1)
    rel = r + t - c
    ok = jnp.logical_and(jnp.logical_and(rel >= 0, rel <= window), c < r + ts)
    return jnp.logical_and(ok, (rel & (step - 1)) == 0)


def _shift_cache(dst_ref, src_ref, new, si, r, ts):
    dst_ref[si, 0:r - ts, :] = src_ref[si, ts:r, :]
    dst_ref[si, r - ts:r, :] = new


def _pad_new(x):
    return jnp.concatenate([x, jnp.zeros((NEWPAD - x.shape[0], x.shape[1]), x.dtype)], axis=0)


def _dil_sample_kernel(q_ref, kn_ref, vn_ref, kc_ref, vc_ref, o_ref, lse_ref, ko_ref, vo_ref,
                       *, window, step):
    sb, ts, _ = q_ref.shape
    r = kc_ref.shape[2]
    scale = HEAD_DIM ** -0.5
    lane = lax.broadcasted_iota(jnp.int32, (1, 128), 1)
    low = lane < HEAD_DIM
    t = lax.broadcasted_iota(jnp.int32, (2 * ts, r + 128), 0) % ts
    c = lax.broadcasted_iota(jnp.int32, (2 * ts, r + 128), 1)
    rel = r + t - jnp.where(c < 128, c, c - 128 + ts)
    valid = jnp.logical_and(jnp.logical_and(rel >= 0, rel <= window), (rel & (step - 1)) == 0)
    valid = jnp.logical_and(valid, jnp.logical_or(c < ts, c >= 128))
    pad = jnp.zeros((128 - ts, SLAB), F32)

    def shifted(src_ref, new_t, rows):
        old = src_ref[rows, :]
        rolled = pltpu.roll(old, r - ts, 1)
        tail = jnp.where(lane < 128 - ts, rolled[:, r - 128:], new_t[rows, :])
        out = tail if r == 128 else jnp.concatenate([rolled[:, :r - 128], tail], axis=1)
        return old, out

    for si, pr in [(si, pr) for si in range(sb) for pr in range(B_HEADS // 2)]:
        if pr == 0:
            kn_t = jnp.concatenate([pad, kn_ref[si]], axis=0).T
            vn_t = jnp.concatenate([pad, vn_ref[si]], axis=0).T
        rows = slice(pr * 128, (pr + 1) * 128)
        k_old, k_out = shifted(kc_ref.at[si], kn_t, rows)
        v_old, v_out = shifted(vc_ref.at[si], vn_t, rows)
        ko_ref[si, rows, :] = k_out
        vo_ref[si, rows, :] = v_out
        k_ext = jnp.concatenate([k_old[:, 0:128], k_out], axis=1).astype(BF16)
        v_ext = jnp.concatenate([v_old[:, 0:128], v_out], axis=1).astype(BF16)
        qp = q_ref[si, :, rows] * scale
        q2 = jnp.concatenate([jnp.where(low, qp, 0.0), jnp.where(low, 0.0, qp)], axis=0).astype(BF16)
        s = jnp.where(valid, jnp.dot(q2, k_ext, preferred_element_type=F32), NEG)
        m = jnp.max(s, axis=1, keepdims=True)
        e = jnp.exp(s - m)
        l = jnp.sum(e, axis=1, keepdims=True)
        a = _dot_t(e.astype(BF16), v_ext) / l
        lse = m + jnp.log(l)
        o_ref[si, :, rows] = jnp.where(low, a[0:ts], a[ts:2 * ts])
        lse_ref[si, :, rows] = jnp.where(low, lse[0:ts], lse[ts:2 * ts])


def _rows_minor(cache):
    nseq, r = cache.shape[:2]
    return jnp.transpose(cache, (0, 2, 3, 1)).reshape(nseq, -1, r)


def _rows_major(cache_t, heads):
    nseq, _, r = cache_t.shape
    return jnp.transpose(cache_t.reshape(nseq, heads, HEAD_DIM, r), (0, 3, 1, 2))


def _dilated_sample(proj, nseq, ts, g, k_cache, v_cache):
    r = k_cache.shape[1]
    assert ts % 8 == 0 and r % 128 == 0 and r >= B_WINDOWS[g]
    kt, vt = _rows_minor(k_cache), _rows_minor(v_cache)
    sb = max(1, min(nseq, 8, 1024 // r))
    p4 = proj.reshape(proj.shape[0], nseq, ts, SLAB)
    new = lambda s: pl.BlockSpec((None, sb, ts, SLAB), lambda i, s=s: (s, i, 0, 0))
    cspec = pl.BlockSpec((sb, SLAB, r), lambda i: (i, 0, 0))
    ospec = pl.BlockSpec((sb, ts, SLAB), lambda i: (i, 0, 0))
    o, lse, kn, vn = pl.pallas_call(
        functools.partial(_dil_sample_kernel, window=B_WINDOWS[g], step=B_DILATIONS[g]),
        grid=(nseq // sb,),
        in_specs=[new(4 + g), new(7 + g), new(10 + g), cspec, cspec],
        out_specs=[ospec, ospec, cspec, cspec],
        out_shape=[jax.ShapeDtypeStruct((nseq, ts, SLAB), F32)] * 2 + [jax.ShapeDtypeStruct(kt.shape, F32)] * 2,
        compiler_params=_cparams("parallel"),
        name=f"dilated_sample_g{g}",
    )(p4, p4, p4, kt, vt)
    return (o.reshape(nseq * ts, SLAB), lse.reshape(nseq * ts, SLAB),
            _rows_major(kn, B_HEADS), _rows_major(vn, B_HEADS))


def _swa_sample_kernel(q_ref, k_ref, v_ref, kc_ref, vc_ref, sink_ref, o_ref, kn_ref, vn_ref):
    sb, ts, _ = q_ref.shape
    r = kc_ref.shape[1]
    rows = D_HEADS * ts
    valid = _ext_mask(rows, r, ts, D_WINDOW - 1, 1)
    lane = lax.broadcasted_iota(jnp.int32, (1, 128), 1)
    low = lane < HEAD_DIM
    rep = D_HEADS // D_KV_HEADS
    scale = HEAD_DIM ** -0.5
    sink = sink_ref[:, 0:1]
    seqs = range(sb)

    def q_rows(si):
        pieces = []
        for pr in range(rep):
            base = q_ref[si, :, pr * 128:(pr + 1) * 128] * scale
            for g in range(D_KV_HEADS):
                pieces.append(jnp.where(low if g == 0 else jnp.logical_not(low), base, 0.0))
        return jnp.concatenate(pieces, axis=0).astype(BF16)

    kn = [k_ref[si] for si in seqs]
    vn = [v_ref[si] for si in seqs]
    qrows = [q_rows(si) for si in seqs]
    kext = [jnp.concatenate([kc_ref[si], _pad_new(kn[si])], axis=0).astype(BF16) for si in seqs]
    vext = [jnp.concatenate([vc_ref[si], _pad_new(vn[si])], axis=0).astype(BF16) for si in seqs]
    s = [jnp.where(valid, _dot_t(qrows[si], kext[si]), NEG) for si in seqs]
    m = [jnp.maximum(jnp.max(s[si], axis=1, keepdims=True), sink) for si in seqs]
    e = [jnp.exp(s[si] - m[si]) for si in seqs]
    l = [jnp.sum(e[si], axis=1, keepdims=True) + jnp.exp(sink - m[si]) for si in seqs]
    o = [jnp.dot(e[si].astype(BF16), vext[si], preferred_element_type=F32) / l[si] for si in seqs]
    for si in seqs:
        pairs = [jnp.where(low, o[si][2 * pr * ts:(2 * pr + 1) * ts], o[si][(2 * pr + 1) * ts:(2 * pr + 2) * ts])
                 for pr in range(rep)]
        o_ref[si] = jnp.concatenate(pairs, axis=1)
        _shift_cache(kn_ref, kc_ref, kn[si], si, r, ts)
        _shift_cache(vn_ref, vc_ref, vn[si], si, r, ts)


def _swa_sample(proj, nseq, ts, sinks, k_cache, v_cache):
    r = k_cache.shape[1]
    sb = min(nseq, 8)
    kvw = D_KV_HEADS * HEAD_DIM
    p4 = proj.reshape(proj.shape[0], nseq, ts, SLAB)
    kc = k_cache.reshape(nseq, r, kvw)
    vc = v_cache.reshape(nseq, r, kvw)
    cspec = pl.BlockSpec((sb, r, kvw), lambda i: (i, 0, 0))
    new = lambda col: pl.BlockSpec((None, sb, ts, kvw), lambda i: (4, i, 0, col))
    rep = D_HEADS // D_KV_HEADS
    by_row_group = sinks.reshape(D_KV_HEADS, rep).T.reshape(-1)
    sink_rows = jnp.broadcast_to(jnp.repeat(by_row_group, ts)[:, None], (D_HEADS * ts, 128))
    o, kn, vn = pl.pallas_call(
        _swa_sample_kernel,
        grid=(nseq // sb,),
        in_specs=[pl.BlockSpec((None, sb, ts, SLAB), lambda i: (3, i, 0, 0)), new(0), new(1), cspec, cspec,
                  pl.BlockSpec(sink_rows.shape, lambda i: (0, 0))],
        out_specs=[pl.BlockSpec((sb, ts, SLAB), lambda i: (i, 0, 0)), cspec, cspec],
        out_shape=[jax.ShapeDtypeStruct((nseq, ts, SLAB), F32)] + [jax.ShapeDtypeStruct(kc.shape, F32)] * 2,
        compiler_params=_cparams("parallel"),
        name="swa_sample",
    )(p4, p4, p4, kc, vc, sink_rows)
    return o.reshape(nseq * ts, SLAB), kn.reshape(k_cache.shape), vn.reshape(v_cache.shape)


def kernel(x_prompt, x_sample, state_a_C, state_a_n, state_a_m, state_a_conv, cache_b_k1, cache_b_v1, cache_b_k2, cache_b_v2, cache_b_k3, cache_b_v3, state_c_conv, state_c_ssm, cache_d_k, cache_d_v, e_norm_mix, e_w_in, e_b_gates, e_a_conv_w, e_a_conv_b, e_a_norm, e_w_out, o_norm_mix, o_w_in, o_c_conv_w, o_c_conv_b, o_c_dt_bias, o_c_A_log, o_c_D, o_c_norm, o_d_sinks, o_w_out, ffn_norm, w_gate, w_up, w_down, final_norm):
    depth = ffn_norm.shape[0]
    even_w = [_even_slabs(e_w_in[j]) for j in range((depth + 1) // 2)]
    odd_w = [_odd_slabs(o_w_in[j]) for j in range(depth // 2)]
    odd_wo = [jnp.concatenate([o_w_out[j][:C_DINNER], _group_aligned(o_w_out[j][C_DINNER:], 0)],
                              axis=0).astype(BF16) for j in range(depth // 2)]
    even_wo = [e_w_out[j].astype(BF16) for j in range((depth + 1) // 2)]
    wg16, wu16, wd16 = w_gate.astype(BF16), w_up.astype(BF16), w_down.astype(BF16)

    def trunk(x, pos, fresh, even_st, odd_st, ride=None, pre=None):
        nseq, t, _ = x.shape
        rode = {}
        assert t >= A_CONV - 1 and t >= C_CONV - 1
        cos_t, sin_t = _rope_tables(pos)
        x2 = x.reshape(nseq * t, D_MODEL)
        new_even, new_odd = [], []

        def tail(proj, s, keep, dil=1, lanes=slice(None)):
            p4 = proj.reshape(proj.shape[0], nseq, t, SLAB)
            if dil == 1:
                return p4[s, :, t - keep:, lanes]
            seg = PERM_TILE // dil
            last = p4[s, :, t - PERM_TILE:, lanes].reshape(nseq, dil, seg, -1)[:, :, seg - keep // dil:]
            return jnp.swapaxes(last, 1, 2).reshape(nseq, keep, -1)

        for layer in range(depth):
            j = layer // 2
            last = layer == depth - 1
            if layer % 2 == 0:
                if pre is not None and layer == 0:
                    pe = pre['proj']
                else:
                    pe = _norm_proj(x2, e_norm_mix[j], even_w[j], cos_t, sin_t, _even_kinds(fresh))
                st = {k: v[j] for k, v in even_st.items()}
                ya, c_new, n_new, m_new = _mlstm(pe, nseq, t, e_a_conv_w[j], e_a_conv_b[j], e_b_gates[j],
                                                 e_a_norm[j], st['a_conv'], st['a_C'], st['a_n'], st['a_m'])
                ns = {'a_C': c_new, 'a_n': n_new, 'a_m': m_new,
                      'a_conv': jnp.concatenate([tail(pe, 0, A_CONV - 1), tail(pe, 1, A_CONV - 1)], axis=-1)}
                parts = []
                for g in range(B_GROUPS):
                    kn, vn = f'b_k{g + 1}', f'b_v{g + 1}'
                    if fresh:
                        o_g, lse_g = _dilated_prompt(pe, nseq, t, g)
                        keep = min(B_WINDOWS[g], t)
                        if keep % 128 == 0 and keep <= PERM_TILE:
                            ns[kn], ns[vn] = _cache_tails(pe, nseq, t, g, keep)
                        else:
                            ns[kn] = tail(pe, 7 + g, keep, B_DILATIONS[g]).reshape(nseq, keep, B_HEADS, HEAD_DIM)
                            ns[vn] = tail(pe, 10 + g, keep, B_DILATIONS[g]).reshape(nseq, keep, B_HEADS, HEAD_DIM)
                    elif pre is not None and layer == 0 and g in pre['groups']:
                        o_g, lse_g, ns[kn], ns[vn] = pre[g]
                    else:
                        o_g, lse_g, ns[kn], ns[vn] = _dilated_sample(pe, nseq, t, g, st[kn], st[vn])
                    parts += [o_g, lse_g]
                yb = _merge_groups(parts, fresh)
                if ride is not None and layer == 0:
                    x2, rode['k'] = _post_rider(
                        x2, ya, yb, even_wo[j], layer, ffn_norm, wg16, wu16, wd16, final_norm, last, 'k',
                        ride['groups'], ride['proj'], ride['kt'], None)
                else:
                    x2 = _post(x2, ya, yb, even_wo[j], layer, ffn_norm, wg16, wu16, wd16, final_norm, last)
                new_even.append(ns)
            else:
                po = _norm_proj(x2, o_norm_mix[j], odd_w[j], cos_t, sin_t, ODD_KINDS)
                st = {k: v[j] for k, v in odd_st.items()}
                yc, ssm_new = _ssd(po, nseq, t, o_c_conv_w[j], o_c_conv_b[j], o_c_dt_bias[j], o_c_A_log[j],
                                   o_c_D[j], o_c_norm[j], st['c_conv'], st['c_ssm'])
                ns = {'c_conv': jnp.concatenate([tail(po, 1, C_CONV - 1), tail(po, 2, C_CONV - 1)], axis=-1),
                      'c_ssm': ssm_new}
                kvw = D_KV_HEADS * HEAD_DIM
                if fresh:
                    yd = _swa_prompt(po, nseq, t, o_d_sinks[j])
                    keep = min(D_WINDOW, t)
                    ns['d_k'] = tail(po, 4, keep, lanes=slice(0, kvw)).reshape(nseq, keep, D_KV_HEADS, HEAD_DIM)
                    ns['d_v'] = tail(po, 4, keep, lanes=slice(kvw, 2 * kvw)).reshape(
                        nseq, keep, D_KV_HEADS, HEAD_DIM)
                else:
                    yd, ns['d_k'], ns['d_v'] = _swa_sample(po, nseq, t, o_d_sinks[j], st['d_k'], st['d_v'])
                if ride is not None and layer == 1:
                    x2, rode['v'] = _post_rider(
                        x2, yc, yd, odd_wo[j], layer, ffn_norm, wg16, wu16, wd16, final_norm, last, 'v',
                        ride['groups'], ride['proj'], ride['vt'], [kpl[1] for kpl in rode['k']])
                else:
                    x2 = _post(x2, yc, yd, odd_wo[j], layer, ffn_norm, wg16, wu16, wd16, final_norm, last)
                new_odd.append(ns)
        out = {k: jnp.stack([ns[k] for ns in new_even]) for k in new_even[0]}
        out.update({k: jnp.stack([ns[k] for ns in new_odd]) for k in new_odd[0]})
        return x2.reshape(nseq, t, D_MODEL), out, rode

    bp, tp, _ = x_prompt.shape
    bs, ts, _ = x_sample.shape
    n_even, n_odd = (depth + 1) // 2, depth // 2
    zeros = lambda *shape: jnp.zeros(shape, F32)
    even_p = {'a_C': zeros(n_even, bp, A_HEADS, A_DK, A_DV), 'a_n': zeros(n_even, bp, A_HEADS, A_DK),
              'a_m': zeros(n_even, bp, A_HEADS), 'a_conv': zeros(n_even, bp, A_CONV - 1, 2 * SLAB)}
    odd_p = {'c_conv': zeros(n_odd, bp, C_CONV - 1, 2 * SLAB),
             'c_ssm': zeros(n_odd, bp, C_HEADS, C_HEADDIM, C_DSTATE)}
    even_s = {'a_C': state_a_C, 'a_n': state_a_n, 'a_m': state_a_m, 'a_conv': state_a_conv,
              'b_k1': cache_b_k1, 'b_v1': cache_b_v1, 'b_k2': cache_b_k2, 'b_v2': cache_b_v2,
              'b_k3': cache_b_k3, 'b_v3': cache_b_v3}
    odd_s = {'c_conv': state_c_conv, 'c_ssm': state_c_ssm, 'd_k': cache_d_k, 'd_v': cache_d_v}
    pos_p = jnp.arange(tp, dtype=jnp.int32)
    pos_s = jnp.tile(PAST_LEN + jnp.arange(ts, dtype=jnp.int32), bs)

    groups = tuple(range(B_GROUPS))
    k_caches = [even_s[f'b_k{g + 1}'][0] for g in groups]
    v_caches = [even_s[f'b_v{g + 1}'][0] for g in groups]
    tile = bp * tp * RIDER_PHASES // bs
    can_ride = (depth >= 2 and bp * tp * RIDER_PHASES % bs == 0 and tile % 8 == 0 and 8 <= tile <= 1024
                and ts == 8 and all(c.shape[1] % 128 == 0 and c.shape[1] >= B_WINDOWS[g]
                                    for g, c in zip(groups, k_caches)))
    if can_ride:
        cos_s, sin_s = _rope_tables(pos_s)
        pe_s = _norm_proj(x_sample.reshape(bs * ts, D_MODEL), e_norm_mix[0], even_w[0], cos_s, sin_s,
                          _even_kinds(False))
        ride = {'groups': groups, 'proj': pe_s, 'kt': [_rows_minor(c) for c in k_caches],
                'vt': [_rows_minor(c) for c in v_caches]}
        y_p, pn, rode = trunk(x_prompt, pos_p, True, even_p, odd_p, ride=ride)
        pre = {'groups': groups, 'proj': pe_s}
        for k, g in enumerate(groups):
            (k_out, _, lse_g), (v_out, o_g) = rode['k'][k], rode['v'][k]
            pre[g] = (o_g, lse_g, _rows_major(k_out, B_HEADS), _rows_major(v_out, B_HEADS))
        y_s, sn, _ = trunk(x_sample, pos_s, False, even_s, odd_s, pre=pre)
    else:
        y_p, pn, _ = trunk(x_prompt, pos_p, True, even_p, odd_p)
        y_s, sn, _ = trunk(x_sample, pos_s, False, even_s, odd_s)

    keys = ('a_C', 'a_n', 'a_m', 'a_conv', 'b_k1', 'b_v1', 'b_k2', 'b_v2', 'b_k3', 'b_v3',
            'c_conv', 'c_ssm', 'd_k', 'd_v')
    states = []
    for k in keys:
        states += [pn[k], sn[k]]
    return (y_p, y_s, *states)
```
